```python
import math
import jax, jax.numpy as jnp
from jax import lax
import numpy as np

D_MODEL = 1024
BATCH = 8
SEQ = 2048
DEPTH = 1
DEC_BATCH = 128
DEC_SEQ = 1
PAST_LEN = 16384
PAGE_SIZE = 128

M_HEADS = 4
M_HEAD_DIM = 128
M_WIDTH = M_HEADS * M_HEAD_DIM
M_CHUNK = 128
A_HEADS = 8
QK_NOPE = 64
QK_ROPE = 32
V_HEAD = 64
A_WIDTH = A_HEADS * V_HEAD
Q_LORA = 384
KV_LORA = 256
ROPE_BASE = 10000.0
Q_BLOCK = 128
N_MEM = 256
X_HEADS = 4
X_HEAD_DIM = 128
X_WIDTH = X_HEADS * X_HEAD_DIM
D_FF = -(-8 * D_MODEL // (3 * 256)) * 256
EPS = 1e-6
IN_SIZES = (M_WIDTH, M_WIDTH, M_WIDTH, M_HEADS, M_HEADS, M_WIDTH, Q_LORA, KV_LORA, QK_ROPE, D_MODEL, D_MODEL)
D_IN = sum(IN_SIZES)

kernel_name = 'hybrid_mlstm_mla_memory_decoder_step'


def rms_norm(x, g):
    xf = x.astype(jnp.float32)
    y = xf * lax.rsqrt(jnp.mean(xf * xf, axis=-1, keepdims=True) + EPS)
    return (y * g.astype(jnp.float32)).astype(x.dtype)


def head_layer_norm(h, g):
    B, T, H, D = h.shape
    mu = jnp.mean(h, axis=-1, keepdims=True)
    var = jnp.mean(jnp.square(h - mu), axis=-1, keepdims=True)
    y = (h - mu) * lax.rsqrt(var + EPS)
    return y.reshape(B, T, H * D) * g


def rope_tables(pos0, T):
    pos = (pos0 + jnp.arange(T)).astype(jnp.float32)
    inv = ROPE_BASE ** (-jnp.arange(0, QK_ROPE, 2, dtype=jnp.float32) / QK_ROPE)
    ang = pos[:, None] * inv[None, :]
    return jnp.cos(ang), jnp.sin(ang)


def apply_rope(x, cos, sin):
    x1, x2 = jnp.split(x.astype(jnp.float32), 2, axis=-1)
    return jnp.concatenate([x1 * cos - x2 * sin, x2 * cos + x1 * sin], axis=-1).astype(x.dtype)


def mlstm_chunkwise(q, k, v, ig, lf, C0, n0, m0):
    B, T, H, D = q.shape
    L = math.gcd(T, M_CHUNK)
    nc = T // L
    causal = jnp.tril(jnp.ones((L, L), dtype=bool))

    def to_chunks(a):
        return jnp.moveaxis(a.reshape((B, nc, L) + a.shape[2:]), 1, 0)

    def step(carry, xs):
        C, n, m = carry
        qc, kc, vc, ic, fc = xs
        b = jnp.cumsum(fc, axis=1)
        a = b + m[:, None, :]
        dmat = b[:, :, None, :] - b[:, None, :, :] + ic[:, None, :, :]
        dmat = jnp.where(causal[None, :, :, None], dmat, -jnp.inf)
        mt = jnp.maximum(a, jnp.max(dmat, axis=2))
        w_intra = jnp.exp(dmat - mt[:, :, None, :])
        w_inter = jnp.exp(a - mt)
        s = jnp.einsum('bthd,bshd->btsh', qc, kc) * w_intra
        num = w_inter[..., None] * jnp.einsum('bhvk,bthk->bthv', C, qc) + jnp.einsum('btsh,bshv->bthv', s, vc)
        den = w_inter * jnp.einsum('bhk,bthk->bth', n, qc) + jnp.sum(s, axis=2)
        h = num / jnp.maximum(jnp.abs(den), jnp.exp(-mt))[..., None]
        gL = b[:, -1]
        wlog = gL[:, None, :] - b + ic
        m_new = jnp.maximum(gL + m, jnp.max(wlog, axis=1))
        ws = jnp.exp(wlog - m_new[:, None, :])
        decay = jnp.exp(gL + m - m_new)
        C_new = decay[..., None, None] * C + jnp.einsum('bsh,bshv,bshk->bhvk', ws, vc, kc)
        n_new = decay[..., None] * n + jnp.einsum('bsh,bshk->bhk', ws, kc)
        return (C_new, n_new, m_new), h

    xs = (to_chunks(q), to_chunks(k), to_chunks(v), to_chunks(ig), to_chunks(lf))
    (C, n, m), hs = lax.scan(step, (C0, n0, m0), xs)
    h = jnp.moveaxis(hs, 0, 1).reshape(B, T, H, D)
    return h, C, n, m


def mla_attention(q_lat, q_rope, ckv, kr, pos0, ckv_past, kr_past):
    B, T, H, R = q_lat.shape
    QB = math.gcd(T, Q_BLOCK)
    nb = T // QB
    scale = (QK_NOPE + QK_ROPE) ** -0.5
    kpos = pos0 + jnp.arange(T)

    def blocks(a):
        return jnp.moveaxis(a.reshape((B, nb, QB) + a.shape[2:]), 1, 0)

    def attend_block(args):
        ql, qr, start = args
        qpos = pos0 + start + jnp.arange(QB)
        s = (jnp.einsum('bqhr,bkr->bhqk', ql, ckv) + jnp.einsum('bqhd,bkd->bhqk', qr, kr)).astype(jnp.float32) * scale
        s = jnp.where(kpos[None, :] <= qpos[:, None], s, -jnp.inf)
        if ckv_past is None:
            p = jax.nn.softmax(s, axis=-1).astype(ckv.dtype)
            return jnp.einsum('bhqk,bkr->bqhr', p, ckv)
        sp = (jnp.einsum('bqhr,bkr->bhqk', ql, ckv_past) + jnp.einsum('bqhd,bkd->bhqk', qr, kr_past)).astype(jnp.float32) * scale
        n_past = ckv_past.shape[1]
        p = jax.nn.softmax(jnp.concatenate([sp, s], axis=-1), axis=-1).astype(ckv.dtype)
        return jnp.einsum('bhqk,bkr->bqhr', p[..., :n_past], ckv_past) + jnp.einsum('bhqk,bkr->bqhr', p[..., n_past:], ckv)

    starts = jnp.arange(nb, dtype=jnp.int32) * QB
    o = lax.map(attend_block, (blocks(q_lat), blocks(q_rope), starts))
    return jnp.moveaxis(o, 0, 1).reshape(B, T, H, R)


def hybrid_mixer(h, pos0, ckv_past, kr_past, C0, n0, m0, p):
    B, T, _ = h.shape
    f32 = jnp.float32
    z = h @ p['w_in']
    offs = np.cumsum(IN_SIZES)[:-1].tolist()
    mq, mk, mv, mi, mf, mo, cq, ckv, kr, ga, gb = jnp.split(z, offs, axis=-1)
    q = mq.reshape(B, T, M_HEADS, M_HEAD_DIM).astype(f32)
    k = mk.reshape(B, T, M_HEADS, M_HEAD_DIM).astype(f32) * (M_HEAD_DIM ** -0.5)
    v = mv.reshape(B, T, M_HEADS, M_HEAD_DIM).astype(f32)
    gates = (jnp.concatenate([mi, mf], axis=-1) + p['b_if']).astype(f32)
    ig = gates[..., :M_HEADS]
    lf = jax.nn.log_sigmoid(gates[..., M_HEADS:])
    hm, C, n, m = mlstm_chunkwise(q, k, v, ig, lf, C0.astype(f32), n0.astype(f32), m0.astype(f32))
    hm = head_layer_norm(hm, p['g_mhead'].astype(f32)).astype(h.dtype) * jax.nn.sigmoid(mo)
    y_a = hm @ p['w_branch_a']
    cq = rms_norm(cq, p['g_q_lora'])
    qa = (cq @ p['w_uq']).reshape(B, T, A_HEADS, QK_NOPE + QK_ROPE)
    q_nope, q_rope = qa[..., :QK_NOPE], qa[..., QK_NOPE:]
    ckv = rms_norm(ckv, p['g_kv_lora'])
    cos, sin = rope_tables(pos0, T)
    q_rope = apply_rope(q_rope, cos[None, :, None], sin[None, :, None])
    kr = apply_rope(kr, cos[None], sin[None])
    q_lat = jnp.einsum('bthn,rhn->bthr', q_nope, p['w_uk'])
    o_lat = mla_attention(q_lat, q_rope, ckv, kr, pos0, ckv_past, kr_past)
    o_b = jnp.einsum('bthr,rhv->bthv', o_lat, p['w_uv']).reshape(B, T, A_WIDTH)
    y_b = o_b @ p['w_branch_b']
    y = (jax.nn.sigmoid(ga) * y_a + jax.nn.sigmoid(gb) * y_b) @ p['w_out']
    return y, ckv, kr, C, n, m


def memory_kv(mem, p):
    B, M, _ = mem.shape
    mn = rms_norm(mem, p['g_mem'])
    mk = (mn @ p['w_mk']).reshape(B, M, X_HEADS, X_HEAD_DIM)
    mv = (mn @ p['w_mv']).reshape(B, M, X_HEADS, X_HEAD_DIM)
    return mk, mv


def memory_attention(h, mem_k, mem_v, p):
    B, T, _ = h.shape
    q = (h @ p['w_mq']).reshape(B, T, X_HEADS, X_HEAD_DIM)
    s = jnp.einsum('bthd,bmhd->bhtm', q, mem_k).astype(jnp.float32) * (X_HEAD_DIM ** -0.5)
    a = jax.nn.softmax(s, axis=-1).astype(mem_v.dtype)
    o = jnp.einsum('bhtm,bmhd->bthd', a, mem_v).reshape(B, T, X_WIDTH)
    return o @ p['w_mo']


def swiglu_ffn(h, p):
    g, u = jnp.split(h @ p['w_ffn_in'], 2, axis=-1)
    return (jax.nn.silu(g) * u) @ p['w_ffn_out']


def decoder_layer(x, pos0, mem_k, mem_v, ckv_past, kr_past, C0, n0, m0, p):
    y, ckv, kr, C, n, m = hybrid_mixer(rms_norm(x, p['g_pre_mix']), pos0, ckv_past, kr_past, C0, n0, m0, p)
    x = x + rms_norm(y, p['g_post_mix'])
    x = x + rms_norm(memory_attention(rms_norm(x, p['g_pre_mem']), mem_k, mem_v, p), p['g_post_mem'])
    x = x + rms_norm(swiglu_ffn(rms_norm(x, p['g_pre_ffn']), p), p['g_post_ffn'])
    return x, ckv, kr, C, n, m


def setup_inputs(seed: int = 0) -> dict:
    key = jax.random.key(seed)
    ks = iter(jax.random.split(key, 48))
    f32 = jnp.float32
    n_pages = PAST_LEN // PAGE_SIZE
    n_phys = (5 * DEC_BATCH * n_pages) // 4

    def nrm(shape, scale=1.0):
        return jax.random.normal(next(ks), shape, f32) * scale

    def gain(width):
        return 1.0 + nrm((DEPTH, width), 0.02)

    inp = {}
    inp['x_prompt'] = nrm((BATCH, SEQ, D_MODEL))
    inp['x_sample'] = nrm((DEC_BATCH, DEC_SEQ, D_MODEL))
    inp['cache_ckv'] = nrm((DEPTH, n_phys, PAGE_SIZE, KV_LORA))
    inp['cache_krope'] = nrm((DEPTH, n_phys, PAGE_SIZE, QK_ROPE))
    inp['cache_mem_k'] = nrm((DEPTH, DEC_BATCH, N_MEM, X_HEADS, X_HEAD_DIM))
    inp['cache_mem_v'] = nrm((DEPTH, DEC_BATCH, N_MEM, X_HEADS, X_HEAD_DIM))
    inp['state_C'] = nrm((DEPTH, DEC_BATCH, M_HEADS, M_HEAD_DIM, M_HEAD_DIM), 0.5)
    inp['state_n'] = nrm((DEPTH, DEC_BATCH, M_HEADS, M_HEAD_DIM))
    inp['state_m'] = 1.0 + nrm((DEPTH, DEC_BATCH, M_HEADS))
    perm = jax.random.permutation(next(ks), n_phys)
    inp['page_table'] = perm[:DEC_BATCH * n_pages].reshape(DEC_BATCH, n_pages).astype(jnp.int32)
    inp['mem_prompt'] = nrm((BATCH, N_MEM, D_MODEL))
    inp['g_pre_mix'] = gain(D_MODEL)
    inp['w_in'] = nrm((DEPTH, D_MODEL, D_IN), D_MODEL ** -0.5)
    inp['b_if'] = jnp.concatenate([nrm((DEPTH, M_HEADS), 0.1), 3.0 + nrm((DEPTH, M_HEADS), 0.5)], axis=-1)
    inp['g_mhead'] = gain(M_WIDTH)
    inp['g_q_lora'] = gain(Q_LORA)
    inp['w_uq'] = nrm((DEPTH, Q_LORA, A_HEADS * (QK_NOPE + QK_ROPE)), Q_LORA ** -0.5)
    inp['g_kv_lora'] = gain(KV_LORA)
    inp['w_uk'] = nrm((DEPTH, KV_LORA, A_HEADS, QK_NOPE), KV_LORA ** -0.5)
    inp['w_uv'] = nrm((DEPTH, KV_LORA, A_HEADS, V_HEAD), KV_LORA ** -0.5)
    inp['w_branch_a'] = nrm((DEPTH, M_WIDTH, D_MODEL), M_WIDTH ** -0.5)
    inp['w_branch_b'] = nrm((DEPTH, A_WIDTH, D_MODEL), A_WIDTH ** -0.5)
    inp['w_out'] = nrm((DEPTH, D_MODEL, D_MODEL), D_MODEL ** -0.5)
    inp['g_post_mix'] = gain(D_MODEL)
    inp['g_pre_mem'] = gain(D_MODEL)
    inp['g_mem'] = gain(D_MODEL)
    inp['w_mq'] = nrm((DEPTH, D_MODEL, X_WIDTH), D_MODEL ** -0.5)
    inp['w_mk'] = nrm((DEPTH, D_MODEL, X_WIDTH), D_MODEL ** -0.5)
    inp['w_mv'] = nrm((DEPTH, D_MODEL, X_WIDTH), D_MODEL ** -0.5)
    inp['w_mo'] = nrm((DEPTH, X_WIDTH, D_MODEL), X_WIDTH ** -0.5)
    inp['g_post_mem'] = gain(D_MODEL)
    inp['g_pre_ffn'] = gain(D_MODEL)
    inp['w_ffn_in'] = nrm((DEPTH, D_MODEL, 2 * D_FF), D_MODEL ** -0.5)
    inp['w_ffn_out'] = nrm((DEPTH, D_FF, D_MODEL), D_FF ** -0.5)
    inp['g_post_ffn'] = gain(D_MODEL)
    return inp


def reference(x_prompt, x_sample, cache_ckv, cache_krope, cache_mem_k, cache_mem_v, state_C, state_n, state_m,
              page_table, mem_prompt, g_pre_mix, w_in, b_if, g_mhead, g_q_lora, w_uq, g_kv_lora, w_uk, w_uv,
              w_branch_a, w_branch_b, w_out, g_post_mix, g_pre_mem, g_mem, w_mq, w_mk, w_mv, w_mo, g_post_mem,
              g_pre_ffn, w_ffn_in, w_ffn_out, g_post_ffn):
    weights = dict(g_pre_mix=g_pre_mix, w_in=w_in, b_if=b_if, g_mhead=g_mhead, g_q_lora=g_q_lora, w_uq=w_uq,
                   g_kv_lora=g_kv_lora, w_uk=w_uk, w_uv=w_uv, w_branch_a=w_branch_a, w_branch_b=w_branch_b,
                   w_out=w_out, g_post_mix=g_post_mix, g_pre_mem=g_pre_mem, g_mem=g_mem, w_mq=w_mq, w_mk=w_mk,
                   w_mv=w_mv, w_mo=w_mo, g_post_mem=g_post_mem, g_pre_ffn=g_pre_ffn, w_ffn_in=w_ffn_in,
                   w_ffn_out=w_ffn_out, g_post_ffn=g_post_ffn)
    f32 = jnp.float32
    Bp = x_prompt.shape[0]
    Bs = x_sample.shape[0]
    past_len = page_table.shape[1] * PAGE_SIZE
    xp, xs = x_prompt, x_sample
    ckv_p, kr_p, C_p, n_p, m_p, mk_p, mv_p = [], [], [], [], [], [], []
    ckv_s, kr_s, C_s, n_s, m_s = [], [], [], [], []
    for layer in range(DEPTH):
        p = {name: w[layer] for name, w in weights.items()}
        mem_k, mem_v = memory_kv(mem_prompt, p)
        C0 = jnp.zeros((Bp, M_HEADS, M_HEAD_DIM, M_HEAD_DIM), f32)
        n0 = jnp.zeros((Bp, M_HEADS, M_HEAD_DIM), f32)
        m0 = jnp.zeros((Bp, M_HEADS), f32)
        xp, ckv, kr, C, n, m = decoder_layer(xp, 0, mem_k, mem_v, None, None, C0, n0, m0, p)
        ckv_p.append(ckv); kr_p.append(kr); C_p.append(C); n_p.append(n); m_p.append(m)
        mk_p.append(mem_k); mv_p.append(mem_v)
        ckv_past = cache_ckv[layer, page_table].reshape(Bs, past_len, KV_LORA)
        kr_past = cache_krope[layer, page_table].reshape(Bs, past_len, QK_ROPE)
        xs, ckv, kr, C, n, m = decoder_layer(xs, past_len, cache_mem_k[layer], cache_mem_v[layer], ckv_past, kr_past,
                                             state_C[layer], state_n[layer], state_m[layer], p)
        ckv_s.append(ckv); kr_s.append(kr); C_s.append(C); n_s.append(n); m_s.append(m)
    y_prompt, y_sample = xp, xs
    new_ckv_prompt = jnp.stack(ckv_p)
    new_krope_prompt = jnp.stack(kr_p)
    new_C_prompt = jnp.stack(C_p)
    new_n_prompt = jnp.stack(n_p)
    new_m_prompt = jnp.stack(m_p)
    new_mem_k_prompt = jnp.stack(mk_p)
    new_mem_v_prompt = jnp.stack(mv_p)
    new_ckv_sample = jnp.stack(ckv_s)
    new_krope_sample = jnp.stack(kr_s)
    new_C_sample = jnp.stack(C_s)
    new_n_sample = jnp.stack(n_s)
    new_m_sample = jnp.stack(m_s)
    return (y_prompt, y_sample, new_ckv_prompt, new_krope_prompt, new_C_prompt, new_n_prompt, new_m_prompt,
            new_mem_k_prompt, new_mem_v_prompt, new_ckv_sample, new_krope_sample, new_C_sample, new_n_sample,
            new_m_sample)
```

```python
import functools

import jax
import jax.numpy as jnp
import numpy as np
from jax import lax
from jax.experimental import pallas as pl
from jax.experimental.pallas import tpu as pltpu

F32 = jnp.float32
BF16 = jnp.bfloat16

D_MODEL = 1024
PAGE_SIZE = 128
M_HEADS = 4
M_HEAD_DIM = 128
M_WIDTH = M_HEADS * M_HEAD_DIM
M_CHUNK = 128
A_HEADS = 8
QK_NOPE = 64
QK_ROPE = 32
V_HEAD = 64
Q_LORA = 384
KV_LORA = 256
ROPE_BASE = 10000.0
N_MEM = 256
X_HEADS = 4
X_HEAD_DIM = 128
X_WIDTH = X_HEADS * X_HEAD_DIM
D_FF = 2816
EPS = 1e-6

LANES = 128
HEAD_SLOT = 128
ABS_SLOT = 384
ATT_SCALE = (QK_NOPE + QK_ROPE) ** -0.5
VMEM_LIMIT = 52 * 1024 * 1024

OFF_QKV = 0
OFF_MO = 1536
OFF_CQ = 2048
OFF_CKV = 2432
OFF_G = 2688
OFF_TAIL = 4736
W_BIG = 4864
TAIL_IG = 64
TAIL_LF = 68


def _rms(x, g):
    return x * lax.rsqrt(jnp.mean(x * x, axis=-1, keepdims=True) + EPS) * g


def _dot(a, b):
    return jnp.dot(a.astype(BF16), b.astype(BF16), preferred_element_type=F32)


def _dot_nt(a, b):
    return lax.dot_general(a.astype(BF16), b.astype(BF16), (((1,), (1,)), ((), ())),
                           preferred_element_type=F32)


def _dot_tn(a, b):
    return lax.dot_general(a.astype(BF16), b.astype(BF16), (((0,), (0,)), ((), ())),
                           preferred_element_type=F32)


def _log_sigmoid(x):
    return jnp.minimum(x, 0.0) - jnp.log1p(jnp.exp(-jnp.abs(x)))


def _const_spec(shape):
    nd = len(shape)
    return pl.BlockSpec(shape, lambda *_: (0,) * nd, pipeline_mode=pl.Buffered(1))


def _params(n_axes):
    return pltpu.CompilerParams(dimension_semantics=("arbitrary",) * n_axes, vmem_limit_bytes=VMEM_LIMIT)


def _inproj_kernel(x_ref, gpre_ref, wbig_ref, btail_ref, ct_ref, st_ref, gq_ref, wq_ref, wqs_ref,
                   c128_ref, s128_ref, gkv_ref, *rest, absorbed):
    if absorbed:
        wabs_ref, qkvm_ref, og_ref, tail_ref, ckv_ref, sg_ref, qabs_ref = rest
    else:
        wuk_ref, wuv_ref, pk_ref, qkvm_ref, og_ref, tail_ref, ckv_ref, sg_ref, qcat_ref, kcat_ref, vcat_ref = rest
    h = _rms(x_ref[...], gpre_ref[...]).astype(BF16)

    zm = jnp.dot(h, wbig_ref[:, OFF_QKV:OFF_MO], preferred_element_type=F32)
    qkvm_ref[:, 0:M_WIDTH] = zm[:, 0:M_WIDTH].astype(qkvm_ref.dtype)
    qkvm_ref[:, M_WIDTH:2 * M_WIDTH] = (zm[:, M_WIDTH:2 * M_WIDTH] * (M_HEAD_DIM ** -0.5)).astype(qkvm_ref.dtype)
    qkvm_ref[:, 2 * M_WIDTH:] = zm[:, 2 * M_WIDTH:].astype(qkvm_ref.dtype)
    og_ref[...] = jax.nn.sigmoid(jnp.dot(h, wbig_ref[:, OFF_MO:OFF_CQ], preferred_element_type=F32))
    sg_ref[...] = jax.nn.sigmoid(jnp.dot(h, wbig_ref[:, OFF_G:OFF_TAIL], preferred_element_type=F32))

    t = jnp.dot(h, wbig_ref[:, OFF_TAIL:W_BIG], preferred_element_type=F32) + btail_ref[...]
    t = t * ct_ref[...] + pltpu.roll(t, LANES - QK_ROPE, 1) * st_ref[...]
    lane = lax.broadcasted_iota(jnp.int32, t.shape, 1)
    is_lf = jnp.logical_and(lane >= TAIL_LF, lane < TAIL_LF + M_HEADS)
    t = jnp.where(is_lf, _log_sigmoid(t), t)
    tail_ref[...] = t

    cqn = _rms(jnp.dot(h, wbig_ref[:, OFF_CQ:OFF_CKV], preferred_element_type=F32), gq_ref[...]).astype(BF16)
    ckvn = _rms(jnp.dot(h, wbig_ref[:, OFF_CKV:OFF_G], preferred_element_type=F32), gkv_ref[...])
    ckv_ref[...] = ckvn

    qc = jnp.dot(cqn, wq_ref[...], preferred_element_type=F32)
    qs = jnp.dot(cqn, wqs_ref[...], preferred_element_type=F32)
    c128 = c128_ref[...]
    s128 = s128_ref[...]
    for hh in range(A_HEADS):
        sl = slice(HEAD_SLOT * hh, HEAD_SLOT * (hh + 1))
        qh = ((qc[:, sl] * c128 + qs[:, sl] * s128) * ATT_SCALE).astype(BF16)
        if absorbed:
            qabs_ref[:, ABS_SLOT * hh:ABS_SLOT * (hh + 1)] = jnp.dot(
                qh, wabs_ref[hh], preferred_element_type=F32).astype(BF16)
        else:
            qcat_ref[:, sl] = qh
    if not absorbed:
        cb = ckvn.astype(BF16)
        kcat_ref[...] = (jnp.dot(cb, wuk_ref[...], preferred_element_type=F32)
                         + jnp.dot(t.astype(BF16), pk_ref[...], preferred_element_type=F32)).astype(BF16)
        vcat_ref[...] = jnp.dot(cb, wuv_ref[...], preferred_element_type=F32).astype(BF16)


def _inproj(x, w, tables, *, tm, absorbed, mdtype):
    m = x.shape[0]
    ct, st, c128, s128 = tables
    nt = ct.shape[0] // tm
    grid = (m // tm,)
    row = lambda i: (i, 0)
    tab = lambda i: (i % nt, 0)
    in_specs = [
        pl.BlockSpec((tm, D_MODEL), row),
        _const_spec((1, D_MODEL)), _const_spec((D_MODEL, W_BIG)), _const_spec((1, LANES)),
        pl.BlockSpec((tm, LANES), tab), pl.BlockSpec((tm, LANES), tab),
        _const_spec((1, Q_LORA)), _const_spec((Q_LORA, A_HEADS * HEAD_SLOT)), _const_spec((Q_LORA, A_HEADS * HEAD_SLOT)),
        pl.BlockSpec((tm, LANES), tab), pl.BlockSpec((tm, LANES), tab),
        _const_spec((1, KV_LORA)),
    ]
    args = [x, w["g_pre_mix"], w["w_big"], w["b_tail"], ct, st, w["g_q_lora"], w["wq_cat"], w["wq_sw"], c128, s128,
            w["g_kv_lora"]]
    outs = [
        (jax.ShapeDtypeStruct((m, 3 * M_WIDTH), mdtype), pl.BlockSpec((tm, 3 * M_WIDTH), row)),
        (jax.ShapeDtypeStruct((m, M_WIDTH), F32), pl.BlockSpec((tm, M_WIDTH), row)),
        (jax.ShapeDtypeStruct((m, LANES), F32), pl.BlockSpec((tm, LANES), row)),
        (jax.ShapeDtypeStruct((m, KV_LORA), F32), pl.BlockSpec((tm, KV_LORA), row)),
        (jax.ShapeDtypeStruct((m, 2 * D_MODEL), F32), pl.BlockSpec((tm, 2 * D_MODEL), row)),
    ]
    if absorbed:
        in_specs += [_const_spec((A_HEADS, HEAD_SLOT, ABS_SLOT))]
        args += [w["w_abs"]]
        outs += [(jax.ShapeDtypeStruct((m, A_HEADS * ABS_SLOT), BF16), pl.BlockSpec((tm, A_HEADS * ABS_SLOT), row))]
    else:
        wide = A_HEADS * HEAD_SLOT
        in_specs += [_const_spec((KV_LORA, wide)), _const_spec((KV_LORA, wide)), _const_spec((LANES, wide))]
        args += [w["wuk_cat"], w["wuv_cat"], w["p_kr"]]
        outs += [(jax.ShapeDtypeStruct((m, wide), BF16), pl.BlockSpec((tm, wide), row))] * 3
    return pl.pallas_call(
        functools.partial(_inproj_kernel, absorbed=absorbed),
        grid=grid, in_specs=in_specs,
        out_specs=[o[1] for o in outs], out_shape=[o[0] for o in outs],
        compiler_params=_params(1), name="inproj_abs" if absorbed else "inproj",
    )(*args)


def _mlstm_kernel(qkv_ref, gr_ref, og_ref, gm_ref, hg_ref, c_ref, n_ref, m_ref):
    L = M_CHUNK

    @pl.when(pl.program_id(1) == 0)
    def _():
        c_ref[...] = jnp.zeros_like(c_ref)
        n_ref[...] = jnp.zeros_like(n_ref)
        m_ref[...] = jnp.zeros_like(m_ref)

    row = lax.broadcasted_iota(jnp.int32, (L, L), 0)
    col = lax.broadcasted_iota(jnp.int32, (L, L), 1)
    tril = col <= row
    eye = col == row
    for h in range(M_HEADS):
        sl = slice(M_HEAD_DIM * h, M_HEAD_DIM * (h + 1))
        q = qkv_ref[:, sl]
        k = qkv_ref[:, M_WIDTH + M_HEAD_DIM * h:M_WIDTH + M_HEAD_DIM * (h + 1)]
        v = qkv_ref[:, 2 * M_WIDTH + M_HEAD_DIM * h:2 * M_WIDTH + M_HEAD_DIM * (h + 1)]
        ig = gr_ref[0, h:h + 1, :]
        lf = gr_ref[0, M_HEADS + h:M_HEADS + h + 1, :]
        m_prev = m_ref[0, h:h + 1, 0:1]
        c_prev = c_ref[0, h]
        n_prev = n_ref[0, h:h + 1, :]

        b_col = jnp.sum(jnp.where(tril, jnp.broadcast_to(lf, (L, L)), 0.0), axis=1, keepdims=True)
        b_row = jnp.sum(jnp.where(eye, jnp.broadcast_to(b_col, (L, L)), 0.0), axis=0, keepdims=True)
        a_col = b_col + m_prev
        dmat = jnp.where(tril, b_col + (ig - b_row), -jnp.inf)
        mt = jnp.maximum(a_col, jnp.max(dmat, axis=1, keepdims=True))
        w_intra = jnp.exp(dmat - mt)
        w_inter = jnp.exp(a_col - mt)
        s = _dot_nt(q, k) * w_intra
        num = w_inter * _dot_nt(q, c_prev) + _dot(s, v)
        den = w_inter * jnp.sum(q.astype(F32) * n_prev, axis=1, keepdims=True) + jnp.sum(s, axis=1, keepdims=True)
        hs = num / jnp.maximum(jnp.abs(den), jnp.exp(-mt))

        g_last = jnp.sum(lf, axis=1, keepdims=True)
        wlog = g_last - b_row + ig
        m_new = jnp.maximum(g_last + m_prev, jnp.max(wlog, axis=1, keepdims=True))
        ws_row = jnp.exp(wlog - m_new)
        ws_col = jnp.sum(jnp.where(eye, jnp.broadcast_to(ws_row, (L, L)), 0.0), axis=1, keepdims=True)
        decay = jnp.exp(g_last + m_prev - m_new)
        kf = k.astype(F32)
        c_ref[0, h] = decay * c_prev + _dot_tn(v.astype(F32) * ws_col, k)
        n_ref[0, h:h + 1, :] = decay * n_prev + jnp.sum(ws_col * kf, axis=0, keepdims=True)
        m_ref[0, h:h + 1, :] = jnp.broadcast_to(m_new, (1, LANES))

        mu = jnp.mean(hs, axis=1, keepdims=True)
        d = hs - mu
        y = d * lax.rsqrt(jnp.mean(d * d, axis=1, keepdims=True) + EPS) * gm_ref[:, sl]
        hg_ref[:, sl] = (y * og_ref[:, sl]).astype(BF16)


def _mlstm_prompt(qkvm, gates_rows, og, g_mhead, *, batch, seq):
    nc = seq // M_CHUNK
    tok = lambda b, c: (b * nc + c, 0)
    return pl.pallas_call(
        _mlstm_kernel,
        grid=(batch, nc),
        in_specs=[
            pl.BlockSpec((M_CHUNK, 3 * M_WIDTH), tok),
            pl.BlockSpec((1, 2 * M_HEADS, M_CHUNK), lambda b, c: (b, 0, c)),
            pl.BlockSpec((M_CHUNK, M_WIDTH), tok),
            pl.BlockSpec((1, M_WIDTH), lambda b, c: (0, 0)),
        ],
        out_specs=[
            pl.BlockSpec((M_CHUNK, M_WIDTH), tok),
            pl.BlockSpec((1, M_HEADS, M_HEAD_DIM, M_HEAD_DIM), lambda b, c: (b, 0, 0, 0)),
            pl.BlockSpec((1, M_HEADS, M_HEAD_DIM), lambda b, c: (b, 0, 0)),
            pl.BlockSpec((1, M_HEADS, LANES), lambda b, c: (b, 0, 0)),
        ],
        out_shape=[
            jax.ShapeDtypeStruct((batch * seq, M_WIDTH), BF16),
            jax.ShapeDtypeStruct((batch, M_HEADS, M_HEAD_DIM, M_HEAD_DIM), F32),
            jax.ShapeDtypeStruct((batch, M_HEADS, M_HEAD_DIM), F32),
            jax.ShapeDtypeStruct((batch, M_HEADS, LANES), F32),
        ],
        compiler_params=_params(2), name="mlstm_prompt",
    )(qkvm, gates_rows, og, g_mhead)


def _mlstm_step_kernel(qkv_ref, tail_ref, c_ref, n_ref, m_ref, og_ref, gm_ref, hg_ref, co_ref, no_ref, mo_ref, *, rows):
    D = M_HEAD_DIM
    r_i = lax.broadcasted_iota(jnp.int32, (D, D), 0)
    c_i = lax.broadcasted_iota(jnp.int32, (D, D), 1)
    eye = r_i == c_i
    lane = lax.broadcasted_iota(jnp.int32, (1, LANES), 1)
    for g in range(rows):
        m_out = jnp.zeros((1, LANES), F32)
        for h in range(M_HEADS):
            sl = slice(D * h, D * (h + 1))
            q = qkv_ref[g:g + 1, sl]
            k = qkv_ref[g:g + 1, M_WIDTH + D * h:M_WIDTH + D * (h + 1)]
            v = qkv_ref[g:g + 1, 2 * M_WIDTH + D * h:2 * M_WIDTH + D * (h + 1)]
            ig = tail_ref[g:g + 1, TAIL_IG + h:TAIL_IG + h + 1]
            lf = tail_ref[g:g + 1, TAIL_LF + h:TAIL_LF + h + 1]
            m_prev = m_ref[g:g + 1, h:h + 1]
            c_prev = c_ref[g, h]
            n_prev = n_ref[g, h:h + 1, :]
            a = lf + m_prev
            mt = jnp.maximum(a, ig)
            w_in = jnp.exp(ig - mt)
            w_st = jnp.exp(a - mt)
            s = jnp.sum(q * k, axis=1, keepdims=True) * w_in
            cq_col = jnp.sum(c_prev * q, axis=1, keepdims=True)
            cq = jnp.sum(jnp.where(eye, jnp.broadcast_to(cq_col, (D, D)), 0.0), axis=0, keepdims=True)
            num = w_st * cq + s * v
            den = w_st * jnp.sum(n_prev * q, axis=1, keepdims=True) + s
            hs = num / jnp.maximum(jnp.abs(den), jnp.exp(-mt))
            v_col = jnp.sum(jnp.where(eye, jnp.broadcast_to(v, (D, D)), 0.0), axis=1, keepdims=True)
            co_ref[g, h] = w_st * c_prev + w_in * (v_col * k)
            no_ref[g, h:h + 1, :] = w_st * n_prev + w_in * k
            m_out = jnp.where(lane == h, mt, m_out)
            mu = jnp.mean(hs, axis=1, keepdims=True)
            d = hs - mu
            y = d * lax.rsqrt(jnp.mean(d * d, axis=1, keepdims=True) + EPS) * gm_ref[:, sl]
            hg_ref[g:g + 1, sl] = (y * og_ref[g:g + 1, sl]).astype(BF16)
        mo_ref[g:g + 1, :] = m_out


def _mlstm_step(qkvm, tail, c0, n0, m0_pad, og, g_mhead, *, rows=8):
    nb = qkvm.shape[0]
    row = lambda i: (i, 0)
    return pl.pallas_call(
        functools.partial(_mlstm_step_kernel, rows=rows),
        grid=(nb // rows,),
        in_specs=[
            pl.BlockSpec((rows, 3 * M_WIDTH), row),
            pl.BlockSpec((rows, LANES), row),
            pl.BlockSpec((rows, M_HEADS, M_HEAD_DIM, M_HEAD_DIM), lambda i: (i, 0, 0, 0)),
            pl.BlockSpec((rows, M_HEADS, M_HEAD_DIM), lambda i: (i, 0, 0)),
            pl.BlockSpec((rows, LANES), row),
            pl.BlockSpec((rows, M_WIDTH), row),
            pl.BlockSpec((1, M_WIDTH), lambda i: (0, 0)),
        ],
        out_specs=[
            pl.BlockSpec((rows, M_WIDTH), row),
            pl.BlockSpec((rows, M_HEADS, M_HEAD_DIM, M_HEAD_DIM), lambda i: (i, 0, 0, 0)),
            pl.BlockSpec((rows, M_HEADS, M_HEAD_DIM), lambda i: (i, 0, 0)),
            pl.BlockSpec((rows, LANES), row),
        ],
        out_shape=[
            jax.ShapeDtypeStruct((nb, M_WIDTH), BF16),
            jax.ShapeDtypeStruct((nb, M_HEADS, M_HEAD_DIM, M_HEAD_DIM), F32),
            jax.ShapeDtypeStruct((nb, M_HEADS, M_HEAD_DIM), F32),
            jax.ShapeDtypeStruct((nb, LANES), F32),
        ],
        compiler_params=_params(1), name="mlstm_step",
    )(qkvm, tail, c0, n0, m0_pad, og, g_mhead)


def _mla_prefill_kernel(q_ref, k_ref, v_ref, o_ref, *, blk):
    i = pl.program_id(2)
    q = q_ref[...]
    row = lax.broadcasted_iota(jnp.int32, (blk, blk), 0)
    col = lax.broadcasted_iota(jnp.int32, (blk, blk), 1)

    def step(j, carry, masked):
        m, l, acc = carry
        start = pl.multiple_of(j * blk, blk)
        ks = k_ref[pl.ds(start, blk), :]
        vs = v_ref[pl.ds(start, blk), :]
        s = _dot_nt(q, ks)
        if masked:
            s = jnp.where(col <= row, s, -jnp.inf)
        m_new = jnp.maximum(m, jnp.max(s, axis=1, keepdims=True))
        p = jnp.exp(s - m_new)
        alpha = jnp.exp(m - m_new)
        l = alpha * l + jnp.sum(p, axis=1, keepdims=True)
        acc = alpha * acc + jnp.dot(p.astype(BF16), vs, preferred_element_type=F32)
        return m_new, l, acc

    carry = (jnp.full((blk, 1), -jnp.inf, F32), jnp.zeros((blk, 1), F32), jnp.zeros((blk, HEAD_SLOT), F32))
    carry = lax.fori_loop(0, i, lambda j, c: step(j, c, False), carry)
    _, l, acc = step(i, carry, True)
    o_ref[...] = (acc / l).astype(BF16)


def _mla_prefill(qcat, kcat, vcat, *, batch, seq, blk):
    nq = seq // blk
    return pl.pallas_call(
        functools.partial(_mla_prefill_kernel, blk=blk),
        grid=(batch, A_HEADS, nq),
        in_specs=[
            pl.BlockSpec((blk, HEAD_SLOT), lambda b, h, i: (b * nq + i, h)),
            pl.BlockSpec((seq, HEAD_SLOT), lambda b, h, i: (b, h)),
            pl.BlockSpec((seq, HEAD_SLOT), lambda b, h, i: (b, h)),
        ],
        out_specs=pl.BlockSpec((blk, HEAD_SLOT), lambda b, h, i: (b * nq + i, h)),
        out_shape=jax.ShapeDtypeStruct((batch * seq, A_HEADS * HEAD_SLOT), BF16),
        compiler_params=_params(3), name="mla_prefill",
    )(qcat, kcat, vcat)


def _mla_decode_kernel(pt_ref, q_ref, ckv_hbm, kr_hbm, cself_ref, tself_ref, o_ref,
                       cbuf, rbuf, sems, m_scr, l_scr, acc_scr, *, n_chunks, chunk_pages):
    b = pl.program_id(0)
    nb = pl.num_programs(0)

    def page_copies(bb, c, slot, p):
        pg = pt_ref[bb, c * chunk_pages + p]
        dst = pl.ds(p * PAGE_SIZE, PAGE_SIZE)
        return (pltpu.make_async_copy(ckv_hbm.at[pg], cbuf.at[slot, dst], sems.at[0, slot]),
                pltpu.make_async_copy(kr_hbm.at[pg], rbuf.at[slot, dst], sems.at[1, slot]))

    def issue(bb, c, slot):
        for p in range(chunk_pages):
            for cp in page_copies(bb, c, slot, p):
                cp.start()

    def wait(slot):
        for p in range(chunk_pages):
            for cp in page_copies(0, 0, slot, p):
                cp.wait()

    @pl.when(b == 0)
    def _():
        issue(0, 0, 0)

    q = q_ref[0]
    q_lat = q[:, 0:KV_LORA]
    q_rope = q[:, KV_LORA:KV_LORA + QK_ROPE]
    m_scr[...] = jnp.full(m_scr.shape, -jnp.inf, F32)
    l_scr[...] = jnp.zeros(l_scr.shape, F32)
    acc_scr[...] = jnp.zeros(acc_scr.shape, F32)

    for c in range(n_chunks):
        slot = c % 2
        if c + 1 < n_chunks:
            issue(b, c + 1, 1 - slot)
        else:
            @pl.when(b + 1 < nb)
            def _():
                issue(b + 1, 0, 1 - slot)
        wait(slot)
        kc = cbuf[slot].astype(BF16)
        kr = rbuf[slot].astype(BF16)
        s = _dot_nt(q_lat, kc) + _dot_nt(q_rope, kr)
        m_old = m_scr[...]
        m_new = jnp.maximum(m_old, jnp.max(s, axis=1, keepdims=True))
        p = jnp.exp(s - m_new)
        alpha = jnp.exp(m_old - m_new)
        l_scr[...] = alpha * l_scr[...] + jnp.sum(p, axis=1, keepdims=True)
        acc_scr[...] = alpha * acc_scr[...] + jnp.dot(p.astype(BF16), kc, preferred_element_type=F32)
        m_scr[...] = m_new

    c_self = cself_ref[0].astype(BF16).astype(F32)
    r_self = tself_ref[0][:, 0:QK_ROPE].astype(BF16).astype(F32)
    s_self = (jnp.sum(q_lat.astype(F32) * c_self, axis=1, keepdims=True)
              + jnp.sum(q_rope.astype(F32) * r_self, axis=1, keepdims=True))
    m_old = m_scr[...]
    m_new = jnp.maximum(m_old, s_self)
    p_self = jnp.exp(s_self - m_new)
    alpha = jnp.exp(m_old - m_new)
    l = alpha * l_scr[...] + p_self
    acc = alpha * acc_scr[...] + p_self.astype(BF16).astype(F32) * c_self
    o_ref[0] = acc / l


def _mla_decode(page_table, qabs, cache_ckv, cache_krope, ckv_self, tail_self, *, chunk_pages=32):
    nb, n_pages = page_table.shape
    n_chunks = n_pages // chunk_pages
    keys = chunk_pages * PAGE_SIZE
    grid_spec = pltpu.PrefetchScalarGridSpec(
        num_scalar_prefetch=1,
        grid=(nb,),
        in_specs=[
            pl.BlockSpec((1, A_HEADS, ABS_SLOT), lambda b, pt: (b, 0, 0)),
            pl.BlockSpec(memory_space=pl.ANY),
            pl.BlockSpec(memory_space=pl.ANY),
            pl.BlockSpec((1, 1, KV_LORA), lambda b, pt: (b, 0, 0)),
            pl.BlockSpec((1, 1, LANES), lambda b, pt: (b, 0, 0)),
        ],
        out_specs=pl.BlockSpec((1, A_HEADS, KV_LORA), lambda b, pt: (b, 0, 0)),
        scratch_shapes=[
            pltpu.VMEM((2, keys, KV_LORA), F32),
            pltpu.VMEM((2, keys, QK_ROPE), F32),
            pltpu.SemaphoreType.DMA((2, 2)),
            pltpu.VMEM((A_HEADS, 1), F32),
            pltpu.VMEM((A_HEADS, 1), F32),
            pltpu.VMEM((A_HEADS, KV_LORA), F32),
        ],
    )
    return pl.pallas_call(
        functools.partial(_mla_decode_kernel, n_chunks=n_chunks, chunk_pages=chunk_pages),
        grid_spec=grid_spec,
        out_shape=jax.ShapeDtypeStruct((nb, A_HEADS, KV_LORA), F32),
        compiler_params=_params(1), name="mla_decode",
    )(page_table, qabs.reshape(nb, A_HEADS, ABS_SLOT), cache_ckv, cache_krope,
      ckv_self.reshape(nb, 1, KV_LORA), tail_self.reshape(nb, 1, LANES))


def _merge_kernel(x_ref, hg_ref, ob_ref, sg_ref, wa_ref, wb_ref, wout_ref, gpost_ref, gmem_ref, wmq_ref,
                  *rest, from_latent):
    if from_latent:
        wuv_ref, x1_ref, qm_ref = rest
        ob = jnp.concatenate(
            [_dot(ob_ref[:, KV_LORA * hh:KV_LORA * (hh + 1)], wuv_ref[hh]).astype(BF16) for hh in range(A_HEADS)],
            axis=1)
    else:
        x1_ref, qm_ref = rest
        ob = ob_ref[...]
    ya = jnp.dot(hg_ref[...], wa_ref[...], preferred_element_type=F32)
    yb = jnp.dot(ob, wb_ref[...], preferred_element_type=F32)
    mix = sg_ref[:, 0:D_MODEL] * ya + sg_ref[:, D_MODEL:] * yb
    y = _dot(mix, wout_ref[...])
    x1 = x_ref[...] + _rms(y, gpost_ref[...])
    x1_ref[...] = x1
    qm = _dot(_rms(x1, gmem_ref[...]), wmq_ref[...]) * (X_HEAD_DIM ** -0.5)
    qm_ref[...] = qm.astype(qm_ref.dtype)


def _merge(x, hg, ob, sg, w, *, tm, from_latent, qdtype):
    m = x.shape[0]
    row = lambda i: (i, 0)
    wide = A_HEADS * HEAD_SLOT
    ob_w = ob.shape[1]
    in_specs = [
        pl.BlockSpec((tm, D_MODEL), row), pl.BlockSpec((tm, M_WIDTH), row), pl.BlockSpec((tm, ob_w), row),
        pl.BlockSpec((tm, 2 * D_MODEL), row),
        _const_spec((M_WIDTH, D_MODEL)), _const_spec((wide, D_MODEL)), _const_spec((D_MODEL, D_MODEL)),
        _const_spec((1, D_MODEL)), _const_spec((1, D_MODEL)), _const_spec((D_MODEL, X_WIDTH)),
    ]
    args = [x, hg, ob, sg, w["w_branch_a"], w["wb_cat"], w["w_out"], w["g_post_mix"], w["g_pre_mem"], w["w_mq"]]
    if from_latent:
        in_specs += [_const_spec((A_HEADS, KV_LORA, HEAD_SLOT))]
        args += [w["wuv_heads"]]
    return pl.pallas_call(
        functools.partial(_merge_kernel, from_latent=from_latent),
        grid=(m // tm,), in_specs=in_specs,
        out_specs=[pl.BlockSpec((tm, D_MODEL), row), pl.BlockSpec((tm, X_WIDTH), row)],
        out_shape=[jax.ShapeDtypeStruct((m, D_MODEL), F32), jax.ShapeDtypeStruct((m, X_WIDTH), qdtype)],
        compiler_params=_params(1), name="merge_lat" if from_latent else "merge",
    )(*args)


def _memkv_kernel(mem_ref, g_ref, wk_ref, wv_ref, k_ref, v_ref):
    mn = _rms(mem_ref[...], g_ref[...]).astype(BF16)
    k_ref[...] = jnp.dot(mn, wk_ref[...], preferred_element_type=F32)
    v_ref[...] = jnp.dot(mn, wv_ref[...], preferred_element_type=F32)


def _memkv(mem, w, *, tm):
    m = mem.shape[0]
    row = lambda i: (i, 0)
    return pl.pallas_call(
        _memkv_kernel, grid=(m // tm,),
        in_specs=[pl.BlockSpec((tm, D_MODEL), row), _const_spec((1, D_MODEL)),
                  _const_spec((D_MODEL, X_WIDTH)), _const_spec((D_MODEL, X_WIDTH))],
        out_specs=[pl.BlockSpec((tm, X_WIDTH), row)] * 2,
        out_shape=[jax.ShapeDtypeStruct((m, X_WIDTH), F32)] * 2,
        compiler_params=_params(1), name="memkv",
    )(mem, w["g_mem"], w["w_mk"], w["w_mv"])


def _memattn_kernel(q_ref, k_ref, v_ref, o_ref):
    kb = k_ref[...].astype(BF16)
    vb = v_ref[...].astype(BF16)
    for h in range(X_HEADS):
        sl = slice(X_HEAD_DIM * h, X_HEAD_DIM * (h + 1))
        s = _dot_nt(q_ref[:, sl], kb[:, sl])
        p = jnp.exp(s - jnp.max(s, axis=1, keepdims=True))
        l = jnp.sum(p, axis=1, keepdims=True)
        o_ref[:, sl] = (jnp.dot(p.astype(BF16), vb[:, sl], preferred_element_type=F32) / l).astype(BF16)


def _memattn(qm, mem_k, mem_v, *, batch, seq, tm):
    nt = seq // tm
    tok = lambda b, i: (b * nt + i, 0)
    kv = lambda b, i: (b, 0)
    return pl.pallas_call(
        _memattn_kernel, grid=(batch, nt),
        in_specs=[pl.BlockSpec((tm, X_WIDTH), tok), pl.BlockSpec((N_MEM, X_WIDTH), kv),
                  pl.BlockSpec((N_MEM, X_WIDTH), kv)],
        out_specs=pl.BlockSpec((tm, X_WIDTH), tok),
        out_shape=jax.ShapeDtypeStruct((batch * seq, X_WIDTH), BF16),
        compiler_params=_params(2), name="memattn",
    )(qm, mem_k, mem_v)


def _memattn_step_kernel(q_ref, k_ref, v_ref, o_ref, *, rows):
    for g in range(rows):
        for h in range(X_HEADS):
            sl = slice(X_HEAD_DIM * h, X_HEAD_DIM * (h + 1))
            q = q_ref[g:g + 1, sl]
            kh = k_ref[g, :, sl]
            vh = v_ref[g, :, sl]
            s = jnp.sum(kh * q, axis=1, keepdims=True)
            p = jnp.exp(s - jnp.max(s, axis=0, keepdims=True))
            l = jnp.sum(p, axis=0, keepdims=True)
            o_ref[g:g + 1, sl] = (jnp.sum(p * vh, axis=0, keepdims=True) / l).astype(BF16)


def _memattn_step(qm, mem_k, mem_v, *, rows=8):
    nb = qm.shape[0]
    row = lambda i: (i, 0)
    kv = lambda i: (i, 0, 0)
    return pl.pallas_call(
        functools.partial(_memattn_step_kernel, rows=rows), grid=(nb // rows,),
        in_specs=[pl.BlockSpec((rows, X_WIDTH), row), pl.BlockSpec((rows, N_MEM, X_WIDTH), kv),
                  pl.BlockSpec((rows, N_MEM, X_WIDTH), kv)],
        out_specs=pl.BlockSpec((rows, X_WIDTH), row),
        out_shape=jax.ShapeDtypeStruct((nb, X_WIDTH), BF16),
        compiler_params=_params(1), name="memattn_step",
    )(qm, mem_k, mem_v)


def _ffn_kernel(x1_ref, o_ref, wmo_ref, gpm_ref, gpf_ref, wg_ref, wu_ref, wdn_ref, gpost_ref, y_ref, *, f_chunk):
    x2 = x1_ref[...] + _rms(jnp.dot(o_ref[...], wmo_ref[...], preferred_element_type=F32), gpm_ref[...])
    h = _rms(x2, gpf_ref[...]).astype(BF16)
    acc = jnp.zeros(x2.shape, F32)
    for c in range(D_FF // f_chunk):
        sl = slice(f_chunk * c, f_chunk * (c + 1))
        g = jnp.dot(h, wg_ref[:, sl], preferred_element_type=F32)
        u = jnp.dot(h, wu_ref[:, sl], preferred_element_type=F32)
        acc = acc + jnp.dot((g * jax.nn.sigmoid(g) * u).astype(BF16), wdn_ref[sl, :], preferred_element_type=F32)
    y_ref[...] = x2 + _rms(acc, gpost_ref[...])


def _ffn(x1, o, w, *, tm, f_chunk=1408):
    m = x1.shape[0]
    row = lambda i: (i, 0)
    return pl.pallas_call(
        functools.partial(_ffn_kernel, f_chunk=f_chunk), grid=(m // tm,),
        in_specs=[pl.BlockSpec((tm, D_MODEL), row), pl.BlockSpec((tm, X_WIDTH), row),
                  _const_spec((X_WIDTH, D_MODEL)), _const_spec((1, D_MODEL)), _const_spec((1, D_MODEL)),
                  _const_spec((D_MODEL, D_FF)), _const_spec((D_MODEL, D_FF)), _const_spec((D_FF, D_MODEL)),
                  _const_spec((1, D_MODEL))],
        out_specs=pl.BlockSpec((tm, D_MODEL), row),
        out_shape=jax.ShapeDtypeStruct((m, D_MODEL), F32),
        compiler_params=_params(1), name="ffn",
    )(x1, o, w["w_mo"], w["g_post_mem"], w["g_pre_ffn"], w["w_ffn_gate"], w["w_ffn_up"], w["w_ffn_out"],
      w["g_post_ffn"])


def _prep_weights(p):
    w = {}
    for name in ("g_pre_mix", "g_q_lora", "g_kv_lora", "g_mhead", "g_post_mix", "g_pre_mem", "g_mem", "g_post_mem",
                 "g_pre_ffn", "g_post_ffn"):
        w[name] = p[name].reshape(1, -1).astype(F32)
    w_in = p["w_in"]
    offs = np.cumsum((0, M_WIDTH, M_WIDTH, M_WIDTH, M_HEADS, M_HEADS, M_WIDTH, Q_LORA, KV_LORA, QK_ROPE,
                      D_MODEL, D_MODEL))
    seg = lambda i: w_in[:, offs[i]:offs[i + 1]]
    mq, mk, mv, mi, mf, mo, cq, ckv, kr, ga, gb = (seg(i) for i in range(11))
    half = QK_ROPE // 2
    swap = lambda a: jnp.concatenate([a[..., half:], a[..., :half]], axis=-1)
    tail = jnp.concatenate([kr, swap(kr), mi, mf,
                            jnp.zeros((D_MODEL, LANES - 2 * QK_ROPE - 2 * M_HEADS), w_in.dtype)], axis=1)
    w["w_big"] = jnp.concatenate([mq, mk, mv, mo, cq, ckv, ga, gb, tail], axis=1).astype(BF16)
    w["b_tail"] = jnp.zeros((1, LANES), F32).at[0, TAIL_IG:TAIL_IG + 2 * M_HEADS].set(p["b_if"].astype(F32))

    pad = HEAD_SLOT - QK_NOPE - QK_ROPE
    wq = p["w_uq"].reshape(Q_LORA, A_HEADS, QK_NOPE + QK_ROPE)
    zq = jnp.zeros((Q_LORA, A_HEADS, pad), wq.dtype)
    w["wq_cat"] = jnp.concatenate([wq, zq], axis=2).reshape(Q_LORA, A_HEADS * HEAD_SLOT).astype(BF16)
    w["wq_sw"] = jnp.concatenate([jnp.zeros_like(wq[..., :QK_NOPE]), swap(wq[..., QK_NOPE:]), zq],
                                 axis=2).reshape(Q_LORA, A_HEADS * HEAD_SLOT).astype(BF16)
    w_uk, w_uv = p["w_uk"], p["w_uv"]
    zk = jnp.zeros((KV_LORA, A_HEADS, HEAD_SLOT - QK_NOPE), w_uk.dtype)
    w["wuk_cat"] = jnp.concatenate([w_uk, zk], axis=2).reshape(KV_LORA, A_HEADS * HEAD_SLOT).astype(BF16)
    w["wuv_cat"] = jnp.concatenate([w_uv, zk], axis=2).reshape(KV_LORA, A_HEADS * HEAD_SLOT).astype(BF16)
    w["wuv_heads"] = jnp.concatenate([w_uv, zk], axis=2).transpose(1, 0, 2).astype(BF16)
    pk = np.zeros((LANES, A_HEADS, HEAD_SLOT), np.float32)
    for r in range(QK_ROPE):
        pk[r, :, QK_NOPE + r] = 1.0
    w["p_kr"] = jnp.asarray(pk.reshape(LANES, A_HEADS * HEAD_SLOT), BF16)
    wabs = jnp.zeros((A_HEADS, HEAD_SLOT, ABS_SLOT), F32)
    wabs = wabs.at[:, :QK_NOPE, :KV_LORA].set(w_uk.transpose(1, 2, 0).astype(F32))
    wabs = wabs.at[:, QK_NOPE + np.arange(QK_ROPE), KV_LORA + np.arange(QK_ROPE)].set(1.0)
    w["w_abs"] = wabs.astype(BF16)
    wb = p["w_branch_b"].reshape(A_HEADS, V_HEAD, D_MODEL)
    w["wb_cat"] = jnp.concatenate([wb, jnp.zeros((A_HEADS, HEAD_SLOT - V_HEAD, D_MODEL), wb.dtype)],
                                  axis=1).reshape(A_HEADS * HEAD_SLOT, D_MODEL).astype(BF16)
    for name in ("w_branch_a", "w_out", "w_mq", "w_mk", "w_mv", "w_mo", "w_ffn_out"):
        w[name] = p[name].astype(BF16)
    w["w_ffn_gate"] = p["w_ffn_in"][:, :D_FF].astype(BF16)
    w["w_ffn_up"] = p["w_ffn_in"][:, D_FF:].astype(BF16)
    return w


def _rope_tables(pos0, n):
    pos = (pos0 + jnp.arange(n)).astype(F32)
    inv = ROPE_BASE ** (-jnp.arange(0, QK_ROPE, 2, dtype=F32) / QK_ROPE)
    ang = pos[:, None] * inv[None, :]
    cos, sin = jnp.cos(ang), jnp.sin(ang)
    c32 = jnp.concatenate([cos, cos], axis=1)
    s32 = jnp.concatenate([-sin, sin], axis=1)
    one = lambda k: jnp.ones((n, k), F32)
    zero = lambda k: jnp.zeros((n, k), F32)
    ct = jnp.concatenate([c32, one(LANES - QK_ROPE)], axis=1)
    st = jnp.concatenate([s32, zero(LANES - QK_ROPE)], axis=1)
    c128 = jnp.concatenate([one(QK_NOPE), c32, one(HEAD_SLOT - QK_NOPE - QK_ROPE)], axis=1)
    s128 = jnp.concatenate([zero(QK_NOPE), s32, zero(HEAD_SLOT - QK_NOPE - QK_ROPE)], axis=1)
    return ct, st, c128, s128


def kernel(x_prompt, x_sample, cache_ckv, cache_krope, cache_mem_k, cache_mem_v, state_C, state_n, state_m, page_table, mem_prompt, g_pre_mix, w_in, b_if, g_mhead, g_q_lora, w_uq, g_kv_lora, w_uk, w_uv, w_branch_a, w_branch_b, w_out, g_post_mix, g_pre_mem, g_mem, w_mq, w_mk, w_mv, w_mo, g_post_mem, g_pre_ffn, w_ffn_in, w_ffn_out, g_post_ffn):
    params = dict(g_pre_mix=g_pre_mix, w_in=w_in, b_if=b_if, g_mhead=g_mhead, g_q_lora=g_q_lora, w_uq=w_uq,
                  g_kv_lora=g_kv_lora, w_uk=w_uk, w_uv=w_uv, w_branch_a=w_branch_a, w_branch_b=w_branch_b,
                  w_out=w_out, g_post_mix=g_post_mix, g_pre_mem=g_pre_mem, g_mem=g_mem, w_mq=w_mq, w_mk=w_mk,
                  w_mv=w_mv, w_mo=w_mo, g_post_mem=g_post_mem, g_pre_ffn=g_pre_ffn, w_ffn_in=w_ffn_in,
                  w_ffn_out=w_ffn_out, g_post_ffn=g_post_ffn)
    depth = w_in.shape[0]
    assert depth == 1, "single-layer stack only"
    bp, seq, _ = x_prompt.shape
    bs, dec_seq, _ = x_sample.shape
    assert dec_seq == 1, "one new token per sample sequence"
    past_len = page_table.shape[1] * PAGE_SIZE
    w = _prep_weights({name: a[0] for name, a in params.items()})

    xp = x_prompt.reshape(bp * seq, D_MODEL)
    qkvm, og, tail, ckv_p, sg, qcat, kcat, vcat = _inproj(
        xp, w, _rope_tables(0, seq), tm=256, absorbed=False, mdtype=BF16)
    gates_rows = tail.reshape(bp, seq, LANES)[:, :, TAIL_IG:TAIL_IG + 2 * M_HEADS].transpose(0, 2, 1)
    hg, c_p, n_p, m_p = _mlstm_prompt(qkvm, gates_rows, og, w["g_mhead"], batch=bp, seq=seq)
    ob = _mla_prefill(qcat, kcat, vcat, batch=bp, seq=seq, blk=512)
    x1, qm = _merge(xp, hg, ob, sg, w, tm=512, from_latent=False, qdtype=BF16)
    mem_k, mem_v = _memkv(mem_prompt.reshape(bp * N_MEM, D_MODEL), w, tm=512)
    om = _memattn(qm, mem_k, mem_v, batch=bp, seq=seq, tm=512)
    y_prompt = _ffn(x1, om, w, tm=512).reshape(bp, seq, D_MODEL)

    xs = x_sample.reshape(bs, D_MODEL)
    tables_s = tuple(jnp.broadcast_to(t, (bs, LANES)) for t in _rope_tables(past_len, 1))
    qkvm_s, og_s, tail_s, ckv_s, sg_s, qabs = _inproj(xs, w, tables_s, tm=bs, absorbed=True, mdtype=F32)
    m0_pad = jnp.pad(state_m.reshape(bs, M_HEADS).astype(F32), ((0, 0), (0, LANES - M_HEADS)))
    hg_s, c_s, n_s, m_s = _mlstm_step(qkvm_s, tail_s, state_C.reshape(bs, M_HEADS, M_HEAD_DIM, M_HEAD_DIM),
                                      state_n.reshape(bs, M_HEADS, M_HEAD_DIM), m0_pad, og_s, w["g_mhead"])
    n_phys = cache_ckv.shape[1]
    o_lat = _mla_decode(page_table, qabs, cache_ckv.reshape(n_phys, PAGE_SIZE, KV_LORA),
                        cache_krope.reshape(n_phys, PAGE_SIZE, QK_ROPE), ckv_s, tail_s)
    x1_s, qm_s = _merge(xs, hg_s, o_lat.reshape(bs, A_HEADS * KV_LORA), sg_s, w, tm=bs, from_latent=True, qdtype=F32)
    om_s = _memattn_step(qm_s, cache_mem_k.reshape(bs, N_MEM, X_WIDTH), cache_mem_v.reshape(bs, N_MEM, X_WIDTH))
    y_sample = _ffn(x1_s, om_s, w, tm=bs).reshape(bs, 1, D_MODEL)

    return (y_prompt, y_sample,
            ckv_p.reshape(1, bp, seq, KV_LORA), tail[:, :QK_ROPE].reshape(1, bp, seq, QK_ROPE),
            c_p.reshape(1, bp, M_HEADS, M_HEAD_DIM, M_HEAD_DIM), n_p.reshape(1, bp, M_HEADS, M_HEAD_DIM),
            m_p[:, :, 0].reshape(1, bp, M_HEADS),
            mem_k.reshape(1, bp, N_MEM, X_HEADS, X_HEAD_DIM), mem_v.reshape(1, bp, N_MEM, X_HEADS, X_HEAD_DIM),
            ckv_s.reshape(1, bs, 1, KV_LORA), tail_s[:, :QK_ROPE].reshape(1, bs, 1, QK_ROPE),
            c_s.reshape(1, bs, M_HEADS, M_HEAD_DIM, M_HEAD_DIM), n_s.reshape(1, bs, M_HEADS, M_HEAD_DIM),
            m_s[:, :M_HEADS].reshape(1, bs, M_HEADS))
```

```python
import functools

import jax
import jax.numpy as jnp
import numpy as np
from jax import lax
from jax.experimental import pallas as pl
from jax.experimental.pallas import tpu as pltpu

F32 = jnp.float32
BF16 = jnp.bfloat16

D_MODEL = 1024
PAGE_SIZE = 128
M_HEADS = 4
M_HEAD_DIM = 128
M_WIDTH = M_HEADS * M_HEAD_DIM
M_CHUNK = 128
A_HEADS = 8
QK_NOPE = 64
QK_ROPE = 32
V_HEAD = 64
Q_LORA = 384
KV_LORA = 256
ROPE_BASE = 10000.0
N_MEM = 256
X_HEADS = 4
X_HEAD_DIM = 128
X_WIDTH = X_HEADS * X_HEAD_DIM
D_FF = 2816
EPS = 1e-6

LANES = 128
HEAD_SLOT = 128
ABS_SLOT = 384
ATT_SCALE = (QK_NOPE + QK_ROPE) ** -0.5
VMEM_LIMIT = 52 * 1024 * 1024

OFF_QKV = 0
OFF_MO = 1536
OFF_CQ = 2048
OFF_CKV = 2432
OFF_G = 2688
OFF_TAIL = 4736
W_BIG = 4864
TAIL_IG = 64
TAIL_LF = 68


def _rms(x, g):
    return x * lax.rsqrt(jnp.mean(x * x, axis=-1, keepdims=True) + EPS) * g


def _dot(a, b):
    return jnp.dot(a.astype(BF16), b.astype(BF16), preferred_element_type=F32)


def _dot_nt(a, b):
    return lax.dot_general(a.astype(BF16), b.astype(BF16), (((1,), (1,)), ((), ())),
                           preferred_element_type=F32)


def _dot_tn(a, b):
    return lax.dot_general(a.astype(BF16), b.astype(BF16), (((0,), (0,)), ((), ())),
                           preferred_element_type=F32)


def _log_sigmoid(x):
    return jnp.minimum(x, 0.0) - jnp.log1p(jnp.exp(-jnp.abs(x)))


def _const_spec(shape):
    nd = len(shape)
    return pl.BlockSpec(shape, lambda *_: (0,) * nd, pipeline_mode=pl.Buffered(1))


def _params(n_axes):
    return pltpu.CompilerParams(dimension_semantics=("arbitrary",) * n_axes, vmem_limit_bytes=VMEM_LIMIT)


def _inproj_kernel(x_ref, gpre_ref, wbig_ref, btail_ref, ct_ref, st_ref, gq_ref, wq_ref, wqs_ref,
                   c128_ref, s128_ref, gkv_ref, *rest, absorbed):
    if absorbed:
        wabs_ref, qkvm_ref, og_ref, tail_ref, ckv_ref, sg_ref, qabs_ref = rest
    else:
        (wuk_ref, wuv_ref, pk_ref, wvt_ref,
         qkvm_ref, og_ref, tail_ref, ckv_ref, sg_ref, qcat_ref, kcat_ref, vcat_ref, vt_ref) = rest
    h = _rms(x_ref[...], gpre_ref[...]).astype(BF16)

    zm = jnp.dot(h, wbig_ref[:, OFF_QKV:OFF_MO], preferred_element_type=F32)
    qkvm_ref[:, 0:M_WIDTH] = zm[:, 0:M_WIDTH].astype(qkvm_ref.dtype)
    qkvm_ref[:, M_WIDTH:2 * M_WIDTH] = (zm[:, M_WIDTH:2 * M_WIDTH] * (M_HEAD_DIM ** -0.5)).astype(qkvm_ref.dtype)
    qkvm_ref[:, 2 * M_WIDTH:] = zm[:, 2 * M_WIDTH:].astype(qkvm_ref.dtype)
    og_ref[...] = jax.nn.sigmoid(jnp.dot(h, wbig_ref[:, OFF_MO:OFF_CQ], preferred_element_type=F32))
    sg_ref[...] = jax.nn.sigmoid(jnp.dot(h, wbig_ref[:, OFF_G:OFF_TAIL], preferred_element_type=F32))

    t = jnp.dot(h, wbig_ref[:, OFF_TAIL:W_BIG], preferred_element_type=F32) + btail_ref[...]
    t = t * ct_ref[...] + pltpu.roll(t, LANES - QK_ROPE, 1) * st_ref[...]
    lane = lax.broadcasted_iota(jnp.int32, t.shape, 1)
    is_lf = jnp.logical_and(lane >= TAIL_LF, lane < TAIL_LF + M_HEADS)
    t = jnp.where(is_lf, _log_sigmoid(t), t)
    tail_ref[...] = t

    cqn = _rms(jnp.dot(h, wbig_ref[:, OFF_CQ:OFF_CKV], preferred_element_type=F32), gq_ref[...]).astype(BF16)
    ckvn = _rms(jnp.dot(h, wbig_ref[:, OFF_CKV:OFF_G], preferred_element_type=F32), gkv_ref[...])
    ckv_ref[...] = ckvn

    qc = jnp.dot(cqn, wq_ref[...], preferred_element_type=F32)
    qs = jnp.dot(cqn, wqs_ref[...], preferred_element_type=F32)
    c128 = c128_ref[...]
    s128 = s128_ref[...]
    for hh in range(A_HEADS):
        sl = slice(HEAD_SLOT * hh, HEAD_SLOT * (hh + 1))
        qh = ((qc[:, sl] * c128 + qs[:, sl] * s128) * ATT_SCALE).astype(BF16)
        if absorbed:
            qabs_ref[:, ABS_SLOT * hh:ABS_SLOT * (hh + 1)] = jnp.dot(
                qh, wabs_ref[hh], preferred_element_type=F32).astype(BF16)
        else:
            qcat_ref[:, sl] = qh
    if not absorbed:
        cb = ckvn.astype(BF16)
        kcat_ref[...] = (jnp.dot(cb, wuk_ref[...], preferred_element_type=F32)
                         + jnp.dot(t.astype(BF16), pk_ref[...], preferred_element_type=F32)).astype(BF16)
        vcat_ref[...] = jnp.dot(cb, wuv_ref[...], preferred_element_type=F32).astype(BF16)
        vt_ref[...] = _dot_nt(wvt_ref[...], h).astype(BF16)


def _inproj(x, w, tables, *, tm, absorbed, mdtype):
    m = x.shape[0]
    ct, st, c128, s128 = tables
    nt = ct.shape[0] // tm
    grid = (m // tm,)
    row = lambda i: (i, 0)
    tab = lambda i: (i % nt, 0)
    in_specs = [
        pl.BlockSpec((tm, D_MODEL), row),
        _const_spec((1, D_MODEL)), _const_spec((D_MODEL, W_BIG)), _const_spec((1, LANES)),
        pl.BlockSpec((tm, LANES), tab), pl.BlockSpec((tm, LANES), tab),
        _const_spec((1, Q_LORA)), _const_spec((Q_LORA, A_HEADS * HEAD_SLOT)), _const_spec((Q_LORA, A_HEADS * HEAD_SLOT)),
        pl.BlockSpec((tm, LANES), tab), pl.BlockSpec((tm, LANES), tab),
        _const_spec((1, KV_LORA)),
    ]
    args = [x, w["g_pre_mix"], w["w_big"], w["b_tail"], ct, st, w["g_q_lora"], w["wq_cat"], w["wq_sw"], c128, s128,
            w["g_kv_lora"]]
    outs = [
        (jax.ShapeDtypeStruct((m, 3 * M_WIDTH), mdtype), pl.BlockSpec((tm, 3 * M_WIDTH), row)),
        (jax.ShapeDtypeStruct((m, M_WIDTH), F32), pl.BlockSpec((tm, M_WIDTH), row)),
        (jax.ShapeDtypeStruct((m, LANES), F32), pl.BlockSpec((tm, LANES), row)),
        (jax.ShapeDtypeStruct((m, KV_LORA), F32), pl.BlockSpec((tm, KV_LORA), row)),
        (jax.ShapeDtypeStruct((m, 2 * D_MODEL), F32), pl.BlockSpec((tm, 2 * D_MODEL), row)),
    ]
    if absorbed:
        in_specs += [_const_spec((A_HEADS, HEAD_SLOT, ABS_SLOT))]
        args += [w["w_abs"]]
        outs += [(jax.ShapeDtypeStruct((m, A_HEADS * ABS_SLOT), BF16), pl.BlockSpec((tm, A_HEADS * ABS_SLOT), row))]
    else:
        wide = A_HEADS * HEAD_SLOT
        in_specs += [_const_spec((KV_LORA, wide)), _const_spec((KV_LORA, wide)), _const_spec((LANES, wide)),
                     _const_spec((M_WIDTH, D_MODEL))]
        args += [w["wuk_cat"], w["wuv_cat"], w["p_kr"], w["wv_t"]]
        outs += [(jax.ShapeDtypeStruct((m, wide), BF16), pl.BlockSpec((tm, wide), row))] * 3
        outs += [(jax.ShapeDtypeStruct((M_WIDTH, m), BF16), pl.BlockSpec((M_WIDTH, tm), lambda i: (0, i)))]
    return pl.pallas_call(
        functools.partial(_inproj_kernel, absorbed=absorbed),
        grid=grid, in_specs=in_specs,
        out_specs=[o[1] for o in outs], out_shape=[o[0] for o in outs],
        compiler_params=_params(1), name="inproj_abs" if absorbed else "inproj",
    )(*args)


def _split3_dot(x, m01):
    hi = x.astype(BF16)
    r1 = x - hi.astype(F32)
    mid = r1.astype(BF16)
    lo = (r1 - mid.astype(F32)).astype(BF16)
    return sum(jnp.dot(part, m01, preferred_element_type=F32) for part in (hi, mid, lo))


def _cumsum_lanes(x, upper):
    return _split3_dot(x, jnp.where(upper, 1.0, 0.0).astype(BF16))


def _cummax_lanes(x):
    lane = lax.broadcasted_iota(jnp.int32, x.shape, 1)
    shift = 1
    while shift < x.shape[1]:
        x = jnp.maximum(x, jnp.where(lane >= shift, pltpu.roll(x, shift, 1), -jnp.inf))
        shift *= 2
    return x


def _mlstm_gates_kernel(g_ref, urow_ref, scal_ref, cols_ref, *, nc):
    L = M_CHUNK
    H = M_HEADS
    rows = nc * 2 * H
    r = g_ref[0].reshape(rows, L)
    ri = lax.broadcasted_iota(jnp.int32, (rows, L), 0)
    ci = lax.broadcasted_iota(jnp.int32, (rows, L), 1)
    top = (ri % (2 * H)) < H
    b_all = _cumsum_lanes(r, lax.broadcasted_iota(jnp.int32, (L, L), 0) <= lax.broadcasted_iota(jnp.int32, (L, L), 1))
    b = pltpu.roll(b_all, rows - H, 0)
    u = r - b
    cm = _cummax_lanes(u)
    g_last = jnp.max(jnp.where(ci == L - 1, b, -jnp.inf), axis=1, keepdims=True)
    wlog = g_last - b + r
    wmax = jnp.max(wlog, axis=1, keepdims=True)
    urow_ref[0] = jnp.where(top, u, pltpu.roll(wlog, H, 0)).reshape(nc, 2 * H, L)
    scal = jnp.where(top, jnp.broadcast_to(g_last, (rows, L)), pltpu.roll(jnp.broadcast_to(wmax, (rows, L)), H, 0))
    scal_ref[0] = scal.reshape(nc, 2 * H, L)
    first = jnp.where(top, b, pltpu.roll(cm, H, 0))
    for c in range(nc):
        sl = slice(2 * H * c, 2 * H * (c + 1))
        cols_ref[0, c] = jnp.transpose(jnp.concatenate([first[sl], wlog[sl]], axis=0))


def _mlstm_gates(g, *, batch, nc):
    blk = lambda b: (b, 0, 0, 0)
    return pl.pallas_call(
        functools.partial(_mlstm_gates_kernel, nc=nc),
        grid=(batch,),
        in_specs=[pl.BlockSpec((1, nc, 2 * M_HEADS, M_CHUNK), blk)],
        out_specs=[pl.BlockSpec((1, nc, 2 * M_HEADS, M_CHUNK), blk), pl.BlockSpec((1, nc, 2 * M_HEADS, M_CHUNK), blk),
                   pl.BlockSpec((1, nc, M_CHUNK, 4 * M_HEADS), blk)],
        out_shape=[jax.ShapeDtypeStruct((batch, nc, 2 * M_HEADS, M_CHUNK), F32),
                   jax.ShapeDtypeStruct((batch, nc, 2 * M_HEADS, M_CHUNK), F32),
                   jax.ShapeDtypeStruct((batch, nc, M_CHUNK, 4 * M_HEADS), F32)],
        compiler_params=_params(1), name="mlstm_gates",
    )(g)


def _mlstm_kernel(qkv_ref, vt_ref, urow_ref, scal_ref, cols_ref, sel_ref, og_ref, gm_ref, hg_ref, c_ref, n_ref, m_ref):
    L = M_CHUNK

    @pl.when(pl.program_id(1) == 0)
    def _():
        c_ref[...] = jnp.zeros_like(c_ref)
        n_ref[...] = jnp.zeros_like(n_ref)
        m_ref[...] = jnp.zeros_like(m_ref)

    row = lax.broadcasted_iota(jnp.int32, (L, L), 0)
    col = lax.broadcasted_iota(jnp.int32, (L, L), 1)
    tril = col <= row

    rows8 = urow_ref[0, 0]
    u4 = rows8[0:M_HEADS]
    wlog4 = rows8[M_HEADS:2 * M_HEADS]
    scal = scal_ref[0, 0]
    g_last4 = scal[0:M_HEADS, 0:1]
    wmax4 = scal[M_HEADS:2 * M_HEADS, 0:1]
    cols = _split3_dot(cols_ref[0, 0], sel_ref[...])
    m_prev4 = m_ref[0][:, 0:1]
    n_prev4 = n_ref[0]
    m_new4 = jnp.maximum(g_last4 + m_prev4, wmax4)
    decay4 = jnp.exp(g_last4 + m_prev4 - m_new4)
    ws4 = jnp.exp(wlog4 - m_new4)
    m_ref[0] = jnp.broadcast_to(m_new4, (M_HEADS, LANES))
    n_rows = []

    for h in range(M_HEADS):
        sl = slice(M_HEAD_DIM * h, M_HEAD_DIM * (h + 1))
        q = qkv_ref[:, sl]
        k = qkv_ref[:, M_WIDTH + M_HEAD_DIM * h:M_WIDTH + M_HEAD_DIM * (h + 1)]
        v = qkv_ref[:, 2 * M_WIDTH + M_HEAD_DIM * h:2 * M_WIDTH + M_HEAD_DIM * (h + 1)]
        m_prev = m_prev4[h:h + 1]
        c_prev = c_ref[0, h]
        n_prev = n_prev4[h:h + 1]
        b_col = cols[:, LANES * h:LANES * (h + 1)]
        mm_col = jnp.maximum(cols[:, LANES * (M_HEADS + h):LANES * (M_HEADS + h + 1)], m_prev)
        w_intra = jnp.where(tril, jnp.exp(u4[h:h + 1] - mm_col), 0.0)
        w_inter = jnp.exp(m_prev - mm_col)
        s = _dot_nt(q, k) * w_intra
        cn = jnp.concatenate([c_prev.astype(BF16), jnp.broadcast_to(n_prev.astype(BF16), (L, M_HEAD_DIM))], axis=0)
        qcn = _dot_nt(q, cn)
        num = w_inter * qcn[:, 0:M_HEAD_DIM] + _dot(s, v)
        den = w_inter * qcn[:, M_HEAD_DIM:] + jnp.sum(s, axis=1, keepdims=True)
        hs = num / jnp.maximum(jnp.abs(den), jnp.exp(-(b_col + mm_col)))

        ws = ws4[h:h + 1]
        lhs = jnp.concatenate([(vt_ref[sl, :].astype(F32) * ws).astype(BF16),
                               jnp.broadcast_to(ws.astype(BF16), (16, L))], axis=0)
        upd = jnp.dot(lhs, k, preferred_element_type=F32)
        decay = decay4[h:h + 1]
        c_ref[0, h] = decay * c_prev + upd[0:M_HEAD_DIM]
        n_rows.append(decay * n_prev + upd[M_HEAD_DIM:M_HEAD_DIM + 1])

        mu = jnp.mean(hs, axis=1, keepdims=True)
        d = hs - mu
        y = d * lax.rsqrt(jnp.mean(d * d, axis=1, keepdims=True) + EPS) * gm_ref[:, sl]
        hg_ref[:, sl] = (y * og_ref[:, sl]).astype(BF16)
    n_ref[0] = jnp.concatenate(n_rows, axis=0)


def _mlstm_prompt(qkvm, vt, gates, og, g_mhead, *, batch, seq):
    nc = seq // M_CHUNK
    tok = lambda b, c: (b * nc + c, 0)
    chunk = lambda b, c: (b, c, 0, 0)
    urow, scal, cols = _mlstm_gates(gates, batch=batch, nc=nc)
    sel = jnp.asarray(np.kron(np.eye(4 * M_HEADS, 2 * M_HEADS), np.ones((1, LANES))), BF16)
    return pl.pallas_call(
        _mlstm_kernel,
        grid=(batch, nc),
        in_specs=[
            pl.BlockSpec((M_CHUNK, 3 * M_WIDTH), tok),
            pl.BlockSpec((M_WIDTH, M_CHUNK), lambda b, c: (0, b * nc + c)),
            pl.BlockSpec((1, 1, 2 * M_HEADS, M_CHUNK), chunk),
            pl.BlockSpec((1, 1, 2 * M_HEADS, M_CHUNK), chunk),
            pl.BlockSpec((1, 1, M_CHUNK, 4 * M_HEADS), chunk),
            pl.BlockSpec((4 * M_HEADS, 2 * M_HEADS * LANES), lambda b, c: (0, 0)),
            pl.BlockSpec((M_CHUNK, M_WIDTH), tok),
            pl.BlockSpec((1, M_WIDTH), lambda b, c: (0, 0)),
        ],
        out_specs=[
            pl.BlockSpec((M_CHUNK, M_WIDTH), tok),
            pl.BlockSpec((1, M_HEADS, M_HEAD_DIM, M_HEAD_DIM), lambda b, c: (b, 0, 0, 0)),
            pl.BlockSpec((1, M_HEADS, M_HEAD_DIM), lambda b, c: (b, 0, 0)),
            pl.BlockSpec((1, M_HEADS, LANES), lambda b, c: (b, 0, 0)),
        ],
        out_shape=[
            jax.ShapeDtypeStruct((batch * seq, M_WIDTH), BF16),
            jax.ShapeDtypeStruct((batch, M_HEADS, M_HEAD_DIM, M_HEAD_DIM), F32),
            jax.ShapeDtypeStruct((batch, M_HEADS, M_HEAD_DIM), F32),
            jax.ShapeDtypeStruct((batch, M_HEADS, LANES), F32),
        ],
        compiler_params=_params(2), name="mlstm_prompt",
    )(qkvm, vt, urow, scal, cols, sel, og, g_mhead)


def _mlstm_step_kernel(qkv_ref, tail_ref, c_ref, n_ref, m_ref, og_ref, gm_ref, hg_ref, co_ref, no_ref, mo_ref, *, rows):
    D = M_HEAD_DIM
    lane = lax.broadcasted_iota(jnp.int32, (rows, LANES), 1)
    rowi = lax.broadcasted_iota(jnp.int32, (rows, D), 0)
    tail = tail_ref[...]
    m_in = m_ref[...]
    m_out = jnp.zeros((rows, LANES), F32)
    for h in range(M_HEADS):
        sl = slice(D * h, D * (h + 1))
        q = qkv_ref[:, sl]
        k = qkv_ref[:, M_WIDTH + D * h:M_WIDTH + D * (h + 1)]
        v = qkv_ref[:, 2 * M_WIDTH + D * h:2 * M_WIDTH + D * (h + 1)]
        ig = tail[:, TAIL_IG + h:TAIL_IG + h + 1]
        lf = tail[:, TAIL_LF + h:TAIL_LF + h + 1]
        m_prev = m_in[:, h:h + 1]
        n_prev = n_ref[:, h, :]
        a = lf + m_prev
        mt = jnp.maximum(a, ig)
        w_in = jnp.exp(ig - mt)
        w_st = jnp.exp(a - mt)
        s = jnp.sum(q * k, axis=1, keepdims=True) * w_in
        cq = jnp.zeros((rows, D), F32)
        for g in range(rows):
            cq = jnp.where(rowi == g, _dot_nt(q, c_ref[g, h]), cq)
        num = w_st * cq + s * v
        den = w_st * jnp.sum(n_prev * q, axis=1, keepdims=True) + s
        hs = num / jnp.maximum(jnp.abs(den), jnp.exp(-mt))
        vw_t = jnp.transpose(v * w_in)
        for g in range(rows):
            co_ref[g, h] = w_st[g:g + 1] * c_ref[g, h] + vw_t[:, g:g + 1] * k[g:g + 1, :]
        no_ref[:, h, :] = w_st * n_prev + w_in * k
        m_out = jnp.where(lane == h, mt, m_out)
        mu = jnp.mean(hs, axis=1, keepdims=True)
        d = hs - mu
        y = d * lax.rsqrt(jnp.mean(d * d, axis=1, keepdims=True) + EPS) * gm_ref[:, sl]
        hg_ref[:, sl] = (y * og_ref[:, sl]).astype(BF16)
    mo_ref[...] = m_out


def _mlstm_step(qkvm, tail, c0, n0, m0_pad, og, g_mhead, *, rows=8):
    nb = qkvm.shape[0]
    row = lambda i: (i, 0)
    return pl.pallas_call(
        functools.partial(_mlstm_step_kernel, rows=rows),
        grid=(nb // rows,),
        in_specs=[
            pl.BlockSpec((rows, 3 * M_WIDTH), row),
            pl.BlockSpec((rows, LANES), row),
            pl.BlockSpec((rows, M_HEADS, M_HEAD_DIM, M_HEAD_DIM), lambda i: (i, 0, 0, 0)),
            pl.BlockSpec((rows, M_HEADS, M_HEAD_DIM), lambda i: (i, 0, 0)),
            pl.BlockSpec((rows, LANES), row),
            pl.BlockSpec((rows, M_WIDTH), row),
            pl.BlockSpec((1, M_WIDTH), lambda i: (0, 0)),
        ],
        out_specs=[
            pl.BlockSpec((rows, M_WIDTH), row),
            pl.BlockSpec((rows, M_HEADS, M_HEAD_DIM, M_HEAD_DIM), lambda i: (i, 0, 0, 0)),
            pl.BlockSpec((rows, M_HEADS, M_HEAD_DIM), lambda i: (i, 0, 0)),
            pl.BlockSpec((rows, LANES), row),
        ],
        out_shape=[
            jax.ShapeDtypeStruct((nb, M_WIDTH), BF16),
            jax.ShapeDtypeStruct((nb, M_HEADS, M_HEAD_DIM, M_HEAD_DIM), F32),
            jax.ShapeDtypeStruct((nb, M_HEADS, M_HEAD_DIM), F32),
            jax.ShapeDtypeStruct((nb, LANES), F32),
        ],
        compiler_params=_params(1), name="mlstm_step",
    )(qkvm, tail, c0, n0, m0_pad, og, g_mhead)


def _mla_prefill_kernel(q_ref, k_ref, v_ref, o_ref, *, blk):
    i = pl.program_id(2)
    q = q_ref[...]
    row = lax.broadcasted_iota(jnp.int32, (blk, blk), 0)
    col = lax.broadcasted_iota(jnp.int32, (blk, blk), 1)

    def step(j, carry, masked):
        m, l, acc = carry
        start = pl.multiple_of(j * blk, blk)
        ks = k_ref[pl.ds(start, blk), :]
        vs = v_ref[pl.ds(start, blk), :]
        s = _dot_nt(q, ks)
        if masked:
            s = jnp.where(col <= row, s, -jnp.inf)
        m_new = jnp.maximum(m, jnp.max(s, axis=1, keepdims=True))
        p = jnp.exp(s - m_new)
        alpha = jnp.exp(m - m_new)
        l = alpha * l + jnp.sum(p, axis=1, keepdims=True)
        acc = alpha * acc + jnp.dot(p.astype(BF16), vs, preferred_element_type=F32)
        return m_new, l, acc

    carry = (jnp.full((blk, 1), -jnp.inf, F32), jnp.zeros((blk, 1), F32), jnp.zeros((blk, HEAD_SLOT), F32))
    carry = lax.fori_loop(0, i, lambda j, c: step(j, c, False), carry)
    _, l, acc = step(i, carry, True)
    o_ref[...] = (acc / l).astype(BF16)


def _mla_prefill(qcat, kcat, vcat, *, batch, seq, blk):
    nq = seq // blk
    return pl.pallas_call(
        functools.partial(_mla_prefill_kernel, blk=blk),
        grid=(batch, A_HEADS, nq),
        in_specs=[
            pl.BlockSpec((blk, HEAD_SLOT), lambda b, h, i: (b * nq + i, h)),
            pl.BlockSpec((seq, HEAD_SLOT), lambda b, h, i: (b, h)),
            pl.BlockSpec((seq, HEAD_SLOT), lambda b, h, i: (b, h)),
        ],
        out_specs=pl.BlockSpec((blk, HEAD_SLOT), lambda b, h, i: (b * nq + i, h)),
        out_shape=jax.ShapeDtypeStruct((batch * seq, A_HEADS * HEAD_SLOT), BF16),
        compiler_params=_params(3), name="mla_prefill",
    )(qcat, kcat, vcat)


def _mla_decode_kernel(pt_ref, q_ref, ckv_hbm, kr_hbm, cself_ref, tself_ref, o_ref,
                       cbuf, rbuf, kbuf, sems, *, n_chunks, chunk_pages):
    b = pl.program_id(0)
    nb = pl.num_programs(0)

    def page_copies(bb, c, slot, p):
        pg = pt_ref[bb, c * chunk_pages + p]
        dst = pl.ds(p * PAGE_SIZE, PAGE_SIZE)
        return (pltpu.make_async_copy(ckv_hbm.at[pg], cbuf.at[slot, dst], sems.at[0, slot]),
                pltpu.make_async_copy(kr_hbm.at[pg], rbuf.at[slot, :, dst], sems.at[1, slot]))

    def issue(bb, c, slot):
        for p in range(chunk_pages):
            for cp in page_copies(bb, c, slot, p):
                cp.start()

    def wait(slot):
        for p in range(chunk_pages):
            for cp in page_copies(0, 0, slot, p):
                cp.wait()

    @pl.when(b == 0)
    def _():
        issue(0, 0, 0)

    q = q_ref[0]
    q_lat = q[:, 0:KV_LORA]
    q_rope = q[:, KV_LORA:KV_LORA + QK_ROPE]

    def scores(slot):
        kc = cbuf[slot].astype(BF16)
        kbuf[slot] = kc
        kr = rbuf[slot].astype(BF16)
        return _dot_nt(q_lat, kc) + jnp.dot(q_rope, kr, preferred_element_type=F32)

    def accumulate(carry, s, slot):
        m_old, l, acc = carry
        m_new = jnp.maximum(m_old, jnp.max(s, axis=1, keepdims=True))
        p = jnp.exp(s - m_new)
        alpha = jnp.exp(m_old - m_new)
        return (m_new, alpha * l + jnp.sum(p, axis=1, keepdims=True),
                alpha * acc + jnp.dot(p.astype(BF16), kbuf[slot], preferred_element_type=F32))

    carry = (jnp.full((A_HEADS, 1), -jnp.inf, F32), jnp.zeros((A_HEADS, 1), F32), jnp.zeros((A_HEADS, KV_LORA), F32))
    s_prev = None
    for c in range(n_chunks):
        slot = c % 2
        if c + 1 < n_chunks:
            issue(b, c + 1, 1 - slot)
        else:
            @pl.when(b + 1 < nb)
            def _():
                issue(b + 1, 0, 1 - slot)
        wait(slot)
        s_cur = scores(slot)
        if c > 0:
            carry = accumulate(carry, s_prev, 1 - slot)
        s_prev = s_cur
    m_old, l, acc = accumulate(carry, s_prev, (n_chunks - 1) % 2)

    c_self = cself_ref[0].astype(BF16).astype(F32)
    r_self = tself_ref[0][:, 0:QK_ROPE].astype(BF16).astype(F32)
    s_self = (jnp.sum(q_lat.astype(F32) * c_self, axis=1, keepdims=True)
              + jnp.sum(q_rope.astype(F32) * r_self, axis=1, keepdims=True))
    m_new = jnp.maximum(m_old, s_self)
    p_self = jnp.exp(s_self - m_new)
    alpha = jnp.exp(m_old - m_new)
    l = alpha * l + p_self
    acc = alpha * acc + p_self.astype(BF16).astype(F32) * c_self
    o_ref[0] = acc / l


def _mla_decode(page_table, qabs, cache_ckv, cache_krope, ckv_self, tail_self, *, chunk_pages=32):
    nb, n_pages = page_table.shape
    n_chunks = n_pages // chunk_pages
    keys = chunk_pages * PAGE_SIZE
    grid_spec = pltpu.PrefetchScalarGridSpec(
        num_scalar_prefetch=1,
        grid=(nb,),
        in_specs=[
            pl.BlockSpec((1, A_HEADS, ABS_SLOT), lambda b, pt: (b, 0, 0)),
            pl.BlockSpec(memory_space=pl.ANY),
            pl.BlockSpec(memory_space=pl.ANY),
            pl.BlockSpec((1, 1, KV_LORA), lambda b, pt: (b, 0, 0)),
            pl.BlockSpec((1, 1, LANES), lambda b, pt: (b, 0, 0)),
        ],
        out_specs=pl.BlockSpec((1, A_HEADS, KV_LORA), lambda b, pt: (b, 0, 0)),
        scratch_shapes=[
            pltpu.VMEM((2, keys, KV_LORA), F32),
            pltpu.VMEM((2, QK_ROPE, keys), F32),
            pltpu.VMEM((2, keys, KV_LORA), BF16),
            pltpu.SemaphoreType.DMA((2, 2)),
        ],
    )
    return pl.pallas_call(
        functools.partial(_mla_decode_kernel, n_chunks=n_chunks, chunk_pages=chunk_pages),
        grid_spec=grid_spec,
        out_shape=jax.ShapeDtypeStruct((nb, A_HEADS, KV_LORA), F32),
        compiler_params=_params(1), name="mla_decode",
    )(page_table, qabs.reshape(nb, A_HEADS, ABS_SLOT), cache_ckv, cache_krope,
      ckv_self.reshape(nb, 1, KV_LORA), tail_self.reshape(nb, 1, LANES))


def _merge_kernel(x_ref, hg_ref, ob_ref, sg_ref, wa_ref, wb_ref, wout_ref, gpost_ref, gmem_ref, wmq_ref,
                  *rest, from_latent):
    if from_latent:
        wuv_ref, x1_ref, qm_ref = rest
        ob = jnp.concatenate(
            [_dot(ob_ref[:, KV_LORA * hh:KV_LORA * (hh + 1)], wuv_ref[hh]).astype(BF16) for hh in range(A_HEADS)],
            axis=1)
    else:
        x1_ref, qm_ref = rest
        ob = ob_ref[...]
    ya = jnp.dot(hg_ref[...], wa_ref[...], preferred_element_type=F32)
    yb = jnp.dot(ob, wb_ref[...], preferred_element_type=F32)
    mix = sg_ref[:, 0:D_MODEL] * ya + sg_ref[:, D_MODEL:] * yb
    y = _dot(mix, wout_ref[...])
    x1 = x_ref[...] + _rms(y, gpost_ref[...])
    x1_ref[...] = x1
    qm = _dot(_rms(x1, gmem_ref[...]), wmq_ref[...]) * (X_HEAD_DIM ** -0.5)
    qm_ref[...] = qm.astype(qm_ref.dtype)


def _merge(x, hg, ob, sg, w, *, tm, from_latent, qdtype):
    m = x.shape[0]
    row = lambda i: (i, 0)
    wide = A_HEADS * HEAD_SLOT
    ob_w = ob.shape[1]
    in_specs = [
        pl.BlockSpec((tm, D_MODEL), row), pl.BlockSpec((tm, M_WIDTH), row), pl.BlockSpec((tm, ob_w), row),
        pl.BlockSpec((tm, 2 * D_MODEL), row),
        _const_spec((M_WIDTH, D_MODEL)), _const_spec((wide, D_MODEL)), _const_spec((D_MODEL, D_MODEL)),
        _const_spec((1, D_MODEL)), _const_spec((1, D_MODEL)), _const_spec((D_MODEL, X_WIDTH)),
    ]
    args = [x, hg, ob, sg, w["w_branch_a"], w["wb_cat"], w["w_out"], w["g_post_mix"], w["g_pre_mem"], w["w_mq"]]
    if from_latent:
        in_specs += [_const_spec((A_HEADS, KV_LORA, HEAD_SLOT))]
        args += [w["wuv_heads"]]
    return pl.pallas_call(
        functools.partial(_merge_kernel, from_latent=from_latent),
        grid=(m // tm,), in_specs=in_specs,
        out_specs=[pl.BlockSpec((tm, D_MODEL), row), pl.BlockSpec((tm, X_WIDTH), row)],
        out_shape=[jax.ShapeDtypeStruct((m, D_MODEL), F32), jax.ShapeDtypeStruct((m, X_WIDTH), qdtype)],
        compiler_params=_params(1), name="merge_lat" if from_latent else "merge",
    )(*args)


def _memkv_kernel(mem_ref, g_ref, wk_ref, wv_ref, k_ref, v_ref):
    mn = _rms(mem_ref[...], g_ref[...]).astype(BF16)
    k_ref[...] = jnp.dot(mn, wk_ref[...], preferred_element_type=F32)
    v_ref[...] = jnp.dot(mn, wv_ref[...], preferred_element_type=F32)


def _memkv(mem, w, *, tm):
    m = mem.shape[0]
    row = lambda i: (i, 0)
    return pl.pallas_call(
        _memkv_kernel, grid=(m // tm,),
        in_specs=[pl.BlockSpec((tm, D_MODEL), row), _const_spec((1, D_MODEL)),
                  _const_spec((D_MODEL, X_WIDTH)), _const_spec((D_MODEL, X_WIDTH))],
        out_specs=[pl.BlockSpec((tm, X_WIDTH), row)] * 2,
        out_shape=[jax.ShapeDtypeStruct((m, X_WIDTH), F32)] * 2,
        compiler_params=_params(1), name="memkv",
    )(mem, w["g_mem"], w["w_mk"], w["w_mv"])


def _memattn_kernel(q_ref, k_ref, v_ref, o_ref):
    kb = k_ref[...].astype(BF16)
    vb = v_ref[...].astype(BF16)
    for h in range(X_HEADS):
        sl = slice(X_HEAD_DIM * h, X_HEAD_DIM * (h + 1))
        s = _dot_nt(q_ref[:, sl], kb[:, sl])
        p = jnp.exp(s - jnp.max(s, axis=1, keepdims=True))
        l = jnp.sum(p, axis=1, keepdims=True)
        o_ref[:, sl] = (jnp.dot(p.astype(BF16), vb[:, sl], preferred_element_type=F32) / l).astype(BF16)


def _memattn(qm, mem_k, mem_v, *, batch, seq, tm):
    nt = seq // tm
    tok = lambda b, i: (b * nt + i, 0)
    kv = lambda b, i: (b, 0)
    return pl.pallas_call(
        _memattn_kernel, grid=(batch, nt),
        in_specs=[pl.BlockSpec((tm, X_WIDTH), tok), pl.BlockSpec((N_MEM, X_WIDTH), kv),
                  pl.BlockSpec((N_MEM, X_WIDTH), kv)],
        out_specs=pl.BlockSpec((tm, X_WIDTH), tok),
        out_shape=jax.ShapeDtypeStruct((batch * seq, X_WIDTH), BF16),
        compiler_params=_params(2), name="memattn",
    )(qm, mem_k, mem_v)


def _memattn_step_kernel(q_ref, k_ref, v_ref, o_ref, *, rows):
    for g in range(rows):
        q4 = jnp.concatenate([q_ref[g:g + 1, X_HEAD_DIM * h:X_HEAD_DIM * (h + 1)] for h in range(X_HEADS)], axis=0)
        s = jnp.sum(k_ref[g] * q4[None], axis=2, keepdims=True)
        p = jnp.exp(s - jnp.max(s, axis=0, keepdims=True))
        l = jnp.sum(p, axis=0)
        o_ref[g] = jnp.sum(p * v_ref[g], axis=0) / l


def _memattn_step(qm, mem_k, mem_v, *, rows=8):
    nb = qm.shape[0]
    row = lambda i: (i, 0)
    kv = lambda i: (i, 0, 0, 0)
    kv_block = (rows, N_MEM, X_HEADS, X_HEAD_DIM)
    return pl.pallas_call(
        functools.partial(_memattn_step_kernel, rows=rows), grid=(nb // rows,),
        in_specs=[pl.BlockSpec((rows, X_WIDTH), row), pl.BlockSpec(kv_block, kv), pl.BlockSpec(kv_block, kv)],
        out_specs=pl.BlockSpec((rows, X_HEADS, X_HEAD_DIM), lambda i: (i, 0, 0)),
        out_shape=jax.ShapeDtypeStruct((nb, X_HEADS, X_HEAD_DIM), F32),
        compiler_params=_params(1), name="memattn_step",
    )(qm, mem_k, mem_v).reshape(nb, X_WIDTH)


def _ffn_kernel(x1_ref, o_ref, wmo_ref, gpm_ref, gpf_ref, wg_ref, wu_ref, wdn_ref, gpost_ref, y_ref, *, f_chunk):
    x2 = x1_ref[...] + _rms(_dot(o_ref[...], wmo_ref[...]), gpm_ref[...])
    h = _rms(x2, gpf_ref[...]).astype(BF16)
    acc = jnp.zeros(x2.shape, F32)
    for c in range(D_FF // f_chunk):
        sl = slice(f_chunk * c, f_chunk * (c + 1))
        g = jnp.dot(h, wg_ref[:, sl], preferred_element_type=F32)
        u = jnp.dot(h, wu_ref[:, sl], preferred_element_type=F32)
        acc = acc + jnp.dot((g * jax.nn.sigmoid(g) * u).astype(BF16), wdn_ref[sl, :], preferred_element_type=F32)
    y_ref[...] = x2 + _rms(acc, gpost_ref[...])


def _ffn(x1, o, w, *, tm, f_chunk=1408):
    m = x1.shape[0]
    row = lambda i: (i, 0)
    return pl.pallas_call(
        functools.partial(_ffn_kernel, f_chunk=f_chunk), grid=(m // tm,),
        in_specs=[pl.BlockSpec((tm, D_MODEL), row), pl.BlockSpec((tm, X_WIDTH), row),
                  _const_spec((X_WIDTH, D_MODEL)), _const_spec((1, D_MODEL)), _const_spec((1, D_MODEL)),
                  _const_spec((D_MODEL, D_FF)), _const_spec((D_MODEL, D_FF)), _const_spec((D_FF, D_MODEL)),
                  _const_spec((1, D_MODEL))],
        out_specs=pl.BlockSpec((tm, D_MODEL), row),
        out_shape=jax.ShapeDtypeStruct((m, D_MODEL), F32),
        compiler_params=_params(1), name="ffn",
    )(x1, o, w["w_mo"], w["g_post_mem"], w["g_pre_ffn"], w["w_ffn_gate"], w["w_ffn_up"], w["w_ffn_out"],
      w["g_post_ffn"])


def _prep_weights(p):
    w = {}
    for name in ("g_pre_mix", "g_q_lora", "g_kv_lora", "g_mhead", "g_post_mix", "g_pre_mem", "g_mem", "g_post_mem",
                 "g_pre_ffn", "g_post_ffn"):
        w[name] = p[name].reshape(1, -1).astype(F32)
    w_in = p["w_in"]
    offs = np.cumsum((0, M_WIDTH, M_WIDTH, M_WIDTH, M_HEADS, M_HEADS, M_WIDTH, Q_LORA, KV_LORA, QK_ROPE,
                      D_MODEL, D_MODEL))
    seg = lambda i: w_in[:, offs[i]:offs[i + 1]]
    mq, mk, mv, mi, mf, mo, cq, ckv, kr, ga, gb = (seg(i) for i in range(11))
    half = QK_ROPE // 2
    swap = lambda a: jnp.concatenate([a[..., half:], a[..., :half]], axis=-1)
    tail = jnp.concatenate([kr, swap(kr), mi, mf,
                            jnp.zeros((D_MODEL, LANES - 2 * QK_ROPE - 2 * M_HEADS), w_in.dtype)], axis=1)
    w["w_big"] = jnp.concatenate([mq, mk, mv, mo, cq, ckv, ga, gb, tail], axis=1).astype(BF16)
    w["wv_t"] = mv.T.astype(BF16)
    w["b_tail"] = jnp.zeros((1, LANES), F32).at[0, TAIL_IG:TAIL_IG + 2 * M_HEADS].set(p["b_if"].astype(F32))

    pad = HEAD_SLOT - QK_NOPE - QK_ROPE
    wq = p["w_uq"].reshape(Q_LORA, A_HEADS, QK_NOPE + QK_ROPE)
    zq = jnp.zeros((Q_LORA, A_HEADS, pad), wq.dtype)
    w["wq_cat"] = jnp.concatenate([wq, zq], axis=2).reshape(Q_LORA, A_HEADS * HEAD_SLOT).astype(BF16)
    w["wq_sw"] = jnp.concatenate([jnp.zeros_like(wq[..., :QK_NOPE]), swap(wq[..., QK_NOPE:]), zq],
                                 axis=2).reshape(Q_LORA, A_HEADS * HEAD_SLOT).astype(BF16)
    w_uk, w_uv = p["w_uk"], p["w_uv"]
    zk = jnp.zeros((KV_LORA, A_HEADS, HEAD_SLOT - QK_NOPE), w_uk.dtype)
    w["wuk_cat"] = jnp.concatenate([w_uk, zk], axis=2).reshape(KV_LORA, A_HEADS * HEAD_SLOT).astype(BF16)
    w["wuv_cat"] = jnp.concatenate([w_uv, zk], axis=2).reshape(KV_LORA, A_HEADS * HEAD_SLOT).astype(BF16)
    w["wuv_heads"] = jnp.concatenate([w_uv, zk], axis=2).transpose(1, 0, 2).astype(BF16)
    pk = np.zeros((LANES, A_HEADS, HEAD_SLOT), np.float32)
    for r in range(QK_ROPE):
        pk[r, :, QK_NOPE + r] = 1.0
    w["p_kr"] = jnp.asarray(pk.reshape(LANES, A_HEADS * HEAD_SLOT), BF16)
    wabs = jnp.zeros((A_HEADS, HEAD_SLOT, ABS_SLOT), F32)
    wabs = wabs.at[:, :QK_NOPE, :KV_LORA].set(w_uk.transpose(1, 2, 0).astype(F32))
    wabs = wabs.at[:, QK_NOPE + np.arange(QK_ROPE), KV_LORA + np.arange(QK_ROPE)].set(1.0)
    w["w_abs"] = wabs.astype(BF16)
    wb = p["w_branch_b"].reshape(A_HEADS, V_HEAD, D_MODEL)
    w["wb_cat"] = jnp.concatenate([wb, jnp.zeros((A_HEADS, HEAD_SLOT - V_HEAD, D_MODEL), wb.dtype)],
                                  axis=1).reshape(A_HEADS * HEAD_SLOT, D_MODEL).astype(BF16)
    for name in ("w_branch_a", "w_out", "w_mq", "w_mk", "w_mv", "w_mo", "w_ffn_out"):
        w[name] = p[name].astype(BF16)
    w["w_ffn_gate"] = p["w_ffn_in"][:, :D_FF].astype(BF16)
    w["w_ffn_up"] = p["w_ffn_in"][:, D_FF:].astype(BF16)
    return w


def _rope_tables(pos0, n):
    pos = (pos0 + jnp.arange(n)).astype(F32)
    inv = ROPE_BASE ** (-jnp.arange(0, QK_ROPE, 2, dtype=F32) / QK_ROPE)
    ang = pos[:, None] * inv[None, :]
    cos, sin = jnp.cos(ang), jnp.sin(ang)
    c32 = jnp.concatenate([cos, cos], axis=1)
    s32 = jnp.concatenate([-sin, sin], axis=1)
    one = lambda k: jnp.ones((n, k), F32)
    zero = lambda k: jnp.zeros((n, k), F32)
    ct = jnp.concatenate([c32, one(LANES - QK_ROPE)], axis=1)
    st = jnp.concatenate([s32, zero(LANES - QK_ROPE)], axis=1)
    c128 = jnp.concatenate([one(QK_NOPE), c32, one(HEAD_SLOT - QK_NOPE - QK_ROPE)], axis=1)
    s128 = jnp.concatenate([zero(QK_NOPE), s32, zero(HEAD_SLOT - QK_NOPE - QK_ROPE)], axis=1)
    return ct, st, c128, s128


def kernel(x_prompt, x_sample, cache_ckv, cache_krope, cache_mem_k, cache_mem_v, state_C, state_n, state_m, page_table, mem_prompt, g_pre_mix, w_in, b_if, g_mhead, g_q_lora, w_uq, g_kv_lora, w_uk, w_uv, w_branch_a, w_branch_b, w_out, g_post_mix, g_pre_mem, g_mem, w_mq, w_mk, w_mv, w_mo, g_post_mem, g_pre_ffn, w_ffn_in, w_ffn_out, g_post_ffn):
    params = dict(g_pre_mix=g_pre_mix, w_in=w_in, b_if=b_if, g_mhead=g_mhead, g_q_lora=g_q_lora, w_uq=w_uq,
                  g_kv_lora=g_kv_lora, w_uk=w_uk, w_uv=w_uv, w_branch_a=w_branch_a, w_branch_b=w_branch_b,
                  w_out=w_out, g_post_mix=g_post_mix, g_pre_mem=g_pre_mem, g_mem=g_mem, w_mq=w_mq, w_mk=w_mk,
                  w_mv=w_mv, w_mo=w_mo, g_post_mem=g_post_mem, g_pre_ffn=g_pre_ffn, w_ffn_in=w_ffn_in,
                  w_ffn_out=w_ffn_out, g_post_ffn=g_post_ffn)
    depth = w_in.shape[0]
    assert depth == 1, "single-layer stack only"
    bp, seq, _ = x_prompt.shape
    bs, dec_seq, _ = x_sample.shape
    assert dec_seq == 1, "one new token per sample sequence"
    past_len = page_table.shape[1] * PAGE_SIZE
    w = _prep_weights({name: a[0] for name, a in params.items()})

    xp = x_prompt.reshape(bp * seq, D_MODEL)
    qkvm, og, tail, ckv_p, sg, qcat, kcat, vcat, vt = _inproj(
        xp, w, _rope_tables(0, seq), tm=256, absorbed=False, mdtype=BF16)
    gates = tail.reshape(bp, seq // M_CHUNK, M_CHUNK, LANES)[:, :, :, TAIL_IG:TAIL_IG + 2 * M_HEADS].transpose(0, 1, 3, 2)
    hg, c_p, n_p, m_p = _mlstm_prompt(qkvm, vt, gates, og, w["g_mhead"], batch=bp, seq=seq)
    ob = _mla_prefill(qcat, kcat, vcat, batch=bp, seq=seq, blk=512)
    x1, qm = _merge(xp, hg, ob, sg, w, tm=512, from_latent=False, qdtype=BF16)
    mem_k, mem_v = _memkv(mem_prompt.reshape(bp * N_MEM, D_MODEL), w, tm=512)
    om = _memattn(qm, mem_k, mem_v, batch=bp, seq=seq, tm=512)
    y_prompt = _ffn(x1, om, w, tm=512).reshape(bp, seq, D_MODEL)

    xs = x_sample.reshape(bs, D_MODEL)
    tables_s = tuple(jnp.broadcast_to(t, (bs, LANES)) for t in _rope_tables(past_len, 1))
    qkvm_s, og_s, tail_s, ckv_s, sg_s, qabs = _inproj(xs, w, tables_s, tm=bs, absorbed=True, mdtype=F32)
    m0_pad = jnp.pad(state_m.reshape(bs, M_HEADS).astype(F32), ((0, 0), (0, LANES - M_HEADS)))
    hg_s, c_s, n_s, m_s = _mlstm_step(qkvm_s, tail_s, state_C.reshape(bs, M_HEADS, M_HEAD_DIM, M_HEAD_DIM),
                                      state_n.reshape(bs, M_HEADS, M_HEAD_DIM), m0_pad, og_s, w["g_mhead"])
    n_phys = cache_ckv.shape[1]
    krope_t = jnp.transpose(cache_krope.reshape(n_phys, PAGE_SIZE, QK_ROPE), (0, 2, 1))
    o_lat = _mla_decode(page_table, qabs, cache_ckv.reshape(n_phys, PAGE_SIZE, KV_LORA), krope_t, ckv_s, tail_s)
    x1_s, qm_s = _merge(xs, hg_s, o_lat.reshape(bs, A_HEADS * KV_LORA), sg_s, w, tm=bs, from_latent=True, qdtype=F32)
    om_s = _memattn_step(qm_s, cache_mem_k.reshape(bs, N_MEM, X_HEADS, X_HEAD_DIM),
                         cache_mem_v.reshape(bs, N_MEM, X_HEADS, X_HEAD_DIM))
    y_sample = _ffn(x1_s, om_s, w, tm=bs).reshape(bs, 1, D_MODEL)

    return (y_prompt, y_sample,
            ckv_p.reshape(1, bp, seq, KV_LORA), tail[:, :QK_ROPE].reshape(1, bp, seq, QK_ROPE),
            c_p.reshape(1, bp, M_HEADS, M_HEAD_DIM, M_HEAD_DIM), n_p.reshape(1, bp, M_HEADS, M_HEAD_DIM),
            m_p[:, :, 0].reshape(1, bp, M_HEADS),
            mem_k.reshape(1, bp, N_MEM, X_HEADS, X_HEAD_DIM), mem_v.reshape(1, bp, N_MEM, X_HEADS, X_HEAD_DIM),
            ckv_s.reshape(1, bs, 1, KV_LORA), tail_s[:, :QK_ROPE].reshape(1, bs, 1, QK_ROPE),
            c_s.reshape(1, bs, M_HEADS, M_HEAD_DIM, M_HEAD_DIM), n_s.reshape(1, bs, M_HEADS, M_HEAD_DIM),
            m_s[:, :M_HEADS].reshape(1, bs, M_HEADS))
```

```python
import functools

import jax
import jax.numpy as jnp
import numpy as np
from jax import lax
from jax.experimental import pallas as pl
from jax.experimental.pallas import tpu as pltpu

F32 = jnp.float32
BF16 = jnp.bfloat16

D_MODEL = 1024
PAGE_SIZE = 128
M_HEADS = 4
M_HEAD_DIM = 128
M_WIDTH = M_HEADS * M_HEAD_DIM
M_CHUNK = 128
A_HEADS = 8
QK_NOPE = 64
QK_ROPE = 32
V_HEAD = 64
Q_LORA = 384
KV_LORA = 256
ROPE_BASE = 10000.0
N_MEM = 256
X_HEADS = 4
X_HEAD_DIM = 128
X_WIDTH = X_HEADS * X_HEAD_DIM
D_FF = 2816
EPS = 1e-6

LANES = 128
HEAD_SLOT = 128
ABS_SLOT = 384
ATT_SCALE = (QK_NOPE + QK_ROPE) ** -0.5
VMEM_LIMIT = 52 * 1024 * 1024

TAIL_IG = 64
TAIL_LF = 68


def _rms(x, g):
    return x * lax.rsqrt(jnp.mean(x * x, axis=-1, keepdims=True) + EPS) * g


def _dot(a, b):
    return jnp.dot(a.astype(BF16), b.astype(BF16), preferred_element_type=F32)


def _dot_nt(a, b):
    return lax.dot_general(a.astype(BF16), b.astype(BF16), (((1,), (1,)), ((), ())),
                           preferred_element_type=F32)


def _dot_tn(a, b):
    return lax.dot_general(a.astype(BF16), b.astype(BF16), (((0,), (0,)), ((), ())),
                           preferred_element_type=F32)


def _log_sigmoid(x):
    return jnp.minimum(x, 0.0) - jnp.log1p(jnp.exp(-jnp.abs(x)))


def _const_spec(shape):
    nd = len(shape)
    return pl.BlockSpec(shape, lambda *_: (0,) * nd, pipeline_mode=pl.Buffered(1))


def _params(n_axes):
    return pltpu.CompilerParams(dimension_semantics=("arbitrary",) * n_axes, vmem_limit_bytes=VMEM_LIMIT)


def _inproj_kernel(x_ref, gpre_ref, wqkv_ref, wmo_ref, wlat_ref, wg_ref, wtail_ref, btail_ref, ct_ref, st_ref,
                   gq_ref, wq_ref, wqs_ref, c128_ref, s128_ref, gkv_ref, *rest, absorbed):
    if absorbed:
        wabs_ref, qkvm_ref, og_ref, ckv_ref, sg_ref, tail_ref, qabs_ref = rest
    else:
        (wuk_ref, wuv_ref, pk_ref, wvt_ref,
         qkvm_ref, og_ref, ckv_ref, sg_ref, krt_ref, gt_ref, qcat_ref, kcat_ref, vcat_ref, vt_ref) = rest
    h = _rms(x_ref[...], gpre_ref[...]).astype(BF16)

    zm = jnp.dot(h, wqkv_ref[...], preferred_element_type=F32)
    qkvm_ref[:, 0:M_WIDTH] = zm[:, 0:M_WIDTH].astype(qkvm_ref.dtype)
    qkvm_ref[:, M_WIDTH:2 * M_WIDTH] = (zm[:, M_WIDTH:2 * M_WIDTH] * (M_HEAD_DIM ** -0.5)).astype(qkvm_ref.dtype)
    qkvm_ref[:, 2 * M_WIDTH:] = zm[:, 2 * M_WIDTH:].astype(qkvm_ref.dtype)
    og_ref[...] = jax.nn.sigmoid(jnp.dot(h, wmo_ref[...], preferred_element_type=F32))
    sg_ref[...] = jax.nn.sigmoid(jnp.dot(h, wg_ref[...], preferred_element_type=F32))

    t = jnp.dot(h, wtail_ref[...], preferred_element_type=F32) + btail_ref[...]
    t = t * ct_ref[...] + pltpu.roll(t, LANES - QK_ROPE, 1) * st_ref[...]
    lane = lax.broadcasted_iota(jnp.int32, t.shape, 1)
    is_lf = jnp.logical_and(lane >= TAIL_LF, lane < TAIL_LF + M_HEADS)
    t = jnp.where(is_lf, _log_sigmoid(t), t)
    if absorbed:
        tail_ref[...] = t
    else:
        tt = jnp.transpose(t)
        krt_ref[0] = tt[0:QK_ROPE]
        gt_ref[...] = tt[TAIL_IG:TAIL_IG + 2 * M_HEADS]

    cqn = _rms(jnp.dot(h, wlat_ref[:, 0:Q_LORA], preferred_element_type=F32), gq_ref[...]).astype(BF16)
    ckvn = _rms(jnp.dot(h, wlat_ref[:, Q_LORA:], preferred_element_type=F32), gkv_ref[...])
    ckv_ref[...] = ckvn

    qc = jnp.dot(cqn, wq_ref[...], preferred_element_type=F32)
    qs = jnp.dot(cqn, wqs_ref[...], preferred_element_type=F32)
    c128 = c128_ref[...]
    s128 = s128_ref[...]
    for hh in range(A_HEADS):
        sl = slice(HEAD_SLOT * hh, HEAD_SLOT * (hh + 1))
        qh = ((qc[:, sl] * c128 + qs[:, sl] * s128) * ATT_SCALE).astype(BF16)
        if absorbed:
            qabs_ref[:, ABS_SLOT * hh:ABS_SLOT * (hh + 1)] = jnp.dot(
                qh, wabs_ref[hh], preferred_element_type=F32).astype(BF16)
        else:
            qcat_ref[:, sl] = qh
    if not absorbed:
        cb = ckvn.astype(BF16)
        kcat_ref[...] = (jnp.dot(cb, wuk_ref[...], preferred_element_type=F32)
                         + jnp.dot(t.astype(BF16), pk_ref[...], preferred_element_type=F32)).astype(BF16)
        vlane = lax.broadcasted_iota(jnp.int32, (1, A_HEADS * HEAD_SLOT), 1) % HEAD_SLOT
        vcat = jnp.dot(cb, wuv_ref[...], preferred_element_type=F32)
        vcat_ref[...] = jnp.where(vlane == V_HEAD, 1.0, vcat).astype(BF16)
        vt_ref[...] = _dot_nt(wvt_ref[...], h).astype(BF16)


def _inproj(x, w, tables, *, tm, absorbed, mdtype):
    m = x.shape[0]
    ct, st, c128, s128 = tables
    seq = ct.shape[0]
    nt = seq // tm
    grid = (m // tm,)
    row = lambda i: (i, 0)
    tab = lambda i: (i % nt, 0)
    wide = A_HEADS * HEAD_SLOT
    in_specs = [
        pl.BlockSpec((tm, D_MODEL), row), _const_spec((1, D_MODEL)),
        _const_spec((D_MODEL, 3 * M_WIDTH)), _const_spec((D_MODEL, M_WIDTH)), _const_spec((D_MODEL, Q_LORA + KV_LORA)),
        _const_spec((D_MODEL, 2 * D_MODEL)), _const_spec((D_MODEL, LANES)), _const_spec((1, LANES)),
        pl.BlockSpec((tm, LANES), tab), pl.BlockSpec((tm, LANES), tab),
        _const_spec((1, Q_LORA)), _const_spec((Q_LORA, wide)), _const_spec((Q_LORA, wide)),
        pl.BlockSpec((tm, LANES), tab), pl.BlockSpec((tm, LANES), tab),
        _const_spec((1, KV_LORA)),
    ]
    args = [x, w["g_pre_mix"], w["w_qkv"], w["w_mo_gate"], w["w_lat"], w["w_g"], w["w_tail"], w["b_tail"], ct, st,
            w["g_q_lora"], w["wq_cat"], w["wq_sw"], c128, s128, w["g_kv_lora"]]
    outs = [
        (jax.ShapeDtypeStruct((m, 3 * M_WIDTH), mdtype), pl.BlockSpec((tm, 3 * M_WIDTH), row)),
        (jax.ShapeDtypeStruct((m, M_WIDTH), F32), pl.BlockSpec((tm, M_WIDTH), row)),
        (jax.ShapeDtypeStruct((m, KV_LORA), F32), pl.BlockSpec((tm, KV_LORA), row)),
        (jax.ShapeDtypeStruct((m, 2 * D_MODEL), F32), pl.BlockSpec((tm, 2 * D_MODEL), row)),
    ]
    if absorbed:
        in_specs += [_const_spec((A_HEADS, HEAD_SLOT, ABS_SLOT))]
        args += [w["w_abs"]]
        outs += [(jax.ShapeDtypeStruct((m, LANES), F32), pl.BlockSpec((tm, LANES), row)),
                 (jax.ShapeDtypeStruct((m, A_HEADS * ABS_SLOT), BF16), pl.BlockSpec((tm, A_HEADS * ABS_SLOT), row))]
    else:
        in_specs += [_const_spec((KV_LORA, wide)), _const_spec((KV_LORA, wide)), _const_spec((LANES, wide)),
                     _const_spec((M_WIDTH, D_MODEL))]
        args += [w["wuk_cat"], w["wuv_cat"], w["p_kr"], w["wv_t"]]
        outs += [(jax.ShapeDtypeStruct((m // seq, QK_ROPE, seq), F32),
                  pl.BlockSpec((1, QK_ROPE, tm), lambda i: (i // nt, 0, i % nt))),
                 (jax.ShapeDtypeStruct((2 * M_HEADS, m), F32), pl.BlockSpec((2 * M_HEADS, tm), lambda i: (0, i)))]
        outs += [(jax.ShapeDtypeStruct((m, wide), BF16), pl.BlockSpec((tm, wide), row))] * 3
        outs += [(jax.ShapeDtypeStruct((M_WIDTH, m), BF16), pl.BlockSpec((M_WIDTH, tm), lambda i: (0, i)))]
    return pl.pallas_call(
        functools.partial(_inproj_kernel, absorbed=absorbed),
        grid=grid, in_specs=in_specs,
        out_specs=[o[1] for o in outs], out_shape=[o[0] for o in outs],
        compiler_params=_params(1), name="inproj_abs" if absorbed else "inproj",
    )(*args)


def _split3_dot(x, m01):
    hi = x.astype(BF16)
    r1 = x - hi.astype(F32)
    mid = r1.astype(BF16)
    lo = (r1 - mid.astype(F32)).astype(BF16)
    return sum(jnp.dot(part, m01, preferred_element_type=F32) for part in (hi, mid, lo))


def _cumsum_lanes(x, upper):
    return _split3_dot(x, jnp.where(upper, 1.0, 0.0).astype(BF16))


def _cummax_lanes(x):
    lane = lax.broadcasted_iota(jnp.int32, x.shape, 1)
    shift = 1
    while shift < x.shape[1]:
        x = jnp.maximum(x, jnp.where(lane >= shift, pltpu.roll(x, shift, 1), -jnp.inf))
        shift *= 2
    return x


def _mlstm_gates_kernel(g_ref, urow_ref, scal_ref, cols_ref, *, nc):
    L = M_CHUNK
    H = M_HEADS
    rows = nc * 2 * H
    g = g_ref[...]
    r = jnp.concatenate([g[:, L * c:L * (c + 1)] for c in range(nc)], axis=0)
    ri = lax.broadcasted_iota(jnp.int32, (rows, L), 0)
    ci = lax.broadcasted_iota(jnp.int32, (rows, L), 1)
    top = (ri % (2 * H)) < H
    b_all = _cumsum_lanes(r, lax.broadcasted_iota(jnp.int32, (L, L), 0) <= lax.broadcasted_iota(jnp.int32, (L, L), 1))
    b = pltpu.roll(b_all, rows - H, 0)
    u = r - b
    cm = _cummax_lanes(u)
    g_last = jnp.max(jnp.where(ci == L - 1, b, -jnp.inf), axis=1, keepdims=True)
    wlog = g_last - b + r
    wmax = jnp.max(wlog, axis=1, keepdims=True)
    urow_ref[0] = jnp.where(top, u, pltpu.roll(wlog, H, 0)).reshape(nc, 2 * H, L)
    scal = jnp.where(top, jnp.broadcast_to(g_last, (rows, L)), pltpu.roll(jnp.broadcast_to(wmax, (rows, L)), H, 0))
    scal_ref[0] = scal.reshape(nc, 2 * H, L)
    first = jnp.where(top, b, pltpu.roll(cm, H, 0))
    for c in range(nc):
        sl = slice(2 * H * c, 2 * H * (c + 1))
        cols_ref[0, c] = jnp.transpose(jnp.concatenate([first[sl], wlog[sl]], axis=0))


def _mlstm_gates(g, *, batch, nc):
    blk = lambda b: (b, 0, 0, 0)
    return pl.pallas_call(
        functools.partial(_mlstm_gates_kernel, nc=nc),
        grid=(batch,),
        in_specs=[pl.BlockSpec((2 * M_HEADS, nc * M_CHUNK), lambda b: (0, b))],
        out_specs=[pl.BlockSpec((1, nc, 2 * M_HEADS, M_CHUNK), blk), pl.BlockSpec((1, nc, 2 * M_HEADS, M_CHUNK), blk),
                   pl.BlockSpec((1, nc, M_CHUNK, 4 * M_HEADS), blk)],
        out_shape=[jax.ShapeDtypeStruct((batch, nc, 2 * M_HEADS, M_CHUNK), F32),
                   jax.ShapeDtypeStruct((batch, nc, 2 * M_HEADS, M_CHUNK), F32),
                   jax.ShapeDtypeStruct((batch, nc, M_CHUNK, 4 * M_HEADS), F32)],
        compiler_params=_params(1), name="mlstm_gates",
    )(g)


def _mlstm_kernel(qkv_ref, vt_ref, urow_ref, scal_ref, cols_ref, sel_ref, og_ref, gm_ref, hg_ref, c_ref, n_ref, m_ref):
    L = M_CHUNK

    @pl.when(pl.program_id(1) == 0)
    def _():
        c_ref[...] = jnp.zeros_like(c_ref)
        n_ref[...] = jnp.zeros_like(n_ref)
        m_ref[...] = jnp.zeros_like(m_ref)

    row = lax.broadcasted_iota(jnp.int32, (L, L), 0)
    col = lax.broadcasted_iota(jnp.int32, (L, L), 1)
    tril = col <= row

    rows8 = urow_ref[0, 0]
    u4 = rows8[0:M_HEADS]
    wlog4 = rows8[M_HEADS:2 * M_HEADS]
    scal = scal_ref[0, 0]
    g_last4 = scal[0:M_HEADS, 0:1]
    wmax4 = scal[M_HEADS:2 * M_HEADS, 0:1]
    cols = _split3_dot(cols_ref[0, 0], sel_ref[...])
    m_prev4 = m_ref[0][:, 0:1]
    n_prev4 = n_ref[0]
    m_new4 = jnp.maximum(g_last4 + m_prev4, wmax4)
    decay4 = jnp.exp(g_last4 + m_prev4 - m_new4)
    ws4 = jnp.exp(wlog4 - m_new4)
    m_ref[0] = jnp.broadcast_to(m_new4, (M_HEADS, LANES))
    n_rows = []

    for h in range(M_HEADS):
        sl = slice(M_HEAD_DIM * h, M_HEAD_DIM * (h + 1))
        q = qkv_ref[:, sl]
        k = qkv_ref[:, M_WIDTH + M_HEAD_DIM * h:M_WIDTH + M_HEAD_DIM * (h + 1)]
        v = qkv_ref[:, 2 * M_WIDTH + M_HEAD_DIM * h:2 * M_WIDTH + M_HEAD_DIM * (h + 1)]
        m_prev = m_prev4[h:h + 1]
        c_prev = c_ref[0, h]
        n_prev = n_prev4[h:h + 1]
        b_col = cols[:, LANES * h:LANES * (h + 1)]
        mm_col = jnp.maximum(cols[:, LANES * (M_HEADS + h):LANES * (M_HEADS + h + 1)], m_prev)
        w_intra = jnp.where(tril, jnp.exp(u4[h:h + 1] - mm_col), 0.0)
        w_inter = jnp.exp(m_prev - mm_col)
        s = _dot_nt(q, k) * w_intra
        cn = jnp.concatenate([c_prev.astype(BF16), jnp.broadcast_to(n_prev.astype(BF16), (L, M_HEAD_DIM))], axis=0)
        qcn = _dot_nt(q, cn)
        num = w_inter * qcn[:, 0:M_HEAD_DIM] + _dot(s, v)
        den = w_inter * qcn[:, M_HEAD_DIM:] + jnp.sum(s, axis=1, keepdims=True)
        hs = num / jnp.maximum(jnp.abs(den), jnp.exp(-(b_col + mm_col)))

        ws = ws4[h:h + 1]
        lhs = jnp.concatenate([(vt_ref[sl, :].astype(F32) * ws).astype(BF16),
                               jnp.broadcast_to(ws.astype(BF16), (16, L))], axis=0)
        upd = jnp.dot(lhs, k, preferred_element_type=F32)
        decay = decay4[h:h + 1]
        c_ref[0, h] = decay * c_prev + upd[0:M_HEAD_DIM]
        n_rows.append(decay * n_prev + upd[M_HEAD_DIM:M_HEAD_DIM + 1])

        mu = jnp.mean(hs, axis=1, keepdims=True)
        d = hs - mu
        y = d * lax.rsqrt(jnp.mean(d * d, axis=1, keepdims=True) + EPS) * gm_ref[:, sl]
        hg_ref[:, sl] = (y * og_ref[:, sl]).astype(BF16)
    n_ref[0] = jnp.concatenate(n_rows, axis=0)


def _mlstm_prompt(qkvm, vt, gates, og, g_mhead, *, batch, seq):
    nc = seq // M_CHUNK
    tok = lambda b, c: (b * nc + c, 0)
    chunk = lambda b, c: (b, c, 0, 0)
    urow, scal, cols = _mlstm_gates(gates, batch=batch, nc=nc)
    sel = jnp.asarray(np.kron(np.eye(4 * M_HEADS, 2 * M_HEADS), np.ones((1, LANES))), BF16)
    return pl.pallas_call(
        _mlstm_kernel,
        grid=(batch, nc),
        in_specs=[
            pl.BlockSpec((M_CHUNK, 3 * M_WIDTH), tok),
            pl.BlockSpec((M_WIDTH, M_CHUNK), lambda b, c: (0, b * nc + c)),
            pl.BlockSpec((1, 1, 2 * M_HEADS, M_CHUNK), chunk),
            pl.BlockSpec((1, 1, 2 * M_HEADS, M_CHUNK), chunk),
            pl.BlockSpec((1, 1, M_CHUNK, 4 * M_HEADS), chunk),
            pl.BlockSpec((4 * M_HEADS, 2 * M_HEADS * LANES), lambda b, c: (0, 0)),
            pl.BlockSpec((M_CHUNK, M_WIDTH), tok),
            pl.BlockSpec((1, M_WIDTH), lambda b, c: (0, 0)),
        ],
        out_specs=[
            pl.BlockSpec((M_CHUNK, M_WIDTH), tok),
            pl.BlockSpec((1, M_HEADS, M_HEAD_DIM, M_HEAD_DIM), lambda b, c: (b, 0, 0, 0)),
            pl.BlockSpec((1, M_HEADS, M_HEAD_DIM), lambda b, c: (b, 0, 0)),
            pl.BlockSpec((1, M_HEADS, LANES), lambda b, c: (b, 0, 0)),
        ],
        out_shape=[
            jax.ShapeDtypeStruct((batch * seq, M_WIDTH), BF16),
            jax.ShapeDtypeStruct((batch, M_HEADS, M_HEAD_DIM, M_HEAD_DIM), F32),
            jax.ShapeDtypeStruct((batch, M_HEADS, M_HEAD_DIM), F32),
            jax.ShapeDtypeStruct((batch, M_HEADS, LANES), F32),
        ],
        compiler_params=_params(2), name="mlstm_prompt",
    )(qkvm, vt, urow, scal, cols, sel, og, g_mhead)


def _mlstm_step_kernel(qkv_ref, tail_ref, c_ref, n_ref, m_ref, og_ref, gm_ref, hg_ref, co_ref, no_ref, mo_ref, *, rows):
    D = M_HEAD_DIM
    lane = lax.broadcasted_iota(jnp.int32, (rows, LANES), 1)
    rowi = lax.broadcasted_iota(jnp.int32, (rows, D), 0)
    tail = tail_ref[...]
    m_in = m_ref[...]
    m_out = jnp.zeros((rows, LANES), F32)
    for h in range(M_HEADS):
        sl = slice(D * h, D * (h + 1))
        q = qkv_ref[:, sl]
        k = qkv_ref[:, M_WIDTH + D * h:M_WIDTH + D * (h + 1)]
        v = qkv_ref[:, 2 * M_WIDTH + D * h:2 * M_WIDTH + D * (h + 1)]
        ig = tail[:, TAIL_IG + h:TAIL_IG + h + 1]
        lf = tail[:, TAIL_LF + h:TAIL_LF + h + 1]
        m_prev = m_in[:, h:h + 1]
        n_prev = n_ref[:, h, :]
        a = lf + m_prev
        mt = jnp.maximum(a, ig)
        w_in = jnp.exp(ig - mt)
        w_st = jnp.exp(a - mt)
        s = jnp.sum(q * k, axis=1, keepdims=True) * w_in
        cq = jnp.zeros((rows, D), F32)
        for g in range(rows):
            cq = jnp.where(rowi == g, _dot_nt(q, c_ref[g, h]), cq)
        num = w_st * cq + s * v
        den = w_st * jnp.sum(n_prev * q, axis=1, keepdims=True) + s
        hs = num / jnp.maximum(jnp.abs(den), jnp.exp(-mt))
        vw_t = jnp.transpose(v * w_in)
        for g in range(rows):
            co_ref[g, h] = w_st[g:g + 1] * c_ref[g, h] + vw_t[:, g:g + 1] * k[g:g + 1, :]
        no_ref[:, h, :] = w_st * n_prev + w_in * k
        m_out = jnp.where(lane == h, mt, m_out)
        mu = jnp.mean(hs, axis=1, keepdims=True)
        d = hs - mu
        y = d * lax.rsqrt(jnp.mean(d * d, axis=1, keepdims=True) + EPS) * gm_ref[:, sl]
        hg_ref[:, sl] = (y * og_ref[:, sl]).astype(BF16)
    mo_ref[...] = m_out


def _mlstm_step(qkvm, tail, c0, n0, m0_pad, og, g_mhead, *, rows=8):
    nb = qkvm.shape[0]
    row = lambda i: (i, 0)
    return pl.pallas_call(
        functools.partial(_mlstm_step_kernel, rows=rows),
        grid=(nb // rows,),
        in_specs=[
            pl.BlockSpec((rows, 3 * M_WIDTH), row),
            pl.BlockSpec((rows, LANES), row),
            pl.BlockSpec((rows, M_HEADS, M_HEAD_DIM, M_HEAD_DIM), lambda i: (i, 0, 0, 0)),
            pl.BlockSpec((rows, M_HEADS, M_HEAD_DIM), lambda i: (i, 0, 0)),
            pl.BlockSpec((rows, LANES), row),
            pl.BlockSpec((rows, M_WIDTH), row),
            pl.BlockSpec((1, M_WIDTH), lambda i: (0, 0)),
        ],
        out_specs=[
            pl.BlockSpec((rows, M_WIDTH), row),
            pl.BlockSpec((rows, M_HEADS, M_HEAD_DIM, M_HEAD_DIM), lambda i: (i, 0, 0, 0)),
            pl.BlockSpec((rows, M_HEADS, M_HEAD_DIM), lambda i: (i, 0, 0)),
            pl.BlockSpec((rows, LANES), row),
        ],
        out_shape=[
            jax.ShapeDtypeStruct((nb, M_WIDTH), BF16),
            jax.ShapeDtypeStruct((nb, M_HEADS, M_HEAD_DIM, M_HEAD_DIM), F32),
            jax.ShapeDtypeStruct((nb, M_HEADS, M_HEAD_DIM), F32),
            jax.ShapeDtypeStruct((nb, LANES), F32),
        ],
        compiler_params=_params(1), name="mlstm_step",
    )(qkvm, tail, c0, n0, m0_pad, og, g_mhead)


def _mla_prefill_kernel(q_ref, k_ref, v_ref, o_ref, s_scr, p_scr, m_scr, acc_scr, *, blk, heads, rows):
    i = pl.program_id(2)
    reps = blk // LANES
    m_scr[...] = jnp.full(m_scr.shape, -jnp.inf, F32)
    acc_scr[...] = jnp.zeros(acc_scr.shape, F32)

    def block(j, masked):
        start = pl.multiple_of(j * blk, blk)
        for hh in range(heads):
            sl = slice(HEAD_SLOT * hh, HEAD_SLOT * (hh + 1))
            s_scr[hh] = _dot_nt(q_ref[:, sl], k_ref[pl.ds(start, blk), sl])
        for hh in range(heads):
            for r in range(blk // rows):
                rs = slice(rows * r, rows * (r + 1))
                s = s_scr[hh, rs, :]
                if masked:
                    qi = lax.broadcasted_iota(jnp.int32, (rows, blk), 0) + rows * r
                    ki = lax.broadcasted_iota(jnp.int32, (rows, blk), 1)
                    s = jnp.where(ki <= qi, s, -jnp.inf)
                m_old = m_scr[hh, rs, :]
                m_new = jnp.maximum(m_old, jnp.max(s, axis=1, keepdims=True))
                p_scr[hh, rs, :] = jnp.exp(s - jnp.concatenate([m_new] * reps, axis=1)).astype(BF16)
                acc_scr[hh, rs, :] = acc_scr[hh, rs, :] * jnp.exp(m_old - m_new)
                m_scr[hh, rs, :] = m_new
        for hh in range(heads):
            sl = slice(HEAD_SLOT * hh, HEAD_SLOT * (hh + 1))
            acc_scr[hh] += jnp.dot(p_scr[hh], v_ref[pl.ds(start, blk), sl], preferred_element_type=F32)

    def body(j, carry):
        block(j, False)
        return carry

    lax.fori_loop(0, i, body, 0)
    block(i, True)
    for hh in range(heads):
        acc = acc_scr[hh]
        o_ref[:, HEAD_SLOT * hh:HEAD_SLOT * (hh + 1)] = (acc / acc[:, V_HEAD:V_HEAD + 1]).astype(BF16)


def _mla_prefill(qcat, kcat, vcat, *, batch, seq, blk, heads=2, rows=32):
    nq = seq // blk
    wide = heads * HEAD_SLOT
    return pl.pallas_call(
        functools.partial(_mla_prefill_kernel, blk=blk, heads=heads, rows=rows),
        grid=(batch, A_HEADS // heads, nq),
        in_specs=[
            pl.BlockSpec((blk, wide), lambda b, h, i: (b * nq + i, h)),
            pl.BlockSpec((seq, wide), lambda b, h, i: (b, h)),
            pl.BlockSpec((seq, wide), lambda b, h, i: (b, h)),
        ],
        out_specs=pl.BlockSpec((blk, wide), lambda b, h, i: (b * nq + i, h)),
        out_shape=jax.ShapeDtypeStruct((batch * seq, A_HEADS * HEAD_SLOT), BF16),
        scratch_shapes=[pltpu.VMEM((heads, blk, blk), F32), pltpu.VMEM((heads, blk, blk), BF16),
                        pltpu.VMEM((heads, blk, LANES), F32), pltpu.VMEM((heads, blk, HEAD_SLOT), F32)],
        compiler_params=_params(3), name="mla_prefill",
    )(qcat, kcat, vcat)


def _mla_decode_kernel(pt_ref, q_ref, ckv_hbm, kr_hbm, cself_ref, tself_ref, o_ref,
                       cbuf, rbuf, kbuf, sems, *, n_chunks, chunk_pages, ahead):
    b = pl.program_id(0)
    nb = pl.num_programs(0)

    def page_copies(bb, c, slot, p):
        pg = pt_ref[bb, c * chunk_pages + p]
        dst = pl.ds(p * PAGE_SIZE, PAGE_SIZE)
        return (pltpu.make_async_copy(ckv_hbm.at[pg], cbuf.at[slot, dst], sems.at[0, slot]),
                pltpu.make_async_copy(kr_hbm.at[pg], rbuf.at[slot, :, dst], sems.at[1, slot]))

    def issue(bb, c, slot):
        for p in range(chunk_pages):
            for cp in page_copies(bb, c, slot, p):
                cp.start()

    def wait(slot):
        for p in range(chunk_pages):
            for cp in page_copies(0, 0, slot, p):
                cp.wait()

    @pl.when(b == 0)
    def _():
        for c in range(ahead):
            issue(0, c, c)

    q = q_ref[0]
    q_lat = q[:, 0:KV_LORA]
    q_rope = q[:, KV_LORA:KV_LORA + QK_ROPE]

    def scores(slot):
        kc = cbuf[slot].astype(BF16)
        kbuf[slot % 2] = kc
        kr = rbuf[slot].astype(BF16)
        return _dot_nt(q_lat, kc) + jnp.dot(q_rope, kr, preferred_element_type=F32)

    def accumulate(carry, s, slot):
        m_old, l, acc = carry
        m_new = jnp.maximum(m_old, jnp.max(s, axis=1, keepdims=True))
        p = jnp.exp(s - m_new)
        alpha = jnp.exp(m_old - m_new)
        return (m_new, alpha * l + jnp.sum(p, axis=1, keepdims=True),
                alpha * acc + jnp.dot(p.astype(BF16), kbuf[slot], preferred_element_type=F32))

    carry = (jnp.full((A_HEADS, 1), -jnp.inf, F32), jnp.zeros((A_HEADS, 1), F32), jnp.zeros((A_HEADS, KV_LORA), F32))
    s_prev = None
    for c in range(n_chunks):
        nxt = c + ahead
        if nxt < n_chunks:
            issue(b, nxt, nxt)
        else:
            @pl.when(b + 1 < nb)
            def _():
                issue(b + 1, nxt - n_chunks, nxt - n_chunks)
        wait(c)
        s_cur = scores(c)
        if c > 0:
            carry = accumulate(carry, s_prev, (c - 1) % 2)
        s_prev = s_cur
    m_old, l, acc = accumulate(carry, s_prev, (n_chunks - 1) % 2)

    c_self = cself_ref[0].astype(BF16).astype(F32)
    r_self = tself_ref[0][:, 0:QK_ROPE].astype(BF16).astype(F32)
    s_self = (jnp.sum(q_lat.astype(F32) * c_self, axis=1, keepdims=True)
              + jnp.sum(q_rope.astype(F32) * r_self, axis=1, keepdims=True))
    m_new = jnp.maximum(m_old, s_self)
    p_self = jnp.exp(s_self - m_new)
    alpha = jnp.exp(m_old - m_new)
    l = alpha * l + p_self
    acc = alpha * acc + p_self.astype(BF16).astype(F32) * c_self
    o_ref[0] = acc / l


def _mla_decode(page_table, qabs, cache_ckv, cache_krope, ckv_self, tail_self, *, chunk_pages=32, ahead=2):
    nb, n_pages = page_table.shape
    n_chunks = n_pages // chunk_pages
    assert n_chunks * chunk_pages == n_pages and ahead < n_chunks
    keys = chunk_pages * PAGE_SIZE
    grid_spec = pltpu.PrefetchScalarGridSpec(
        num_scalar_prefetch=1,
        grid=(nb,),
        in_specs=[
            pl.BlockSpec((1, A_HEADS, ABS_SLOT), lambda b, pt: (b, 0, 0)),
            pl.BlockSpec(memory_space=pl.ANY),
            pl.BlockSpec(memory_space=pl.ANY),
            pl.BlockSpec((1, 1, KV_LORA), lambda b, pt: (b, 0, 0)),
            pl.BlockSpec((1, 1, LANES), lambda b, pt: (b, 0, 0)),
        ],
        out_specs=pl.BlockSpec((1, A_HEADS, KV_LORA), lambda b, pt: (b, 0, 0)),
        scratch_shapes=[
            pltpu.VMEM((n_chunks, keys, KV_LORA), F32),
            pltpu.VMEM((n_chunks, QK_ROPE, keys), F32),
            pltpu.VMEM((2, keys, KV_LORA), BF16),
            pltpu.SemaphoreType.DMA((2, n_chunks)),
        ],
    )
    return pl.pallas_call(
        functools.partial(_mla_decode_kernel, n_chunks=n_chunks, chunk_pages=chunk_pages, ahead=ahead),
        grid_spec=grid_spec,
        out_shape=jax.ShapeDtypeStruct((nb, A_HEADS, KV_LORA), F32),
        compiler_params=_params(1), name="mla_decode",
    )(page_table, qabs.reshape(nb, A_HEADS, ABS_SLOT), cache_ckv, cache_krope,
      ckv_self.reshape(nb, 1, KV_LORA), tail_self.reshape(nb, 1, LANES))


def _merge_kernel(x_ref, hg_ref, ob_ref, sg_ref, wa_ref, wb_ref, wout_ref, gpost_ref, gmem_ref, wmq_ref,
                  *rest, from_latent):
    if from_latent:
        wuv_ref, x1_ref, qm_ref = rest
        ob = jnp.concatenate(
            [_dot(ob_ref[:, KV_LORA * hh:KV_LORA * (hh + 1)], wuv_ref[hh]).astype(BF16) for hh in range(A_HEADS)],
            axis=1)
    else:
        x1_ref, qm_ref = rest
        ob = ob_ref[...]
    ya = jnp.dot(hg_ref[...], wa_ref[...], preferred_element_type=F32)
    yb = jnp.dot(ob, wb_ref[...], preferred_element_type=F32)
    mix = sg_ref[:, 0:D_MODEL] * ya + sg_ref[:, D_MODEL:] * yb
    y = _dot(mix, wout_ref[...])
    x1 = x_ref[...] + _rms(y, gpost_ref[...])
    x1_ref[...] = x1
    qm = _dot(_rms(x1, gmem_ref[...]), wmq_ref[...]) * (X_HEAD_DIM ** -0.5)
    qm_ref[...] = qm.astype(qm_ref.dtype)


def _merge(x, hg, ob, sg, w, *, tm, from_latent, qdtype):
    m = x.shape[0]
    row = lambda i: (i, 0)
    wide = A_HEADS * HEAD_SLOT
    ob_w = ob.shape[1]
    in_specs = [
        pl.BlockSpec((tm, D_MODEL), row), pl.BlockSpec((tm, M_WIDTH), row), pl.BlockSpec((tm, ob_w), row),
        pl.BlockSpec((tm, 2 * D_MODEL), row),
        _const_spec((M_WIDTH, D_MODEL)), _const_spec((wide, D_MODEL)), _const_spec((D_MODEL, D_MODEL)),
        _const_spec((1, D_MODEL)), _const_spec((1, D_MODEL)), _const_spec((D_MODEL, X_WIDTH)),
    ]
    args = [x, hg, ob, sg, w["w_branch_a"], w["wb_cat"], w["w_out"], w["g_post_mix"], w["g_pre_mem"], w["w_mq"]]
    if from_latent:
        in_specs += [_const_spec((A_HEADS, KV_LORA, HEAD_SLOT))]
        args += [w["wuv_heads"]]
    return pl.pallas_call(
        functools.partial(_merge_kernel, from_latent=from_latent),
        grid=(m // tm,), in_specs=in_specs,
        out_specs=[pl.BlockSpec((tm, D_MODEL), row), pl.BlockSpec((tm, X_WIDTH), row)],
        out_shape=[jax.ShapeDtypeStruct((m, D_MODEL), F32), jax.ShapeDtypeStruct((m, X_WIDTH), qdtype)],
        compiler_params=_params(1), name="merge_lat" if from_latent else "merge",
    )(*args)


def _memkv_kernel(mem_ref, g_ref, wk_ref, wv_ref, k_ref, v_ref):
    mn = _rms(mem_ref[...], g_ref[...]).astype(BF16)
    k_ref[...] = jnp.dot(mn, wk_ref[...], preferred_element_type=F32)
    v_ref[...] = jnp.dot(mn, wv_ref[...], preferred_element_type=F32)


def _memkv(mem, w, *, tm):
    m = mem.shape[0]
    row = lambda i: (i, 0)
    return pl.pallas_call(
        _memkv_kernel, grid=(m // tm,),
        in_specs=[pl.BlockSpec((tm, D_MODEL), row), _const_spec((1, D_MODEL)),
                  _const_spec((D_MODEL, X_WIDTH)), _const_spec((D_MODEL, X_WIDTH))],
        out_specs=[pl.BlockSpec((tm, X_WIDTH), row)] * 2,
        out_shape=[jax.ShapeDtypeStruct((m, X_WIDTH), F32)] * 2,
        compiler_params=_params(1), name="memkv",
    )(mem, w["g_mem"], w["w_mk"], w["w_mv"])


def _memattn_kernel(q_ref, k_ref, v_ref, o_ref):
    kb = k_ref[...].astype(BF16)
    vb = v_ref[...].astype(BF16)
    for h in range(X_HEADS):
        sl = slice(X_HEAD_DIM * h, X_HEAD_DIM * (h + 1))
        s = _dot_nt(q_ref[:, sl], kb[:, sl])
        p = jnp.exp(s - jnp.max(s, axis=1, keepdims=True))
        l = jnp.sum(p, axis=1, keepdims=True)
        o_ref[:, sl] = (jnp.dot(p.astype(BF16), vb[:, sl], preferred_element_type=F32) / l).astype(BF16)


def _memattn(qm, mem_k, mem_v, *, batch, seq, tm):
    nt = seq // tm
    tok = lambda b, i: (b * nt + i, 0)
    kv = lambda b, i: (b, 0)
    return pl.pallas_call(
        _memattn_kernel, grid=(batch, nt),
        in_specs=[pl.BlockSpec((tm, X_WIDTH), tok), pl.BlockSpec((N_MEM, X_WIDTH), kv),
                  pl.BlockSpec((N_MEM, X_WIDTH), kv)],
        out_specs=pl.BlockSpec((tm, X_WIDTH), tok),
        out_shape=jax.ShapeDtypeStruct((batch * seq, X_WIDTH), BF16),
        compiler_params=_params(2), name="memattn",
    )(qm, mem_k, mem_v)


def _memattn_step_kernel(q_ref, k_ref, v_ref, o_ref, *, rows):
    for g in range(rows):
        q4 = [q_ref[g:g + 1, X_HEAD_DIM * h:X_HEAD_DIM * (h + 1)] for h in range(X_HEADS)]
        q8 = jnp.concatenate(q4 + q4, axis=0)
        s = jnp.sum(k_ref[g] * q8[None], axis=2, keepdims=True)
        mx = jnp.max(s, axis=0)
        mx = jnp.maximum(mx, pltpu.roll(mx, X_HEADS, 0))
        p = jnp.exp(s - mx[None])
        l8 = jnp.sum(p, axis=0)
        o8 = jnp.sum(p * v_ref[g], axis=0)
        o_ref[g] = (o8[0:X_HEADS] + o8[X_HEADS:]) / (l8[0:X_HEADS] + l8[X_HEADS:])


def _memattn_step(qm, mem_k, mem_v, *, rows=8):
    nb = qm.shape[0]
    row = lambda i: (i, 0)
    kv = lambda i: (i, 0, 0, 0)
    kv_block = (rows, N_MEM // 2, 2 * X_HEADS, X_HEAD_DIM)
    return pl.pallas_call(
        functools.partial(_memattn_step_kernel, rows=rows), grid=(nb // rows,),
        in_specs=[pl.BlockSpec((rows, X_WIDTH), row), pl.BlockSpec(kv_block, kv), pl.BlockSpec(kv_block, kv)],
        out_specs=pl.BlockSpec((rows, X_HEADS, X_HEAD_DIM), lambda i: (i, 0, 0)),
        out_shape=jax.ShapeDtypeStruct((nb, X_HEADS, X_HEAD_DIM), F32),
        compiler_params=_params(1), name="memattn_step",
    )(qm, mem_k, mem_v).reshape(nb, X_WIDTH)


def _ffn_kernel(x1_ref, o_ref, wmo_ref, gpm_ref, gpf_ref, win_ref, wdn_ref, gpost_ref, y_ref, *, f_chunk):
    x2 = x1_ref[...] + _rms(_dot(o_ref[...], wmo_ref[...]), gpm_ref[...])
    h = _rms(x2, gpf_ref[...]).astype(BF16)
    acc = jnp.zeros(x2.shape, F32)
    for c in range(D_FF // f_chunk):
        sl = slice(f_chunk * c, f_chunk * (c + 1))
        g = jnp.dot(h, win_ref[:, sl], preferred_element_type=F32)
        u = jnp.dot(h, win_ref[:, D_FF + f_chunk * c:D_FF + f_chunk * (c + 1)], preferred_element_type=F32)
        acc = acc + jnp.dot((g * jax.nn.sigmoid(g) * u).astype(BF16), wdn_ref[sl, :], preferred_element_type=F32)
    y_ref[...] = x2 + _rms(acc, gpost_ref[...])


def _ffn(x1, o, w, *, tm, f_chunk=1408):
    m = x1.shape[0]
    row = lambda i: (i, 0)
    return pl.pallas_call(
        functools.partial(_ffn_kernel, f_chunk=f_chunk), grid=(m // tm,),
        in_specs=[pl.BlockSpec((tm, D_MODEL), row), pl.BlockSpec((tm, X_WIDTH), row),
                  _const_spec((X_WIDTH, D_MODEL)), _const_spec((1, D_MODEL)), _const_spec((1, D_MODEL)),
                  _const_spec((D_MODEL, 2 * D_FF)), _const_spec((D_FF, D_MODEL)),
                  _const_spec((1, D_MODEL))],
        out_specs=pl.BlockSpec((tm, D_MODEL), row),
        out_shape=jax.ShapeDtypeStruct((m, D_MODEL), F32),
        compiler_params=_params(1), name="ffn",
    )(x1, o, w["w_mo"], w["g_post_mem"], w["g_pre_ffn"], w["w_ffn_in"], w["w_ffn_out"], w["g_post_ffn"])


def _prep_weights(p):
    w = {}
    for name in ("g_pre_mix", "g_q_lora", "g_kv_lora", "g_mhead", "g_post_mix", "g_pre_mem", "g_mem", "g_post_mem",
                 "g_pre_ffn", "g_post_ffn"):
        w[name] = p[name].reshape(1, -1).astype(F32)
    w_in = p["w_in"]
    offs = np.cumsum((0, M_WIDTH, M_WIDTH, M_WIDTH, M_HEADS, M_HEADS, M_WIDTH, Q_LORA, KV_LORA, QK_ROPE,
                      D_MODEL, D_MODEL))
    seg = lambda i: w_in[:, offs[i]:offs[i + 1]]
    mq, mk, mv, mi, mf, mo, cq, ckv, kr, ga, gb = (seg(i) for i in range(11))
    half = QK_ROPE // 2
    swap = lambda a: jnp.concatenate([a[..., half:], a[..., :half]], axis=-1)
    tail = jnp.concatenate([kr, swap(kr), mi, mf,
                            jnp.zeros((D_MODEL, LANES - 2 * QK_ROPE - 2 * M_HEADS), w_in.dtype)], axis=1)
    w["w_qkv"] = w_in[:, offs[0]:offs[3]].astype(BF16)
    w["w_mo_gate"] = mo.astype(BF16)
    w["w_lat"] = w_in[:, offs[6]:offs[8]].astype(BF16)
    w["w_g"] = w_in[:, offs[9]:offs[11]].astype(BF16)
    w["w_tail"] = tail.astype(BF16)
    w["wv_t"] = mv.T.astype(BF16)
    w["b_tail"] = jnp.zeros((1, LANES), F32).at[0, TAIL_IG:TAIL_IG + 2 * M_HEADS].set(p["b_if"].astype(F32))

    pad = HEAD_SLOT - QK_NOPE - QK_ROPE
    wq = p["w_uq"].reshape(Q_LORA, A_HEADS, QK_NOPE + QK_ROPE)
    zq = jnp.zeros((Q_LORA, A_HEADS, pad), wq.dtype)
    w["wq_cat"] = jnp.concatenate([wq, zq], axis=2).reshape(Q_LORA, A_HEADS * HEAD_SLOT).astype(BF16)
    w["wq_sw"] = jnp.concatenate([jnp.zeros_like(wq[..., :QK_NOPE]), swap(wq[..., QK_NOPE:]), zq],
                                 axis=2).reshape(Q_LORA, A_HEADS * HEAD_SLOT).astype(BF16)
    w_uk, w_uv = p["w_uk"], p["w_uv"]
    zk = jnp.zeros((KV_LORA, A_HEADS, HEAD_SLOT - QK_NOPE), w_uk.dtype)
    w["wuk_cat"] = jnp.concatenate([w_uk, zk], axis=2).reshape(KV_LORA, A_HEADS * HEAD_SLOT).astype(BF16)
    w["wuv_cat"] = jnp.concatenate([w_uv, zk], axis=2).reshape(KV_LORA, A_HEADS * HEAD_SLOT).astype(BF16)
    w["wuv_heads"] = jnp.concatenate([w_uv, zk], axis=2).transpose(1, 0, 2).astype(BF16)
    pk = np.zeros((LANES, A_HEADS, HEAD_SLOT), np.float32)
    for r in range(QK_ROPE):
        pk[r, :, QK_NOPE + r] = 1.0
    w["p_kr"] = jnp.asarray(pk.reshape(LANES, A_HEADS * HEAD_SLOT), BF16)
    pass_np = np.zeros((A_HEADS, HEAD_SLOT - QK_NOPE, ABS_SLOT), np.float32)
    pass_np[:, np.arange(QK_ROPE), KV_LORA + np.arange(QK_ROPE)] = 1.0
    w["w_abs"] = jnp.concatenate(
        [jnp.concatenate([w_uk.transpose(1, 2, 0), jnp.zeros((A_HEADS, QK_NOPE, ABS_SLOT - KV_LORA), w_uk.dtype)],
                         axis=2).astype(BF16), jnp.asarray(pass_np, BF16)], axis=1)
    wb = p["w_branch_b"].reshape(A_HEADS, V_HEAD, D_MODEL)
    w["wb_cat"] = jnp.concatenate([wb, jnp.zeros((A_HEADS, HEAD_SLOT - V_HEAD, D_MODEL), wb.dtype)],
                                  axis=1).reshape(A_HEADS * HEAD_SLOT, D_MODEL).astype(BF16)
    for name in ("w_branch_a", "w_out", "w_mq", "w_mk", "w_mv", "w_mo", "w_ffn_out"):
        w[name] = p[name].astype(BF16)
    w["w_ffn_in"] = p["w_ffn_in"].astype(BF16)
    return w


def _rope_tables(pos0, n):
    pos = (pos0 + jnp.arange(n)).astype(F32)
    inv = ROPE_BASE ** (-jnp.arange(0, QK_ROPE, 2, dtype=F32) / QK_ROPE)
    ang = pos[:, None] * inv[None, :]
    cos, sin = jnp.cos(ang), jnp.sin(ang)
    c32 = jnp.concatenate([cos, cos], axis=1)
    s32 = jnp.concatenate([-sin, sin], axis=1)
    one = lambda k: jnp.ones((n, k), F32)
    zero = lambda k: jnp.zeros((n, k), F32)
    ct = jnp.concatenate([c32, one(LANES - QK_ROPE)], axis=1)
    st = jnp.concatenate([s32, zero(LANES - QK_ROPE)], axis=1)
    c128 = jnp.concatenate([one(QK_NOPE), c32, one(HEAD_SLOT - QK_NOPE - QK_ROPE)], axis=1)
    s128 = jnp.concatenate([zero(QK_NOPE), s32, zero(HEAD_SLOT - QK_NOPE - QK_ROPE)], axis=1)
    return ct, st, c128, s128


def kernel(x_prompt, x_sample, cache_ckv, cache_krope, cache_mem_k, cache_mem_v, state_C, state_n, state_m, page_table, mem_prompt, g_pre_mix, w_in, b_if, g_mhead, g_q_lora, w_uq, g_kv_lora, w_uk, w_uv, w_branch_a, w_branch_b, w_out, g_post_mix, g_pre_mem, g_mem, w_mq, w_mk, w_mv, w_mo, g_post_mem, g_pre_ffn, w_ffn_in, w_ffn_out, g_post_ffn):
    params = dict(g_pre_mix=g_pre_mix, w_in=w_in, b_if=b_if, g_mhead=g_mhead, g_q_lora=g_q_lora, w_uq=w_uq,
                  g_kv_lora=g_kv_lora, w_uk=w_uk, w_uv=w_uv, w_branch_a=w_branch_a, w_branch_b=w_branch_b,
                  w_out=w_out, g_post_mix=g_post_mix, g_pre_mem=g_pre_mem, g_mem=g_mem, w_mq=w_mq, w_mk=w_mk,
                  w_mv=w_mv, w_mo=w_mo, g_post_mem=g_post_mem, g_pre_ffn=g_pre_ffn, w_ffn_in=w_ffn_in,
                  w_ffn_out=w_ffn_out, g_post_ffn=g_post_ffn)
    depth = w_in.shape[0]
    assert depth == 1, "single-layer stack only"
    bp, seq, _ = x_prompt.shape
    bs, dec_seq, _ = x_sample.shape
    assert dec_seq == 1, "one new token per sample sequence"
    past_len = page_table.shape[1] * PAGE_SIZE
    w = _prep_weights({name: a[0] for name, a in params.items()})

    xp = x_prompt.reshape(bp * seq, D_MODEL)
    qkvm, og, ckv_p, sg, krope_t, gates_t, qcat, kcat, vcat, vt = _inproj(
        xp, w, _rope_tables(0, seq), tm=256, absorbed=False, mdtype=BF16)
    hg, c_p, n_p, m_p = _mlstm_prompt(qkvm, vt, gates_t, og, w["g_mhead"], batch=bp, seq=seq)
    ob = _mla_prefill(qcat, kcat, vcat, batch=bp, seq=seq, blk=512)
    x1, qm = _merge(xp, hg, ob, sg, w, tm=512, from_latent=False, qdtype=BF16)
    mem_k, mem_v = _memkv(mem_prompt.reshape(bp * N_MEM, D_MODEL), w, tm=512)
    om = _memattn(qm, mem_k, mem_v, batch=bp, seq=seq, tm=512)
    y_prompt = _ffn(x1, om, w, tm=512).reshape(bp, seq, D_MODEL)

    xs = x_sample.reshape(bs, D_MODEL)
    tables_s = tuple(jnp.broadcast_to(t, (bs, LANES)) for t in _rope_tables(past_len, 1))
    qkvm_s, og_s, ckv_s, sg_s, tail_s, qabs = _inproj(xs, w, tables_s, tm=bs, absorbed=True, mdtype=F32)
    m0_pad = jnp.pad(state_m.reshape(bs, M_HEADS).astype(F32), ((0, 0), (0, LANES - M_HEADS)))
    hg_s, c_s, n_s, m_s = _mlstm_step(qkvm_s, tail_s, state_C.reshape(bs, M_HEADS, M_HEAD_DIM, M_HEAD_DIM),
                                      state_n.reshape(bs, M_HEADS, M_HEAD_DIM), m0_pad, og_s, w["g_mhead"])
    n_phys = cache_ckv.shape[1]
    cache_krope_t = jnp.transpose(cache_krope.reshape(n_phys, PAGE_SIZE, QK_ROPE), (0, 2, 1))
    o_lat = _mla_decode(page_table, qabs, cache_ckv.reshape(n_phys, PAGE_SIZE, KV_LORA), cache_krope_t, ckv_s, tail_s)
    x1_s, qm_s = _merge(xs, hg_s, o_lat.reshape(bs, A_HEADS * KV_LORA), sg_s, w, tm=bs, from_latent=True, qdtype=F32)
    om_s = _memattn_step(qm_s, cache_mem_k.reshape(bs, N_MEM // 2, 2 * X_HEADS, X_HEAD_DIM),
                         cache_mem_v.reshape(bs, N_MEM // 2, 2 * X_HEADS, X_HEAD_DIM))
    y_sample = _ffn(x1_s, om_s, w, tm=bs).reshape(bs, 1, D_MODEL)

    return (y_prompt, y_sample,
            ckv_p.reshape(1, bp, seq, KV_LORA), jnp.transpose(krope_t, (0, 2, 1)).reshape(1, bp, seq, QK_ROPE),
            c_p.reshape(1, bp, M_HEADS, M_HEAD_DIM, M_HEAD_DIM), n_p.reshape(1, bp, M_HEADS, M_HEAD_DIM),
            m_p[:, :, 0].reshape(1, bp, M_HEADS),
            mem_k.reshape(1, bp, N_MEM, X_HEADS, X_HEAD_DIM), mem_v.reshape(1, bp, N_MEM, X_HEADS, X_HEAD_DIM),
            ckv_s.reshape(1, bs, 1, KV_LORA), tail_s[:, :QK_ROPE].reshape(1, bs, 1, QK_ROPE),
            c_s.reshape(1, bs, M_HEADS, M_HEAD_DIM, M_HEAD_DIM), n_s.reshape(1, bs, M_HEADS, M_HEAD_DIM),
            m_s[:, :M_HEADS].reshape(1, bs, M_HEADS))
```

```python
import functools

import jax
import jax.numpy as jnp
import numpy as np
from jax import lax
from jax.experimental import pallas as pl
from jax.experimental.pallas import tpu as pltpu

F32 = jnp.float32
BF16 = jnp.bfloat16

D_MODEL = 1024
PAGE_SIZE = 128
M_HEADS = 4
M_HEAD_DIM = 128
M_WIDTH = M_HEADS * M_HEAD_DIM
M_CHUNK = 128
A_HEADS = 8
QK_NOPE = 64
QK_ROPE = 32
V_HEAD = 64
Q_LORA = 384
KV_LORA = 256
ROPE_BASE = 10000.0
N_MEM = 256
X_HEADS = 4
X_HEAD_DIM = 128
X_WIDTH = X_HEADS * X_HEAD_DIM
D_FF = 2816
EPS = 1e-6

LANES = 128
HEAD_SLOT = 128
ABS_SLOT = 384
ATT_SCALE = (QK_NOPE + QK_ROPE) ** -0.5
VMEM_LIMIT = 52 * 1024 * 1024

TAIL_IG = 64
TAIL_LF = 68


def _rms(x, g):
    return x * lax.rsqrt(jnp.mean(x * x, axis=-1, keepdims=True) + EPS) * g


def _dot(a, b):
    return jnp.dot(a.astype(BF16), b.astype(BF16), preferred_element_type=F32)


def _dot_nt(a, b):
    return lax.dot_general(a.astype(BF16), b.astype(BF16), (((1,), (1,)), ((), ())),
                           preferred_element_type=F32)


def _dot_tn(a, b):
    return lax.dot_general(a.astype(BF16), b.astype(BF16), (((0,), (0,)), ((), ())),
                           preferred_element_type=F32)


def _log_sigmoid(x):
    return jnp.minimum(x, 0.0) - jnp.log1p(jnp.exp(-jnp.abs(x)))


def _const_spec(shape):
    nd = len(shape)
    return pl.BlockSpec(shape, lambda *_: (0,) * nd, pipeline_mode=pl.Buffered(1))


def _params(n_axes):
    return pltpu.CompilerParams(dimension_semantics=("arbitrary",) * n_axes, vmem_limit_bytes=VMEM_LIMIT)


def _inproj_kernel(x_ref, gpre_ref, wqkv_ref, wmo_ref, wlat_ref, wg_ref, wtail_ref, btail_ref, ct_ref, st_ref,
                   gq_ref, wq_ref, c128_ref, s128_ref, gkv_ref, *rest, absorbed):
    if absorbed:
        wabs_ref, qkvm_ref, og_ref, ckv_ref, sg_ref, tail_ref, qabs_ref = rest
    else:
        (wuk_ref, wuv_ref, pk_ref, wvt_ref,
         qkvm_ref, og_ref, ckv_ref, sg_ref, krt_ref, gt_ref, qcat_ref, kcat_ref, vcat_ref, vt_ref) = rest
    h = _rms(x_ref[...], gpre_ref[...]).astype(BF16)

    zm = jnp.dot(h, wqkv_ref[...], preferred_element_type=F32)
    qkvm_ref[:, 0:M_WIDTH] = zm[:, 0:M_WIDTH].astype(qkvm_ref.dtype)
    qkvm_ref[:, M_WIDTH:2 * M_WIDTH] = (zm[:, M_WIDTH:2 * M_WIDTH] * (M_HEAD_DIM ** -0.5)).astype(qkvm_ref.dtype)
    qkvm_ref[:, 2 * M_WIDTH:] = zm[:, 2 * M_WIDTH:].astype(qkvm_ref.dtype)
    og_ref[...] = jax.nn.sigmoid(jnp.dot(h, wmo_ref[...], preferred_element_type=F32))
    sg_ref[...] = jax.nn.sigmoid(jnp.dot(h, wg_ref[...], preferred_element_type=F32))

    t = jnp.dot(h, wtail_ref[...], preferred_element_type=F32) + btail_ref[...]
    t = t * ct_ref[...] + pltpu.roll(t, LANES - QK_ROPE, 1) * st_ref[...]
    lane = lax.broadcasted_iota(jnp.int32, t.shape, 1)
    is_lf = jnp.logical_and(lane >= TAIL_LF, lane < TAIL_LF + M_HEADS)
    t = jnp.where(is_lf, _log_sigmoid(t), t)
    if absorbed:
        tail_ref[...] = t
    else:
        tt = jnp.transpose(t)
        krt_ref[0] = tt[0:QK_ROPE]
        gt_ref[...] = tt[TAIL_IG:TAIL_IG + 2 * M_HEADS]

    cqn = _rms(jnp.dot(h, wlat_ref[:, 0:Q_LORA], preferred_element_type=F32), gq_ref[...]).astype(BF16)
    ckvn = _rms(jnp.dot(h, wlat_ref[:, Q_LORA:], preferred_element_type=F32), gkv_ref[...])
    ckv_ref[...] = ckvn

    qc = jnp.dot(cqn, wq_ref[...], preferred_element_type=F32)
    width = A_HEADS * HEAD_SLOT
    first_half = (lax.broadcasted_iota(jnp.int32, qc.shape, 1) % HEAD_SLOT) < QK_NOPE + QK_ROPE // 2
    qs = jnp.where(first_half, pltpu.roll(qc, width - QK_ROPE // 2, 1), pltpu.roll(qc, QK_ROPE // 2, 1))
    c128 = c128_ref[...]
    s128 = s128_ref[...]
    for hh in range(A_HEADS):
        sl = slice(HEAD_SLOT * hh, HEAD_SLOT * (hh + 1))
        qh = ((qc[:, sl] * c128 + qs[:, sl] * s128) * ATT_SCALE).astype(BF16)
        if absorbed:
            qabs_ref[:, ABS_SLOT * hh:ABS_SLOT * (hh + 1)] = jnp.dot(
                qh, wabs_ref[hh], preferred_element_type=F32).astype(BF16)
        else:
            qcat_ref[:, sl] = qh
    if not absorbed:
        cb = ckvn.astype(BF16)
        kcat_ref[...] = (jnp.dot(cb, wuk_ref[...], preferred_element_type=F32)
                         + jnp.dot(t.astype(BF16), pk_ref[...], preferred_element_type=F32)).astype(BF16)
        vlane = lax.broadcasted_iota(jnp.int32, (1, A_HEADS * HEAD_SLOT), 1) % HEAD_SLOT
        vcat = jnp.dot(cb, wuv_ref[...], preferred_element_type=F32)
        vcat_ref[...] = jnp.where(vlane == V_HEAD, 1.0, vcat).astype(BF16)
        vt_ref[...] = _dot_nt(wvt_ref[...], h).astype(BF16)


def _inproj(x, w, tables, *, tm, absorbed, mdtype):
    m = x.shape[0]
    ct, st, c128, s128 = tables
    seq = ct.shape[0]
    nt = seq // tm
    grid = (m // tm,)
    row = lambda i: (i, 0)
    tab = lambda i: (i % nt, 0)
    wide = A_HEADS * HEAD_SLOT
    in_specs = [
        pl.BlockSpec((tm, D_MODEL), row), _const_spec((1, D_MODEL)),
        _const_spec((D_MODEL, 3 * M_WIDTH)), _const_spec((D_MODEL, M_WIDTH)), _const_spec((D_MODEL, Q_LORA + KV_LORA)),
        _const_spec((D_MODEL, 2 * D_MODEL)), _const_spec((D_MODEL, LANES)), _const_spec((1, LANES)),
        pl.BlockSpec((tm, LANES), tab), pl.BlockSpec((tm, LANES), tab),
        _const_spec((1, Q_LORA)), _const_spec((Q_LORA, wide)),
        pl.BlockSpec((tm, LANES), tab), pl.BlockSpec((tm, LANES), tab),
        _const_spec((1, KV_LORA)),
    ]
    args = [x, w["g_pre_mix"], w["w_qkv"], w["w_mo_gate"], w["w_lat"], w["w_g"], w["w_tail"], w["b_tail"], ct, st,
            w["g_q_lora"], w["wq_cat"], c128, s128, w["g_kv_lora"]]
    outs = [
        (jax.ShapeDtypeStruct((m, 3 * M_WIDTH), mdtype), pl.BlockSpec((tm, 3 * M_WIDTH), row)),
        (jax.ShapeDtypeStruct((m, M_WIDTH), F32), pl.BlockSpec((tm, M_WIDTH), row)),
        (jax.ShapeDtypeStruct((m, KV_LORA), F32), pl.BlockSpec((tm, KV_LORA), row)),
        (jax.ShapeDtypeStruct((m, 2 * D_MODEL), F32), pl.BlockSpec((tm, 2 * D_MODEL), row)),
    ]
    if absorbed:
        in_specs += [_const_spec((A_HEADS, HEAD_SLOT, ABS_SLOT))]
        args += [w["w_abs"]]
        outs += [(jax.ShapeDtypeStruct((m, LANES), F32), pl.BlockSpec((tm, LANES), row)),
                 (jax.ShapeDtypeStruct((m, A_HEADS * ABS_SLOT), BF16), pl.BlockSpec((tm, A_HEADS * ABS_SLOT), row))]
    else:
        in_specs += [_const_spec((KV_LORA, wide)), _const_spec((KV_LORA, wide)), _const_spec((LANES, wide)),
                     _const_spec((M_WIDTH, D_MODEL))]
        args += [w["wuk_cat"], w["wuv_cat"], w["p_kr"], w["wv_t"]]
        outs += [(jax.ShapeDtypeStruct((m // seq, QK_ROPE, seq), F32),
                  pl.BlockSpec((1, QK_ROPE, tm), lambda i: (i // nt, 0, i % nt))),
                 (jax.ShapeDtypeStruct((2 * M_HEADS, m), F32), pl.BlockSpec((2 * M_HEADS, tm), lambda i: (0, i)))]
        outs += [(jax.ShapeDtypeStruct((m, wide), BF16), pl.BlockSpec((tm, wide), row))] * 3
        outs += [(jax.ShapeDtypeStruct((M_WIDTH, m), BF16), pl.BlockSpec((M_WIDTH, tm), lambda i: (0, i)))]
    return pl.pallas_call(
        functools.partial(_inproj_kernel, absorbed=absorbed),
        grid=grid, in_specs=in_specs,
        out_specs=[o[1] for o in outs], out_shape=[o[0] for o in outs],
        compiler_params=_params(1), name="inproj_abs" if absorbed else "inproj",
    )(*args)


def _split3_dot(x, m01):
    hi = x.astype(BF16)
    r1 = x - hi.astype(F32)
    mid = r1.astype(BF16)
    lo = (r1 - mid.astype(F32)).astype(BF16)
    return sum(jnp.dot(part, m01, preferred_element_type=F32) for part in (hi, mid, lo))


def _cumsum_lanes(x, upper):
    return _split3_dot(x, jnp.where(upper, 1.0, 0.0).astype(BF16))


def _cummax_lanes(x):
    lane = lax.broadcasted_iota(jnp.int32, x.shape, 1)
    shift = 1
    while shift < x.shape[1]:
        x = jnp.maximum(x, jnp.where(lane >= shift, pltpu.roll(x, shift, 1), -jnp.inf))
        shift *= 2
    return x


def _mlstm_gates_kernel(g_ref, urow_ref, scal_ref, cols_ref, *, nc):
    L = M_CHUNK
    H = M_HEADS
    rows = nc * 2 * H
    g = g_ref[...]
    r = jnp.concatenate([g[:, L * c:L * (c + 1)] for c in range(nc)], axis=0)
    ri = lax.broadcasted_iota(jnp.int32, (rows, L), 0)
    ci = lax.broadcasted_iota(jnp.int32, (rows, L), 1)
    top = (ri % (2 * H)) < H
    b_all = _cumsum_lanes(r, lax.broadcasted_iota(jnp.int32, (L, L), 0) <= lax.broadcasted_iota(jnp.int32, (L, L), 1))
    b = pltpu.roll(b_all, rows - H, 0)
    u = r - b
    cm = _cummax_lanes(u)
    g_last = jnp.max(jnp.where(ci == L - 1, b, -jnp.inf), axis=1, keepdims=True)
    wlog = g_last - b + r
    wmax = jnp.max(wlog, axis=1, keepdims=True)
    urow_ref[0] = jnp.where(top, u, pltpu.roll(wlog, H, 0)).reshape(nc, 2 * H, L)
    scal = jnp.where(top, jnp.broadcast_to(g_last, (rows, L)), pltpu.roll(jnp.broadcast_to(wmax, (rows, L)), H, 0))
    scal_ref[0] = scal.reshape(nc, 2 * H, L)
    first = jnp.where(top, b, pltpu.roll(cm, H, 0))
    for c in range(nc):
        sl = slice(2 * H * c, 2 * H * (c + 1))
        cols_ref[0, c] = jnp.transpose(jnp.concatenate([first[sl], wlog[sl]], axis=0))


def _mlstm_gates(g, *, batch, nc):
    blk = lambda b: (b, 0, 0, 0)
    return pl.pallas_call(
        functools.partial(_mlstm_gates_kernel, nc=nc),
        grid=(batch,),
        in_specs=[pl.BlockSpec((2 * M_HEADS, nc * M_CHUNK), lambda b: (0, b))],
        out_specs=[pl.BlockSpec((1, nc, 2 * M_HEADS, M_CHUNK), blk), pl.BlockSpec((1, nc, 2 * M_HEADS, M_CHUNK), blk),
                   pl.BlockSpec((1, nc, M_CHUNK, 4 * M_HEADS), blk)],
        out_shape=[jax.ShapeDtypeStruct((batch, nc, 2 * M_HEADS, M_CHUNK), F32),
                   jax.ShapeDtypeStruct((batch, nc, 2 * M_HEADS, M_CHUNK), F32),
                   jax.ShapeDtypeStruct((batch, nc, M_CHUNK, 4 * M_HEADS), F32)],
        compiler_params=_params(1), name="mlstm_gates",
    )(g)


def _mlstm_kernel(*refs, nb):
    qkv_ref = refs[0]
    vt_refs = refs[1:1 + nb]
    urow_ref, scal_ref, cols_ref, sel_ref, og_ref, gm_ref, hg_ref, c_ref, n_ref, m_ref = refs[1 + nb:]
    L = M_CHUNK

    @pl.when(pl.program_id(1) == 0)
    def _():
        c_ref[...] = jnp.zeros_like(c_ref)
        n_ref[...] = jnp.zeros_like(n_ref)
        m_ref[...] = jnp.zeros_like(m_ref)

    row = lax.broadcasted_iota(jnp.int32, (L, L), 0)
    col = lax.broadcasted_iota(jnp.int32, (L, L), 1)
    tril = col <= row

    for bb in range(nb):
        vt_ref = vt_refs[bb]
        rows8 = urow_ref[bb, 0]
        u4 = rows8[0:M_HEADS]
        wlog4 = rows8[M_HEADS:2 * M_HEADS]
        scal = scal_ref[bb, 0]
        g_last4 = scal[0:M_HEADS, 0:1]
        wmax4 = scal[M_HEADS:2 * M_HEADS, 0:1]
        cols = _split3_dot(cols_ref[bb, 0], sel_ref[...])
        m_prev4 = m_ref[bb][:, 0:1]
        n_prev4 = n_ref[bb]
        m_new4 = jnp.maximum(g_last4 + m_prev4, wmax4)
        decay4 = jnp.exp(g_last4 + m_prev4 - m_new4)
        ws4 = jnp.exp(wlog4 - m_new4)
        m_ref[bb] = jnp.broadcast_to(m_new4, (M_HEADS, LANES))
        n_rows = []

        for h in range(M_HEADS):
            sl = slice(M_HEAD_DIM * h, M_HEAD_DIM * (h + 1))
            q = qkv_ref[bb, :, sl]
            k = qkv_ref[bb, :, M_WIDTH + M_HEAD_DIM * h:M_WIDTH + M_HEAD_DIM * (h + 1)]
            v = qkv_ref[bb, :, 2 * M_WIDTH + M_HEAD_DIM * h:2 * M_WIDTH + M_HEAD_DIM * (h + 1)]
            m_prev = m_prev4[h:h + 1]
            c_prev = c_ref[bb, h]
            n_prev = n_prev4[h:h + 1]
            b_col = cols[:, LANES * h:LANES * (h + 1)]
            mm_col = jnp.maximum(cols[:, LANES * (M_HEADS + h):LANES * (M_HEADS + h + 1)], m_prev)
            w_intra = jnp.where(tril, jnp.exp(u4[h:h + 1] - mm_col), 0.0)
            w_inter = jnp.exp(m_prev - mm_col)
            s = _dot_nt(q, k) * w_intra
            cn = jnp.concatenate([c_prev.astype(BF16), jnp.broadcast_to(n_prev.astype(BF16), (L, M_HEAD_DIM))],
                                 axis=0)
            qcn = _dot_nt(q, cn)
            num = w_inter * qcn[:, 0:M_HEAD_DIM] + _dot(s, v)
            den = w_inter * qcn[:, M_HEAD_DIM:] + jnp.sum(s, axis=1, keepdims=True)
            hs = num / jnp.maximum(jnp.abs(den), jnp.exp(-(b_col + mm_col)))

            ws = ws4[h:h + 1]
            lhs = jnp.concatenate([(vt_ref[sl, :].astype(F32) * ws).astype(BF16),
                                   jnp.broadcast_to(ws.astype(BF16), (16, L))], axis=0)
            upd = jnp.dot(lhs, k, preferred_element_type=F32)
            decay = decay4[h:h + 1]
            c_ref[bb, h] = decay * c_prev + upd[0:M_HEAD_DIM]
            n_rows.append(decay * n_prev + upd[M_HEAD_DIM:M_HEAD_DIM + 1])

            mu = jnp.mean(hs, axis=1, keepdims=True)
            d = hs - mu
            y = d * lax.rsqrt(jnp.mean(d * d, axis=1, keepdims=True) + EPS) * gm_ref[:, sl]
            hg_ref[bb, :, sl] = (y * og_ref[bb, :, sl]).astype(BF16)
        n_ref[bb] = jnp.concatenate(n_rows, axis=0)


def _mlstm_prompt(qkvm, vt, gates, og, g_mhead, *, batch, seq, nb=2):
    nc = seq // M_CHUNK
    tok = lambda b, c: (b, c, 0)
    chunk = lambda b, c: (b, c, 0, 0)
    urow, scal, cols = _mlstm_gates(gates, batch=batch, nc=nc)
    sel = jnp.asarray(np.kron(np.eye(4 * M_HEADS, 2 * M_HEADS), np.ones((1, LANES))), BF16)
    vt_specs = [pl.BlockSpec((M_WIDTH, M_CHUNK), functools.partial(lambda b, c, bb: (0, (nb * b + bb) * nc + c), bb=bb))
                for bb in range(nb)]
    hg, c_p, n_p, m_p = pl.pallas_call(
        functools.partial(_mlstm_kernel, nb=nb),
        grid=(batch // nb, nc),
        in_specs=[pl.BlockSpec((nb, M_CHUNK, 3 * M_WIDTH), tok)] + vt_specs + [
            pl.BlockSpec((nb, 1, 2 * M_HEADS, M_CHUNK), chunk),
            pl.BlockSpec((nb, 1, 2 * M_HEADS, M_CHUNK), chunk),
            pl.BlockSpec((nb, 1, M_CHUNK, 4 * M_HEADS), chunk),
            pl.BlockSpec((4 * M_HEADS, 2 * M_HEADS * LANES), lambda b, c: (0, 0)),
            pl.BlockSpec((nb, M_CHUNK, M_WIDTH), tok),
            pl.BlockSpec((1, M_WIDTH), lambda b, c: (0, 0)),
        ],
        out_specs=[
            pl.BlockSpec((nb, M_CHUNK, M_WIDTH), tok),
            pl.BlockSpec((nb, M_HEADS, M_HEAD_DIM, M_HEAD_DIM), lambda b, c: (b, 0, 0, 0)),
            pl.BlockSpec((nb, M_HEADS, M_HEAD_DIM), lambda b, c: (b, 0, 0)),
            pl.BlockSpec((nb, M_HEADS, LANES), lambda b, c: (b, 0, 0)),
        ],
        out_shape=[
            jax.ShapeDtypeStruct((batch, seq, M_WIDTH), BF16),
            jax.ShapeDtypeStruct((batch, M_HEADS, M_HEAD_DIM, M_HEAD_DIM), F32),
            jax.ShapeDtypeStruct((batch, M_HEADS, M_HEAD_DIM), F32),
            jax.ShapeDtypeStruct((batch, M_HEADS, LANES), F32),
        ],
        compiler_params=_params(2), name="mlstm_prompt",
    )(qkvm.reshape(batch, seq, 3 * M_WIDTH), *([vt] * nb), urow, scal, cols, sel,
      og.reshape(batch, seq, M_WIDTH), g_mhead)
    return hg.reshape(batch * seq, M_WIDTH), c_p, n_p, m_p


def _mlstm_step_kernel(qkv_ref, tail_ref, c_ref, n_ref, m_ref, og_ref, gm_ref, hg_ref, co_ref, no_ref, mo_ref, *, rows):
    D = M_HEAD_DIM
    lane = lax.broadcasted_iota(jnp.int32, (rows, LANES), 1)
    rowi = lax.broadcasted_iota(jnp.int32, (rows, D), 0)
    tail = tail_ref[...]
    m_in = m_ref[...]
    m_out = jnp.zeros((rows, LANES), F32)
    for h in range(M_HEADS):
        sl = slice(D * h, D * (h + 1))
        q = qkv_ref[:, sl]
        k = qkv_ref[:, M_WIDTH + D * h:M_WIDTH + D * (h + 1)]
        v = qkv_ref[:, 2 * M_WIDTH + D * h:2 * M_WIDTH + D * (h + 1)]
        ig = tail[:, TAIL_IG + h:TAIL_IG + h + 1]
        lf = tail[:, TAIL_LF + h:TAIL_LF + h + 1]
        m_prev = m_in[:, h:h + 1]
        n_prev = n_ref[:, h, :]
        a = lf + m_prev
        mt = jnp.maximum(a, ig)
        w_in = jnp.exp(ig - mt)
        w_st = jnp.exp(a - mt)
        s = jnp.sum(q * k, axis=1, keepdims=True) * w_in
        cq = jnp.zeros((rows, D), F32)
        for g in range(rows):
            cq = jnp.where(rowi == g, _dot_nt(q, c_ref[g, h]), cq)
        num = w_st * cq + s * v
        den = w_st * jnp.sum(n_prev * q, axis=1, keepdims=True) + s
        hs = num / jnp.maximum(jnp.abs(den), jnp.exp(-mt))
        vw_t = jnp.transpose(v * w_in)
        for g in range(rows):
            co_ref[g, h] = w_st[g:g + 1] * c_ref[g, h] + vw_t[:, g:g + 1] * k[g:g + 1, :]
        no_ref[:, h, :] = w_st * n_prev + w_in * k
        m_out = jnp.where(lane == h, mt, m_out)
        mu = jnp.mean(hs, axis=1, keepdims=True)
        d = hs - mu
        y = d * lax.rsqrt(jnp.mean(d * d, axis=1, keepdims=True) + EPS) * gm_ref[:, sl]
        hg_ref[:, sl] = (y * og_ref[:, sl]).astype(BF16)
    mo_ref[...] = m_out


def _mlstm_step(qkvm, tail, c0, n0, m0_pad, og, g_mhead, *, rows=8):
    nb = qkvm.shape[0]
    row = lambda i: (i, 0)
    return pl.pallas_call(
        functools.partial(_mlstm_step_kernel, rows=rows),
        grid=(nb // rows,),
        in_specs=[
            pl.BlockSpec((rows, 3 * M_WIDTH), row),
            pl.BlockSpec((rows, LANES), row),
            pl.BlockSpec((rows, M_HEADS, M_HEAD_DIM, M_HEAD_DIM), lambda i: (i, 0, 0, 0)),
            pl.BlockSpec((rows, M_HEADS, M_HEAD_DIM), lambda i: (i, 0, 0)),
            pl.BlockSpec((rows, LANES), row),
            pl.BlockSpec((rows, M_WIDTH), row),
            pl.BlockSpec((1, M_WIDTH), lambda i: (0, 0)),
        ],
        out_specs=[
            pl.BlockSpec((rows, M_WIDTH), row),
            pl.BlockSpec((rows, M_HEADS, M_HEAD_DIM, M_HEAD_DIM), lambda i: (i, 0, 0, 0)),
            pl.BlockSpec((rows, M_HEADS, M_HEAD_DIM), lambda i: (i, 0, 0)),
            pl.BlockSpec((rows, LANES), row),
        ],
        out_shape=[
            jax.ShapeDtypeStruct((nb, M_WIDTH), BF16),
            jax.ShapeDtypeStruct((nb, M_HEADS, M_HEAD_DIM, M_HEAD_DIM), F32),
            jax.ShapeDtypeStruct((nb, M_HEADS, M_HEAD_DIM), F32),
            jax.ShapeDtypeStruct((nb, LANES), F32),
        ],
        compiler_params=_params(1), name="mlstm_step",
    )(qkvm, tail, c0, n0, m0_pad, og, g_mhead)


def _mla_prefill_kernel(q_ref, k_ref, v_ref, o_ref, s_scr, p_scr, m_scr, acc_scr, *, blk, heads, rows):
    i = pl.program_id(2)
    reps = blk // LANES
    m_scr[...] = jnp.full(m_scr.shape, -jnp.inf, F32)
    acc_scr[...] = jnp.zeros(acc_scr.shape, F32)

    def block(j, masked):
        start = pl.multiple_of(j * blk, blk)
        for hh in range(heads):
            sl = slice(HEAD_SLOT * hh, HEAD_SLOT * (hh + 1))
            s_scr[hh] = _dot_nt(q_ref[:, sl], k_ref[pl.ds(start, blk), sl])
        for hh in range(heads):
            for r in range(blk // rows):
                rs = slice(rows * r, rows * (r + 1))
                s = s_scr[hh, rs, :]
                if masked:
                    qi = lax.broadcasted_iota(jnp.int32, (rows, blk), 0) + rows * r
                    ki = lax.broadcasted_iota(jnp.int32, (rows, blk), 1)
                    s = jnp.where(ki <= qi, s, -jnp.inf)
                m_old = m_scr[hh, rs, :]
                m_new = jnp.maximum(m_old, jnp.max(s, axis=1, keepdims=True))
                p_scr[hh, rs, :] = jnp.exp(s - jnp.concatenate([m_new] * reps, axis=1)).astype(BF16)
                acc_scr[hh, rs, :] = acc_scr[hh, rs, :] * jnp.exp(m_old - m_new)
                m_scr[hh, rs, :] = m_new
        for hh in range(heads):
            sl = slice(HEAD_SLOT * hh, HEAD_SLOT * (hh + 1))
            acc_scr[hh] += jnp.dot(p_scr[hh], v_ref[pl.ds(start, blk), sl], preferred_element_type=F32)

    def body(j, carry):
        block(j, False)
        return carry

    lax.fori_loop(0, i, body, 0)
    block(i, True)
    o0, o1 = (acc_scr[hh] / acc_scr[hh][:, V_HEAD:V_HEAD + 1] for hh in range(2))
    lane = lax.broadcasted_iota(jnp.int32, (blk, HEAD_SLOT), 1)
    o_ref[...] = jnp.where(lane < V_HEAD, o0, pltpu.roll(o1, V_HEAD, 1)).astype(BF16)


def _mla_prefill(qcat, kcat, vcat, *, batch, seq, blk, rows=32):
    heads = 2
    nq = seq // blk
    wide = heads * HEAD_SLOT
    return pl.pallas_call(
        functools.partial(_mla_prefill_kernel, blk=blk, heads=heads, rows=rows),
        grid=(batch, A_HEADS // heads, nq),
        in_specs=[
            pl.BlockSpec((blk, wide), lambda b, h, i: (b * nq + i, h)),
            pl.BlockSpec((seq, wide), lambda b, h, i: (b, h)),
            pl.BlockSpec((seq, wide), lambda b, h, i: (b, h)),
        ],
        out_specs=pl.BlockSpec((blk, heads * V_HEAD), lambda b, h, i: (b * nq + i, h)),
        out_shape=jax.ShapeDtypeStruct((batch * seq, A_HEADS * V_HEAD), BF16),
        scratch_shapes=[pltpu.VMEM((heads, blk, blk), F32), pltpu.VMEM((heads, blk, blk), BF16),
                        pltpu.VMEM((heads, blk, LANES), F32), pltpu.VMEM((heads, blk, HEAD_SLOT), F32)],
        compiler_params=_params(3), name="mla_prefill",
    )(qcat, kcat, vcat)


def _mla_decode_kernel(pt_ref, q_ref, ckv_hbm, kr_hbm, cself_ref, tself_ref, o_ref,
                       cbuf, rbuf, kbuf, sems, *, n_chunks, chunk_pages, ahead):
    b = pl.program_id(0)
    nb = pl.num_programs(0)

    def page_copies(bb, c, slot, p):
        pg = pt_ref[bb, c * chunk_pages + p]
        dst = pl.ds(p * PAGE_SIZE, PAGE_SIZE)
        return (pltpu.make_async_copy(ckv_hbm.at[pg], cbuf.at[slot, dst], sems.at[0, slot]),
                pltpu.make_async_copy(kr_hbm.at[pg], rbuf.at[slot, :, dst], sems.at[1, slot]))

    def issue(bb, c, slot):
        for p in range(chunk_pages):
            for cp in page_copies(bb, c, slot, p):
                cp.start()

    def wait(slot):
        for p in range(chunk_pages):
            for cp in page_copies(0, 0, slot, p):
                cp.wait()

    @pl.when(b == 0)
    def _():
        for c in range(ahead):
            issue(0, c, c)

    q = q_ref[0]
    q_lat = q[:, 0:KV_LORA]
    q_rope = q[:, KV_LORA:KV_LORA + QK_ROPE]

    def scores(slot):
        kc = cbuf[slot].astype(BF16)
        kbuf[slot % 2] = kc
        kr = rbuf[slot].astype(BF16)
        return _dot_nt(q_lat, kc) + jnp.dot(q_rope, kr, preferred_element_type=F32)

    def accumulate(carry, s, slot):
        m_old, l, acc = carry
        m_new = jnp.maximum(m_old, jnp.max(s, axis=1, keepdims=True))
        p = jnp.exp(s - m_new)
        alpha = jnp.exp(m_old - m_new)
        return (m_new, alpha * l + jnp.sum(p, axis=1, keepdims=True),
                alpha * acc + jnp.dot(p.astype(BF16), kbuf[slot], preferred_element_type=F32))

    carry = (jnp.full((A_HEADS, 1), -jnp.inf, F32), jnp.zeros((A_HEADS, 1), F32), jnp.zeros((A_HEADS, KV_LORA), F32))
    s_prev = None
    for c in range(n_chunks):
        nxt = c + ahead
        if nxt < n_chunks:
            issue(b, nxt, nxt)
        else:
            @pl.when(b + 1 < nb)
            def _():
                issue(b + 1, nxt - n_chunks, nxt - n_chunks)
        wait(c)
        s_cur = scores(c)
        if c > 0:
            carry = accumulate(carry, s_prev, (c - 1) % 2)
        s_prev = s_cur
    m_old, l, acc = accumulate(carry, s_prev, (n_chunks - 1) % 2)

    c_self = cself_ref[0].astype(BF16).astype(F32)
    r_self = tself_ref[0][:, 0:QK_ROPE].astype(BF16).astype(F32)
    s_self = (jnp.sum(q_lat.astype(F32) * c_self, axis=1, keepdims=True)
              + jnp.sum(q_rope.astype(F32) * r_self, axis=1, keepdims=True))
    m_new = jnp.maximum(m_old, s_self)
    p_self = jnp.exp(s_self - m_new)
    alpha = jnp.exp(m_old - m_new)
    l = alpha * l + p_self
    acc = alpha * acc + p_self.astype(BF16).astype(F32) * c_self
    o_ref[0] = acc / l


def _mla_decode(page_table, qabs, cache_ckv, cache_krope, ckv_self, tail_self, *, chunk_pages=32, ahead=2):
    nb, n_pages = page_table.shape
    n_chunks = n_pages // chunk_pages
    assert n_chunks * chunk_pages == n_pages and ahead < n_chunks
    keys = chunk_pages * PAGE_SIZE
    grid_spec = pltpu.PrefetchScalarGridSpec(
        num_scalar_prefetch=1,
        grid=(nb,),
        in_specs=[
            pl.BlockSpec((1, A_HEADS, ABS_SLOT), lambda b, pt: (b, 0, 0)),
            pl.BlockSpec(memory_space=pl.ANY),
            pl.BlockSpec(memory_space=pl.ANY),
            pl.BlockSpec((1, 1, KV_LORA), lambda b, pt: (b, 0, 0)),
            pl.BlockSpec((1, 1, LANES), lambda b, pt: (b, 0, 0)),
        ],
        out_specs=pl.BlockSpec((1, A_HEADS, KV_LORA), lambda b, pt: (b, 0, 0)),
        scratch_shapes=[
            pltpu.VMEM((n_chunks, keys, KV_LORA), F32),
            pltpu.VMEM((n_chunks, QK_ROPE, keys), F32),
            pltpu.VMEM((2, keys, KV_LORA), BF16),
            pltpu.SemaphoreType.DMA((2, n_chunks)),
        ],
    )
    return pl.pallas_call(
        functools.partial(_mla_decode_kernel, n_chunks=n_chunks, chunk_pages=chunk_pages, ahead=ahead),
        grid_spec=grid_spec,
        out_shape=jax.ShapeDtypeStruct((nb, A_HEADS, KV_LORA), F32),
        compiler_params=_params(1), name="mla_decode",
    )(page_table, qabs.reshape(nb, A_HEADS, ABS_SLOT), cache_ckv, cache_krope,
      ckv_self.reshape(nb, 1, KV_LORA), tail_self.reshape(nb, 1, LANES))


def _merge_kernel(x_ref, hg_ref, ob_ref, sg_ref, wa_ref, wb_ref, wout_ref, gpost_ref, gmem_ref, wmq_ref,
                  *rest, from_latent):
    if from_latent:
        wuv_ref, x1_ref, qm_ref = rest
        ob = sum(_dot(ob_ref[:, KV_LORA * hh:KV_LORA * (hh + 1)], wuv_ref[hh]) for hh in range(A_HEADS)).astype(BF16)
    else:
        x1_ref, qm_ref = rest
        ob = ob_ref[...]
    ya = jnp.dot(hg_ref[...], wa_ref[...], preferred_element_type=F32)
    yb = jnp.dot(ob, wb_ref[...], preferred_element_type=F32)
    mix = sg_ref[:, 0:D_MODEL] * ya + sg_ref[:, D_MODEL:] * yb
    y = _dot(mix, wout_ref[...])
    x1 = x_ref[...] + _rms(y, gpost_ref[...])
    x1_ref[...] = x1
    qm = _dot(_rms(x1, gmem_ref[...]), wmq_ref[...]) * (X_HEAD_DIM ** -0.5)
    qm_ref[...] = qm.astype(qm_ref.dtype)


def _merge(x, hg, ob, sg, w, *, tm, from_latent, qdtype):
    m = x.shape[0]
    row = lambda i: (i, 0)
    a_width = A_HEADS * V_HEAD
    ob_w = ob.shape[1]
    in_specs = [
        pl.BlockSpec((tm, D_MODEL), row), pl.BlockSpec((tm, M_WIDTH), row), pl.BlockSpec((tm, ob_w), row),
        pl.BlockSpec((tm, 2 * D_MODEL), row),
        _const_spec((M_WIDTH, D_MODEL)), _const_spec((a_width, D_MODEL)), _const_spec((D_MODEL, D_MODEL)),
        _const_spec((1, D_MODEL)), _const_spec((1, D_MODEL)), _const_spec((D_MODEL, X_WIDTH)),
    ]
    args = [x, hg, ob, sg, w["w_branch_a"], w["w_branch_b"], w["w_out"], w["g_post_mix"], w["g_pre_mem"], w["w_mq"]]
    if from_latent:
        in_specs += [_const_spec((A_HEADS, KV_LORA, a_width))]
        args += [w["wuv_heads"]]
    return pl.pallas_call(
        functools.partial(_merge_kernel, from_latent=from_latent),
        grid=(m // tm,), in_specs=in_specs,
        out_specs=[pl.BlockSpec((tm, D_MODEL), row), pl.BlockSpec((tm, X_WIDTH), row)],
        out_shape=[jax.ShapeDtypeStruct((m, D_MODEL), F32), jax.ShapeDtypeStruct((m, X_WIDTH), qdtype)],
        compiler_params=_params(1), name="merge_lat" if from_latent else "merge",
    )(*args)


def _memkv_kernel(mem_ref, g_ref, wk_ref, wv_ref, k_ref, v_ref):
    mn = _rms(mem_ref[...], g_ref[...]).astype(BF16)
    k_ref[...] = jnp.dot(mn, wk_ref[...], preferred_element_type=F32)
    v_ref[...] = jnp.dot(mn, wv_ref[...], preferred_element_type=F32)


def _memkv(mem, w, *, tm):
    m = mem.shape[0]
    row = lambda i: (i, 0)
    return pl.pallas_call(
        _memkv_kernel, grid=(m // tm,),
        in_specs=[pl.BlockSpec((tm, D_MODEL), row), _const_spec((1, D_MODEL)),
                  _const_spec((D_MODEL, X_WIDTH)), _const_spec((D_MODEL, X_WIDTH))],
        out_specs=[pl.BlockSpec((tm, X_WIDTH), row)] * 2,
        out_shape=[jax.ShapeDtypeStruct((m, X_WIDTH), F32)] * 2,
        compiler_params=_params(1), name="memkv",
    )(mem, w["g_mem"], w["w_mk"], w["w_mv"])


def _memattn_kernel(q_ref, k_ref, v_ref, o_ref):
    kb = k_ref[...].astype(BF16)
    vb = v_ref[...].astype(BF16)
    for h in range(X_HEADS):
        sl = slice(X_HEAD_DIM * h, X_HEAD_DIM * (h + 1))
        s = _dot_nt(q_ref[:, sl], kb[:, sl])
        p = jnp.exp(s - jnp.max(s, axis=1, keepdims=True))
        l = jnp.sum(p, axis=1, keepdims=True)
        o_ref[:, sl] = (jnp.dot(p.astype(BF16), vb[:, sl], preferred_element_type=F32) / l).astype(BF16)


def _memattn(qm, mem_k, mem_v, *, batch, seq, tm):
    nt = seq // tm
    tok = lambda b, i: (b * nt + i, 0)
    kv = lambda b, i: (b, 0)
    return pl.pallas_call(
        _memattn_kernel, grid=(batch, nt),
        in_specs=[pl.BlockSpec((tm, X_WIDTH), tok), pl.BlockSpec((N_MEM, X_WIDTH), kv),
                  pl.BlockSpec((N_MEM, X_WIDTH), kv)],
        out_specs=pl.BlockSpec((tm, X_WIDTH), tok),
        out_shape=jax.ShapeDtypeStruct((batch * seq, X_WIDTH), BF16),
        compiler_params=_params(2), name="memattn",
    )(qm, mem_k, mem_v)


def _memattn_step_kernel(q_ref, k_ref, v_ref, o_ref, *, rows):
    for g in range(rows):
        q4 = [q_ref[g:g + 1, X_HEAD_DIM * h:X_HEAD_DIM * (h + 1)] for h in range(X_HEADS)]
        q8 = jnp.concatenate(q4 + q4, axis=0)
        s = jnp.sum(k_ref[g] * q8[None], axis=2, keepdims=True)
        mx = jnp.max(s, axis=0)
        mx = jnp.maximum(mx, pltpu.roll(mx, X_HEADS, 0))
        p = jnp.exp(s - mx[None])
        l8 = jnp.sum(p, axis=0)
        o8 = jnp.sum(p * v_ref[g], axis=0)
        o_ref[g] = (o8[0:X_HEADS] + o8[X_HEADS:]) / (l8[0:X_HEADS] + l8[X_HEADS:])


def _memattn_step(qm, mem_k, mem_v, *, rows=8):
    nb = qm.shape[0]
    row = lambda i: (i, 0)
    kv = lambda i: (i, 0, 0, 0)
    kv_block = (rows, N_MEM // 2, 2 * X_HEADS, X_HEAD_DIM)
    return pl.pallas_call(
        functools.partial(_memattn_step_kernel, rows=rows), grid=(nb // rows,),
        in_specs=[pl.BlockSpec((rows, X_WIDTH), row), pl.BlockSpec(kv_block, kv), pl.BlockSpec(kv_block, kv)],
        out_specs=pl.BlockSpec((rows, X_HEADS, X_HEAD_DIM), lambda i: (i, 0, 0)),
        out_shape=jax.ShapeDtypeStruct((nb, X_HEADS, X_HEAD_DIM), F32),
        compiler_params=_params(1), name="memattn_step",
    )(qm, mem_k, mem_v).reshape(nb, X_WIDTH)


def _ffn_kernel(x1_ref, o_ref, wmo_ref, gpm_ref, gpf_ref, win_ref, wdn_ref, gpost_ref, y_ref, *, f_chunk):
    x2 = x1_ref[...] + _rms(_dot(o_ref[...], wmo_ref[...]), gpm_ref[...])
    h = _rms(x2, gpf_ref[...]).astype(BF16)
    acc = jnp.zeros(x2.shape, F32)
    for c in range(D_FF // f_chunk):
        sl = slice(f_chunk * c, f_chunk * (c + 1))
        g = jnp.dot(h, win_ref[:, sl], preferred_element_type=F32)
        u = jnp.dot(h, win_ref[:, D_FF + f_chunk * c:D_FF + f_chunk * (c + 1)], preferred_element_type=F32)
        acc = acc + jnp.dot((g * jax.nn.sigmoid(g) * u).astype(BF16), wdn_ref[sl, :], preferred_element_type=F32)
    y_ref[...] = x2 + _rms(acc, gpost_ref[...])


def _ffn(x1, o, w, *, tm, f_chunk=2816):
    m = x1.shape[0]
    row = lambda i: (i, 0)
    return pl.pallas_call(
        functools.partial(_ffn_kernel, f_chunk=f_chunk), grid=(m // tm,),
        in_specs=[pl.BlockSpec((tm, D_MODEL), row), pl.BlockSpec((tm, X_WIDTH), row),
                  _const_spec((X_WIDTH, D_MODEL)), _const_spec((1, D_MODEL)), _const_spec((1, D_MODEL)),
                  _const_spec((D_MODEL, 2 * D_FF)), _const_spec((D_FF, D_MODEL)),
                  _const_spec((1, D_MODEL))],
        out_specs=pl.BlockSpec((tm, D_MODEL), row),
        out_shape=jax.ShapeDtypeStruct((m, D_MODEL), F32),
        compiler_params=_params(1), name="ffn",
    )(x1, o, w["w_mo"], w["g_post_mem"], w["g_pre_ffn"], w["w_ffn_in"], w["w_ffn_out"], w["g_post_ffn"])


def _prep_weights(p):
    w = {}
    for name in ("g_pre_mix", "g_q_lora", "g_kv_lora", "g_mhead", "g_post_mix", "g_pre_mem", "g_mem", "g_post_mem",
                 "g_pre_ffn", "g_post_ffn"):
        w[name] = p[name].reshape(1, -1).astype(F32)
    w_in = p["w_in"]
    offs = np.cumsum((0, M_WIDTH, M_WIDTH, M_WIDTH, M_HEADS, M_HEADS, M_WIDTH, Q_LORA, KV_LORA, QK_ROPE,
                      D_MODEL, D_MODEL))
    seg = lambda i: w_in[:, offs[i]:offs[i + 1]]
    mq, mk, mv, mi, mf, mo, cq, ckv, kr, ga, gb = (seg(i) for i in range(11))
    half = QK_ROPE // 2
    swap = lambda a: jnp.concatenate([a[..., half:], a[..., :half]], axis=-1)
    tail = jnp.concatenate([kr, swap(kr), mi, mf,
                            jnp.zeros((D_MODEL, LANES - 2 * QK_ROPE - 2 * M_HEADS), w_in.dtype)], axis=1)
    w["w_qkv"] = w_in[:, offs[0]:offs[3]].astype(BF16)
    w["w_mo_gate"] = mo.astype(BF16)
    w["w_lat"] = w_in[:, offs[6]:offs[8]].astype(BF16)
    w["w_g"] = w_in[:, offs[9]:offs[11]].astype(BF16)
    w["w_tail"] = tail.astype(BF16)
    w["wv_t"] = mv.T.astype(BF16)
    w["b_tail"] = jnp.zeros((1, LANES), F32).at[0, TAIL_IG:TAIL_IG + 2 * M_HEADS].set(p["b_if"].astype(F32))

    pad = HEAD_SLOT - QK_NOPE - QK_ROPE
    wq = p["w_uq"].reshape(Q_LORA, A_HEADS, QK_NOPE + QK_ROPE)
    zq = jnp.zeros((Q_LORA, A_HEADS, pad), wq.dtype)
    w["wq_cat"] = jnp.concatenate([wq, zq], axis=2).reshape(Q_LORA, A_HEADS * HEAD_SLOT).astype(BF16)
    w_uk, w_uv = p["w_uk"], p["w_uv"]
    zk = jnp.zeros((KV_LORA, A_HEADS, HEAD_SLOT - QK_NOPE), w_uk.dtype)
    w["wuk_cat"] = jnp.concatenate([w_uk, zk], axis=2).reshape(KV_LORA, A_HEADS * HEAD_SLOT).astype(BF16)
    w["wuv_cat"] = jnp.concatenate([w_uv, zk], axis=2).reshape(KV_LORA, A_HEADS * HEAD_SLOT).astype(BF16)
    w["wuv_heads"] = (w_uv.transpose(1, 0, 2)[:, :, None, :]
                      * jnp.eye(A_HEADS, dtype=w_uv.dtype)[:, None, :, None]).reshape(
                          A_HEADS, KV_LORA, A_HEADS * V_HEAD).astype(BF16)
    pk = np.zeros((LANES, A_HEADS, HEAD_SLOT), np.float32)
    for r in range(QK_ROPE):
        pk[r, :, QK_NOPE + r] = 1.0
    w["p_kr"] = jnp.asarray(pk.reshape(LANES, A_HEADS * HEAD_SLOT), BF16)
    pass_np = np.zeros((A_HEADS, HEAD_SLOT - QK_NOPE, ABS_SLOT), np.float32)
    pass_np[:, np.arange(QK_ROPE), KV_LORA + np.arange(QK_ROPE)] = 1.0
    w["w_abs"] = jnp.concatenate(
        [jnp.concatenate([w_uk.transpose(1, 2, 0), jnp.zeros((A_HEADS, QK_NOPE, ABS_SLOT - KV_LORA), w_uk.dtype)],
                         axis=2).astype(BF16), jnp.asarray(pass_np, BF16)], axis=1)
    for name in ("w_branch_a", "w_branch_b", "w_out", "w_mq", "w_mk", "w_mv", "w_mo", "w_ffn_out"):
        w[name] = p[name].astype(BF16)
    w["w_ffn_in"] = p["w_ffn_in"].astype(BF16)
    return w


def _rope_tables(pos0, n):
    pos = (pos0 + jnp.arange(n)).astype(F32)
    inv = ROPE_BASE ** (-jnp.arange(0, QK_ROPE, 2, dtype=F32) / QK_ROPE)
    ang = pos[:, None] * inv[None, :]
    cos, sin = jnp.cos(ang), jnp.sin(ang)
    c32 = jnp.concatenate([cos, cos], axis=1)
    s32 = jnp.concatenate([-sin, sin], axis=1)
    one = lambda k: jnp.ones((n, k), F32)
    zero = lambda k: jnp.zeros((n, k), F32)
    ct = jnp.concatenate([c32, one(LANES - QK_ROPE)], axis=1)
    st = jnp.concatenate([s32, zero(LANES - QK_ROPE)], axis=1)
    c128 = jnp.concatenate([one(QK_NOPE), c32, one(HEAD_SLOT - QK_NOPE - QK_ROPE)], axis=1)
    s128 = jnp.concatenate([zero(QK_NOPE), s32, zero(HEAD_SLOT - QK_NOPE - QK_ROPE)], axis=1)
    return ct, st, c128, s128


def kernel(x_prompt, x_sample, cache_ckv, cache_krope, cache_mem_k, cache_mem_v, state_C, state_n, state_m, page_table, mem_prompt, g_pre_mix, w_in, b_if, g_mhead, g_q_lora, w_uq, g_kv_lora, w_uk, w_uv, w_branch_a, w_branch_b, w_out, g_post_mix, g_pre_mem, g_mem, w_mq, w_mk, w_mv, w_mo, g_post_mem, g_pre_ffn, w_ffn_in, w_ffn_out, g_post_ffn):
    params = dict(g_pre_mix=g_pre_mix, w_in=w_in, b_if=b_if, g_mhead=g_mhead, g_q_lora=g_q_lora, w_uq=w_uq,
                  g_kv_lora=g_kv_lora, w_uk=w_uk, w_uv=w_uv, w_branch_a=w_branch_a, w_branch_b=w_branch_b,
                  w_out=w_out, g_post_mix=g_post_mix, g_pre_mem=g_pre_mem, g_mem=g_mem, w_mq=w_mq, w_mk=w_mk,
                  w_mv=w_mv, w_mo=w_mo, g_post_mem=g_post_mem, g_pre_ffn=g_pre_ffn, w_ffn_in=w_ffn_in,
                  w_ffn_out=w_ffn_out, g_post_ffn=g_post_ffn)
    depth = w_in.shape[0]
    assert depth == 1, "single-layer stack only"
    bp, seq, _ = x_prompt.shape
    bs, dec_seq, _ = x_sample.shape
    assert dec_seq == 1, "one new token per sample sequence"
    past_len = page_table.shape[1] * PAGE_SIZE
    w = _prep_weights({name: a[0] for name, a in params.items()})

    xp = x_prompt.reshape(bp * seq, D_MODEL)
    qkvm, og, ckv_p, sg, krope_t, gates_t, qcat, kcat, vcat, vt = _inproj(
        xp, w, _rope_tables(0, seq), tm=512, absorbed=False, mdtype=BF16)
    hg, c_p, n_p, m_p = _mlstm_prompt(qkvm, vt, gates_t, og, w["g_mhead"], batch=bp, seq=seq)
    ob = _mla_prefill(qcat, kcat, vcat, batch=bp, seq=seq, blk=512)
    x1, qm = _merge(xp, hg, ob, sg, w, tm=512, from_latent=False, qdtype=BF16)
    mem_k, mem_v = _memkv(mem_prompt.reshape(bp * N_MEM, D_MODEL), w, tm=512)
    om = _memattn(qm, mem_k, mem_v, batch=bp, seq=seq, tm=512)
    y_prompt = _ffn(x1, om, w, tm=512).reshape(bp, seq, D_MODEL)

    xs = x_sample.reshape(bs, D_MODEL)
    tables_s = tuple(jnp.broadcast_to(t, (bs, LANES)) for t in _rope_tables(past_len, 1))
    qkvm_s, og_s, ckv_s, sg_s, tail_s, qabs = _inproj(xs, w, tables_s, tm=bs, absorbed=True, mdtype=F32)
    m0_pad = jnp.pad(state_m.reshape(bs, M_HEADS).astype(F32), ((0, 0), (0, LANES - M_HEADS)))
    hg_s, c_s, n_s, m_s = _mlstm_step(qkvm_s, tail_s, state_C.reshape(bs, M_HEADS, M_HEAD_DIM, M_HEAD_DIM),
                                      state_n.reshape(bs, M_HEADS, M_HEAD_DIM), m0_pad, og_s, w["g_mhead"])
    n_phys = cache_ckv.shape[1]
    cache_krope_t = jnp.transpose(cache_krope.reshape(n_phys, PAGE_SIZE, QK_ROPE), (0, 2, 1))
    o_lat = _mla_decode(page_table, qabs, cache_ckv.reshape(n_phys, PAGE_SIZE, KV_LORA), cache_krope_t, ckv_s, tail_s)
    x1_s, qm_s = _merge(xs, hg_s, o_lat.reshape(bs, A_HEADS * KV_LORA), sg_s, w, tm=bs, from_latent=True, qdtype=F32)
    om_s = _memattn_step(qm_s, cache_mem_k.reshape(bs, N_MEM // 2, 2 * X_HEADS, X_HEAD_DIM),
                         cache_mem_v.reshape(bs, N_MEM // 2, 2 * X_HEADS, X_HEAD_DIM))
    y_sample = _ffn(x1_s, om_s, w, tm=bs).reshape(bs, 1, D_MODEL)

    return (y_prompt, y_sample,
            ckv_p.reshape(1, bp, seq, KV_LORA), jnp.transpose(krope_t, (0, 2, 1)).reshape(1, bp, seq, QK_ROPE),
            c_p.reshape(1, bp, M_HEADS, M_HEAD_DIM, M_HEAD_DIM), n_p.reshape(1, bp, M_HEADS, M_HEAD_DIM),
            m_p[:, :, 0].reshape(1, bp, M_HEADS),
            mem_k.reshape(1, bp, N_MEM, X_HEADS, X_HEAD_DIM), mem_v.reshape(1, bp, N_MEM, X_HEADS, X_HEAD_DIM),
            ckv_s.reshape(1, bs, 1, KV_LORA), tail_s[:, :QK_ROPE].reshape(1, bs, 1, QK_ROPE),
            c_s.reshape(1, bs, M_HEADS, M_HEAD_DIM, M_HEAD_DIM), n_s.reshape(1, bs, M_HEADS, M_HEAD_DIM),
            m_s[:, :M_HEADS].reshape(1, bs, M_HEADS))
```

```python
import functools

import jax
import jax.numpy as jnp
import numpy as np
from jax import lax
from jax.experimental import pallas as pl
from jax.experimental.pallas import tpu as pltpu

F32 = jnp.float32
BF16 = jnp.bfloat16

D_MODEL = 1024
PAGE_SIZE = 128
M_HEADS = 4
M_HEAD_DIM = 128
M_WIDTH = M_HEADS * M_HEAD_DIM
M_CHUNK = 128
A_HEADS = 8
QK_NOPE = 64
QK_ROPE = 32
V_HEAD = 64
Q_LORA = 384
KV_LORA = 256
ROPE_BASE = 10000.0
N_MEM = 256
X_HEADS = 4
X_HEAD_DIM = 128
X_WIDTH = X_HEADS * X_HEAD_DIM
D_FF = 2816
EPS = 1e-6

LANES = 128
HEAD_SLOT = 128
ABS_SLOT = 384
ATT_SCALE = (QK_NOPE + QK_ROPE) ** -0.5
VMEM_LIMIT = 52 * 1024 * 1024

TAIL_IG = 64
TAIL_LF = 68


def _rms(x, g):
    return x * lax.rsqrt(jnp.mean(x * x, axis=-1, keepdims=True) + EPS) * g


def _dot(a, b):
    return jnp.dot(a.astype(BF16), b.astype(BF16), preferred_element_type=F32)


def _dot_nt(a, b):
    return lax.dot_general(a.astype(BF16), b.astype(BF16), (((1,), (1,)), ((), ())),
                           preferred_element_type=F32)


def _dot_tn(a, b):
    return lax.dot_general(a.astype(BF16), b.astype(BF16), (((0,), (0,)), ((), ())),
                           preferred_element_type=F32)


def _log_sigmoid(x):
    return jnp.minimum(x, 0.0) - jnp.log1p(jnp.exp(-jnp.abs(x)))


def _const_spec(shape):
    nd = len(shape)
    return pl.BlockSpec(shape, lambda *_: (0,) * nd, pipeline_mode=pl.Buffered(1))


def _params(n_axes):
    return pltpu.CompilerParams(dimension_semantics=("arbitrary",) * n_axes, vmem_limit_bytes=VMEM_LIMIT)


def _inproj_kernel(x_ref, gpre_ref, wqkv_ref, wmo_ref, wlat_ref, wg_ref, wtail_ref, btail_ref, ct_ref, st_ref,
                   gq_ref, wq_ref, c128_ref, s128_ref, gkv_ref, *rest, absorbed):
    if absorbed:
        wabs_ref, qkvm_ref, og_ref, ckv_ref, sg_ref, tail_ref, qabs_ref = rest
    else:
        (wuk_ref, wuv_ref, wvt_ref,
         qkvm_ref, og_ref, ckv_ref, sg_ref, krt_ref, gt_ref, qcat_ref, kcat_ref, vcat_ref, vt_ref) = rest
    h = _rms(x_ref[...], gpre_ref[...]).astype(BF16)

    zm = jnp.dot(h, wqkv_ref[...], preferred_element_type=F32)
    qkvm_ref[:, 0:M_WIDTH] = zm[:, 0:M_WIDTH].astype(qkvm_ref.dtype)
    qkvm_ref[:, M_WIDTH:2 * M_WIDTH] = (zm[:, M_WIDTH:2 * M_WIDTH] * (M_HEAD_DIM ** -0.5)).astype(qkvm_ref.dtype)
    qkvm_ref[:, 2 * M_WIDTH:] = zm[:, 2 * M_WIDTH:].astype(qkvm_ref.dtype)
    og_ref[...] = jax.nn.sigmoid(jnp.dot(h, wmo_ref[...], preferred_element_type=F32))
    sg_ref[...] = jax.nn.sigmoid(jnp.dot(h, wg_ref[...], preferred_element_type=F32))

    t = jnp.dot(h, wtail_ref[...], preferred_element_type=F32) + btail_ref[...]
    t = t * ct_ref[...] + pltpu.roll(t, LANES - QK_ROPE, 1) * st_ref[...]
    lane = lax.broadcasted_iota(jnp.int32, t.shape, 1)
    is_lf = jnp.logical_and(lane >= TAIL_LF, lane < TAIL_LF + M_HEADS)
    t = jnp.where(is_lf, _log_sigmoid(t), t)
    if absorbed:
        tail_ref[...] = t
    else:
        tt = jnp.transpose(t)
        krt_ref[0] = tt[0:QK_ROPE]
        gt_ref[...] = tt[TAIL_IG:TAIL_IG + 2 * M_HEADS]

    cqn = _rms(jnp.dot(h, wlat_ref[:, 0:Q_LORA], preferred_element_type=F32), gq_ref[...]).astype(BF16)
    ckvn = _rms(jnp.dot(h, wlat_ref[:, Q_LORA:], preferred_element_type=F32), gkv_ref[...])
    ckv_ref[...] = ckvn

    qc = jnp.dot(cqn, wq_ref[...], preferred_element_type=F32)
    width = A_HEADS * HEAD_SLOT
    first_half = (lax.broadcasted_iota(jnp.int32, qc.shape, 1) % HEAD_SLOT) < QK_NOPE + QK_ROPE // 2
    qs = jnp.where(first_half, pltpu.roll(qc, width - QK_ROPE // 2, 1), pltpu.roll(qc, QK_ROPE // 2, 1))
    c128 = c128_ref[...]
    s128 = s128_ref[...]
    for hh in range(A_HEADS):
        sl = slice(HEAD_SLOT * hh, HEAD_SLOT * (hh + 1))
        qh = ((qc[:, sl] * c128 + qs[:, sl] * s128) * ATT_SCALE).astype(BF16)
        if absorbed:
            qabs_ref[:, ABS_SLOT * hh:ABS_SLOT * (hh + 1)] = jnp.dot(
                qh, wabs_ref[hh], preferred_element_type=F32).astype(BF16)
        else:
            qcat_ref[:, sl] = qh
    if not absorbed:
        cb = ckvn.astype(BF16)
        kr_slot = jnp.where(jnp.logical_and(lane >= QK_NOPE, lane < QK_NOPE + QK_ROPE), pltpu.roll(t, QK_NOPE, 1), 0.0)
        kn = jnp.dot(cb, wuk_ref[...], preferred_element_type=F32)
        for hh in range(A_HEADS):
            sl = slice(HEAD_SLOT * hh, HEAD_SLOT * (hh + 1))
            kcat_ref[:, sl] = (kn[:, sl] + kr_slot).astype(BF16)
        vlane = lax.broadcasted_iota(jnp.int32, (1, A_HEADS * HEAD_SLOT), 1) % HEAD_SLOT
        vcat = jnp.dot(cb, wuv_ref[...], preferred_element_type=F32)
        vcat_ref[...] = jnp.where(vlane == V_HEAD, 1.0, vcat).astype(BF16)
        vt_ref[...] = _dot_nt(wvt_ref[...], h).astype(BF16)


def _inproj(x, w, tables, *, tm, absorbed, mdtype):
    m = x.shape[0]
    ct, st, c128, s128 = tables
    seq = ct.shape[0]
    nt = seq // tm
    grid = (m // tm,)
    row = lambda i: (i, 0)
    tab = lambda i: (i % nt, 0)
    wide = A_HEADS * HEAD_SLOT
    in_specs = [
        pl.BlockSpec((tm, D_MODEL), row), _const_spec((1, D_MODEL)),
        _const_spec((D_MODEL, 3 * M_WIDTH)), _const_spec((D_MODEL, M_WIDTH)), _const_spec((D_MODEL, Q_LORA + KV_LORA)),
        _const_spec((D_MODEL, 2 * D_MODEL)), _const_spec((D_MODEL, LANES)), _const_spec((1, LANES)),
        pl.BlockSpec((tm, LANES), tab), pl.BlockSpec((tm, LANES), tab),
        _const_spec((1, Q_LORA)), _const_spec((Q_LORA, wide)),
        pl.BlockSpec((tm, LANES), tab), pl.BlockSpec((tm, LANES), tab),
        _const_spec((1, KV_LORA)),
    ]
    args = [x, w["g_pre_mix"], w["w_qkv"], w["w_mo_gate"], w["w_lat"], w["w_g"], w["w_tail"], w["b_tail"], ct, st,
            w["g_q_lora"], w["wq_cat"], c128, s128, w["g_kv_lora"]]
    outs = [
        (jax.ShapeDtypeStruct((m, 3 * M_WIDTH), mdtype), pl.BlockSpec((tm, 3 * M_WIDTH), row)),
        (jax.ShapeDtypeStruct((m, M_WIDTH), F32), pl.BlockSpec((tm, M_WIDTH), row)),
        (jax.ShapeDtypeStruct((m, KV_LORA), F32), pl.BlockSpec((tm, KV_LORA), row)),
        (jax.ShapeDtypeStruct((m, 2 * D_MODEL), F32), pl.BlockSpec((tm, 2 * D_MODEL), row)),
    ]
    if absorbed:
        in_specs += [_const_spec((A_HEADS, HEAD_SLOT, ABS_SLOT))]
        args += [w["w_abs"]]
        outs += [(jax.ShapeDtypeStruct((m, LANES), F32), pl.BlockSpec((tm, LANES), row)),
                 (jax.ShapeDtypeStruct((m, A_HEADS * ABS_SLOT), BF16), pl.BlockSpec((tm, A_HEADS * ABS_SLOT), row))]
    else:
        in_specs += [_const_spec((KV_LORA, wide)), _const_spec((KV_LORA, wide)), _const_spec((M_WIDTH, D_MODEL))]
        args += [w["wuk_cat"], w["wuv_cat"], w["wv_t"]]
        outs += [(jax.ShapeDtypeStruct((m // seq, QK_ROPE, seq), F32),
                  pl.BlockSpec((1, QK_ROPE, tm), lambda i: (i // nt, 0, i % nt))),
                 (jax.ShapeDtypeStruct((2 * M_HEADS, m), F32), pl.BlockSpec((2 * M_HEADS, tm), lambda i: (0, i)))]
        outs += [(jax.ShapeDtypeStruct((m, wide), BF16), pl.BlockSpec((tm, wide), row))] * 3
        outs += [(jax.ShapeDtypeStruct((M_WIDTH, m), BF16), pl.BlockSpec((M_WIDTH, tm), lambda i: (0, i)))]
    return pl.pallas_call(
        functools.partial(_inproj_kernel, absorbed=absorbed),
        grid=grid, in_specs=in_specs,
        out_specs=[o[1] for o in outs], out_shape=[o[0] for o in outs],
        compiler_params=_params(1), name="inproj_abs" if absorbed else "inproj",
    )(*args)


def _split3_dot(x, m01):
    hi = x.astype(BF16)
    r1 = x - hi.astype(F32)
    mid = r1.astype(BF16)
    lo = (r1 - mid.astype(F32)).astype(BF16)
    return sum(jnp.dot(part, m01, preferred_element_type=F32) for part in (hi, mid, lo))


def _cumsum_lanes(x, upper):
    return _split3_dot(x, jnp.where(upper, 1.0, 0.0).astype(BF16))


def _cummax_lanes(x):
    lane = lax.broadcasted_iota(jnp.int32, x.shape, 1)
    shift = 1
    while shift < x.shape[1]:
        x = jnp.maximum(x, jnp.where(lane >= shift, pltpu.roll(x, shift, 1), -jnp.inf))
        shift *= 2
    return x


def _mlstm_gates_kernel(g_ref, urow_ref, scal_ref, cols_ref, *, nc, nseq):
    L = M_CHUNK
    H = M_HEADS
    rows = nseq * nc * 2 * H
    g = g_ref[...]
    r = jnp.concatenate([g[:, L * c:L * (c + 1)] for c in range(nseq * nc)], axis=0)
    ri = lax.broadcasted_iota(jnp.int32, (rows, L), 0)
    ci = lax.broadcasted_iota(jnp.int32, (rows, L), 1)
    top = (ri % (2 * H)) < H
    b_all = _cumsum_lanes(r, lax.broadcasted_iota(jnp.int32, (L, L), 0) <= lax.broadcasted_iota(jnp.int32, (L, L), 1))
    b = pltpu.roll(b_all, rows - H, 0)
    u = r - b
    cm = _cummax_lanes(u)
    g_last = jnp.max(jnp.where(ci == L - 1, b, -jnp.inf), axis=1, keepdims=True)
    wlog = g_last - b + r
    wmax = jnp.max(wlog, axis=1, keepdims=True)
    urow_ref[...] = jnp.where(top, u, pltpu.roll(wlog, H, 0)).reshape(nseq, nc, 2 * H, L)
    scal = jnp.where(top, jnp.broadcast_to(g_last, (rows, L)), pltpu.roll(jnp.broadcast_to(wmax, (rows, L)), H, 0))
    scal_ref[...] = scal.reshape(nseq, nc, 2 * H, L)
    first = jnp.where(top, b, pltpu.roll(cm, H, 0))
    for c in range(nseq * nc):
        sl = slice(2 * H * c, 2 * H * (c + 1))
        cols_ref[c // nc, c % nc] = jnp.transpose(jnp.concatenate([first[sl], wlog[sl]], axis=0))


def _mlstm_gates(g, *, batch, nc, nseq=4):
    blk = lambda b: (b, 0, 0, 0)
    return pl.pallas_call(
        functools.partial(_mlstm_gates_kernel, nc=nc, nseq=nseq),
        grid=(batch // nseq,),
        in_specs=[pl.BlockSpec((2 * M_HEADS, nseq * nc * M_CHUNK), lambda b: (0, b))],
        out_specs=[pl.BlockSpec((nseq, nc, 2 * M_HEADS, M_CHUNK), blk),
                   pl.BlockSpec((nseq, nc, 2 * M_HEADS, M_CHUNK), blk),
                   pl.BlockSpec((nseq, nc, M_CHUNK, 4 * M_HEADS), blk)],
        out_shape=[jax.ShapeDtypeStruct((batch, nc, 2 * M_HEADS, M_CHUNK), F32),
                   jax.ShapeDtypeStruct((batch, nc, 2 * M_HEADS, M_CHUNK), F32),
                   jax.ShapeDtypeStruct((batch, nc, M_CHUNK, 4 * M_HEADS), F32)],
        compiler_params=_params(1), name="mlstm_gates",
    )(g)


def _mlstm_kernel(*refs, nb):
    qkv_ref = refs[0]
    vt_refs = refs[1:1 + nb]
    urow_ref, scal_ref, cols_ref, sel_ref, og_ref, gm_ref, hg_ref, c_ref, n_ref, m_ref = refs[1 + nb:]
    L = M_CHUNK

    @pl.when(pl.program_id(1) == 0)
    def _():
        c_ref[...] = jnp.zeros_like(c_ref)
        n_ref[...] = jnp.zeros_like(n_ref)
        m_ref[...] = jnp.zeros_like(m_ref)

    row = lax.broadcasted_iota(jnp.int32, (L, L), 0)
    col = lax.broadcasted_iota(jnp.int32, (L, L), 1)
    tril = col <= row

    for bb in range(nb):
        vt_ref = vt_refs[bb]
        rows8 = urow_ref[bb, 0]
        u4 = rows8[0:M_HEADS]
        wlog4 = rows8[M_HEADS:2 * M_HEADS]
        scal = scal_ref[bb, 0]
        g_last4 = scal[0:M_HEADS, 0:1]
        wmax4 = scal[M_HEADS:2 * M_HEADS, 0:1]
        cols = _split3_dot(cols_ref[bb, 0], sel_ref[...])
        m_prev4 = m_ref[bb][:, 0:1]
        n_prev4 = n_ref[bb]
        m_new4 = jnp.maximum(g_last4 + m_prev4, wmax4)
        decay4 = jnp.exp(g_last4 + m_prev4 - m_new4)
        ws4 = jnp.exp(wlog4 - m_new4)
        m_ref[bb] = jnp.broadcast_to(m_new4, (M_HEADS, LANES))
        n_rows = []

        for h in range(M_HEADS):
            sl = slice(M_HEAD_DIM * h, M_HEAD_DIM * (h + 1))
            q = qkv_ref[bb, :, sl]
            k = qkv_ref[bb, :, M_WIDTH + M_HEAD_DIM * h:M_WIDTH + M_HEAD_DIM * (h + 1)]
            v = qkv_ref[bb, :, 2 * M_WIDTH + M_HEAD_DIM * h:2 * M_WIDTH + M_HEAD_DIM * (h + 1)]
            m_prev = m_prev4[h:h + 1]
            c_prev = c_ref[bb, h]
            n_prev = n_prev4[h:h + 1]
            b_col = cols[:, LANES * h:LANES * (h + 1)]
            mm_col = jnp.maximum(cols[:, LANES * (M_HEADS + h):LANES * (M_HEADS + h + 1)], m_prev)
            w_intra = jnp.where(tril, jnp.exp(u4[h:h + 1] - mm_col), 0.0)
            w_inter = jnp.exp(m_prev - mm_col)
            s = _dot_nt(q, k) * w_intra
            cn = jnp.concatenate([c_prev.astype(BF16), jnp.broadcast_to(n_prev.astype(BF16), (L, M_HEAD_DIM))],
                                 axis=0)
            qcn = _dot_nt(q, cn)
            num = w_inter * qcn[:, 0:M_HEAD_DIM] + _dot(s, v)
            den = w_inter * qcn[:, M_HEAD_DIM:] + jnp.sum(s, axis=1, keepdims=True)
            hs = num / jnp.maximum(jnp.abs(den), jnp.exp(-(b_col + mm_col)))

            ws = ws4[h:h + 1]
            lhs = jnp.concatenate([(vt_ref[sl, :].astype(F32) * ws).astype(BF16),
                                   jnp.broadcast_to(ws.astype(BF16), (16, L))], axis=0)
            upd = jnp.dot(lhs, k, preferred_element_type=F32)
            decay = decay4[h:h + 1]
            c_ref[bb, h] = decay * c_prev + upd[0:M_HEAD_DIM]
            n_rows.append(decay * n_prev + upd[M_HEAD_DIM:M_HEAD_DIM + 1])

            mu = jnp.mean(hs, axis=1, keepdims=True)
            d = hs - mu
            y = d * lax.rsqrt(jnp.mean(d * d, axis=1, keepdims=True) + EPS) * gm_ref[:, sl]
            hg_ref[bb, :, sl] = (y * og_ref[bb, :, sl]).astype(BF16)
        n_ref[bb] = jnp.concatenate(n_rows, axis=0)


def _mlstm_prompt(qkvm, vt, gates, og, g_mhead, *, batch, seq, nb=8):
    nc = seq // M_CHUNK
    tok = lambda b, c: (b, c, 0)
    chunk = lambda b, c: (b, c, 0, 0)
    urow, scal, cols = _mlstm_gates(gates, batch=batch, nc=nc)
    sel = jnp.asarray(np.kron(np.eye(4 * M_HEADS, 2 * M_HEADS), np.ones((1, LANES))), BF16)
    vt_specs = [pl.BlockSpec((M_WIDTH, M_CHUNK), functools.partial(lambda b, c, bb: (0, (nb * b + bb) * nc + c), bb=bb))
                for bb in range(nb)]
    hg, c_p, n_p, m_p = pl.pallas_call(
        functools.partial(_mlstm_kernel, nb=nb),
        grid=(batch // nb, nc),
        in_specs=[pl.BlockSpec((nb, M_CHUNK, 3 * M_WIDTH), tok)] + vt_specs + [
            pl.BlockSpec((nb, 1, 2 * M_HEADS, M_CHUNK), chunk),
            pl.BlockSpec((nb, 1, 2 * M_HEADS, M_CHUNK), chunk),
            pl.BlockSpec((nb, 1, M_CHUNK, 4 * M_HEADS), chunk),
            pl.BlockSpec((4 * M_HEADS, 2 * M_HEADS * LANES), lambda b, c: (0, 0)),
            pl.BlockSpec((nb, M_CHUNK, M_WIDTH), tok),
            pl.BlockSpec((1, M_WIDTH), lambda b, c: (0, 0)),
        ],
        out_specs=[
            pl.BlockSpec((nb, M_CHUNK, M_WIDTH), tok),
            pl.BlockSpec((nb, M_HEADS, M_HEAD_DIM, M_HEAD_DIM), lambda b, c: (b, 0, 0, 0)),
            pl.BlockSpec((nb, M_HEADS, M_HEAD_DIM), lambda b, c: (b, 0, 0)),
            pl.BlockSpec((nb, M_HEADS, LANES), lambda b, c: (b, 0, 0)),
        ],
        out_shape=[
            jax.ShapeDtypeStruct((batch, seq, M_WIDTH), BF16),
            jax.ShapeDtypeStruct((batch, M_HEADS, M_HEAD_DIM, M_HEAD_DIM), F32),
            jax.ShapeDtypeStruct((batch, M_HEADS, M_HEAD_DIM), F32),
            jax.ShapeDtypeStruct((batch, M_HEADS, LANES), F32),
        ],
        compiler_params=_params(2), name="mlstm_prompt",
    )(qkvm.reshape(batch, seq, 3 * M_WIDTH), *([vt] * nb), urow, scal, cols, sel,
      og.reshape(batch, seq, M_WIDTH), g_mhead)
    return hg.reshape(batch * seq, M_WIDTH), c_p, n_p, m_p


def _mlstm_step_kernel(qkv_ref, tail_ref, c_ref, n_ref, m_ref, og_ref, gm_ref, hg_ref, co_ref, no_ref, mo_ref, *, rows):
    D = M_HEAD_DIM
    lane = lax.broadcasted_iota(jnp.int32, (rows, LANES), 1)
    rowi = lax.broadcasted_iota(jnp.int32, (rows, D), 0)
    tail = tail_ref[...]
    m_in = m_ref[...]
    m_out = jnp.zeros((rows, LANES), F32)
    for h in range(M_HEADS):
        sl = slice(D * h, D * (h + 1))
        q = qkv_ref[:, sl]
        k = qkv_ref[:, M_WIDTH + D * h:M_WIDTH + D * (h + 1)]
        v = qkv_ref[:, 2 * M_WIDTH + D * h:2 * M_WIDTH + D * (h + 1)]
        ig = tail[:, TAIL_IG + h:TAIL_IG + h + 1]
        lf = tail[:, TAIL_LF + h:TAIL_LF + h + 1]
        m_prev = m_in[:, h:h + 1]
        n_prev = n_ref[:, h, :]
        a = lf + m_prev
        mt = jnp.maximum(a, ig)
        w_in = jnp.exp(ig - mt)
        w_st = jnp.exp(a - mt)
        s = jnp.sum(q * k, axis=1, keepdims=True) * w_in
        cq = jnp.zeros((rows, D), F32)
        for g in range(rows):
            cq = jnp.where(rowi == g, _dot_nt(q, c_ref[g, h]), cq)
        num = w_st * cq + s * v
        den = w_st * jnp.sum(n_prev * q, axis=1, keepdims=True) + s
        hs = num / jnp.maximum(jnp.abs(den), jnp.exp(-mt))
        vw_t = jnp.transpose(v * w_in)
        for g in range(rows):
            co_ref[g, h] = w_st[g:g + 1] * c_ref[g, h] + vw_t[:, g:g + 1] * k[g:g + 1, :]
        no_ref[:, h, :] = w_st * n_prev + w_in * k
        m_out = jnp.where(lane == h, mt, m_out)
        mu = jnp.mean(hs, axis=1, keepdims=True)
        d = hs - mu
        y = d * lax.rsqrt(jnp.mean(d * d, axis=1, keepdims=True) + EPS) * gm_ref[:, sl]
        hg_ref[:, sl] = (y * og_ref[:, sl]).astype(BF16)
    mo_ref[...] = m_out


def _mlstm_step(qkvm, tail, c0, n0, m0_pad, og, g_mhead, *, rows=8):
    nb = qkvm.shape[0]
    row = lambda i: (i, 0)
    return pl.pallas_call(
        functools.partial(_mlstm_step_kernel, rows=rows),
        grid=(nb // rows,),
        in_specs=[
            pl.BlockSpec((rows, 3 * M_WIDTH), row),
            pl.BlockSpec((rows, LANES), row),
            pl.BlockSpec((rows, M_HEADS, M_HEAD_DIM, M_HEAD_DIM), lambda i: (i, 0, 0, 0)),
            pl.BlockSpec((rows, M_HEADS, M_HEAD_DIM), lambda i: (i, 0, 0)),
            pl.BlockSpec((rows, LANES), row),
            pl.BlockSpec((rows, M_WIDTH), row),
            pl.BlockSpec((1, M_WIDTH), lambda i: (0, 0)),
        ],
        out_specs=[
            pl.BlockSpec((rows, M_WIDTH), row),
            pl.BlockSpec((rows, M_HEADS, M_HEAD_DIM, M_HEAD_DIM), lambda i: (i, 0, 0, 0)),
            pl.BlockSpec((rows, M_HEADS, M_HEAD_DIM), lambda i: (i, 0, 0)),
            pl.BlockSpec((rows, LANES), row),
        ],
        out_shape=[
            jax.ShapeDtypeStruct((nb, M_WIDTH), BF16),
            jax.ShapeDtypeStruct((nb, M_HEADS, M_HEAD_DIM, M_HEAD_DIM), F32),
            jax.ShapeDtypeStruct((nb, M_HEADS, M_HEAD_DIM), F32),
            jax.ShapeDtypeStruct((nb, LANES), F32),
        ],
        compiler_params=_params(1), name="mlstm_step",
    )(qkvm, tail, c0, n0, m0_pad, og, g_mhead)


def _mla_prefill_kernel(q_ref, k_ref, v_ref, o_ref, s_scr, p_scr, m_scr, acc_scr, *, blk, nq, heads, rows):
    i = pl.program_id(2)
    reps = blk // LANES
    m_scr[...] = jnp.full(m_scr.shape, -jnp.inf, F32)
    acc_scr[...] = jnp.zeros(acc_scr.shape, F32)

    def block(j, masked):
        start = j * blk
        slot = j % 2
        for hh in range(heads):
            sl = slice(HEAD_SLOT * hh, HEAD_SLOT * (hh + 1))
            s_scr[slot, hh] = _dot_nt(q_ref[:, sl], k_ref[pl.ds(start, blk), sl])
        for hh in range(heads):
            for r in range(blk // rows):
                rs = slice(rows * r, rows * (r + 1))
                s = s_scr[slot, hh, rs, :]
                if masked:
                    qi = lax.broadcasted_iota(jnp.int32, (rows, blk), 0) + rows * r
                    ki = lax.broadcasted_iota(jnp.int32, (rows, blk), 1)
                    s = jnp.where(ki <= qi, s, -jnp.inf)
                m_old = m_scr[hh, rs, :]
                m_new = jnp.maximum(m_old, jnp.max(s, axis=1, keepdims=True))
                p_scr[hh, rs, :] = jnp.exp(s - jnp.concatenate([m_new] * reps, axis=1)).astype(BF16)
                acc_scr[hh, rs, :] = acc_scr[hh, rs, :] * jnp.exp(m_old - m_new)
                m_scr[hh, rs, :] = m_new
        for hh in range(heads):
            sl = slice(HEAD_SLOT * hh, HEAD_SLOT * (hh + 1))
            acc_scr[hh] += jnp.dot(p_scr[hh], v_ref[pl.ds(start, blk), sl], preferred_element_type=F32)

    for n_blocks in range(1, nq + 1):
        @pl.when(i == n_blocks - 1)
        def _():
            for j in range(n_blocks):
                block(j, j == n_blocks - 1)

    o0, o1 = (acc_scr[hh] / acc_scr[hh][:, V_HEAD:V_HEAD + 1] for hh in range(2))
    lane = lax.broadcasted_iota(jnp.int32, (blk, HEAD_SLOT), 1)
    o_ref[...] = jnp.where(lane < V_HEAD, o0, pltpu.roll(o1, V_HEAD, 1)).astype(BF16)


def _mla_prefill(qcat, kcat, vcat, *, batch, seq, blk, rows=32):
    heads = 2
    nq = seq // blk
    wide = heads * HEAD_SLOT
    return pl.pallas_call(
        functools.partial(_mla_prefill_kernel, blk=blk, nq=nq, heads=heads, rows=rows),
        grid=(batch, A_HEADS // heads, nq),
        in_specs=[
            pl.BlockSpec((blk, wide), lambda b, h, i: (b * nq + i, h)),
            pl.BlockSpec((seq, wide), lambda b, h, i: (b, h)),
            pl.BlockSpec((seq, wide), lambda b, h, i: (b, h)),
        ],
        out_specs=pl.BlockSpec((blk, heads * V_HEAD), lambda b, h, i: (b * nq + i, h)),
        out_shape=jax.ShapeDtypeStruct((batch * seq, A_HEADS * V_HEAD), BF16),
        scratch_shapes=[pltpu.VMEM((2, heads, blk, blk), F32), pltpu.VMEM((heads, blk, blk), BF16),
                        pltpu.VMEM((heads, blk, LANES), F32), pltpu.VMEM((heads, blk, HEAD_SLOT), F32)],
        compiler_params=_params(3), name="mla_prefill",
    )(qcat, kcat, vcat)


def _mla_decode_kernel(pt_ref, q_ref, ckv_hbm, kr_hbm, cself_ref, tself_ref, o_ref,
                       cbuf, rbuf, kbuf, sems, *, n_chunks, chunk_pages, ahead):
    b = pl.program_id(0)
    nb = pl.num_programs(0)

    def page_copies(bb, c, slot, p):
        pg = pt_ref[bb, c * chunk_pages + p]
        dst = pl.ds(p * PAGE_SIZE, PAGE_SIZE)
        return (pltpu.make_async_copy(ckv_hbm.at[pg], cbuf.at[slot, dst], sems.at[0, slot]),
                pltpu.make_async_copy(kr_hbm.at[pg], rbuf.at[slot, :, dst], sems.at[1, slot]))

    def issue(bb, c, slot):
        for p in range(chunk_pages):
            for cp in page_copies(bb, c, slot, p):
                cp.start()

    def wait(slot):
        for p in range(chunk_pages):
            for cp in page_copies(0, 0, slot, p):
                cp.wait()

    @pl.when(b == 0)
    def _():
        for c in range(ahead):
            issue(0, c, c)

    q = q_ref[0]
    q_lat = q[:, 0:KV_LORA]
    q_rope = q[:, KV_LORA:KV_LORA + QK_ROPE]

    def scores(slot):
        kc = cbuf[slot].astype(BF16)
        kbuf[slot % 2] = kc
        kr = rbuf[slot].astype(BF16)
        return _dot_nt(q_lat, kc) + jnp.dot(q_rope, kr, preferred_element_type=F32)

    def accumulate(carry, s, slot):
        m_old, l, acc = carry
        m_new = jnp.maximum(m_old, jnp.max(s, axis=1, keepdims=True))
        p = jnp.exp(s - m_new)
        alpha = jnp.exp(m_old - m_new)
        return (m_new, alpha * l + jnp.sum(p, axis=1, keepdims=True),
                alpha * acc + jnp.dot(p.astype(BF16), kbuf[slot], preferred_element_type=F32))

    carry = (jnp.full((A_HEADS, 1), -jnp.inf, F32), jnp.zeros((A_HEADS, 1), F32), jnp.zeros((A_HEADS, KV_LORA), F32))
    s_prev = None
    for c in range(n_chunks):
        nxt = c + ahead
        if nxt < n_chunks:
            issue(b, nxt, nxt)
        else:
            @pl.when(b + 1 < nb)
            def _():
                issue(b + 1, nxt - n_chunks, nxt - n_chunks)
        wait(c)
        s_cur = scores(c)
        if c > 0:
            carry = accumulate(carry, s_prev, (c - 1) % 2)
        s_prev = s_cur
    m_old, l, acc = accumulate(carry, s_prev, (n_chunks - 1) % 2)

    c_self = cself_ref[0].astype(BF16).astype(F32)
    r_self = tself_ref[0][:, 0:QK_ROPE].astype(BF16).astype(F32)
    s_self = (jnp.sum(q_lat.astype(F32) * c_self, axis=1, keepdims=True)
              + jnp.sum(q_rope.astype(F32) * r_self, axis=1, keepdims=True))
    m_new = jnp.maximum(m_old, s_self)
    p_self = jnp.exp(s_self - m_new)
    alpha = jnp.exp(m_old - m_new)
    l = alpha * l + p_self
    acc = alpha * acc + p_self.astype(BF16).astype(F32) * c_self
    o_ref[0] = acc / l


def _mla_decode(page_table, qabs, cache_ckv, cache_krope, ckv_self, tail_self, *, chunk_pages=32, ahead=2):
    nb, n_pages = page_table.shape
    n_chunks = n_pages // chunk_pages
    assert n_chunks * chunk_pages == n_pages and ahead < n_chunks
    keys = chunk_pages * PAGE_SIZE
    grid_spec = pltpu.PrefetchScalarGridSpec(
        num_scalar_prefetch=1,
        grid=(nb,),
        in_specs=[
            pl.BlockSpec((1, A_HEADS, ABS_SLOT), lambda b, pt: (b, 0, 0)),
            pl.BlockSpec(memory_space=pl.ANY),
            pl.BlockSpec(memory_space=pl.ANY),
            pl.BlockSpec((1, 1, KV_LORA), lambda b, pt: (b, 0, 0)),
            pl.BlockSpec((1, 1, LANES), lambda b, pt: (b, 0, 0)),
        ],
        out_specs=pl.BlockSpec((1, A_HEADS, KV_LORA), lambda b, pt: (b, 0, 0)),
        scratch_shapes=[
            pltpu.VMEM((n_chunks, keys, KV_LORA), F32),
            pltpu.VMEM((n_chunks, QK_ROPE, keys), F32),
            pltpu.VMEM((2, keys, KV_LORA), BF16),
            pltpu.SemaphoreType.DMA((2, n_chunks)),
        ],
    )
    return pl.pallas_call(
        functools.partial(_mla_decode_kernel, n_chunks=n_chunks, chunk_pages=chunk_pages, ahead=ahead),
        grid_spec=grid_spec,
        out_shape=jax.ShapeDtypeStruct((nb, A_HEADS, KV_LORA), F32),
        compiler_params=_params(1), name="mla_decode",
    )(page_table, qabs.reshape(nb, A_HEADS, ABS_SLOT), cache_ckv, cache_krope,
      ckv_self.reshape(nb, 1, KV_LORA), tail_self.reshape(nb, 1, LANES))


def _merge_kernel(x_ref, hg_ref, ob_ref, sg_ref, wa_ref, wb_ref, wout_ref, gpost_ref, gmem_ref, wmq_ref,
                  *rest, from_latent):
    if from_latent:
        wuv_ref, x1_ref, qm_ref = rest
        ob = sum(_dot(ob_ref[:, KV_LORA * hh:KV_LORA * (hh + 1)], wuv_ref[hh]) for hh in range(A_HEADS)).astype(BF16)
    else:
        x1_ref, qm_ref = rest
        ob = ob_ref[...]
    ya = jnp.dot(hg_ref[...], wa_ref[...], preferred_element_type=F32)
    yb = jnp.dot(ob, wb_ref[...], preferred_element_type=F32)
    mix = sg_ref[:, 0:D_MODEL] * ya + sg_ref[:, D_MODEL:] * yb
    y = _dot(mix, wout_ref[...])
    x1 = x_ref[...] + _rms(y, gpost_ref[...])
    x1_ref[...] = x1
    qm = _dot(_rms(x1, gmem_ref[...]), wmq_ref[...]) * (X_HEAD_DIM ** -0.5)
    qm_ref[...] = qm.astype(qm_ref.dtype)


def _merge(x, hg, ob, sg, w, *, tm, from_latent, qdtype):
    m = x.shape[0]
    row = lambda i: (i, 0)
    a_width = A_HEADS * V_HEAD
    ob_w = ob.shape[1]
    in_specs = [
        pl.BlockSpec((tm, D_MODEL), row), pl.BlockSpec((tm, M_WIDTH), row), pl.BlockSpec((tm, ob_w), row),
        pl.BlockSpec((tm, 2 * D_MODEL), row),
        _const_spec((M_WIDTH, D_MODEL)), _const_spec((a_width, D_MODEL)), _const_spec((D_MODEL, D_MODEL)),
        _const_spec((1, D_MODEL)), _const_spec((1, D_MODEL)), _const_spec((D_MODEL, X_WIDTH)),
    ]
    args = [x, hg, ob, sg, w["w_branch_a"], w["w_branch_b"], w["w_out"], w["g_post_mix"], w["g_pre_mem"], w["w_mq"]]
    if from_latent:
        in_specs += [_const_spec((A_HEADS, KV_LORA, a_width))]
        args += [w["wuv_heads"]]
    return pl.pallas_call(
        functools.partial(_merge_kernel, from_latent=from_latent),
        grid=(m // tm,), in_specs=in_specs,
        out_specs=[pl.BlockSpec((tm, D_MODEL), row), pl.BlockSpec((tm, X_WIDTH), row)],
        out_shape=[jax.ShapeDtypeStruct((m, D_MODEL), F32), jax.ShapeDtypeStruct((m, X_WIDTH), qdtype)],
        compiler_params=_params(1), name="merge_lat" if from_latent else "merge",
    )(*args)


def _memkv_kernel(mem_ref, g_ref, wk_ref, wv_ref, k_ref, v_ref):
    mn = _rms(mem_ref[...], g_ref[...]).astype(BF16)
    k_ref[...] = jnp.dot(mn, wk_ref[...], preferred_element_type=F32)
    v_ref[...] = jnp.dot(mn, wv_ref[...], preferred_element_type=F32)


def _memkv(mem, w, *, tm):
    m = mem.shape[0]
    row = lambda i: (i, 0)
    return pl.pallas_call(
        _memkv_kernel, grid=(m // tm,),
        in_specs=[pl.BlockSpec((tm, D_MODEL), row), _const_spec((1, D_MODEL)),
                  _const_spec((D_MODEL, X_WIDTH)), _const_spec((D_MODEL, X_WIDTH))],
        out_specs=[pl.BlockSpec((tm, X_WIDTH), row)] * 2,
        out_shape=[jax.ShapeDtypeStruct((m, X_WIDTH), F32)] * 2,
        compiler_params=_params(1), name="memkv",
    )(mem, w["g_mem"], w["w_mk"], w["w_mv"])


def _memattn_kernel(q_ref, k_ref, v_ref, o_ref):
    kb = k_ref[...].astype(BF16)
    vb = v_ref[...].astype(BF16)
    for h in range(X_HEADS):
        sl = slice(X_HEAD_DIM * h, X_HEAD_DIM * (h + 1))
        s = _dot_nt(q_ref[:, sl], kb[:, sl])
        p = jnp.exp(s - jnp.max(s, axis=1, keepdims=True))
        l = jnp.sum(p, axis=1, keepdims=True)
        o_ref[:, sl] = (jnp.dot(p.astype(BF16), vb[:, sl], preferred_element_type=F32) / l).astype(BF16)


def _memattn(qm, mem_k, mem_v, *, batch, seq, tm):
    nt = seq // tm
    tok = lambda b, i: (b * nt + i, 0)
    kv = lambda b, i: (b, 0)
    return pl.pallas_call(
        _memattn_kernel, grid=(batch, nt),
        in_specs=[pl.BlockSpec((tm, X_WIDTH), tok), pl.BlockSpec((N_MEM, X_WIDTH), kv),
                  pl.BlockSpec((N_MEM, X_WIDTH), kv)],
        out_specs=pl.BlockSpec((tm, X_WIDTH), tok),
        out_shape=jax.ShapeDtypeStruct((batch * seq, X_WIDTH), BF16),
        compiler_params=_params(2), name="memattn",
    )(qm, mem_k, mem_v)


def _memattn_step_kernel(q_ref, k_ref, v_ref, o_ref, *, rows):
    for g in range(rows):
        q4 = [q_ref[g:g + 1, X_HEAD_DIM * h:X_HEAD_DIM * (h + 1)] for h in range(X_HEADS)]
        q8 = jnp.concatenate(q4 + q4, axis=0)
        s = jnp.sum(k_ref[g] * q8[None], axis=2, keepdims=True)
        mx = jnp.max(s, axis=0)
        mx = jnp.maximum(mx, pltpu.roll(mx, X_HEADS, 0))
        p = jnp.exp(s - mx[None])
        l8 = jnp.sum(p, axis=0)
        o8 = jnp.sum(p * v_ref[g], axis=0)
        o_ref[g] = (o8[0:X_HEADS] + o8[X_HEADS:]) / (l8[0:X_HEADS] + l8[X_HEADS:])


def _memattn_step(qm, mem_k, mem_v, *, rows=8):
    nb = qm.shape[0]
    row = lambda i: (i, 0)
    kv = lambda i: (i, 0, 0, 0)
    kv_block = (rows, N_MEM // 2, 2 * X_HEADS, X_HEAD_DIM)
    return pl.pallas_call(
        functools.partial(_memattn_step_kernel, rows=rows), grid=(nb // rows,),
        in_specs=[pl.BlockSpec((rows, X_WIDTH), row), pl.BlockSpec(kv_block, kv), pl.BlockSpec(kv_block, kv)],
        out_specs=pl.BlockSpec((rows, X_HEADS, X_HEAD_DIM), lambda i: (i, 0, 0)),
        out_shape=jax.ShapeDtypeStruct((nb, X_HEADS, X_HEAD_DIM), F32),
        compiler_params=_params(1), name="memattn_step",
    )(qm, mem_k, mem_v).reshape(nb, X_WIDTH)


def _ffn_kernel(x1_ref, o_ref, wmo_ref, gpm_ref, gpf_ref, win_ref, wdn_ref, gpost_ref, y_ref, *, f_chunk):
    x2 = x1_ref[...] + _rms(_dot(o_ref[...], wmo_ref[...]), gpm_ref[...])
    h = _rms(x2, gpf_ref[...]).astype(BF16)
    acc = jnp.zeros(x2.shape, F32)
    for c in range(D_FF // f_chunk):
        sl = slice(f_chunk * c, f_chunk * (c + 1))
        g = jnp.dot(h, win_ref[:, sl], preferred_element_type=F32)
        u = jnp.dot(h, win_ref[:, D_FF + f_chunk * c:D_FF + f_chunk * (c + 1)], preferred_element_type=F32)
        acc = acc + jnp.dot((g * jax.nn.sigmoid(g) * u).astype(BF16), wdn_ref[sl, :], preferred_element_type=F32)
    y_ref[...] = x2 + _rms(acc, gpost_ref[...])


def _ffn(x1, o, w, *, tm, f_chunk=2816):
    m = x1.shape[0]
    row = lambda i: (i, 0)
    return pl.pallas_call(
        functools.partial(_ffn_kernel, f_chunk=f_chunk), grid=(m // tm,),
        in_specs=[pl.BlockSpec((tm, D_MODEL), row), pl.BlockSpec((tm, X_WIDTH), row),
                  _const_spec((X_WIDTH, D_MODEL)), _const_spec((1, D_MODEL)), _const_spec((1, D_MODEL)),
                  _const_spec((D_MODEL, 2 * D_FF)), _const_spec((D_FF, D_MODEL)),
                  _const_spec((1, D_MODEL))],
        out_specs=pl.BlockSpec((tm, D_MODEL), row),
        out_shape=jax.ShapeDtypeStruct((m, D_MODEL), F32),
        compiler_params=_params(1), name="ffn",
    )(x1, o, w["w_mo"], w["g_post_mem"], w["g_pre_ffn"], w["w_ffn_in"], w["w_ffn_out"], w["g_post_ffn"])


def _prep_weights(p):
    w = {}
    for name in ("g_pre_mix", "g_q_lora", "g_kv_lora", "g_mhead", "g_post_mix", "g_pre_mem", "g_mem", "g_post_mem",
                 "g_pre_ffn", "g_post_ffn"):
        w[name] = p[name].reshape(1, -1).astype(F32)
    w_in = p["w_in"]
    offs = np.cumsum((0, M_WIDTH, M_WIDTH, M_WIDTH, M_HEADS, M_HEADS, M_WIDTH, Q_LORA, KV_LORA, QK_ROPE,
                      D_MODEL, D_MODEL))
    seg = lambda i: w_in[:, offs[i]:offs[i + 1]]
    mq, mk, mv, mi, mf, mo, cq, ckv, kr, ga, gb = (seg(i) for i in range(11))
    half = QK_ROPE // 2
    swap = lambda a: jnp.concatenate([a[..., half:], a[..., :half]], axis=-1)
    tail = jnp.concatenate([kr, swap(kr), mi, mf,
                            jnp.zeros((D_MODEL, LANES - 2 * QK_ROPE - 2 * M_HEADS), w_in.dtype)], axis=1)
    w["w_qkv"] = w_in[:, offs[0]:offs[3]].astype(BF16)
    w["w_mo_gate"] = mo.astype(BF16)
    w["w_lat"] = w_in[:, offs[6]:offs[8]].astype(BF16)
    w["w_g"] = w_in[:, offs[9]:offs[11]].astype(BF16)
    w["w_tail"] = tail.astype(BF16)
    w["wv_t"] = mv.T.astype(BF16)
    w["b_tail"] = jnp.zeros((1, LANES), F32).at[0, TAIL_IG:TAIL_IG + 2 * M_HEADS].set(p["b_if"].astype(F32))

    pad = HEAD_SLOT - QK_NOPE - QK_ROPE
    wq = p["w_uq"].reshape(Q_LORA, A_HEADS, QK_NOPE + QK_ROPE)
    zq = jnp.zeros((Q_LORA, A_HEADS, pad), wq.dtype)
    w["wq_cat"] = jnp.concatenate([wq, zq], axis=2).reshape(Q_LORA, A_HEADS * HEAD_SLOT).astype(BF16)
    w_uk, w_uv = p["w_uk"], p["w_uv"]
    zk = jnp.zeros((KV_LORA, A_HEADS, HEAD_SLOT - QK_NOPE), w_uk.dtype)
    w["wuk_cat"] = jnp.concatenate([w_uk, zk], axis=2).reshape(KV_LORA, A_HEADS * HEAD_SLOT).astype(BF16)
    w["wuv_cat"] = jnp.concatenate([w_uv, zk], axis=2).reshape(KV_LORA, A_HEADS * HEAD_SLOT).astype(BF16)
    w["wuv_heads"] = (w_uv.transpose(1, 0, 2)[:, :, None, :]
                      * jnp.eye(A_HEADS, dtype=w_uv.dtype)[:, None, :, None]).reshape(
                          A_HEADS, KV_LORA, A_HEADS * V_HEAD).astype(BF16)
    pass_np = np.zeros((A_HEADS, HEAD_SLOT - QK_NOPE, ABS_SLOT), np.float32)
    pass_np[:, np.arange(QK_ROPE), KV_LORA + np.arange(QK_ROPE)] = 1.0
    w["w_abs"] = jnp.concatenate(
        [jnp.concatenate([w_uk.transpose(1, 2, 0), jnp.zeros((A_HEADS, QK_NOPE, ABS_SLOT - KV_LORA), w_uk.dtype)],
                         axis=2).astype(BF16), jnp.asarray(pass_np, BF16)], axis=1)
    for name in ("w_branch_a", "w_branch_b", "w_out", "w_mq", "w_mk", "w_mv", "w_mo", "w_ffn_out"):
        w[name] = p[name].astype(BF16)
    w["w_ffn_in"] = p["w_ffn_in"].astype(BF16)
    return w


def _rope_tables(pos0, n):
    pos = (pos0 + jnp.arange(n)).astype(F32)
    inv = ROPE_BASE ** (-jnp.arange(0, QK_ROPE, 2, dtype=F32) / QK_ROPE)
    ang = pos[:, None] * inv[None, :]
    cos, sin = jnp.cos(ang), jnp.sin(ang)
    c32 = jnp.concatenate([cos, cos], axis=1)
    s32 = jnp.concatenate([-sin, sin], axis=1)
    one = lambda k: jnp.ones((n, k), F32)
    zero = lambda k: jnp.zeros((n, k), F32)
    ct = jnp.concatenate([c32, one(LANES - QK_ROPE)], axis=1)
    st = jnp.concatenate([s32, zero(LANES - QK_ROPE)], axis=1)
    c128 = jnp.concatenate([one(QK_NOPE), c32, one(HEAD_SLOT - QK_NOPE - QK_ROPE)], axis=1)
    s128 = jnp.concatenate([zero(QK_NOPE), s32, zero(HEAD_SLOT - QK_NOPE - QK_ROPE)], axis=1)
    return ct, st, c128, s128


def kernel(x_prompt, x_sample, cache_ckv, cache_krope, cache_mem_k, cache_mem_v, state_C, state_n, state_m, page_table, mem_prompt, g_pre_mix, w_in, b_if, g_mhead, g_q_lora, w_uq, g_kv_lora, w_uk, w_uv, w_branch_a, w_branch_b, w_out, g_post_mix, g_pre_mem, g_mem, w_mq, w_mk, w_mv, w_mo, g_post_mem, g_pre_ffn, w_ffn_in, w_ffn_out, g_post_ffn):
    params = dict(g_pre_mix=g_pre_mix, w_in=w_in, b_if=b_if, g_mhead=g_mhead, g_q_lora=g_q_lora, w_uq=w_uq,
                  g_kv_lora=g_kv_lora, w_uk=w_uk, w_uv=w_uv, w_branch_a=w_branch_a, w_branch_b=w_branch_b,
                  w_out=w_out, g_post_mix=g_post_mix, g_pre_mem=g_pre_mem, g_mem=g_mem, w_mq=w_mq, w_mk=w_mk,
                  w_mv=w_mv, w_mo=w_mo, g_post_mem=g_post_mem, g_pre_ffn=g_pre_ffn, w_ffn_in=w_ffn_in,
                  w_ffn_out=w_ffn_out, g_post_ffn=g_post_ffn)
    depth = w_in.shape[0]
    assert depth == 1, "single-layer stack only"
    bp, seq, _ = x_prompt.shape
    bs, dec_seq, _ = x_sample.shape
    assert dec_seq == 1, "one new token per sample sequence"
    past_len = page_table.shape[1] * PAGE_SIZE
    w = _prep_weights({name: a[0] for name, a in params.items()})

    xp = x_prompt.reshape(bp * seq, D_MODEL)
    qkvm, og, ckv_p, sg, krope_t, gates_t, qcat, kcat, vcat, vt = _inproj(
        xp, w, _rope_tables(0, seq), tm=512, absorbed=False, mdtype=BF16)
    hg, c_p, n_p, m_p = _mlstm_prompt(qkvm, vt, gates_t, og, w["g_mhead"], batch=bp, seq=seq)
    ob = _mla_prefill(qcat, kcat, vcat, batch=bp, seq=seq, blk=512)
    x1, qm = _merge(xp, hg, ob, sg, w, tm=512, from_latent=False, qdtype=BF16)
    mem_k, mem_v = _memkv(mem_prompt.reshape(bp * N_MEM, D_MODEL), w, tm=512)
    om = _memattn(qm, mem_k, mem_v, batch=bp, seq=seq, tm=512)
    y_prompt = _ffn(x1, om, w, tm=512).reshape(bp, seq, D_MODEL)

    xs = x_sample.reshape(bs, D_MODEL)
    tables_s = tuple(jnp.broadcast_to(t, (bs, LANES)) for t in _rope_tables(past_len, 1))
    qkvm_s, og_s, ckv_s, sg_s, tail_s, qabs = _inproj(xs, w, tables_s, tm=bs, absorbed=True, mdtype=F32)
    m0_pad = jnp.pad(state_m.reshape(bs, M_HEADS).astype(F32), ((0, 0), (0, LANES - M_HEADS)))
    hg_s, c_s, n_s, m_s = _mlstm_step(qkvm_s, tail_s, state_C.reshape(bs, M_HEADS, M_HEAD_DIM, M_HEAD_DIM),
                                      state_n.reshape(bs, M_HEADS, M_HEAD_DIM), m0_pad, og_s, w["g_mhead"])
    n_phys = cache_ckv.shape[1]
    cache_krope_t = jnp.transpose(cache_krope.reshape(n_phys, PAGE_SIZE, QK_ROPE), (0, 2, 1))
    o_lat = _mla_decode(page_table, qabs, cache_ckv.reshape(n_phys, PAGE_SIZE, KV_LORA), cache_krope_t, ckv_s, tail_s)
    x1_s, qm_s = _merge(xs, hg_s, o_lat.reshape(bs, A_HEADS * KV_LORA), sg_s, w, tm=bs, from_latent=True, qdtype=F32)
    om_s = _memattn_step(qm_s, cache_mem_k.reshape(bs, N_MEM // 2, 2 * X_HEADS, X_HEAD_DIM),
                         cache_mem_v.reshape(bs, N_MEM // 2, 2 * X_HEADS, X_HEAD_DIM))
    y_sample = _ffn(x1_s, om_s, w, tm=bs).reshape(bs, 1, D_MODEL)

    return (y_prompt, y_sample,
            ckv_p.reshape(1, bp, seq, KV_LORA), jnp.transpose(krope_t, (0, 2, 1)).reshape(1, bp, seq, QK_ROPE),
            c_p.reshape(1, bp, M_HEADS, M_HEAD_DIM, M_HEAD_DIM), n_p.reshape(1, bp, M_HEADS, M_HEAD_DIM),
            m_p[:, :, 0].reshape(1, bp, M_HEADS),
            mem_k.reshape(1, bp, N_MEM, X_HEADS, X_HEAD_DIM), mem_v.reshape(1, bp, N_MEM, X_HEADS, X_HEAD_DIM),
            ckv_s.reshape(1, bs, 1, KV_LORA), tail_s[:, :QK_ROPE].reshape(1, bs, 1, QK_ROPE),
            c_s.reshape(1, bs, M_HEADS, M_HEAD_DIM, M_HEAD_DIM), n_s.reshape(1, bs, M_HEADS, M_HEAD_DIM),
            m_s[:, :M_HEADS].reshape(1, bs, M_HEADS))
```

```python
import functools

import jax
import jax.numpy as jnp
import numpy as np
from jax import lax
from jax.experimental import pallas as pl
from jax.experimental.pallas import tpu as pltpu

F32 = jnp.float32
BF16 = jnp.bfloat16

D_MODEL = 1024
PAGE_SIZE = 128
M_HEADS = 4
M_HEAD_DIM = 128
M_WIDTH = M_HEADS * M_HEAD_DIM
M_CHUNK = 128
A_HEADS = 8
QK_NOPE = 64
QK_ROPE = 32
V_HEAD = 64
Q_LORA = 384
KV_LORA = 256
ROPE_BASE = 10000.0
N_MEM = 256
X_HEADS = 4
X_HEAD_DIM = 128
X_WIDTH = X_HEADS * X_HEAD_DIM
D_FF = 2816
EPS = 1e-6

LANES = 128
HEAD_SLOT = 128
ABS_SLOT = 384
ATT_SCALE = (QK_NOPE + QK_ROPE) ** -0.5
VMEM_LIMIT = 52 * 1024 * 1024

TAIL_IG = 64
TAIL_LF = 68


def _rms(x, g):
    return x * lax.rsqrt(jnp.mean(x * x, axis=-1, keepdims=True) + EPS) * g


def _dot(a, b):
    return jnp.dot(a.astype(BF16), b.astype(BF16), preferred_element_type=F32)


def _dot_nt(a, b):
    return lax.dot_general(a.astype(BF16), b.astype(BF16), (((1,), (1,)), ((), ())),
                           preferred_element_type=F32)


def _dot_tn(a, b):
    return lax.dot_general(a.astype(BF16), b.astype(BF16), (((0,), (0,)), ((), ())),
                           preferred_element_type=F32)


def _log_sigmoid(x):
    return jnp.minimum(x, 0.0) - jnp.log1p(jnp.exp(-jnp.abs(x)))


def _const_spec(shape):
    nd = len(shape)
    return pl.BlockSpec(shape, lambda *_: (0,) * nd, pipeline_mode=pl.Buffered(1))


def _params(n_axes):
    return pltpu.CompilerParams(dimension_semantics=("arbitrary",) * n_axes, vmem_limit_bytes=VMEM_LIMIT)


def _inproj_kernel(x_ref, gpre_ref, wqkv_ref, wmo_ref, wlat_ref, wg_ref, wtail_ref, btail_ref, ct_ref, st_ref,
                   gq_ref, wq_ref, c128_ref, s128_ref, gkv_ref, *rest, absorbed):
    if absorbed:
        wabs_ref, qkvm_ref, og_ref, ckv_ref, sg_ref, tail_ref, qabs_ref = rest
    else:
        (wuk_ref, wuv_ref, wvt_ref,
         qkvm_ref, og_ref, ckv_ref, sg_ref, krt_ref, gt_ref, qcat_ref, kcat_ref, vcat_ref, vt_ref) = rest
    h = _rms(x_ref[...], gpre_ref[...]).astype(BF16)

    zm = jnp.dot(h, wqkv_ref[...], preferred_element_type=F32)
    qkvm_ref[:, 0:M_WIDTH] = zm[:, 0:M_WIDTH].astype(qkvm_ref.dtype)
    qkvm_ref[:, M_WIDTH:2 * M_WIDTH] = (zm[:, M_WIDTH:2 * M_WIDTH] * (M_HEAD_DIM ** -0.5)).astype(qkvm_ref.dtype)
    qkvm_ref[:, 2 * M_WIDTH:] = zm[:, 2 * M_WIDTH:].astype(qkvm_ref.dtype)
    og_ref[...] = jax.nn.sigmoid(jnp.dot(h, wmo_ref[...], preferred_element_type=F32))
    sg_ref[...] = jax.nn.sigmoid(jnp.dot(h, wg_ref[...], preferred_element_type=F32))

    t = jnp.dot(h, wtail_ref[...], preferred_element_type=F32) + btail_ref[...]
    t = t * ct_ref[...] + pltpu.roll(t, LANES - QK_ROPE, 1) * st_ref[...]
    lane = lax.broadcasted_iota(jnp.int32, t.shape, 1)
    is_lf = jnp.logical_and(lane >= TAIL_LF, lane < TAIL_LF + M_HEADS)
    t = jnp.where(is_lf, _log_sigmoid(t), t)
    if absorbed:
        tail_ref[...] = t
    else:
        tt = jnp.transpose(t)
        krt_ref[0] = tt[0:QK_ROPE]
        gt_ref[...] = tt[TAIL_IG:TAIL_IG + 2 * M_HEADS]

    cqn = _rms(jnp.dot(h, wlat_ref[:, 0:Q_LORA], preferred_element_type=F32), gq_ref[...]).astype(BF16)
    ckvn = _rms(jnp.dot(h, wlat_ref[:, Q_LORA:], preferred_element_type=F32), gkv_ref[...])
    ckv_ref[...] = ckvn

    qc = jnp.dot(cqn, wq_ref[...], preferred_element_type=F32)
    width = A_HEADS * HEAD_SLOT
    first_half = (lax.broadcasted_iota(jnp.int32, qc.shape, 1) % HEAD_SLOT) < QK_NOPE + QK_ROPE // 2
    qs = jnp.where(first_half, pltpu.roll(qc, width - QK_ROPE // 2, 1), pltpu.roll(qc, QK_ROPE // 2, 1))
    c128 = c128_ref[...]
    s128 = s128_ref[...]
    for hh in range(A_HEADS):
        sl = slice(HEAD_SLOT * hh, HEAD_SLOT * (hh + 1))
        qh = ((qc[:, sl] * c128 + qs[:, sl] * s128) * ATT_SCALE).astype(BF16)
        if absorbed:
            qabs_ref[:, ABS_SLOT * hh:ABS_SLOT * (hh + 1)] = jnp.dot(
                qh, wabs_ref[hh], preferred_element_type=F32).astype(BF16)
        else:
            qcat_ref[:, sl] = qh
    if not absorbed:
        cb = ckvn.astype(BF16)
        kr_slot = jnp.where(jnp.logical_and(lane >= QK_NOPE, lane < QK_NOPE + QK_ROPE), pltpu.roll(t, QK_NOPE, 1), 0.0)
        kn = jnp.dot(cb, wuk_ref[...], preferred_element_type=F32)
        for hh in range(A_HEADS):
            sl = slice(HEAD_SLOT * hh, HEAD_SLOT * (hh + 1))
            kcat_ref[:, sl] = (kn[:, sl] + kr_slot).astype(BF16)
        vlane = lax.broadcasted_iota(jnp.int32, (1, A_HEADS * HEAD_SLOT), 1) % HEAD_SLOT
        vcat = jnp.dot(cb, wuv_ref[...], preferred_element_type=F32)
        vcat_ref[...] = jnp.where(vlane == V_HEAD, 1.0, vcat).astype(BF16)
        vt_ref[...] = _dot_nt(wvt_ref[...], h).astype(BF16)


def _inproj(x, w, tables, *, tm, absorbed, mdtype):
    m = x.shape[0]
    ct, st, c128, s128 = tables
    seq = ct.shape[0]
    nt = seq // tm
    grid = (m // tm,)
    row = lambda i: (i, 0)
    tab = lambda i: (i % nt, 0)
    wide = A_HEADS * HEAD_SLOT
    in_specs = [
        pl.BlockSpec((tm, D_MODEL), row), _const_spec((1, D_MODEL)),
        _const_spec((D_MODEL, 3 * M_WIDTH)), _const_spec((D_MODEL, M_WIDTH)), _const_spec((D_MODEL, Q_LORA + KV_LORA)),
        _const_spec((D_MODEL, 2 * D_MODEL)), _const_spec((D_MODEL, LANES)), _const_spec((1, LANES)),
        pl.BlockSpec((tm, LANES), tab), pl.BlockSpec((tm, LANES), tab),
        _const_spec((1, Q_LORA)), _const_spec((Q_LORA, wide)),
        pl.BlockSpec((tm, LANES), tab), pl.BlockSpec((tm, LANES), tab),
        _const_spec((1, KV_LORA)),
    ]
    args = [x, w["g_pre_mix"], w["w_qkv"], w["w_mo_gate"], w["w_lat"], w["w_g"], w["w_tail"], w["b_tail"], ct, st,
            w["g_q_lora"], w["wq_cat"], c128, s128, w["g_kv_lora"]]
    outs = [
        (jax.ShapeDtypeStruct((m, 3 * M_WIDTH), mdtype), pl.BlockSpec((tm, 3 * M_WIDTH), row)),
        (jax.ShapeDtypeStruct((m, M_WIDTH), F32), pl.BlockSpec((tm, M_WIDTH), row)),
        (jax.ShapeDtypeStruct((m, KV_LORA), F32), pl.BlockSpec((tm, KV_LORA), row)),
        (jax.ShapeDtypeStruct((m, 2 * D_MODEL), F32), pl.BlockSpec((tm, 2 * D_MODEL), row)),
    ]
    if absorbed:
        in_specs += [_const_spec((A_HEADS, HEAD_SLOT, ABS_SLOT))]
        args += [w["w_abs"]]
        outs += [(jax.ShapeDtypeStruct((m, LANES), F32), pl.BlockSpec((tm, LANES), row)),
                 (jax.ShapeDtypeStruct((m, A_HEADS * ABS_SLOT), BF16), pl.BlockSpec((tm, A_HEADS * ABS_SLOT), row))]
    else:
        in_specs += [_const_spec((KV_LORA, wide)), _const_spec((KV_LORA, wide)), _const_spec((M_WIDTH, D_MODEL))]
        args += [w["wuk_cat"], w["wuv_cat"], w["wv_t"]]
        outs += [(jax.ShapeDtypeStruct((m // seq, QK_ROPE, seq), F32),
                  pl.BlockSpec((1, QK_ROPE, tm), lambda i: (i // nt, 0, i % nt))),
                 (jax.ShapeDtypeStruct((2 * M_HEADS, m), F32), pl.BlockSpec((2 * M_HEADS, tm), lambda i: (0, i)))]
        outs += [(jax.ShapeDtypeStruct((m, wide), BF16), pl.BlockSpec((tm, wide), row))] * 3
        outs += [(jax.ShapeDtypeStruct((M_WIDTH, m), BF16), pl.BlockSpec((M_WIDTH, tm), lambda i: (0, i)))]
    return pl.pallas_call(
        functools.partial(_inproj_kernel, absorbed=absorbed),
        grid=grid, in_specs=in_specs,
        out_specs=[o[1] for o in outs], out_shape=[o[0] for o in outs],
        compiler_params=_params(1), name="inproj_abs" if absorbed else "inproj",
    )(*args)


def _split3_dot(x, m01):
    hi = x.astype(BF16)
    r1 = x - hi.astype(F32)
    mid = r1.astype(BF16)
    lo = (r1 - mid.astype(F32)).astype(BF16)
    return sum(jnp.dot(part, m01, preferred_element_type=F32) for part in (hi, mid, lo))


def _cumsum_lanes(x, upper):
    return _split3_dot(x, jnp.where(upper, 1.0, 0.0).astype(BF16))


def _cummax_lanes(x):
    lane = lax.broadcasted_iota(jnp.int32, x.shape, 1)
    shift = 1
    while shift < x.shape[1]:
        x = jnp.maximum(x, jnp.where(lane >= shift, pltpu.roll(x, shift, 1), -jnp.inf))
        shift *= 2
    return x


def _mlstm_gates_kernel(g_ref, urow_ref, scal_ref, cols_ref, *, nc, nseq):
    L = M_CHUNK
    H = M_HEADS
    rows = nseq * nc * 2 * H
    g = g_ref[...]
    r = jnp.concatenate([g[:, L * c:L * (c + 1)] for c in range(nseq * nc)], axis=0)
    ri = lax.broadcasted_iota(jnp.int32, (rows, L), 0)
    ci = lax.broadcasted_iota(jnp.int32, (rows, L), 1)
    top = (ri % (2 * H)) < H
    b_all = _cumsum_lanes(r, lax.broadcasted_iota(jnp.int32, (L, L), 0) <= lax.broadcasted_iota(jnp.int32, (L, L), 1))
    b = pltpu.roll(b_all, rows - H, 0)
    u = r - b
    cm = _cummax_lanes(u)
    g_last = jnp.max(jnp.where(ci == L - 1, b, -jnp.inf), axis=1, keepdims=True)
    wlog = g_last - b + r
    wmax = jnp.max(wlog, axis=1, keepdims=True)
    urow_ref[...] = jnp.where(top, u, pltpu.roll(wlog, H, 0)).reshape(nseq, nc, 2 * H, L)
    scal = jnp.where(top, jnp.broadcast_to(g_last, (rows, L)), pltpu.roll(jnp.broadcast_to(wmax, (rows, L)), H, 0))
    scal_ref[...] = scal.reshape(nseq, nc, 2 * H, L)
    first = jnp.where(top, b, pltpu.roll(cm, H, 0))
    for c in range(nseq * nc):
        sl = slice(2 * H * c, 2 * H * (c + 1))
        cols_ref[c // nc, c % nc] = jnp.transpose(jnp.concatenate([first[sl], wlog[sl]], axis=0))


def _mlstm_gates(g, *, batch, nc, nseq=4):
    blk = lambda b: (b, 0, 0, 0)
    return pl.pallas_call(
        functools.partial(_mlstm_gates_kernel, nc=nc, nseq=nseq),
        grid=(batch // nseq,),
        in_specs=[pl.BlockSpec((2 * M_HEADS, nseq * nc * M_CHUNK), lambda b: (0, b))],
        out_specs=[pl.BlockSpec((nseq, nc, 2 * M_HEADS, M_CHUNK), blk),
                   pl.BlockSpec((nseq, nc, 2 * M_HEADS, M_CHUNK), blk),
                   pl.BlockSpec((nseq, nc, M_CHUNK, 4 * M_HEADS), blk)],
        out_shape=[jax.ShapeDtypeStruct((batch, nc, 2 * M_HEADS, M_CHUNK), F32),
                   jax.ShapeDtypeStruct((batch, nc, 2 * M_HEADS, M_CHUNK), F32),
                   jax.ShapeDtypeStruct((batch, nc, M_CHUNK, 4 * M_HEADS), F32)],
        compiler_params=_params(1), name="mlstm_gates",
    )(g)


def _mlstm_kernel(*refs, nb):
    qkv_ref = refs[0]
    vt_refs = refs[1:1 + nb]
    urow_ref, scal_ref, cols_ref, sel_ref, og_ref, gm_ref, hg_ref, c_ref, n_ref, m_ref = refs[1 + nb:]
    L = M_CHUNK

    @pl.when(pl.program_id(1) == 0)
    def _():
        c_ref[...] = jnp.zeros_like(c_ref)
        n_ref[...] = jnp.zeros_like(n_ref)
        m_ref[...] = jnp.zeros_like(m_ref)

    row = lax.broadcasted_iota(jnp.int32, (L, L), 0)
    col = lax.broadcasted_iota(jnp.int32, (L, L), 1)
    tril = col <= row

    for bb in range(nb):
        vt_ref = vt_refs[bb]
        rows8 = urow_ref[bb, 0]
        u4 = rows8[0:M_HEADS]
        wlog4 = rows8[M_HEADS:2 * M_HEADS]
        scal = scal_ref[bb, 0]
        g_last4 = scal[0:M_HEADS, 0:1]
        wmax4 = scal[M_HEADS:2 * M_HEADS, 0:1]
        cols = _split3_dot(cols_ref[bb, 0], sel_ref[...])
        m_prev4 = m_ref[bb][:, 0:1]
        n_prev4 = n_ref[bb]
        m_new4 = jnp.maximum(g_last4 + m_prev4, wmax4)
        decay4 = jnp.exp(g_last4 + m_prev4 - m_new4)
        ws4 = jnp.exp(wlog4 - m_new4)
        m_ref[bb] = jnp.broadcast_to(m_new4, (M_HEADS, LANES))
        n_rows = []

        for h in range(M_HEADS):
            sl = slice(M_HEAD_DIM * h, M_HEAD_DIM * (h + 1))
            q = qkv_ref[bb, :, sl]
            k = qkv_ref[bb, :, M_WIDTH + M_HEAD_DIM * h:M_WIDTH + M_HEAD_DIM * (h + 1)]
            v = qkv_ref[bb, :, 2 * M_WIDTH + M_HEAD_DIM * h:2 * M_WIDTH + M_HEAD_DIM * (h + 1)]
            m_prev = m_prev4[h:h + 1]
            c_prev = c_ref[bb, h]
            n_prev = n_prev4[h:h + 1]
            b_col = cols[:, LANES * h:LANES * (h + 1)]
            mm_col = jnp.maximum(cols[:, LANES * (M_HEADS + h):LANES * (M_HEADS + h + 1)], m_prev)
            w_intra = jnp.where(tril, jnp.exp(u4[h:h + 1] - mm_col), 0.0)
            w_inter = jnp.exp(m_prev - mm_col)
            s = _dot_nt(q, k) * w_intra
            cn = jnp.concatenate([c_prev.astype(BF16), jnp.broadcast_to(n_prev.astype(BF16), (L, M_HEAD_DIM))],
                                 axis=0)
            qcn = _dot_nt(q, cn)
            num = w_inter * qcn[:, 0:M_HEAD_DIM] + _dot(s, v)
            den = w_inter * qcn[:, M_HEAD_DIM:] + jnp.sum(s, axis=1, keepdims=True)
            hs = num / jnp.maximum(jnp.abs(den), jnp.exp(-(b_col + mm_col)))

            ws = ws4[h:h + 1]
            lhs = jnp.concatenate([(vt_ref[sl, :].astype(F32) * ws).astype(BF16),
                                   jnp.broadcast_to(ws.astype(BF16), (16, L))], axis=0)
            upd = jnp.dot(lhs, k, preferred_element_type=F32)
            decay = decay4[h:h + 1]
            c_ref[bb, h] = decay * c_prev + upd[0:M_HEAD_DIM]
            n_rows.append(decay * n_prev + upd[M_HEAD_DIM:M_HEAD_DIM + 1])

            mu = jnp.mean(hs, axis=1, keepdims=True)
            d = hs - mu
            y = d * lax.rsqrt(jnp.mean(d * d, axis=1, keepdims=True) + EPS) * gm_ref[:, sl]
            hg_ref[bb, :, sl] = (y * og_ref[bb, :, sl]).astype(BF16)
        n_ref[bb] = jnp.concatenate(n_rows, axis=0)


def _mlstm_prompt(qkvm, vt, gates, og, g_mhead, *, batch, seq, nb=8):
    nc = seq // M_CHUNK
    tok = lambda b, c: (b, c, 0)
    chunk = lambda b, c: (b, c, 0, 0)
    urow, scal, cols = _mlstm_gates(gates, batch=batch, nc=nc)
    sel = jnp.asarray(np.kron(np.eye(4 * M_HEADS, 2 * M_HEADS), np.ones((1, LANES))), BF16)
    vt_specs = [pl.BlockSpec((M_WIDTH, M_CHUNK), functools.partial(lambda b, c, bb: (0, (nb * b + bb) * nc + c), bb=bb))
                for bb in range(nb)]
    hg, c_p, n_p, m_p = pl.pallas_call(
        functools.partial(_mlstm_kernel, nb=nb),
        grid=(batch // nb, nc),
        in_specs=[pl.BlockSpec((nb, M_CHUNK, 3 * M_WIDTH), tok)] + vt_specs + [
            pl.BlockSpec((nb, 1, 2 * M_HEADS, M_CHUNK), chunk),
            pl.BlockSpec((nb, 1, 2 * M_HEADS, M_CHUNK), chunk),
            pl.BlockSpec((nb, 1, M_CHUNK, 4 * M_HEADS), chunk),
            pl.BlockSpec((4 * M_HEADS, 2 * M_HEADS * LANES), lambda b, c: (0, 0)),
            pl.BlockSpec((nb, M_CHUNK, M_WIDTH), tok),
            pl.BlockSpec((1, M_WIDTH), lambda b, c: (0, 0)),
        ],
        out_specs=[
            pl.BlockSpec((nb, M_CHUNK, M_WIDTH), tok),
            pl.BlockSpec((nb, M_HEADS, M_HEAD_DIM, M_HEAD_DIM), lambda b, c: (b, 0, 0, 0)),
            pl.BlockSpec((nb, M_HEADS, M_HEAD_DIM), lambda b, c: (b, 0, 0)),
            pl.BlockSpec((nb, M_HEADS, LANES), lambda b, c: (b, 0, 0)),
        ],
        out_shape=[
            jax.ShapeDtypeStruct((batch, seq, M_WIDTH), BF16),
            jax.ShapeDtypeStruct((batch, M_HEADS, M_HEAD_DIM, M_HEAD_DIM), F32),
            jax.ShapeDtypeStruct((batch, M_HEADS, M_HEAD_DIM), F32),
            jax.ShapeDtypeStruct((batch, M_HEADS, LANES), F32),
        ],
        compiler_params=_params(2), name="mlstm_prompt",
    )(qkvm.reshape(batch, seq, 3 * M_WIDTH), *([vt] * nb), urow, scal, cols, sel,
      og.reshape(batch, seq, M_WIDTH), g_mhead)
    return hg.reshape(batch * seq, M_WIDTH), c_p, n_p, m_p


def _mlstm_step_kernel(qkv_ref, tail_ref, c_ref, n_ref, m_ref, og_ref, gm_ref, hg_ref, co_ref, no_ref, mo_ref, *, rows):
    D = M_HEAD_DIM
    lane = lax.broadcasted_iota(jnp.int32, (rows, LANES), 1)
    rowi = lax.broadcasted_iota(jnp.int32, (rows, D), 0)
    tail = tail_ref[...]
    m_in = m_ref[...]
    m_out = jnp.zeros((rows, LANES), F32)
    for h in range(M_HEADS):
        sl = slice(D * h, D * (h + 1))
        q = qkv_ref[:, sl]
        k = qkv_ref[:, M_WIDTH + D * h:M_WIDTH + D * (h + 1)]
        v = qkv_ref[:, 2 * M_WIDTH + D * h:2 * M_WIDTH + D * (h + 1)]
        ig = tail[:, TAIL_IG + h:TAIL_IG + h + 1]
        lf = tail[:, TAIL_LF + h:TAIL_LF + h + 1]
        m_prev = m_in[:, h:h + 1]
        n_prev = n_ref[:, h, :]
        a = lf + m_prev
        mt = jnp.maximum(a, ig)
        w_in = jnp.exp(ig - mt)
        w_st = jnp.exp(a - mt)
        s = jnp.sum(q * k, axis=1, keepdims=True) * w_in
        cq = jnp.zeros((rows, D), F32)
        for g in range(rows):
            cq = jnp.where(rowi == g, _dot_nt(q, c_ref[g, h]), cq)
        num = w_st * cq + s * v
        den = w_st * jnp.sum(n_prev * q, axis=1, keepdims=True) + s
        hs = num / jnp.maximum(jnp.abs(den), jnp.exp(-mt))
        vw_t = jnp.transpose(v * w_in)
        for g in range(rows):
            co_ref[g, h] = w_st[g:g + 1] * c_ref[g, h] + vw_t[:, g:g + 1] * k[g:g + 1, :]
        no_ref[:, h, :] = w_st * n_prev + w_in * k
        m_out = jnp.where(lane == h, mt, m_out)
        mu = jnp.mean(hs, axis=1, keepdims=True)
        d = hs - mu
        y = d * lax.rsqrt(jnp.mean(d * d, axis=1, keepdims=True) + EPS) * gm_ref[:, sl]
        hg_ref[:, sl] = (y * og_ref[:, sl]).astype(BF16)
    mo_ref[...] = m_out


def _mlstm_step(qkvm, tail, c0, n0, m0_pad, og, g_mhead, *, rows=8):
    nb = qkvm.shape[0]
    row = lambda i: (i, 0)
    return pl.pallas_call(
        functools.partial(_mlstm_step_kernel, rows=rows),
        grid=(nb // rows,),
        in_specs=[
            pl.BlockSpec((rows, 3 * M_WIDTH), row),
            pl.BlockSpec((rows, LANES), row),
            pl.BlockSpec((rows, M_HEADS, M_HEAD_DIM, M_HEAD_DIM), lambda i: (i, 0, 0, 0)),
            pl.BlockSpec((rows, M_HEADS, M_HEAD_DIM), lambda i: (i, 0, 0)),
            pl.BlockSpec((rows, LANES), row),
            pl.BlockSpec((rows, M_WIDTH), row),
            pl.BlockSpec((1, M_WIDTH), lambda i: (0, 0)),
        ],
        out_specs=[
            pl.BlockSpec((rows, M_WIDTH), row),
            pl.BlockSpec((rows, M_HEADS, M_HEAD_DIM, M_HEAD_DIM), lambda i: (i, 0, 0, 0)),
            pl.BlockSpec((rows, M_HEADS, M_HEAD_DIM), lambda i: (i, 0, 0)),
            pl.BlockSpec((rows, LANES), row),
        ],
        out_shape=[
            jax.ShapeDtypeStruct((nb, M_WIDTH), BF16),
            jax.ShapeDtypeStruct((nb, M_HEADS, M_HEAD_DIM, M_HEAD_DIM), F32),
            jax.ShapeDtypeStruct((nb, M_HEADS, M_HEAD_DIM), F32),
            jax.ShapeDtypeStruct((nb, LANES), F32),
        ],
        compiler_params=_params(1), name="mlstm_step",
    )(qkvm, tail, c0, n0, m0_pad, og, g_mhead)


def _mla_prefill_kernel(q_ref, k_ref, v_ref, o_ref, s_scr, p_scr, m_scr, acc_scr, *, blk, nq, heads, rows):
    i = pl.program_id(2)
    reps = blk // LANES
    m_scr[...] = jnp.full(m_scr.shape, -jnp.inf, F32)
    acc_scr[...] = jnp.zeros(acc_scr.shape, F32)

    def block(j, masked):
        start = j * blk
        slot = j % 2
        for hh in range(heads):
            sl = slice(HEAD_SLOT * hh, HEAD_SLOT * (hh + 1))
            s_scr[slot, hh] = _dot_nt(q_ref[:, sl], k_ref[pl.ds(start, blk), sl])
        for hh in range(heads):
            for r in range(blk // rows):
                rs = slice(rows * r, rows * (r + 1))
                s = s_scr[slot, hh, rs, :]
                if masked:
                    qi = lax.broadcasted_iota(jnp.int32, (rows, blk), 0) + rows * r
                    ki = lax.broadcasted_iota(jnp.int32, (rows, blk), 1)
                    s = jnp.where(ki <= qi, s, -jnp.inf)
                m_old = m_scr[hh, rs, :]
                m_new = jnp.maximum(m_old, jnp.max(s, axis=1, keepdims=True))
                p_scr[hh, rs, :] = jnp.exp(s - jnp.concatenate([m_new] * reps, axis=1)).astype(BF16)
                acc_scr[hh, rs, :] = acc_scr[hh, rs, :] * jnp.exp(m_old - m_new)
                m_scr[hh, rs, :] = m_new
        for hh in range(heads):
            sl = slice(HEAD_SLOT * hh, HEAD_SLOT * (hh + 1))
            acc_scr[hh] += jnp.dot(p_scr[hh], v_ref[pl.ds(start, blk), sl], preferred_element_type=F32)

    for n_blocks in range(1, nq + 1):
        @pl.when(i == n_blocks - 1)
        def _():
            for j in range(n_blocks):
                block(j, j == n_blocks - 1)

    o0, o1 = (acc_scr[hh] / acc_scr[hh][:, V_HEAD:V_HEAD + 1] for hh in range(2))
    lane = lax.broadcasted_iota(jnp.int32, (blk, HEAD_SLOT), 1)
    o_ref[...] = jnp.where(lane < V_HEAD, o0, pltpu.roll(o1, V_HEAD, 1)).astype(BF16)


def _mla_prefill(qcat, kcat, vcat, *, batch, seq, blk, rows=32):
    heads = 2
    nq = seq // blk
    wide = heads * HEAD_SLOT
    return pl.pallas_call(
        functools.partial(_mla_prefill_kernel, blk=blk, nq=nq, heads=heads, rows=rows),
        grid=(batch, A_HEADS // heads, nq),
        in_specs=[
            pl.BlockSpec((blk, wide), lambda b, h, i: (b * nq + i, h)),
            pl.BlockSpec((seq, wide), lambda b, h, i: (b, h)),
            pl.BlockSpec((seq, wide), lambda b, h, i: (b, h)),
        ],
        out_specs=pl.BlockSpec((blk, heads * V_HEAD), lambda b, h, i: (b * nq + i, h)),
        out_shape=jax.ShapeDtypeStruct((batch * seq, A_HEADS * V_HEAD), BF16),
        scratch_shapes=[pltpu.VMEM((2, heads, blk, blk), F32), pltpu.VMEM((heads, blk, blk), BF16),
                        pltpu.VMEM((heads, blk, LANES), F32), pltpu.VMEM((heads, blk, HEAD_SLOT), F32)],
        compiler_params=_params(3), name="mla_prefill",
    )(qcat, kcat, vcat)


def _merge_kernel(x_ref, hg_ref, ob_ref, sg_ref, wa_ref, wb_ref, wout_ref, gpost_ref, gmem_ref, wmq_ref,
                  *rest, from_latent):
    if from_latent:
        wuv_ref, x1_ref, qm_ref = rest
        ob = sum(_dot(ob_ref[:, KV_LORA * hh:KV_LORA * (hh + 1)], wuv_ref[hh]) for hh in range(A_HEADS)).astype(BF16)
    else:
        x1_ref, qm_ref = rest
        ob = ob_ref[...]
    ya = jnp.dot(hg_ref[...], wa_ref[...], preferred_element_type=F32)
    yb = jnp.dot(ob, wb_ref[...], preferred_element_type=F32)
    mix = sg_ref[:, 0:D_MODEL] * ya + sg_ref[:, D_MODEL:] * yb
    y = _dot(mix, wout_ref[...])
    x1 = x_ref[...] + _rms(y, gpost_ref[...])
    x1_ref[...] = x1
    qm = _dot(_rms(x1, gmem_ref[...]), wmq_ref[...]) * (X_HEAD_DIM ** -0.5)
    qm_ref[...] = qm.astype(qm_ref.dtype)


def _merge(x, hg, ob, sg, w, *, tm, from_latent, qdtype):
    m = x.shape[0]
    row = lambda i: (i, 0)
    a_width = A_HEADS * V_HEAD
    ob_w = ob.shape[1]
    in_specs = [
        pl.BlockSpec((tm, D_MODEL), row), pl.BlockSpec((tm, M_WIDTH), row), pl.BlockSpec((tm, ob_w), row),
        pl.BlockSpec((tm, 2 * D_MODEL), row),
        _const_spec((M_WIDTH, D_MODEL)), _const_spec((a_width, D_MODEL)), _const_spec((D_MODEL, D_MODEL)),
        _const_spec((1, D_MODEL)), _const_spec((1, D_MODEL)), _const_spec((D_MODEL, X_WIDTH)),
    ]
    args = [x, hg, ob, sg, w["w_branch_a"], w["w_branch_b"], w["w_out"], w["g_post_mix"], w["g_pre_mem"], w["w_mq"]]
    if from_latent:
        in_specs += [_const_spec((A_HEADS, KV_LORA, a_width))]
        args += [w["wuv_heads"]]
    return pl.pallas_call(
        functools.partial(_merge_kernel, from_latent=from_latent),
        grid=(m // tm,), in_specs=in_specs,
        out_specs=[pl.BlockSpec((tm, D_MODEL), row), pl.BlockSpec((tm, X_WIDTH), row)],
        out_shape=[jax.ShapeDtypeStruct((m, D_MODEL), F32), jax.ShapeDtypeStruct((m, X_WIDTH), qdtype)],
        compiler_params=_params(1), name="merge_lat" if from_latent else "merge",
    )(*args)


def _memkv_kernel(mem_ref, g_ref, wk_ref, wv_ref, k_ref, v_ref):
    mn = _rms(mem_ref[...], g_ref[...]).astype(BF16)
    k_ref[...] = jnp.dot(mn, wk_ref[...], preferred_element_type=F32)
    v_ref[...] = jnp.dot(mn, wv_ref[...], preferred_element_type=F32)


def _memkv(mem, w, *, tm):
    m = mem.shape[0]
    row = lambda i: (i, 0)
    return pl.pallas_call(
        _memkv_kernel, grid=(m // tm,),
        in_specs=[pl.BlockSpec((tm, D_MODEL), row), _const_spec((1, D_MODEL)),
                  _const_spec((D_MODEL, X_WIDTH)), _const_spec((D_MODEL, X_WIDTH))],
        out_specs=[pl.BlockSpec((tm, X_WIDTH), row)] * 2,
        out_shape=[jax.ShapeDtypeStruct((m, X_WIDTH), F32)] * 2,
        compiler_params=_params(1), name="memkv",
    )(mem, w["g_mem"], w["w_mk"], w["w_mv"])


def _memattn_kernel(q_ref, k_ref, v_ref, o_ref):
    kb = k_ref[...].astype(BF16)
    vb = v_ref[...].astype(BF16)
    for h in range(X_HEADS):
        sl = slice(X_HEAD_DIM * h, X_HEAD_DIM * (h + 1))
        s = _dot_nt(q_ref[:, sl], kb[:, sl])
        p = jnp.exp(s - jnp.max(s, axis=1, keepdims=True))
        l = jnp.sum(p, axis=1, keepdims=True)
        o_ref[:, sl] = (jnp.dot(p.astype(BF16), vb[:, sl], preferred_element_type=F32) / l).astype(BF16)


def _memattn(qm, mem_k, mem_v, *, batch, seq, tm):
    nt = seq // tm
    tok = lambda b, i: (b * nt + i, 0)
    kv = lambda b, i: (b, 0)
    return pl.pallas_call(
        _memattn_kernel, grid=(batch, nt),
        in_specs=[pl.BlockSpec((tm, X_WIDTH), tok), pl.BlockSpec((N_MEM, X_WIDTH), kv),
                  pl.BlockSpec((N_MEM, X_WIDTH), kv)],
        out_specs=pl.BlockSpec((tm, X_WIDTH), tok),
        out_shape=jax.ShapeDtypeStruct((batch * seq, X_WIDTH), BF16),
        compiler_params=_params(2), name="memattn",
    )(qm, mem_k, mem_v)


def _memattn_step_kernel(q_ref, k_ref, v_ref, o_ref, *, rows):
    for g in range(rows):
        q4 = [q_ref[g:g + 1, X_HEAD_DIM * h:X_HEAD_DIM * (h + 1)] for h in range(X_HEADS)]
        q8 = jnp.concatenate(q4 + q4, axis=0)
        s = jnp.sum(k_ref[g] * q8[None], axis=2, keepdims=True)
        mx = jnp.max(s, axis=0)
        mx = jnp.maximum(mx, pltpu.roll(mx, X_HEADS, 0))
        p = jnp.exp(s - mx[None])
        l8 = jnp.sum(p, axis=0)
        o8 = jnp.sum(p * v_ref[g], axis=0)
        o_ref[g] = (o8[0:X_HEADS] + o8[X_HEADS:]) / (l8[0:X_HEADS] + l8[X_HEADS:])


def _memattn_step(qm, mem_k, mem_v, *, rows=8):
    nb = qm.shape[0]
    row = lambda i: (i, 0)
    kv = lambda i: (i, 0, 0, 0)
    kv_block = (rows, N_MEM // 2, 2 * X_HEADS, X_HEAD_DIM)
    return pl.pallas_call(
        functools.partial(_memattn_step_kernel, rows=rows), grid=(nb // rows,),
        in_specs=[pl.BlockSpec((rows, X_WIDTH), row), pl.BlockSpec(kv_block, kv), pl.BlockSpec(kv_block, kv)],
        out_specs=pl.BlockSpec((rows, X_HEADS, X_HEAD_DIM), lambda i: (i, 0, 0)),
        out_shape=jax.ShapeDtypeStruct((nb, X_HEADS, X_HEAD_DIM), F32),
        compiler_params=_params(1), name="memattn_step",
    )(qm, mem_k, mem_v).reshape(nb, X_WIDTH)


def _ffn_kernel(x1_ref, o_ref, wmo_ref, gpm_ref, gpf_ref, win_ref, wdn_ref, gpost_ref, y_ref, *, f_chunk):
    x2 = x1_ref[...] + _rms(_dot(o_ref[...], wmo_ref[...]), gpm_ref[...])
    h = _rms(x2, gpf_ref[...]).astype(BF16)
    acc = jnp.zeros(x2.shape, F32)
    for c in range(D_FF // f_chunk):
        sl = slice(f_chunk * c, f_chunk * (c + 1))
        g = jnp.dot(h, win_ref[:, sl], preferred_element_type=F32)
        u = jnp.dot(h, win_ref[:, D_FF + f_chunk * c:D_FF + f_chunk * (c + 1)], preferred_element_type=F32)
        acc = acc + jnp.dot((g * jax.nn.sigmoid(g) * u).astype(BF16), wdn_ref[sl, :], preferred_element_type=F32)
    y_ref[...] = x2 + _rms(acc, gpost_ref[...])


def _ffn(x1, o, w, *, tm, f_chunk=2816):
    m = x1.shape[0]
    row = lambda i: (i, 0)
    return pl.pallas_call(
        functools.partial(_ffn_kernel, f_chunk=f_chunk), grid=(m // tm,),
        in_specs=[pl.BlockSpec((tm, D_MODEL), row), pl.BlockSpec((tm, X_WIDTH), row),
                  _const_spec((X_WIDTH, D_MODEL)), _const_spec((1, D_MODEL)), _const_spec((1, D_MODEL)),
                  _const_spec((D_MODEL, 2 * D_FF)), _const_spec((D_FF, D_MODEL)),
                  _const_spec((1, D_MODEL))],
        out_specs=pl.BlockSpec((tm, D_MODEL), row),
        out_shape=jax.ShapeDtypeStruct((m, D_MODEL), F32),
        compiler_params=_params(1), name="ffn",
    )(x1, o, w["w_mo"], w["g_post_mem"], w["g_pre_ffn"], w["w_ffn_in"], w["w_ffn_out"], w["g_post_ffn"])


def _ffn_decode_kernel(pt_ref, x1_ref, om_ref, wmo_ref, gpm_ref, gpf_ref, win_ref, wdn_ref, gpost_ref,
                       q_ref, ckv_hbm, kr_hbm, cself_ref, tself_ref, y_ref, o_ref,
                       cbuf, rbuf, kbuf, sems, x2_scr, h_scr, acc_scr,
                       *, seqs, n_chunks, chunk_pages, ahead, f_chunk):
    i = pl.program_id(0)
    n_steps = pl.num_programs(0)
    n_slots = cbuf.shape[0]

    def page_copies(seq, c, p):
        slot = c % n_slots
        pg = pt_ref[seq, c * chunk_pages + p]
        dst = pl.ds(p * PAGE_SIZE, PAGE_SIZE)
        return (pltpu.make_async_copy(ckv_hbm.at[pg], cbuf.at[slot, dst], sems.at[0, slot]),
                pltpu.make_async_copy(kr_hbm.at[pg], rbuf.at[slot, :, dst], sems.at[1, slot]))

    def issue(seq, c):
        for p in range(chunk_pages):
            for cp in page_copies(seq, c, p):
                cp.start()

    def wait(c):
        for p in range(chunk_pages):
            for cp in page_copies(0, c, p):
                cp.wait()

    n_f = D_FF // f_chunk
    n_pieces = n_f + 2

    def ffn_piece(k):
        if k == 0:
            x2 = x1_ref[...] + _rms(_dot(om_ref[...], wmo_ref[...]), gpm_ref[...])
            x2_scr[...] = x2
            h_scr[...] = _rms(x2, gpf_ref[...]).astype(BF16)
            acc_scr[...] = jnp.zeros(acc_scr.shape, F32)
        elif k <= n_f:
            lo = f_chunk * (k - 1)
            h = h_scr[...]
            g = jnp.dot(h, win_ref[:, lo:lo + f_chunk], preferred_element_type=F32)
            u = jnp.dot(h, win_ref[:, D_FF + lo:D_FF + lo + f_chunk], preferred_element_type=F32)
            acc_scr[...] += jnp.dot((g * jax.nn.sigmoid(g) * u).astype(BF16), wdn_ref[lo:lo + f_chunk, :],
                                    preferred_element_type=F32)
        else:
            y_ref[...] = x2_scr[...] + _rms(acc_scr[...], gpost_ref[...])

    regions = seqs * n_chunks
    piece_at = {(k * regions) // n_pieces: k for k in range(n_pieces)}

    @pl.when(i == 0)
    def _():
        for c in range(ahead):
            issue(0, c)

    region = 0
    for q in range(seqs):
        seq = i * seqs + q
        qv = q_ref[q]
        q_lat = qv[:, 0:KV_LORA]
        q_rope = qv[:, KV_LORA:KV_LORA + QK_ROPE]

        def scores(c):
            slot = c % n_slots
            kc = cbuf[slot].astype(BF16)
            kbuf[c % 2] = kc
            kr = rbuf[slot].astype(BF16)
            return _dot_nt(q_lat, kc) + jnp.dot(q_rope, kr, preferred_element_type=F32)

        def accumulate(carry, s, c):
            m_old, l, acc = carry
            m_new = jnp.maximum(m_old, jnp.max(s, axis=1, keepdims=True))
            p = jnp.exp(s - m_new)
            alpha = jnp.exp(m_old - m_new)
            return (m_new, alpha * l + jnp.sum(p, axis=1, keepdims=True),
                    alpha * acc + jnp.dot(p.astype(BF16), kbuf[c % 2], preferred_element_type=F32))

        carry = (jnp.full((A_HEADS, 1), -jnp.inf, F32), jnp.zeros((A_HEADS, 1), F32),
                 jnp.zeros((A_HEADS, KV_LORA), F32))
        s_prev = None
        for c in range(n_chunks):
            nxt = c + ahead
            if nxt < n_chunks:
                issue(seq, nxt)
            elif q + 1 < seqs:
                issue(seq + 1, nxt - n_chunks)
            else:
                @pl.when(i + 1 < n_steps)
                def _():
                    issue(seq + 1, nxt - n_chunks)
            wait(c)
            s_cur = scores(c)
            if c > 0:
                carry = accumulate(carry, s_prev, c - 1)
            s_prev = s_cur
            if region in piece_at:
                ffn_piece(piece_at[region])
            region += 1
        m_old, l, acc = accumulate(carry, s_prev, n_chunks - 1)

        c_self = cself_ref[q].astype(BF16).astype(F32)
        r_self = tself_ref[q][:, 0:QK_ROPE].astype(BF16).astype(F32)
        s_self = (jnp.sum(q_lat.astype(F32) * c_self, axis=1, keepdims=True)
                  + jnp.sum(q_rope.astype(F32) * r_self, axis=1, keepdims=True))
        m_new = jnp.maximum(m_old, s_self)
        p_self = jnp.exp(s_self - m_new)
        alpha = jnp.exp(m_old - m_new)
        l = alpha * l + p_self
        acc = alpha * acc + p_self.astype(BF16).astype(F32) * c_self
        o_ref[q] = acc / l


def _ffn_decode(x1, om, w, page_table, qabs, cache_ckv, cache_krope_t, ckv_self, tail_self, *,
                tm=512, f_chunk=256, chunk_pages=16, n_slots=4, ahead=2):
    m = x1.shape[0]
    nb, n_pages = page_table.shape
    steps = m // tm
    seqs = nb // steps
    n_chunks = n_pages // chunk_pages
    assert steps * tm == m and seqs * steps == nb and n_chunks * chunk_pages == n_pages
    assert n_chunks % n_slots == 0 and ahead + 2 <= n_slots and D_FF % f_chunk == 0
    assert seqs * n_chunks >= D_FF // f_chunk + 2
    keys = chunk_pages * PAGE_SIZE
    row = lambda i, pt: (i, 0)
    per_seq = lambda i, pt: (i, 0, 0)
    const = lambda shape: pl.BlockSpec(shape, lambda i, pt: (0,) * len(shape), pipeline_mode=pl.Buffered(1))
    grid_spec = pltpu.PrefetchScalarGridSpec(
        num_scalar_prefetch=1,
        grid=(steps,),
        in_specs=[
            pl.BlockSpec((tm, D_MODEL), row), pl.BlockSpec((tm, X_WIDTH), row),
            const((X_WIDTH, D_MODEL)), const((1, D_MODEL)), const((1, D_MODEL)),
            const((D_MODEL, 2 * D_FF)), const((D_FF, D_MODEL)), const((1, D_MODEL)),
            pl.BlockSpec((seqs, A_HEADS, ABS_SLOT), per_seq),
            pl.BlockSpec(memory_space=pl.ANY),
            pl.BlockSpec(memory_space=pl.ANY),
            pl.BlockSpec((seqs, 1, KV_LORA), per_seq),
            pl.BlockSpec((seqs, 1, LANES), per_seq),
        ],
        out_specs=[pl.BlockSpec((tm, D_MODEL), row), pl.BlockSpec((seqs, A_HEADS, KV_LORA), per_seq)],
        scratch_shapes=[
            pltpu.VMEM((n_slots, keys, KV_LORA), F32),
            pltpu.VMEM((n_slots, QK_ROPE, keys), F32),
            pltpu.VMEM((2, keys, KV_LORA), BF16),
            pltpu.SemaphoreType.DMA((2, n_slots)),
            pltpu.VMEM((tm, D_MODEL), F32),
            pltpu.VMEM((tm, D_MODEL), BF16),
            pltpu.VMEM((tm, D_MODEL), F32),
        ],
    )
    return pl.pallas_call(
        functools.partial(_ffn_decode_kernel, seqs=seqs, n_chunks=n_chunks, chunk_pages=chunk_pages, ahead=ahead,
                          f_chunk=f_chunk),
        grid_spec=grid_spec,
        out_shape=[jax.ShapeDtypeStruct((m, D_MODEL), F32), jax.ShapeDtypeStruct((nb, A_HEADS, KV_LORA), F32)],
        compiler_params=_params(1), name="ffn_decode",
    )(page_table, x1, om, w["w_mo"], w["g_post_mem"], w["g_pre_ffn"], w["w_ffn_in"], w["w_ffn_out"], w["g_post_ffn"],
      qabs.reshape(nb, A_HEADS, ABS_SLOT), cache_ckv, cache_krope_t,
      ckv_self.reshape(nb, 1, KV_LORA), tail_self.reshape(nb, 1, LANES))


def _prep_weights(p):
    w = {}
    for name in ("g_pre_mix", "g_q_lora", "g_kv_lora", "g_mhead", "g_post_mix", "g_pre_mem", "g_mem", "g_post_mem",
                 "g_pre_ffn", "g_post_ffn"):
        w[name] = p[name].reshape(1, -1).astype(F32)
    w_in = p["w_in"]
    offs = np.cumsum((0, M_WIDTH, M_WIDTH, M_WIDTH, M_HEADS, M_HEADS, M_WIDTH, Q_LORA, KV_LORA, QK_ROPE,
                      D_MODEL, D_MODEL))
    seg = lambda i: w_in[:, offs[i]:offs[i + 1]]
    mq, mk, mv, mi, mf, mo, cq, ckv, kr, ga, gb = (seg(i) for i in range(11))
    half = QK_ROPE // 2
    swap = lambda a: jnp.concatenate([a[..., half:], a[..., :half]], axis=-1)
    tail = jnp.concatenate([kr, swap(kr), mi, mf,
                            jnp.zeros((D_MODEL, LANES - 2 * QK_ROPE - 2 * M_HEADS), w_in.dtype)], axis=1)
    w["w_qkv"] = w_in[:, offs[0]:offs[3]].astype(BF16)
    w["w_mo_gate"] = mo.astype(BF16)
    w["w_lat"] = w_in[:, offs[6]:offs[8]].astype(BF16)
    w["w_g"] = w_in[:, offs[9]:offs[11]].astype(BF16)
    w["w_tail"] = tail.astype(BF16)
    w["wv_t"] = mv.T.astype(BF16)
    w["b_tail"] = jnp.zeros((1, LANES), F32).at[0, TAIL_IG:TAIL_IG + 2 * M_HEADS].set(p["b_if"].astype(F32))

    pad = HEAD_SLOT - QK_NOPE - QK_ROPE
    wq = p["w_uq"].reshape(Q_LORA, A_HEADS, QK_NOPE + QK_ROPE)
    zq = jnp.zeros((Q_LORA, A_HEADS, pad), wq.dtype)
    w["wq_cat"] = jnp.concatenate([wq, zq], axis=2).reshape(Q_LORA, A_HEADS * HEAD_SLOT).astype(BF16)
    w_uk, w_uv = p["w_uk"], p["w_uv"]
    zk = jnp.zeros((KV_LORA, A_HEADS, HEAD_SLOT - QK_NOPE), w_uk.dtype)
    w["wuk_cat"] = jnp.concatenate([w_uk, zk], axis=2).reshape(KV_LORA, A_HEADS * HEAD_SLOT).astype(BF16)
    w["wuv_cat"] = jnp.concatenate([w_uv, zk], axis=2).reshape(KV_LORA, A_HEADS * HEAD_SLOT).astype(BF16)
    w["wuv_heads"] = (w_uv.transpose(1, 0, 2)[:, :, None, :]
                      * jnp.eye(A_HEADS, dtype=w_uv.dtype)[:, None, :, None]).reshape(
                          A_HEADS, KV_LORA, A_HEADS * V_HEAD).astype(BF16)
    pass_np = np.zeros((A_HEADS, HEAD_SLOT - QK_NOPE, ABS_SLOT), np.float32)
    pass_np[:, np.arange(QK_ROPE), KV_LORA + np.arange(QK_ROPE)] = 1.0
    w["w_abs"] = jnp.concatenate(
        [jnp.concatenate([w_uk.transpose(1, 2, 0), jnp.zeros((A_HEADS, QK_NOPE, ABS_SLOT - KV_LORA), w_uk.dtype)],
                         axis=2).astype(BF16), jnp.asarray(pass_np, BF16)], axis=1)
    for name in ("w_branch_a", "w_branch_b", "w_out", "w_mq", "w_mk", "w_mv", "w_mo", "w_ffn_out"):
        w[name] = p[name].astype(BF16)
    w["w_ffn_in"] = p["w_ffn_in"].astype(BF16)
    return w


def _rope_tables(pos0, n):
    pos = (pos0 + jnp.arange(n)).astype(F32)
    inv = ROPE_BASE ** (-jnp.arange(0, QK_ROPE, 2, dtype=F32) / QK_ROPE)
    ang = pos[:, None] * inv[None, :]
    cos, sin = jnp.cos(ang), jnp.sin(ang)
    c32 = jnp.concatenate([cos, cos], axis=1)
    s32 = jnp.concatenate([-sin, sin], axis=1)
    one = lambda k: jnp.ones((n, k), F32)
    zero = lambda k: jnp.zeros((n, k), F32)
    ct = jnp.concatenate([c32, one(LANES - QK_ROPE)], axis=1)
    st = jnp.concatenate([s32, zero(LANES - QK_ROPE)], axis=1)
    c128 = jnp.concatenate([one(QK_NOPE), c32, one(HEAD_SLOT - QK_NOPE - QK_ROPE)], axis=1)
    s128 = jnp.concatenate([zero(QK_NOPE), s32, zero(HEAD_SLOT - QK_NOPE - QK_ROPE)], axis=1)
    return ct, st, c128, s128


def kernel(x_prompt, x_sample, cache_ckv, cache_krope, cache_mem_k, cache_mem_v, state_C, state_n, state_m, page_table, mem_prompt, g_pre_mix, w_in, b_if, g_mhead, g_q_lora, w_uq, g_kv_lora, w_uk, w_uv, w_branch_a, w_branch_b, w_out, g_post_mix, g_pre_mem, g_mem, w_mq, w_mk, w_mv, w_mo, g_post_mem, g_pre_ffn, w_ffn_in, w_ffn_out, g_post_ffn):
    params = dict(g_pre_mix=g_pre_mix, w_in=w_in, b_if=b_if, g_mhead=g_mhead, g_q_lora=g_q_lora, w_uq=w_uq,
                  g_kv_lora=g_kv_lora, w_uk=w_uk, w_uv=w_uv, w_branch_a=w_branch_a, w_branch_b=w_branch_b,
                  w_out=w_out, g_post_mix=g_post_mix, g_pre_mem=g_pre_mem, g_mem=g_mem, w_mq=w_mq, w_mk=w_mk,
                  w_mv=w_mv, w_mo=w_mo, g_post_mem=g_post_mem, g_pre_ffn=g_pre_ffn, w_ffn_in=w_ffn_in,
                  w_ffn_out=w_ffn_out, g_post_ffn=g_post_ffn)
    depth = w_in.shape[0]
    assert depth == 1, "single-layer stack only"
    bp, seq, _ = x_prompt.shape
    bs, dec_seq, _ = x_sample.shape
    assert dec_seq == 1, "one new token per sample sequence"
    past_len = page_table.shape[1] * PAGE_SIZE
    w = _prep_weights({name: a[0] for name, a in params.items()})

    xp = x_prompt.reshape(bp * seq, D_MODEL)
    qkvm, og, ckv_p, sg, krope_t, gates_t, qcat, kcat, vcat, vt = _inproj(
        xp, w, _rope_tables(0, seq), tm=512, absorbed=False, mdtype=BF16)
    hg, c_p, n_p, m_p = _mlstm_prompt(qkvm, vt, gates_t, og, w["g_mhead"], batch=bp, seq=seq)
    ob = _mla_prefill(qcat, kcat, vcat, batch=bp, seq=seq, blk=512)
    x1, qm = _merge(xp, hg, ob, sg, w, tm=512, from_latent=False, qdtype=BF16)
    mem_k, mem_v = _memkv(mem_prompt.reshape(bp * N_MEM, D_MODEL), w, tm=512)
    om = _memattn(qm, mem_k, mem_v, batch=bp, seq=seq, tm=512)

    xs = x_sample.reshape(bs, D_MODEL)
    tables_s = tuple(jnp.broadcast_to(t, (bs, LANES)) for t in _rope_tables(past_len, 1))
    qkvm_s, og_s, ckv_s, sg_s, tail_s, qabs = _inproj(xs, w, tables_s, tm=bs, absorbed=True, mdtype=F32)
    m0_pad = jnp.pad(state_m.reshape(bs, M_HEADS).astype(F32), ((0, 0), (0, LANES - M_HEADS)))
    hg_s, c_s, n_s, m_s = _mlstm_step(qkvm_s, tail_s, state_C.reshape(bs, M_HEADS, M_HEAD_DIM, M_HEAD_DIM),
                                      state_n.reshape(bs, M_HEADS, M_HEAD_DIM), m0_pad, og_s, w["g_mhead"])
    n_phys = cache_ckv.shape[1]
    cache_krope_t = jnp.transpose(cache_krope.reshape(n_phys, PAGE_SIZE, QK_ROPE), (0, 2, 1))
    y_prompt, o_lat = _ffn_decode(x1, om, w, page_table, qabs, cache_ckv.reshape(n_phys, PAGE_SIZE, KV_LORA),
                                  cache_krope_t, ckv_s, tail_s)
    y_prompt = y_prompt.reshape(bp, seq, D_MODEL)
    x1_s, qm_s = _merge(xs, hg_s, o_lat.reshape(bs, A_HEADS * KV_LORA), sg_s, w, tm=bs, from_latent=True, qdtype=F32)
    om_s = _memattn_step(qm_s, cache_mem_k.reshape(bs, N_MEM // 2, 2 * X_HEADS, X_HEAD_DIM),
                         cache_mem_v.reshape(bs, N_MEM // 2, 2 * X_HEADS, X_HEAD_DIM))
    y_sample = _ffn(x1_s, om_s, w, tm=bs).reshape(bs, 1, D_MODEL)

    return (y_prompt, y_sample,
            ckv_p.reshape(1, bp, seq, KV_LORA), jnp.transpose(krope_t, (0, 2, 1)).reshape(1, bp, seq, QK_ROPE),
            c_p.reshape(1, bp, M_HEADS, M_HEAD_DIM, M_HEAD_DIM), n_p.reshape(1, bp, M_HEADS, M_HEAD_DIM),
            m_p[:, :, 0].reshape(1, bp, M_HEADS),
            mem_k.reshape(1, bp, N_MEM, X_HEADS, X_HEAD_DIM), mem_v.reshape(1, bp, N_MEM, X_HEADS, X_HEAD_DIM),
            ckv_s.reshape(1, bs, 1, KV_LORA), tail_s[:, :QK_ROPE].reshape(1, bs, 1, QK_ROPE),
            c_s.reshape(1, bs, M_HEADS, M_HEAD_DIM, M_HEAD_DIM), n_s.reshape(1, bs, M_HEADS, M_HEAD_DIM),
            m_s[:, :M_HEADS].reshape(1, bs, M_HEADS))
```

```python
import functools

import jax
import jax.numpy as jnp
import numpy as np
from jax import lax
from jax.experimental import pallas as pl
from jax.experimental.pallas import tpu as pltpu

F32 = jnp.float32
BF16 = jnp.bfloat16

D_MODEL = 1024
PAGE_SIZE = 128
M_HEADS = 4
M_HEAD_DIM = 128
M_WIDTH = M_HEADS * M_HEAD_DIM
M_CHUNK = 128
A_HEADS = 8
QK_NOPE = 64
QK_ROPE = 32
V_HEAD = 64
Q_LORA = 384
KV_LORA = 256
ROPE_BASE = 10000.0
N_MEM = 256
X_HEADS = 4
X_HEAD_DIM = 128
X_WIDTH = X_HEADS * X_HEAD_DIM
D_FF = 2816
EPS = 1e-6

LANES = 128
HEAD_SLOT = 128
ABS_SLOT = 384
ATT_SCALE = (QK_NOPE + QK_ROPE) ** -0.5
VMEM_LIMIT = 52 * 1024 * 1024

TAIL_IG = 64
TAIL_LF = 68


def _rms(x, g):
    return x * lax.rsqrt(jnp.mean(x * x, axis=-1, keepdims=True) + EPS) * g


def _dot(a, b):
    return jnp.dot(a.astype(BF16), b.astype(BF16), preferred_element_type=F32)


def _dot_nt(a, b):
    return lax.dot_general(a.astype(BF16), b.astype(BF16), (((1,), (1,)), ((), ())),
                           preferred_element_type=F32)


def _dot_tn(a, b):
    return lax.dot_general(a.astype(BF16), b.astype(BF16), (((0,), (0,)), ((), ())),
                           preferred_element_type=F32)


def _log_sigmoid(x):
    return jnp.minimum(x, 0.0) - jnp.log1p(jnp.exp(-jnp.abs(x)))


def _const_spec(shape):
    nd = len(shape)
    return pl.BlockSpec(shape, lambda *_: (0,) * nd, pipeline_mode=pl.Buffered(1))


def _params(n_axes):
    return pltpu.CompilerParams(dimension_semantics=("arbitrary",) * n_axes, vmem_limit_bytes=VMEM_LIMIT)


def _inproj_kernel(x_ref, gpre_ref, wqkv_ref, wmo_ref, wlat_ref, wg_ref, wtail_ref, btail_ref, ct_ref, st_ref,
                   gq_ref, wq_ref, c128_ref, s128_ref, gkv_ref, *rest, absorbed):
    if absorbed:
        wabs_ref, qkvm_ref, og_ref, ckv_ref, sg_ref, tail_ref, qabs_ref = rest
    else:
        (wuk_ref, wuv_ref, wvt_ref,
         qkvm_ref, og_ref, ckv_ref, sg_ref, krt_ref, gt_ref, qcat_ref, kcat_ref, vcat_ref, vt_ref) = rest
    h = _rms(x_ref[...], gpre_ref[...]).astype(BF16)

    zm = jnp.dot(h, wqkv_ref[...], preferred_element_type=F32)
    qkvm_ref[:, 0:M_WIDTH] = zm[:, 0:M_WIDTH].astype(qkvm_ref.dtype)
    qkvm_ref[:, M_WIDTH:2 * M_WIDTH] = (zm[:, M_WIDTH:2 * M_WIDTH] * (M_HEAD_DIM ** -0.5)).astype(qkvm_ref.dtype)
    qkvm_ref[:, 2 * M_WIDTH:] = zm[:, 2 * M_WIDTH:].astype(qkvm_ref.dtype)
    og_ref[...] = jax.nn.sigmoid(jnp.dot(h, wmo_ref[...], preferred_element_type=F32)).astype(og_ref.dtype)
    sg_ref[...] = jax.nn.sigmoid(jnp.dot(h, wg_ref[...], preferred_element_type=F32)).astype(sg_ref.dtype)

    t = jnp.dot(h, wtail_ref[...], preferred_element_type=F32) + btail_ref[...]
    t = t * ct_ref[...] + pltpu.roll(t, LANES - QK_ROPE, 1) * st_ref[...]
    lane = lax.broadcasted_iota(jnp.int32, t.shape, 1)
    is_lf = jnp.logical_and(lane >= TAIL_LF, lane < TAIL_LF + M_HEADS)
    t = jnp.where(is_lf, _log_sigmoid(t), t)
    if absorbed:
        tail_ref[...] = t
    else:
        tt = jnp.transpose(t)
        krt_ref[0] = tt[0:QK_ROPE]
        gt_ref[...] = tt[TAIL_IG:TAIL_IG + 2 * M_HEADS]

    cqn = _rms(jnp.dot(h, wlat_ref[:, 0:Q_LORA], preferred_element_type=F32), gq_ref[...]).astype(BF16)
    ckvn = _rms(jnp.dot(h, wlat_ref[:, Q_LORA:], preferred_element_type=F32), gkv_ref[...])
    ckv_ref[...] = ckvn

    qc = jnp.dot(cqn, wq_ref[...], preferred_element_type=F32)
    width = A_HEADS * HEAD_SLOT
    first_half = (lax.broadcasted_iota(jnp.int32, qc.shape, 1) % HEAD_SLOT) < QK_NOPE + QK_ROPE // 2
    qs = jnp.where(first_half, pltpu.roll(qc, width - QK_ROPE // 2, 1), pltpu.roll(qc, QK_ROPE // 2, 1))
    c128 = c128_ref[...]
    s128 = s128_ref[...]
    for hh in range(A_HEADS):
        sl = slice(HEAD_SLOT * hh, HEAD_SLOT * (hh + 1))
        qh = ((qc[:, sl] * c128 + qs[:, sl] * s128) * ATT_SCALE).astype(BF16)
        if absorbed:
            qabs_ref[:, ABS_SLOT * hh:ABS_SLOT * (hh + 1)] = jnp.dot(
                qh, wabs_ref[hh], preferred_element_type=F32).astype(BF16)
        else:
            qcat_ref[:, sl] = qh
    if not absorbed:
        cb = ckvn.astype(BF16)
        kr_slot = jnp.where(jnp.logical_and(lane >= QK_NOPE, lane < QK_NOPE + QK_ROPE), pltpu.roll(t, QK_NOPE, 1), 0.0)
        kn = jnp.dot(cb, wuk_ref[...], preferred_element_type=F32)
        for hh in range(A_HEADS):
            sl = slice(HEAD_SLOT * hh, HEAD_SLOT * (hh + 1))
            kcat_ref[:, sl] = (kn[:, sl] + kr_slot).astype(BF16)
        vlane = lax.broadcasted_iota(jnp.int32, (1, A_HEADS * HEAD_SLOT), 1) % HEAD_SLOT
        vcat = jnp.dot(cb, wuv_ref[...], preferred_element_type=F32)
        vcat_ref[...] = jnp.where(vlane == V_HEAD, 1.0, vcat).astype(BF16)
        vt_ref[...] = _dot_nt(wvt_ref[...], h).astype(BF16)


def _inproj(x, w, tables, *, tm, absorbed, mdtype):
    m = x.shape[0]
    ct, st, c128, s128 = tables
    seq = ct.shape[0]
    nt = seq // tm
    grid = (m // tm,)
    row = lambda i: (i, 0)
    tab = lambda i: (i % nt, 0)
    wide = A_HEADS * HEAD_SLOT
    in_specs = [
        pl.BlockSpec((tm, D_MODEL), row), _const_spec((1, D_MODEL)),
        _const_spec((D_MODEL, 3 * M_WIDTH)), _const_spec((D_MODEL, M_WIDTH)), _const_spec((D_MODEL, Q_LORA + KV_LORA)),
        _const_spec((D_MODEL, 2 * D_MODEL)), _const_spec((D_MODEL, LANES)), _const_spec((1, LANES)),
        pl.BlockSpec((tm, LANES), tab), pl.BlockSpec((tm, LANES), tab),
        _const_spec((1, Q_LORA)), _const_spec((Q_LORA, wide)),
        pl.BlockSpec((tm, LANES), tab), pl.BlockSpec((tm, LANES), tab),
        _const_spec((1, KV_LORA)),
    ]
    args = [x, w["g_pre_mix"], w["w_qkv"], w["w_mo_gate"], w["w_lat"], w["w_g"], w["w_tail"], w["b_tail"], ct, st,
            w["g_q_lora"], w["wq_cat"], c128, s128, w["g_kv_lora"]]
    outs = [
        (jax.ShapeDtypeStruct((m, 3 * M_WIDTH), mdtype), pl.BlockSpec((tm, 3 * M_WIDTH), row)),
        (jax.ShapeDtypeStruct((m, M_WIDTH), mdtype), pl.BlockSpec((tm, M_WIDTH), row)),
        (jax.ShapeDtypeStruct((m, KV_LORA), F32), pl.BlockSpec((tm, KV_LORA), row)),
        (jax.ShapeDtypeStruct((m, 2 * D_MODEL), mdtype), pl.BlockSpec((tm, 2 * D_MODEL), row)),
    ]
    if absorbed:
        in_specs += [_const_spec((A_HEADS, HEAD_SLOT, ABS_SLOT))]
        args += [w["w_abs"]]
        outs += [(jax.ShapeDtypeStruct((m, LANES), F32), pl.BlockSpec((tm, LANES), row)),
                 (jax.ShapeDtypeStruct((m, A_HEADS * ABS_SLOT), BF16), pl.BlockSpec((tm, A_HEADS * ABS_SLOT), row))]
    else:
        in_specs += [_const_spec((KV_LORA, wide)), _const_spec((KV_LORA, wide)), _const_spec((M_WIDTH, D_MODEL))]
        args += [w["wuk_cat"], w["wuv_cat"], w["wv_t"]]
        outs += [(jax.ShapeDtypeStruct((m // seq, QK_ROPE, seq), F32),
                  pl.BlockSpec((1, QK_ROPE, tm), lambda i: (i // nt, 0, i % nt))),
                 (jax.ShapeDtypeStruct((2 * M_HEADS, m), F32), pl.BlockSpec((2 * M_HEADS, tm), lambda i: (0, i)))]
        outs += [(jax.ShapeDtypeStruct((m, wide), BF16), pl.BlockSpec((tm, wide), row))] * 3
        outs += [(jax.ShapeDtypeStruct((M_WIDTH, m), BF16), pl.BlockSpec((M_WIDTH, tm), lambda i: (0, i)))]
    return pl.pallas_call(
        functools.partial(_inproj_kernel, absorbed=absorbed),
        grid=grid, in_specs=in_specs,
        out_specs=[o[1] for o in outs], out_shape=[o[0] for o in outs],
        compiler_params=_params(1), name="inproj_abs" if absorbed else "inproj",
    )(*args)


def _split3_dot(x, m01):
    hi = x.astype(BF16)
    r1 = x - hi.astype(F32)
    mid = r1.astype(BF16)
    lo = (r1 - mid.astype(F32)).astype(BF16)
    return sum(jnp.dot(part, m01, preferred_element_type=F32) for part in (hi, mid, lo))


def _cumsum_lanes(x, upper):
    return _split3_dot(x, jnp.where(upper, 1.0, 0.0).astype(BF16))


def _cummax_lanes(x):
    lane = lax.broadcasted_iota(jnp.int32, x.shape, 1)
    shift = 1
    while shift < x.shape[1]:
        x = jnp.maximum(x, jnp.where(lane >= shift, pltpu.roll(x, shift, 1), -jnp.inf))
        shift *= 2
    return x


def _mlstm_gates_kernel(g_ref, urow_ref, scal_ref, cols_ref, *, nc, nseq):
    L = M_CHUNK
    H = M_HEADS
    rows = nseq * nc * 2 * H
    g = g_ref[...]
    r = jnp.concatenate([g[:, L * c:L * (c + 1)] for c in range(nseq * nc)], axis=0)
    ri = lax.broadcasted_iota(jnp.int32, (rows, L), 0)
    ci = lax.broadcasted_iota(jnp.int32, (rows, L), 1)
    top = (ri % (2 * H)) < H
    b_all = _cumsum_lanes(r, lax.broadcasted_iota(jnp.int32, (L, L), 0) <= lax.broadcasted_iota(jnp.int32, (L, L), 1))
    b = pltpu.roll(b_all, rows - H, 0)
    u = r - b
    cm = _cummax_lanes(u)
    g_last = jnp.max(jnp.where(ci == L - 1, b, -jnp.inf), axis=1, keepdims=True)
    wlog = g_last - b + r
    wmax = jnp.max(wlog, axis=1, keepdims=True)
    urow_ref[...] = jnp.where(top, u, pltpu.roll(wlog, H, 0)).reshape(nseq, nc, 2 * H, L)
    scal = jnp.where(top, jnp.broadcast_to(g_last, (rows, L)), pltpu.roll(jnp.broadcast_to(wmax, (rows, L)), H, 0))
    scal_ref[...] = scal.reshape(nseq, nc, 2 * H, L)
    first = jnp.where(top, b, pltpu.roll(cm, H, 0))
    for c in range(nseq * nc):
        sl = slice(2 * H * c, 2 * H * (c + 1))
        cols_ref[c // nc, c % nc] = jnp.transpose(jnp.concatenate([first[sl], wlog[sl]], axis=0))


def _mlstm_gates(g, *, batch, nc, nseq=4):
    blk = lambda b: (b, 0, 0, 0)
    return pl.pallas_call(
        functools.partial(_mlstm_gates_kernel, nc=nc, nseq=nseq),
        grid=(batch // nseq,),
        in_specs=[pl.BlockSpec((2 * M_HEADS, nseq * nc * M_CHUNK), lambda b: (0, b))],
        out_specs=[pl.BlockSpec((nseq, nc, 2 * M_HEADS, M_CHUNK), blk),
                   pl.BlockSpec((nseq, nc, 2 * M_HEADS, M_CHUNK), blk),
                   pl.BlockSpec((nseq, nc, M_CHUNK, 4 * M_HEADS), blk)],
        out_shape=[jax.ShapeDtypeStruct((batch, nc, 2 * M_HEADS, M_CHUNK), F32),
                   jax.ShapeDtypeStruct((batch, nc, 2 * M_HEADS, M_CHUNK), F32),
                   jax.ShapeDtypeStruct((batch, nc, M_CHUNK, 4 * M_HEADS), F32)],
        compiler_params=_params(1), name="mlstm_gates",
    )(g)


def _mlstm_kernel(*refs, nb):
    qkv_ref = refs[0]
    vt_refs = refs[1:1 + nb]
    urow_ref, scal_ref, cols_ref, sel_ref, og_ref, gm_ref, hg_ref, c_ref, n_ref, m_ref = refs[1 + nb:]
    L = M_CHUNK

    @pl.when(pl.program_id(1) == 0)
    def _():
        c_ref[...] = jnp.zeros_like(c_ref)
        n_ref[...] = jnp.zeros_like(n_ref)
        m_ref[...] = jnp.zeros_like(m_ref)

    row = lax.broadcasted_iota(jnp.int32, (L, L), 0)
    col = lax.broadcasted_iota(jnp.int32, (L, L), 1)
    tril = col <= row

    for bb in range(nb):
        vt_ref = vt_refs[bb]
        rows8 = urow_ref[bb, 0]
        u4 = rows8[0:M_HEADS]
        wlog4 = rows8[M_HEADS:2 * M_HEADS]
        scal = scal_ref[bb, 0]
        g_last4 = scal[0:M_HEADS, 0:1]
        wmax4 = scal[M_HEADS:2 * M_HEADS, 0:1]
        cols = _split3_dot(cols_ref[bb, 0], sel_ref[...])
        m_prev4 = m_ref[bb][:, 0:1]
        n_prev4 = n_ref[bb]
        m_new4 = jnp.maximum(g_last4 + m_prev4, wmax4)
        decay4 = jnp.exp(g_last4 + m_prev4 - m_new4)
        ws4 = jnp.exp(wlog4 - m_new4)
        m_ref[bb] = jnp.broadcast_to(m_new4, (M_HEADS, LANES))
        n_rows = []

        for h in range(M_HEADS):
            sl = slice(M_HEAD_DIM * h, M_HEAD_DIM * (h + 1))
            q = qkv_ref[bb, :, sl]
            k = qkv_ref[bb, :, M_WIDTH + M_HEAD_DIM * h:M_WIDTH + M_HEAD_DIM * (h + 1)]
            v = qkv_ref[bb, :, 2 * M_WIDTH + M_HEAD_DIM * h:2 * M_WIDTH + M_HEAD_DIM * (h + 1)]
            m_prev = m_prev4[h:h + 1]
            c_prev = c_ref[bb, h]
            n_prev = n_prev4[h:h + 1]
            b_col = cols[:, LANES * h:LANES * (h + 1)]
            mm_col = jnp.maximum(cols[:, LANES * (M_HEADS + h):LANES * (M_HEADS + h + 1)], m_prev)
            w_intra = jnp.where(tril, jnp.exp(u4[h:h + 1] - mm_col), 0.0)
            w_inter = jnp.exp(m_prev - mm_col)
            s = _dot_nt(q, k) * w_intra
            cn = jnp.concatenate([c_prev.astype(BF16), jnp.broadcast_to(n_prev.astype(BF16), (L, M_HEAD_DIM))],
                                 axis=0)
            qcn = _dot_nt(q, cn)
            num = w_inter * qcn[:, 0:M_HEAD_DIM] + _dot(s, v)
            den = w_inter * qcn[:, M_HEAD_DIM:] + jnp.sum(s, axis=1, keepdims=True)
            hs = num / jnp.maximum(jnp.abs(den), jnp.exp(-(b_col + mm_col)))

            ws = ws4[h:h + 1]
            lhs = jnp.concatenate([(vt_ref[sl, :].astype(F32) * ws).astype(BF16),
                                   jnp.broadcast_to(ws.astype(BF16), (16, L))], axis=0)
            upd = jnp.dot(lhs, k, preferred_element_type=F32)
            decay = decay4[h:h + 1]
            c_ref[bb, h] = decay * c_prev + upd[0:M_HEAD_DIM]
            n_rows.append(decay * n_prev + upd[M_HEAD_DIM:M_HEAD_DIM + 1])

            mu = jnp.mean(hs, axis=1, keepdims=True)
            d = hs - mu
            y = d * lax.rsqrt(jnp.mean(d * d, axis=1, keepdims=True) + EPS) * gm_ref[:, sl]
            hg_ref[bb, :, sl] = (y * og_ref[bb, :, sl]).astype(BF16)
        n_ref[bb] = jnp.concatenate(n_rows, axis=0)


def _mlstm_prompt(qkvm, vt, gates, og, g_mhead, *, batch, seq, nb=8):
    nc = seq // M_CHUNK
    tok = lambda b, c: (b, c, 0)
    chunk = lambda b, c: (b, c, 0, 0)
    urow, scal, cols = _mlstm_gates(gates, batch=batch, nc=nc)
    sel = jnp.asarray(np.kron(np.eye(4 * M_HEADS, 2 * M_HEADS), np.ones((1, LANES))), BF16)
    vt_specs = [pl.BlockSpec((M_WIDTH, M_CHUNK), functools.partial(lambda b, c, bb: (0, (nb * b + bb) * nc + c), bb=bb))
                for bb in range(nb)]
    hg, c_p, n_p, m_p = pl.pallas_call(
        functools.partial(_mlstm_kernel, nb=nb),
        grid=(batch // nb, nc),
        in_specs=[pl.BlockSpec((nb, M_CHUNK, 3 * M_WIDTH), tok)] + vt_specs + [
            pl.BlockSpec((nb, 1, 2 * M_HEADS, M_CHUNK), chunk),
            pl.BlockSpec((nb, 1, 2 * M_HEADS, M_CHUNK), chunk),
            pl.BlockSpec((nb, 1, M_CHUNK, 4 * M_HEADS), chunk),
            pl.BlockSpec((4 * M_HEADS, 2 * M_HEADS * LANES), lambda b, c: (0, 0)),
            pl.BlockSpec((nb, M_CHUNK, M_WIDTH), tok),
            pl.BlockSpec((1, M_WIDTH), lambda b, c: (0, 0)),
        ],
        out_specs=[
            pl.BlockSpec((nb, M_CHUNK, M_WIDTH), tok),
            pl.BlockSpec((nb, M_HEADS, M_HEAD_DIM, M_HEAD_DIM), lambda b, c: (b, 0, 0, 0)),
            pl.BlockSpec((nb, M_HEADS, M_HEAD_DIM), lambda b, c: (b, 0, 0)),
            pl.BlockSpec((nb, M_HEADS, LANES), lambda b, c: (b, 0, 0)),
        ],
        out_shape=[
            jax.ShapeDtypeStruct((batch, seq, M_WIDTH), BF16),
            jax.ShapeDtypeStruct((batch, M_HEADS, M_HEAD_DIM, M_HEAD_DIM), F32),
            jax.ShapeDtypeStruct((batch, M_HEADS, M_HEAD_DIM), F32),
            jax.ShapeDtypeStruct((batch, M_HEADS, LANES), F32),
        ],
        compiler_params=_params(2), name="mlstm_prompt",
    )(qkvm.reshape(batch, seq, 3 * M_WIDTH), *([vt] * nb), urow, scal, cols, sel,
      og.reshape(batch, seq, M_WIDTH), g_mhead)
    return hg.reshape(batch * seq, M_WIDTH), c_p, n_p, m_p


def _mlstm_step_kernel(qkv_ref, tail_ref, c_ref, n_ref, m_ref, og_ref, gm_ref, hg_ref, co_ref, no_ref, mo_ref, *, rows):
    D = M_HEAD_DIM
    lane = lax.broadcasted_iota(jnp.int32, (rows, LANES), 1)
    rowi = lax.broadcasted_iota(jnp.int32, (rows, D), 0)
    tail = tail_ref[...]
    m_in = m_ref[...]
    m_out = jnp.zeros((rows, LANES), F32)
    for h in range(M_HEADS):
        sl = slice(D * h, D * (h + 1))
        q = qkv_ref[:, sl]
        k = qkv_ref[:, M_WIDTH + D * h:M_WIDTH + D * (h + 1)]
        v = qkv_ref[:, 2 * M_WIDTH + D * h:2 * M_WIDTH + D * (h + 1)]
        ig = tail[:, TAIL_IG + h:TAIL_IG + h + 1]
        lf = tail[:, TAIL_LF + h:TAIL_LF + h + 1]
        m_prev = m_in[:, h:h + 1]
        n_prev = n_ref[:, h, :]
        a = lf + m_prev
        mt = jnp.maximum(a, ig)
        w_in = jnp.exp(ig - mt)
        w_st = jnp.exp(a - mt)
        s = jnp.sum(q * k, axis=1, keepdims=True) * w_in
        cq = jnp.zeros((rows, D), F32)
        for g in range(rows):
            cq = jnp.where(rowi == g, _dot_nt(q, c_ref[g, h]), cq)
        num = w_st * cq + s * v
        den = w_st * jnp.sum(n_prev * q, axis=1, keepdims=True) + s
        hs = num / jnp.maximum(jnp.abs(den), jnp.exp(-mt))
        vw_t = jnp.transpose(v * w_in)
        for g in range(rows):
            co_ref[g, h] = w_st[g:g + 1] * c_ref[g, h] + vw_t[:, g:g + 1] * k[g:g + 1, :]
        no_ref[:, h, :] = w_st * n_prev + w_in * k
        m_out = jnp.where(lane == h, mt, m_out)
        mu = jnp.mean(hs, axis=1, keepdims=True)
        d = hs - mu
        y = d * lax.rsqrt(jnp.mean(d * d, axis=1, keepdims=True) + EPS) * gm_ref[:, sl]
        hg_ref[:, sl] = (y * og_ref[:, sl]).astype(BF16)
    mo_ref[...] = m_out


def _mlstm_step(qkvm, tail, c0, n0, m0_pad, og, g_mhead, *, rows=8):
    nb = qkvm.shape[0]
    row = lambda i: (i, 0)
    return pl.pallas_call(
        functools.partial(_mlstm_step_kernel, rows=rows),
        grid=(nb // rows,),
        in_specs=[
            pl.BlockSpec((rows, 3 * M_WIDTH), row),
            pl.BlockSpec((rows, LANES), row),
            pl.BlockSpec((rows, M_HEADS, M_HEAD_DIM, M_HEAD_DIM), lambda i: (i, 0, 0, 0)),
            pl.BlockSpec((rows, M_HEADS, M_HEAD_DIM), lambda i: (i, 0, 0)),
            pl.BlockSpec((rows, LANES), row),
            pl.BlockSpec((rows, M_WIDTH), row),
            pl.BlockSpec((1, M_WIDTH), lambda i: (0, 0)),
        ],
        out_specs=[
            pl.BlockSpec((rows, M_WIDTH), row),
            pl.BlockSpec((rows, M_HEADS, M_HEAD_DIM, M_HEAD_DIM), lambda i: (i, 0, 0, 0)),
            pl.BlockSpec((rows, M_HEADS, M_HEAD_DIM), lambda i: (i, 0, 0)),
            pl.BlockSpec((rows, LANES), row),
        ],
        out_shape=[
            jax.ShapeDtypeStruct((nb, M_WIDTH), BF16),
            jax.ShapeDtypeStruct((nb, M_HEADS, M_HEAD_DIM, M_HEAD_DIM), F32),
            jax.ShapeDtypeStruct((nb, M_HEADS, M_HEAD_DIM), F32),
            jax.ShapeDtypeStruct((nb, LANES), F32),
        ],
        compiler_params=_params(1), name="mlstm_step",
    )(qkvm, tail, c0, n0, m0_pad, og, g_mhead)


def _mla_prefill_kernel(q_ref, k_ref, v_ref, o_ref, s_scr, p_scr, m_scr, acc_scr, *, blk, nq, heads, rows):
    i = pl.program_id(2)
    reps = blk // LANES
    m_scr[...] = jnp.full(m_scr.shape, -jnp.inf, F32)
    acc_scr[...] = jnp.zeros(acc_scr.shape, F32)

    def block(j, masked):
        start = j * blk
        slot = j % 2
        for hh in range(heads):
            sl = slice(HEAD_SLOT * hh, HEAD_SLOT * (hh + 1))
            s_scr[slot, hh] = _dot_nt(q_ref[:, sl], k_ref[pl.ds(start, blk), sl])
        for hh in range(heads):
            for r in range(blk // rows):
                rs = slice(rows * r, rows * (r + 1))
                s = s_scr[slot, hh, rs, :]
                if masked:
                    qi = lax.broadcasted_iota(jnp.int32, (rows, blk), 0) + rows * r
                    ki = lax.broadcasted_iota(jnp.int32, (rows, blk), 1)
                    s = jnp.where(ki <= qi, s, -jnp.inf)
                m_old = m_scr[hh, rs, :]
                m_new = jnp.maximum(m_old, jnp.max(s, axis=1, keepdims=True))
                p_scr[hh, rs, :] = jnp.exp(s - jnp.concatenate([m_new] * reps, axis=1)).astype(BF16)
                acc_scr[hh, rs, :] = acc_scr[hh, rs, :] * jnp.exp(m_old - m_new)
                m_scr[hh, rs, :] = m_new
        for hh in range(heads):
            sl = slice(HEAD_SLOT * hh, HEAD_SLOT * (hh + 1))
            acc_scr[hh] += jnp.dot(p_scr[hh], v_ref[pl.ds(start, blk), sl], preferred_element_type=F32)

    for n_blocks in range(1, nq + 1):
        @pl.when(i == n_blocks - 1)
        def _():
            for j in range(n_blocks):
                block(j, j == n_blocks - 1)

    o0, o1 = (acc_scr[hh] / acc_scr[hh][:, V_HEAD:V_HEAD + 1] for hh in range(2))
    lane = lax.broadcasted_iota(jnp.int32, (blk, HEAD_SLOT), 1)
    o_ref[...] = jnp.where(lane < V_HEAD, o0, pltpu.roll(o1, V_HEAD, 1)).astype(BF16)


def _mla_prefill(qcat, kcat, vcat, *, batch, seq, blk, rows=32):
    heads = 2
    nq = seq // blk
    wide = heads * HEAD_SLOT
    return pl.pallas_call(
        functools.partial(_mla_prefill_kernel, blk=blk, nq=nq, heads=heads, rows=rows),
        grid=(batch, A_HEADS // heads, nq),
        in_specs=[
            pl.BlockSpec((blk, wide), lambda b, h, i: (b * nq + i, h)),
            pl.BlockSpec((seq, wide), lambda b, h, i: (b, h)),
            pl.BlockSpec((seq, wide), lambda b, h, i: (b, h)),
        ],
        out_specs=pl.BlockSpec((blk, heads * V_HEAD), lambda b, h, i: (b * nq + i, h)),
        out_shape=jax.ShapeDtypeStruct((batch * seq, A_HEADS * V_HEAD), BF16),
        scratch_shapes=[pltpu.VMEM((2, heads, blk, blk), F32), pltpu.VMEM((heads, blk, blk), BF16),
                        pltpu.VMEM((heads, blk, LANES), F32), pltpu.VMEM((heads, blk, HEAD_SLOT), F32)],
        compiler_params=_params(3), name="mla_prefill",
    )(qcat, kcat, vcat)


def _mla_decode_kernel(pt_ref, q_ref, ckv_hbm, kr_hbm, cself_ref, tself_ref, o_ref,
                       cbuf, rbuf, kbuf, sems, *, n_chunks, chunk_pages, ahead):
    b = pl.program_id(0)
    nb = pl.num_programs(0)

    def page_copies(bb, c, slot, p):
        pg = pt_ref[bb, c * chunk_pages + p]
        dst = pl.ds(p * PAGE_SIZE, PAGE_SIZE)
        return (pltpu.make_async_copy(ckv_hbm.at[pg], cbuf.at[slot, dst], sems.at[0, slot]),
                pltpu.make_async_copy(kr_hbm.at[pg], rbuf.at[slot, :, dst], sems.at[1, slot]))

    def issue(bb, c, slot):
        for p in range(chunk_pages):
            for cp in page_copies(bb, c, slot, p):
                cp.start(priority=p % 2)

    def wait(slot):
        for p in range(chunk_pages):
            for cp in page_copies(0, 0, slot, p):
                cp.wait()

    @pl.when(b == 0)
    def _():
        for c in range(ahead):
            issue(0, c, c)

    q = q_ref[0]
    q_lat = q[:, 0:KV_LORA]
    q_rope = q[:, KV_LORA:KV_LORA + QK_ROPE]

    def scores(slot):
        kc = cbuf[slot].astype(BF16)
        kbuf[slot % 2] = kc
        kr = rbuf[slot].astype(BF16)
        return _dot_nt(q_lat, kc) + jnp.dot(q_rope, kr, preferred_element_type=F32)

    def accumulate(carry, s, slot):
        m_old, l, acc = carry
        m_new = jnp.maximum(m_old, jnp.max(s, axis=1, keepdims=True))
        p = jnp.exp(s - m_new)
        alpha = jnp.exp(m_old - m_new)
        return (m_new, alpha * l + jnp.sum(p, axis=1, keepdims=True),
                alpha * acc + jnp.dot(p.astype(BF16), kbuf[slot], preferred_element_type=F32))

    carry = (jnp.full((A_HEADS, 1), -jnp.inf, F32), jnp.zeros((A_HEADS, 1), F32), jnp.zeros((A_HEADS, KV_LORA), F32))
    s_prev = None
    for c in range(n_chunks):
        nxt = c + ahead
        if nxt < n_chunks:
            issue(b, nxt, nxt)
        else:
            @pl.when(b + 1 < nb)
            def _():
                issue(b + 1, nxt - n_chunks, nxt - n_chunks)
        wait(c)
        s_cur = scores(c)
        if c > 0:
            carry = accumulate(carry, s_prev, (c - 1) % 2)
        s_prev = s_cur
    m_old, l, acc = accumulate(carry, s_prev, (n_chunks - 1) % 2)

    c_self = cself_ref[0].astype(BF16).astype(F32)
    r_self = tself_ref[0][:, 0:QK_ROPE].astype(BF16).astype(F32)
    s_self = (jnp.sum(q_lat.astype(F32) * c_self, axis=1, keepdims=True)
              + jnp.sum(q_rope.astype(F32) * r_self, axis=1, keepdims=True))
    m_new = jnp.maximum(m_old, s_self)
    p_self = jnp.exp(s_self - m_new)
    alpha = jnp.exp(m_old - m_new)
    l = alpha * l + p_self
    acc = alpha * acc + p_self.astype(BF16).astype(F32) * c_self
    o_ref[0] = acc / l


def _mla_decode(page_table, qabs, cache_ckv, cache_krope, ckv_self, tail_self, *, chunk_pages=32, ahead=2):
    nb, n_pages = page_table.shape
    n_chunks = n_pages // chunk_pages
    assert n_chunks * chunk_pages == n_pages and ahead < n_chunks
    keys = chunk_pages * PAGE_SIZE
    grid_spec = pltpu.PrefetchScalarGridSpec(
        num_scalar_prefetch=1,
        grid=(nb,),
        in_specs=[
            pl.BlockSpec((1, A_HEADS, ABS_SLOT), lambda b, pt: (b, 0, 0)),
            pl.BlockSpec(memory_space=pl.ANY),
            pl.BlockSpec(memory_space=pl.ANY),
            pl.BlockSpec((1, 1, KV_LORA), lambda b, pt: (b, 0, 0)),
            pl.BlockSpec((1, 1, LANES), lambda b, pt: (b, 0, 0)),
        ],
        out_specs=pl.BlockSpec((1, A_HEADS, KV_LORA), lambda b, pt: (b, 0, 0)),
        scratch_shapes=[
            pltpu.VMEM((n_chunks, keys, KV_LORA), F32),
            pltpu.VMEM((n_chunks, QK_ROPE, keys), F32),
            pltpu.VMEM((2, keys, KV_LORA), BF16),
            pltpu.SemaphoreType.DMA((2, n_chunks)),
        ],
    )
    return pl.pallas_call(
        functools.partial(_mla_decode_kernel, n_chunks=n_chunks, chunk_pages=chunk_pages, ahead=ahead),
        grid_spec=grid_spec,
        out_shape=jax.ShapeDtypeStruct((nb, A_HEADS, KV_LORA), F32),
        compiler_params=_params(1), name="mla_decode",
    )(page_table, qabs.reshape(nb, A_HEADS, ABS_SLOT), cache_ckv, cache_krope,
      ckv_self.reshape(nb, 1, KV_LORA), tail_self.reshape(nb, 1, LANES))


def _merge_kernel(x_ref, hg_ref, ob_ref, sg_ref, wa_ref, wb_ref, wout_ref, gpost_ref, gmem_ref, wmq_ref,
                  *rest, from_latent):
    if from_latent:
        wuv_ref, x1_ref, qm_ref = rest
        ob = sum(_dot(ob_ref[:, KV_LORA * hh:KV_LORA * (hh + 1)], wuv_ref[hh]) for hh in range(A_HEADS)).astype(BF16)
    else:
        x1_ref, qm_ref = rest
        ob = ob_ref[...]
    ya = jnp.dot(hg_ref[...], wa_ref[...], preferred_element_type=F32)
    yb = jnp.dot(ob, wb_ref[...], preferred_element_type=F32)
    mix = sg_ref[:, 0:D_MODEL] * ya + sg_ref[:, D_MODEL:] * yb
    y = _dot(mix, wout_ref[...])
    x1 = x_ref[...] + _rms(y, gpost_ref[...])
    x1_ref[...] = x1
    qm = _dot(_rms(x1, gmem_ref[...]), wmq_ref[...]) * (X_HEAD_DIM ** -0.5)
    qm_ref[...] = qm.astype(qm_ref.dtype)


def _merge(x, hg, ob, sg, w, *, tm, from_latent, qdtype):
    m = x.shape[0]
    row = lambda i: (i, 0)
    a_width = A_HEADS * V_HEAD
    ob_w = ob.shape[1]
    in_specs = [
        pl.BlockSpec((tm, D_MODEL), row), pl.BlockSpec((tm, M_WIDTH), row), pl.BlockSpec((tm, ob_w), row),
        pl.BlockSpec((tm, 2 * D_MODEL), row),
        _const_spec((M_WIDTH, D_MODEL)), _const_spec((a_width, D_MODEL)), _const_spec((D_MODEL, D_MODEL)),
        _const_spec((1, D_MODEL)), _const_spec((1, D_MODEL)), _const_spec((D_MODEL, X_WIDTH)),
    ]
    args = [x, hg, ob, sg, w["w_branch_a"], w["w_branch_b"], w["w_out"], w["g_post_mix"], w["g_pre_mem"], w["w_mq"]]
    if from_latent:
        in_specs += [_const_spec((A_HEADS, KV_LORA, a_width))]
        args += [w["wuv_heads"]]
    return pl.pallas_call(
        functools.partial(_merge_kernel, from_latent=from_latent),
        grid=(m // tm,), in_specs=in_specs,
        out_specs=[pl.BlockSpec((tm, D_MODEL), row), pl.BlockSpec((tm, X_WIDTH), row)],
        out_shape=[jax.ShapeDtypeStruct((m, D_MODEL), F32), jax.ShapeDtypeStruct((m, X_WIDTH), qdtype)],
        compiler_params=_params(1), name="merge_lat" if from_latent else "merge",
    )(*args)


def _memkv_kernel(mem_ref, g_ref, wk_ref, wv_ref, k_ref, v_ref):
    mn = _rms(mem_ref[...], g_ref[...]).astype(BF16)
    k_ref[...] = jnp.dot(mn, wk_ref[...], preferred_element_type=F32)
    v_ref[...] = jnp.dot(mn, wv_ref[...], preferred_element_type=F32)


def _memkv(mem, w, *, tm):
    m = mem.shape[0]
    row = lambda i: (i, 0)
    return pl.pallas_call(
        _memkv_kernel, grid=(m // tm,),
        in_specs=[pl.BlockSpec((tm, D_MODEL), row), _const_spec((1, D_MODEL)),
                  _const_spec((D_MODEL, X_WIDTH)), _const_spec((D_MODEL, X_WIDTH))],
        out_specs=[pl.BlockSpec((tm, X_WIDTH), row)] * 2,
        out_shape=[jax.ShapeDtypeStruct((m, X_WIDTH), F32)] * 2,
        compiler_params=_params(1), name="memkv",
    )(mem, w["g_mem"], w["w_mk"], w["w_mv"])


def _memattn_kernel(q_ref, k_ref, v_ref, o_ref):
    kb = k_ref[...].astype(BF16)
    vb = v_ref[...].astype(BF16)
    for h in range(X_HEADS):
        sl = slice(X_HEAD_DIM * h, X_HEAD_DIM * (h + 1))
        s = _dot_nt(q_ref[:, sl], kb[:, sl])
        p = jnp.exp(s - jnp.max(s, axis=1, keepdims=True))
        l = jnp.sum(p, axis=1, keepdims=True)
        o_ref[:, sl] = (jnp.dot(p.astype(BF16), vb[:, sl], preferred_element_type=F32) / l).astype(BF16)


def _memattn(qm, mem_k, mem_v, *, batch, seq, tm):
    nt = seq // tm
    tok = lambda b, i: (b * nt + i, 0)
    kv = lambda b, i: (b, 0)
    return pl.pallas_call(
        _memattn_kernel, grid=(batch, nt),
        in_specs=[pl.BlockSpec((tm, X_WIDTH), tok), pl.BlockSpec((N_MEM, X_WIDTH), kv),
                  pl.BlockSpec((N_MEM, X_WIDTH), kv)],
        out_specs=pl.BlockSpec((tm, X_WIDTH), tok),
        out_shape=jax.ShapeDtypeStruct((batch * seq, X_WIDTH), BF16),
        compiler_params=_params(2), name="memattn",
    )(qm, mem_k, mem_v)


def _memattn_step_kernel(q_ref, k_ref, v_ref, o_ref, *, rows):
    for g in range(rows):
        q4 = [q_ref[g:g + 1, X_HEAD_DIM * h:X_HEAD_DIM * (h + 1)] for h in range(X_HEADS)]
        q8 = jnp.concatenate(q4 + q4, axis=0)
        s = jnp.sum(k_ref[g] * q8[None], axis=2, keepdims=True)
        mx = jnp.max(s, axis=0)
        mx = jnp.maximum(mx, pltpu.roll(mx, X_HEADS, 0))
        p = jnp.exp(s - mx[None])
        l8 = jnp.sum(p, axis=0)
        o8 = jnp.sum(p * v_ref[g], axis=0)
        o_ref[g] = (o8[0:X_HEADS] + o8[X_HEADS:]) / (l8[0:X_HEADS] + l8[X_HEADS:])


def _memattn_step(qm, mem_k, mem_v, *, rows=8):
    nb = qm.shape[0]
    row = lambda i: (i, 0)
    kv = lambda i: (i, 0, 0, 0)
    kv_block = (rows, N_MEM // 2, 2 * X_HEADS, X_HEAD_DIM)
    return pl.pallas_call(
        functools.partial(_memattn_step_kernel, rows=rows), grid=(nb // rows,),
        in_specs=[pl.BlockSpec((rows, X_WIDTH), row), pl.BlockSpec(kv_block, kv), pl.BlockSpec(kv_block, kv)],
        out_specs=pl.BlockSpec((rows, X_HEADS, X_HEAD_DIM), lambda i: (i, 0, 0)),
        out_shape=jax.ShapeDtypeStruct((nb, X_HEADS, X_HEAD_DIM), F32),
        compiler_params=_params(1), name="memattn_step",
    )(qm, mem_k, mem_v).reshape(nb, X_WIDTH)


def _ffn_kernel(x1_ref, o_ref, wmo_ref, gpm_ref, gpf_ref, win_ref, wdn_ref, gpost_ref, y_ref, *, f_chunk):
    x2 = x1_ref[...] + _rms(_dot(o_ref[...], wmo_ref[...]), gpm_ref[...])
    h = _rms(x2, gpf_ref[...]).astype(BF16)
    acc = jnp.zeros(x2.shape, F32)
    for c in range(D_FF // f_chunk):
        sl = slice(f_chunk * c, f_chunk * (c + 1))
        g = jnp.dot(h, win_ref[:, sl], preferred_element_type=F32)
        u = jnp.dot(h, win_ref[:, D_FF + f_chunk * c:D_FF + f_chunk * (c + 1)], preferred_element_type=F32)
        acc = acc + jnp.dot((g * jax.nn.sigmoid(g) * u).astype(BF16), wdn_ref[sl, :], preferred_element_type=F32)
    y_ref[...] = x2 + _rms(acc, gpost_ref[...])


def _ffn(x1, o, w, *, tm, f_chunk=2816):
    m = x1.shape[0]
    row = lambda i: (i, 0)
    return pl.pallas_call(
        functools.partial(_ffn_kernel, f_chunk=f_chunk), grid=(m // tm,),
        in_specs=[pl.BlockSpec((tm, D_MODEL), row), pl.BlockSpec((tm, X_WIDTH), row),
                  _const_spec((X_WIDTH, D_MODEL)), _const_spec((1, D_MODEL)), _const_spec((1, D_MODEL)),
                  _const_spec((D_MODEL, 2 * D_FF)), _const_spec((D_FF, D_MODEL)),
                  _const_spec((1, D_MODEL))],
        out_specs=pl.BlockSpec((tm, D_MODEL), row),
        out_shape=jax.ShapeDtypeStruct((m, D_MODEL), F32),
        compiler_params=_params(1), name="ffn",
    )(x1, o, w["w_mo"], w["g_post_mem"], w["g_pre_ffn"], w["w_ffn_in"], w["w_ffn_out"], w["g_post_ffn"])


def _prep_weights(p):
    w = {}
    for name in ("g_pre_mix", "g_q_lora", "g_kv_lora", "g_mhead", "g_post_mix", "g_pre_mem", "g_mem", "g_post_mem",
                 "g_pre_ffn", "g_post_ffn"):
        w[name] = p[name].reshape(1, -1).astype(F32)
    w_in = p["w_in"]
    offs = np.cumsum((0, M_WIDTH, M_WIDTH, M_WIDTH, M_HEADS, M_HEADS, M_WIDTH, Q_LORA, KV_LORA, QK_ROPE,
                      D_MODEL, D_MODEL))
    seg = lambda i: w_in[:, offs[i]:offs[i + 1]]
    mq, mk, mv, mi, mf, mo, cq, ckv, kr, ga, gb = (seg(i) for i in range(11))
    half = QK_ROPE // 2
    swap = lambda a: jnp.concatenate([a[..., half:], a[..., :half]], axis=-1)
    tail = jnp.concatenate([kr, swap(kr), mi, mf,
                            jnp.zeros((D_MODEL, LANES - 2 * QK_ROPE - 2 * M_HEADS), w_in.dtype)], axis=1)
    w["w_qkv"] = w_in[:, offs[0]:offs[3]].astype(BF16)
    w["w_mo_gate"] = mo.astype(BF16)
    w["w_lat"] = w_in[:, offs[6]:offs[8]].astype(BF16)
    w["w_g"] = w_in[:, offs[9]:offs[11]].astype(BF16)
    w["w_tail"] = tail.astype(BF16)
    w["wv_t"] = mv.T.astype(BF16)
    w["b_tail"] = jnp.zeros((1, LANES), F32).at[0, TAIL_IG:TAIL_IG + 2 * M_HEADS].set(p["b_if"].astype(F32))

    pad = HEAD_SLOT - QK_NOPE - QK_ROPE
    wq = p["w_uq"].reshape(Q_LORA, A_HEADS, QK_NOPE + QK_ROPE)
    zq = jnp.zeros((Q_LORA, A_HEADS, pad), wq.dtype)
    w["wq_cat"] = jnp.concatenate([wq, zq], axis=2).reshape(Q_LORA, A_HEADS * HEAD_SLOT).astype(BF16)
    w_uk, w_uv = p["w_uk"], p["w_uv"]
    zk = jnp.zeros((KV_LORA, A_HEADS, HEAD_SLOT - QK_NOPE), w_uk.dtype)
    w["wuk_cat"] = jnp.concatenate([w_uk, zk], axis=2).reshape(KV_LORA, A_HEADS * HEAD_SLOT).astype(BF16)
    w["wuv_cat"] = jnp.concatenate([w_uv, zk], axis=2).reshape(KV_LORA, A_HEADS * HEAD_SLOT).astype(BF16)
    w["wuv_heads"] = (w_uv.transpose(1, 0, 2)[:, :, None, :]
                      * jnp.eye(A_HEADS, dtype=w_uv.dtype)[:, None, :, None]).reshape(
                          A_HEADS, KV_LORA, A_HEADS * V_HEAD).astype(BF16)
    pass_np = np.zeros((A_HEADS, HEAD_SLOT - QK_NOPE, ABS_SLOT), np.float32)
    pass_np[:, np.arange(QK_ROPE), KV_LORA + np.arange(QK_ROPE)] = 1.0
    w["w_abs"] = jnp.concatenate(
        [jnp.concatenate([w_uk.transpose(1, 2, 0), jnp.zeros((A_HEADS, QK_NOPE, ABS_SLOT - KV_LORA), w_uk.dtype)],
                         axis=2).astype(BF16), jnp.asarray(pass_np, BF16)], axis=1)
    for name in ("w_branch_a", "w_branch_b", "w_out", "w_mq", "w_mk", "w_mv", "w_mo", "w_ffn_out"):
        w[name] = p[name].astype(BF16)
    w["w_ffn_in"] = p["w_ffn_in"].astype(BF16)
    return w


def _rope_tables(pos0, n):
    pos = (pos0 + jnp.arange(n)).astype(F32)
    inv = ROPE_BASE ** (-jnp.arange(0, QK_ROPE, 2, dtype=F32) / QK_ROPE)
    ang = pos[:, None] * inv[None, :]
    cos, sin = jnp.cos(ang), jnp.sin(ang)
    c32 = jnp.concatenate([cos, cos], axis=1)
    s32 = jnp.concatenate([-sin, sin], axis=1)
    one = lambda k: jnp.ones((n, k), F32)
    zero = lambda k: jnp.zeros((n, k), F32)
    ct = jnp.concatenate([c32, one(LANES - QK_ROPE)], axis=1)
    st = jnp.concatenate([s32, zero(LANES - QK_ROPE)], axis=1)
    c128 = jnp.concatenate([one(QK_NOPE), c32, one(HEAD_SLOT - QK_NOPE - QK_ROPE)], axis=1)
    s128 = jnp.concatenate([zero(QK_NOPE), s32, zero(HEAD_SLOT - QK_NOPE - QK_ROPE)], axis=1)
    return ct, st, c128, s128


def kernel(x_prompt, x_sample, cache_ckv, cache_krope, cache_mem_k, cache_mem_v, state_C, state_n, state_m, page_table, mem_prompt, g_pre_mix, w_in, b_if, g_mhead, g_q_lora, w_uq, g_kv_lora, w_uk, w_uv, w_branch_a, w_branch_b, w_out, g_post_mix, g_pre_mem, g_mem, w_mq, w_mk, w_mv, w_mo, g_post_mem, g_pre_ffn, w_ffn_in, w_ffn_out, g_post_ffn):
    params = dict(g_pre_mix=g_pre_mix, w_in=w_in, b_if=b_if, g_mhead=g_mhead, g_q_lora=g_q_lora, w_uq=w_uq,
                  g_kv_lora=g_kv_lora, w_uk=w_uk, w_uv=w_uv, w_branch_a=w_branch_a, w_branch_b=w_branch_b,
                  w_out=w_out, g_post_mix=g_post_mix, g_pre_mem=g_pre_mem, g_mem=g_mem, w_mq=w_mq, w_mk=w_mk,
                  w_mv=w_mv, w_mo=w_mo, g_post_mem=g_post_mem, g_pre_ffn=g_pre_ffn, w_ffn_in=w_ffn_in,
                  w_ffn_out=w_ffn_out, g_post_ffn=g_post_ffn)
    depth = w_in.shape[0]
    assert depth == 1, "single-layer stack only"
    bp, seq, _ = x_prompt.shape
    bs, dec_seq, _ = x_sample.shape
    assert dec_seq == 1, "one new token per sample sequence"
    past_len = page_table.shape[1] * PAGE_SIZE
    w = _prep_weights({name: a[0] for name, a in params.items()})

    xp = x_prompt.reshape(bp * seq, D_MODEL)
    qkvm, og, ckv_p, sg, krope_t, gates_t, qcat, kcat, vcat, vt = _inproj(
        xp, w, _rope_tables(0, seq), tm=512, absorbed=False, mdtype=BF16)
    hg, c_p, n_p, m_p = _mlstm_prompt(qkvm, vt, gates_t, og, w["g_mhead"], batch=bp, seq=seq)
    ob = _mla_prefill(qcat, kcat, vcat, batch=bp, seq=seq, blk=512)
    x1, qm = _merge(xp, hg, ob, sg, w, tm=512, from_latent=False, qdtype=BF16)
    mem_k, mem_v = _memkv(mem_prompt.reshape(bp * N_MEM, D_MODEL), w, tm=512)
    om = _memattn(qm, mem_k, mem_v, batch=bp, seq=seq, tm=512)
    y_prompt = _ffn(x1, om, w, tm=512).reshape(bp, seq, D_MODEL)

    xs = x_sample.reshape(bs, D_MODEL)
    tables_s = tuple(jnp.broadcast_to(t, (bs, LANES)) for t in _rope_tables(past_len, 1))
    qkvm_s, og_s, ckv_s, sg_s, tail_s, qabs = _inproj(xs, w, tables_s, tm=bs, absorbed=True, mdtype=F32)
    m0_pad = jnp.pad(state_m.reshape(bs, M_HEADS).astype(F32), ((0, 0), (0, LANES - M_HEADS)))
    hg_s, c_s, n_s, m_s = _mlstm_step(qkvm_s, tail_s, state_C.reshape(bs, M_HEADS, M_HEAD_DIM, M_HEAD_DIM),
                                      state_n.reshape(bs, M_HEADS, M_HEAD_DIM), m0_pad, og_s, w["g_mhead"])
    n_phys = cache_ckv.shape[1]
    cache_krope_t = jnp.transpose(cache_krope.reshape(n_phys, PAGE_SIZE, QK_ROPE), (0, 2, 1))
    o_lat = _mla_decode(page_table, qabs, cache_ckv.reshape(n_phys, PAGE_SIZE, KV_LORA), cache_krope_t, ckv_s, tail_s)
    x1_s, qm_s = _merge(xs, hg_s, o_lat.reshape(bs, A_HEADS * KV_LORA), sg_s, w, tm=bs, from_latent=True, qdtype=F32)
    om_s = _memattn_step(qm_s, cache_mem_k.reshape(bs, N_MEM // 2, 2 * X_HEADS, X_HEAD_DIM),
                         cache_mem_v.reshape(bs, N_MEM // 2, 2 * X_HEADS, X_HEAD_DIM))
    y_sample = _ffn(x1_s, om_s, w, tm=bs).reshape(bs, 1, D_MODEL)

    return (y_prompt, y_sample,
            ckv_p.reshape(1, bp, seq, KV_LORA), jnp.transpose(krope_t, (0, 2, 1)).reshape(1, bp, seq, QK_ROPE),
            c_p.reshape(1, bp, M_HEADS, M_HEAD_DIM, M_HEAD_DIM), n_p.reshape(1, bp, M_HEADS, M_HEAD_DIM),
            m_p[:, :, 0].reshape(1, bp, M_HEADS),
            mem_k.reshape(1, bp, N_MEM, X_HEADS, X_HEAD_DIM), mem_v.reshape(1, bp, N_MEM, X_HEADS, X_HEAD_DIM),
            ckv_s.reshape(1, bs, 1, KV_LORA), tail_s[:, :QK_ROPE].reshape(1, bs, 1, QK_ROPE),
            c_s.reshape(1, bs, M_HEADS, M_HEAD_DIM, M_HEAD_DIM), n_s.reshape(1, bs, M_HEADS, M_HEAD_DIM),
            m_s[:, :M_HEADS].reshape(1, bs, M_HEADS))
```

```python
import functools

import jax
import jax.numpy as jnp
import numpy as np
from jax import lax
from jax.experimental import pallas as pl
from jax.experimental.pallas import tpu as pltpu

F32 = jnp.float32
BF16 = jnp.bfloat16

D_MODEL = 1024
PAGE_SIZE = 128
M_HEADS = 4
M_HEAD_DIM = 128
M_WIDTH = M_HEADS * M_HEAD_DIM
M_CHUNK = 128
A_HEADS = 8
QK_NOPE = 64
QK_ROPE = 32
V_HEAD = 64
Q_LORA = 384
KV_LORA = 256
ROPE_BASE = 10000.0
N_MEM = 256
X_HEADS = 4
X_HEAD_DIM = 128
X_WIDTH = X_HEADS * X_HEAD_DIM
D_FF = 2816
EPS = 1e-6

LANES = 128
HEAD_SLOT = 128
ABS_SLOT = 384
ATT_SCALE = (QK_NOPE + QK_ROPE) ** -0.5

V7X_VMEM_BYTES = 64 * 1024 * 1024
VMEM_LIMIT = V7X_VMEM_BYTES * 13 // 16
TOKEN_TILE = 512
ATTN_BLOCK = 512
ATTN_STRIP = 32
MLSTM_SEQS = 8
GATE_SEQS = 4
STEP_SEQS = 16
DECODE_CHUNK_PAGES = 32
DECODE_AHEAD = 2

TAIL_IG = 64
TAIL_LF = 68


def _rms(x, g):
    return x * lax.rsqrt(jnp.mean(x * x, axis=-1, keepdims=True) + EPS) * g


def _dot(a, b):
    return jnp.dot(a.astype(BF16), b.astype(BF16), preferred_element_type=F32)


def _dot_nt(a, b):
    return lax.dot_general(a.astype(BF16), b.astype(BF16), (((1,), (1,)), ((), ())),
                           preferred_element_type=F32)


def _log_sigmoid(x):
    return jnp.minimum(x, 0.0) - jnp.log1p(jnp.exp(-jnp.abs(x)))


def _const_spec(shape):
    nd = len(shape)
    return pl.BlockSpec(shape, lambda *_: (0,) * nd, pipeline_mode=pl.Buffered(1))


def _params(n_axes):
    return pltpu.CompilerParams(dimension_semantics=("arbitrary",) * n_axes, vmem_limit_bytes=VMEM_LIMIT)


def _inproj_kernel(x_ref, gpre_ref, wqkv_ref, wmo_ref, wlat_ref, wg_ref, wtail_ref, btail_ref, ct_ref, st_ref,
                   gq_ref, wq_ref, c128_ref, s128_ref, gkv_ref, *rest, absorbed):
    if absorbed:
        wabs_ref, qkvm_ref, og_ref, ckv_ref, sg_ref, tail_ref, qabs_ref = rest
    else:
        (wuk_ref, wuv_ref, wvt_ref,
         qkvm_ref, og_ref, ckv_ref, sg_ref, krt_ref, gt_ref, qcat_ref, kcat_ref, vcat_ref, vt_ref) = rest
    h = _rms(x_ref[...], gpre_ref[...]).astype(BF16)

    zm = jnp.dot(h, wqkv_ref[...], preferred_element_type=F32)
    qkvm_ref[:, 0:M_WIDTH] = zm[:, 0:M_WIDTH].astype(qkvm_ref.dtype)
    qkvm_ref[:, M_WIDTH:2 * M_WIDTH] = (zm[:, M_WIDTH:2 * M_WIDTH] * (M_HEAD_DIM ** -0.5)).astype(qkvm_ref.dtype)
    qkvm_ref[:, 2 * M_WIDTH:] = zm[:, 2 * M_WIDTH:].astype(qkvm_ref.dtype)
    og_ref[...] = jax.nn.sigmoid(jnp.dot(h, wmo_ref[...], preferred_element_type=F32))
    sg_ref[...] = jax.nn.sigmoid(jnp.dot(h, wg_ref[...], preferred_element_type=F32))

    t = jnp.dot(h, wtail_ref[...], preferred_element_type=F32) + btail_ref[...]
    t = t * ct_ref[...] + pltpu.roll(t, LANES - QK_ROPE, 1) * st_ref[...]
    lane = lax.broadcasted_iota(jnp.int32, t.shape, 1)
    is_lf = jnp.logical_and(lane >= TAIL_LF, lane < TAIL_LF + M_HEADS)
    t = jnp.where(is_lf, _log_sigmoid(t), t)
    if absorbed:
        tail_ref[...] = t
    else:
        tt = jnp.transpose(t)
        krt_ref[0] = tt[0:QK_ROPE]
        gt_ref[...] = tt[TAIL_IG:TAIL_IG + 2 * M_HEADS]

    cqn = _rms(jnp.dot(h, wlat_ref[:, 0:Q_LORA], preferred_element_type=F32), gq_ref[...]).astype(BF16)
    ckvn = _rms(jnp.dot(h, wlat_ref[:, Q_LORA:], preferred_element_type=F32), gkv_ref[...])
    ckv_ref[...] = ckvn

    qc = jnp.dot(cqn, wq_ref[...], preferred_element_type=F32)
    width = A_HEADS * HEAD_SLOT
    first_half = (lax.broadcasted_iota(jnp.int32, qc.shape, 1) % HEAD_SLOT) < QK_NOPE + QK_ROPE // 2
    qs = jnp.where(first_half, pltpu.roll(qc, width - QK_ROPE // 2, 1), pltpu.roll(qc, QK_ROPE // 2, 1))
    c128 = c128_ref[...]
    s128 = s128_ref[...]
    for hh in range(A_HEADS):
        sl = slice(HEAD_SLOT * hh, HEAD_SLOT * (hh + 1))
        qh = ((qc[:, sl] * c128 + qs[:, sl] * s128) * ATT_SCALE).astype(BF16)
        if absorbed:
            qabs_ref[:, ABS_SLOT * hh:ABS_SLOT * (hh + 1)] = jnp.dot(
                qh, wabs_ref[hh], preferred_element_type=F32).astype(BF16)
        else:
            qcat_ref[:, sl] = qh
    if not absorbed:
        cb = ckvn.astype(BF16)
        kr_slot = jnp.where(jnp.logical_and(lane >= QK_NOPE, lane < QK_NOPE + QK_ROPE), pltpu.roll(t, QK_NOPE, 1), 0.0)
        kn = jnp.dot(cb, wuk_ref[...], preferred_element_type=F32)
        for hh in range(A_HEADS):
            sl = slice(HEAD_SLOT * hh, HEAD_SLOT * (hh + 1))
            kcat_ref[:, sl] = (kn[:, sl] + kr_slot).astype(BF16)
        vlane = lax.broadcasted_iota(jnp.int32, (1, A_HEADS * HEAD_SLOT), 1) % HEAD_SLOT
        vcat = jnp.dot(cb, wuv_ref[...], preferred_element_type=F32)
        vcat_ref[...] = jnp.where(vlane == V_HEAD, 1.0, vcat).astype(BF16)
        vt_ref[...] = _dot_nt(wvt_ref[...], h).astype(BF16)


def _inproj(x, w, tables, *, tm, absorbed, mdtype):
    m = x.shape[0]
    ct, st, c128, s128 = tables
    seq = ct.shape[0]
    nt = seq // tm
    grid = (m // tm,)
    row = lambda i: (i, 0)
    tab = lambda i: (i % nt, 0)
    wide = A_HEADS * HEAD_SLOT
    in_specs = [
        pl.BlockSpec((tm, D_MODEL), row), _const_spec((1, D_MODEL)),
        _const_spec((D_MODEL, 3 * M_WIDTH)), _const_spec((D_MODEL, M_WIDTH)), _const_spec((D_MODEL, Q_LORA + KV_LORA)),
        _const_spec((D_MODEL, 2 * D_MODEL)), _const_spec((D_MODEL, LANES)), _const_spec((1, LANES)),
        pl.BlockSpec((tm, LANES), tab), pl.BlockSpec((tm, LANES), tab),
        _const_spec((1, Q_LORA)), _const_spec((Q_LORA, wide)),
        pl.BlockSpec((tm, LANES), tab), pl.BlockSpec((tm, LANES), tab),
        _const_spec((1, KV_LORA)),
    ]
    args = [x, w["g_pre_mix"], w["w_qkv"], w["w_mo_gate"], w["w_lat"], w["w_g"], w["w_tail"], w["b_tail"], ct, st,
            w["g_q_lora"], w["wq_cat"], c128, s128, w["g_kv_lora"]]
    outs = [
        (jax.ShapeDtypeStruct((m, 3 * M_WIDTH), mdtype), pl.BlockSpec((tm, 3 * M_WIDTH), row)),
        (jax.ShapeDtypeStruct((m, M_WIDTH), F32), pl.BlockSpec((tm, M_WIDTH), row)),
        (jax.ShapeDtypeStruct((m, KV_LORA), F32), pl.BlockSpec((tm, KV_LORA), row)),
        (jax.ShapeDtypeStruct((m, 2 * D_MODEL), F32), pl.BlockSpec((tm, 2 * D_MODEL), row)),
    ]
    if absorbed:
        in_specs += [_const_spec((A_HEADS, HEAD_SLOT, ABS_SLOT))]
        args += [w["w_abs"]]
        outs += [(jax.ShapeDtypeStruct((m, LANES), F32), pl.BlockSpec((tm, LANES), row)),
                 (jax.ShapeDtypeStruct((m, A_HEADS * ABS_SLOT), BF16), pl.BlockSpec((tm, A_HEADS * ABS_SLOT), row))]
    else:
        in_specs += [_const_spec((KV_LORA, wide)), _const_spec((KV_LORA, wide)), _const_spec((M_WIDTH, D_MODEL))]
        args += [w["wuk_cat"], w["wuv_cat"], w["wv_t"]]
        outs += [(jax.ShapeDtypeStruct((m // seq, QK_ROPE, seq), F32),
                  pl.BlockSpec((1, QK_ROPE, tm), lambda i: (i // nt, 0, i % nt))),
                 (jax.ShapeDtypeStruct((2 * M_HEADS, m), F32), pl.BlockSpec((2 * M_HEADS, tm), lambda i: (0, i)))]
        outs += [(jax.ShapeDtypeStruct((m, wide), BF16), pl.BlockSpec((tm, wide), row))] * 3
        outs += [(jax.ShapeDtypeStruct((M_WIDTH, m), BF16), pl.BlockSpec((M_WIDTH, tm), lambda i: (0, i)))]
    return pl.pallas_call(
        functools.partial(_inproj_kernel, absorbed=absorbed),
        grid=grid, in_specs=in_specs,
        out_specs=[o[1] for o in outs], out_shape=[o[0] for o in outs],
        compiler_params=_params(1), name="inproj_abs" if absorbed else "inproj",
    )(*args)


def _split3_dot(x, m01):
    hi = x.astype(BF16)
    r1 = x - hi.astype(F32)
    mid = r1.astype(BF16)
    lo = (r1 - mid.astype(F32)).astype(BF16)
    return sum(jnp.dot(part, m01, preferred_element_type=F32) for part in (hi, mid, lo))


def _cumsum_lanes(x, upper):
    return _split3_dot(x, jnp.where(upper, 1.0, 0.0).astype(BF16))


def _cummax_lanes(x):
    lane = lax.broadcasted_iota(jnp.int32, x.shape, 1)
    shift = 1
    while shift < x.shape[1]:
        x = jnp.maximum(x, jnp.where(lane >= shift, pltpu.roll(x, shift, 1), -jnp.inf))
        shift *= 2
    return x


def _mlstm_gates_kernel(g_ref, urow_ref, scal_ref, cols_ref, *, nc, nseq):
    L = M_CHUNK
    H = M_HEADS
    rows = nseq * nc * 2 * H
    g = g_ref[...]
    r = jnp.concatenate([g[:, L * c:L * (c + 1)] for c in range(nseq * nc)], axis=0)
    ri = lax.broadcasted_iota(jnp.int32, (rows, L), 0)
    ci = lax.broadcasted_iota(jnp.int32, (rows, L), 1)
    top = (ri % (2 * H)) < H
    b_all = _cumsum_lanes(r, lax.broadcasted_iota(jnp.int32, (L, L), 0) <= lax.broadcasted_iota(jnp.int32, (L, L), 1))
    b = pltpu.roll(b_all, rows - H, 0)
    u = r - b
    cm = _cummax_lanes(u)
    g_last = jnp.max(jnp.where(ci == L - 1, b, -jnp.inf), axis=1, keepdims=True)
    wlog = g_last - b + r
    wmax = jnp.max(wlog, axis=1, keepdims=True)
    urow_ref[...] = jnp.where(top, u, pltpu.roll(wlog, H, 0)).reshape(nseq, nc, 2 * H, L)
    scal = jnp.where(top, jnp.broadcast_to(g_last, (rows, L)), pltpu.roll(jnp.broadcast_to(wmax, (rows, L)), H, 0))
    scal_ref[...] = scal.reshape(nseq, nc, 2 * H, L)
    first = jnp.where(top, b, pltpu.roll(cm, H, 0))
    for c in range(nseq * nc):
        sl = slice(2 * H * c, 2 * H * (c + 1))
        cols_ref[c // nc, c % nc] = jnp.transpose(jnp.concatenate([first[sl], wlog[sl]], axis=0))


def _mlstm_gates(g, *, batch, nc, nseq=GATE_SEQS):
    blk = lambda b: (b, 0, 0, 0)
    return pl.pallas_call(
        functools.partial(_mlstm_gates_kernel, nc=nc, nseq=nseq),
        grid=(batch // nseq,),
        in_specs=[pl.BlockSpec((2 * M_HEADS, nseq * nc * M_CHUNK), lambda b: (0, b))],
        out_specs=[pl.BlockSpec((nseq, nc, 2 * M_HEADS, M_CHUNK), blk),
                   pl.BlockSpec((nseq, nc, 2 * M_HEADS, M_CHUNK), blk),
                   pl.BlockSpec((nseq, nc, M_CHUNK, 4 * M_HEADS), blk)],
        out_shape=[jax.ShapeDtypeStruct((batch, nc, 2 * M_HEADS, M_CHUNK), F32),
                   jax.ShapeDtypeStruct((batch, nc, 2 * M_HEADS, M_CHUNK), F32),
                   jax.ShapeDtypeStruct((batch, nc, M_CHUNK, 4 * M_HEADS), F32)],
        compiler_params=_params(1), name="mlstm_gates",
    )(g)


def _mlstm_kernel(*refs, nb):
    qkv_ref = refs[0]
    vt_refs = refs[1:1 + nb]
    urow_ref, scal_ref, cols_ref, sel_ref, og_ref, gm_ref, hg_ref, c_ref, n_ref, m_ref = refs[1 + nb:]
    L = M_CHUNK

    @pl.when(pl.program_id(1) == 0)
    def _():
        c_ref[...] = jnp.zeros_like(c_ref)
        n_ref[...] = jnp.zeros_like(n_ref)
        m_ref[...] = jnp.zeros_like(m_ref)

    row = lax.broadcasted_iota(jnp.int32, (L, L), 0)
    col = lax.broadcasted_iota(jnp.int32, (L, L), 1)
    tril = col <= row

    for bb in range(nb):
        vt_ref = vt_refs[bb]
        rows8 = urow_ref[bb, 0]
        u4 = rows8[0:M_HEADS]
        wlog4 = rows8[M_HEADS:2 * M_HEADS]
        scal = scal_ref[bb, 0]
        g_last4 = scal[0:M_HEADS, 0:1]
        wmax4 = scal[M_HEADS:2 * M_HEADS, 0:1]
        cols = _split3_dot(cols_ref[bb, 0], sel_ref[...])
        m_prev4 = m_ref[bb][:, 0:1]
        n_prev4 = n_ref[bb]
        m_new4 = jnp.maximum(g_last4 + m_prev4, wmax4)
        decay4 = jnp.exp(g_last4 + m_prev4 - m_new4)
        ws4 = jnp.exp(wlog4 - m_new4)
        m_ref[bb] = jnp.broadcast_to(m_new4, (M_HEADS, LANES))
        n_rows = []

        for h in range(M_HEADS):
            sl = slice(M_HEAD_DIM * h, M_HEAD_DIM * (h + 1))
            q = qkv_ref[bb, :, sl]
            k = qkv_ref[bb, :, M_WIDTH + M_HEAD_DIM * h:M_WIDTH + M_HEAD_DIM * (h + 1)]
            v = qkv_ref[bb, :, 2 * M_WIDTH + M_HEAD_DIM * h:2 * M_WIDTH + M_HEAD_DIM * (h + 1)]
            m_prev = m_prev4[h:h + 1]
            c_prev = c_ref[bb, h]
            n_prev = n_prev4[h:h + 1]
            b_col = cols[:, LANES * h:LANES * (h + 1)]
            mm_col = jnp.maximum(cols[:, LANES * (M_HEADS + h):LANES * (M_HEADS + h + 1)], m_prev)
            w_intra = jnp.where(tril, jnp.exp(u4[h:h + 1] - mm_col), 0.0)
            w_inter = jnp.exp(m_prev - mm_col)
            s = _dot_nt(q, k) * w_intra
            cn = jnp.concatenate([c_prev.astype(BF16), jnp.broadcast_to(n_prev.astype(BF16), (L, M_HEAD_DIM))],
                                 axis=0)
            qcn = _dot_nt(q, cn)
            num = w_inter * qcn[:, 0:M_HEAD_DIM] + _dot(s, v)
            den = w_inter * qcn[:, M_HEAD_DIM:] + jnp.sum(s, axis=1, keepdims=True)
            hs = num / jnp.maximum(jnp.abs(den), jnp.exp(-(b_col + mm_col)))

            ws = ws4[h:h + 1]
            lhs = jnp.concatenate([(vt_ref[sl, :].astype(F32) * ws).astype(BF16),
                                   jnp.broadcast_to(ws.astype(BF16), (16, L))], axis=0)
            upd = jnp.dot(lhs, k, preferred_element_type=F32)
            decay = decay4[h:h + 1]
            c_ref[bb, h] = decay * c_prev + upd[0:M_HEAD_DIM]
            n_rows.append(decay * n_prev + upd[M_HEAD_DIM:M_HEAD_DIM + 1])

            mu = jnp.mean(hs, axis=1, keepdims=True)
            d = hs - mu
            y = d * lax.rsqrt(jnp.mean(d * d, axis=1, keepdims=True) + EPS) * gm_ref[:, sl]
            hg_ref[bb, :, sl] = (y * og_ref[bb, :, sl]).astype(BF16)
        n_ref[bb] = jnp.concatenate(n_rows, axis=0)


def _mlstm_prompt(qkvm, vt, gates, og, g_mhead, *, batch, seq, nb=MLSTM_SEQS):
    nc = seq // M_CHUNK
    tok = lambda b, c: (b, c, 0)
    chunk = lambda b, c: (b, c, 0, 0)
    urow, scal, cols = _mlstm_gates(gates, batch=batch, nc=nc)
    sel = jnp.asarray(np.kron(np.eye(4 * M_HEADS, 2 * M_HEADS), np.ones((1, LANES))), BF16)
    vt_specs = [pl.BlockSpec((M_WIDTH, M_CHUNK), functools.partial(lambda b, c, bb: (0, (nb * b + bb) * nc + c), bb=bb))
                for bb in range(nb)]
    hg, c_p, n_p, m_p = pl.pallas_call(
        functools.partial(_mlstm_kernel, nb=nb),
        grid=(batch // nb, nc),
        in_specs=[pl.BlockSpec((nb, M_CHUNK, 3 * M_WIDTH), tok)] + vt_specs + [
            pl.BlockSpec((nb, 1, 2 * M_HEADS, M_CHUNK), chunk),
            pl.BlockSpec((nb, 1, 2 * M_HEADS, M_CHUNK), chunk),
            pl.BlockSpec((nb, 1, M_CHUNK, 4 * M_HEADS), chunk),
            pl.BlockSpec((4 * M_HEADS, 2 * M_HEADS * LANES), lambda b, c: (0, 0)),
            pl.BlockSpec((nb, M_CHUNK, M_WIDTH), tok),
            pl.BlockSpec((1, M_WIDTH), lambda b, c: (0, 0)),
        ],
        out_specs=[
            pl.BlockSpec((nb, M_CHUNK, M_WIDTH), tok),
            pl.BlockSpec((nb, M_HEADS, M_HEAD_DIM, M_HEAD_DIM), lambda b, c: (b, 0, 0, 0)),
            pl.BlockSpec((nb, M_HEADS, M_HEAD_DIM), lambda b, c: (b, 0, 0)),
            pl.BlockSpec((nb, M_HEADS, LANES), lambda b, c: (b, 0, 0)),
        ],
        out_shape=[
            jax.ShapeDtypeStruct((batch, seq, M_WIDTH), BF16),
            jax.ShapeDtypeStruct((batch, M_HEADS, M_HEAD_DIM, M_HEAD_DIM), F32),
            jax.ShapeDtypeStruct((batch, M_HEADS, M_HEAD_DIM), F32),
            jax.ShapeDtypeStruct((batch, M_HEADS, LANES), F32),
        ],
        compiler_params=_params(2), name="mlstm_prompt",
    )(qkvm.reshape(batch, seq, 3 * M_WIDTH), *([vt] * nb), urow, scal, cols, sel,
      og.reshape(batch, seq, M_WIDTH), g_mhead)
    return hg.reshape(batch * seq, M_WIDTH), c_p, n_p, m_p


def _mlstm_step_kernel(qkv_ref, tail_ref, c_ref, n_ref, m_ref, og_ref, gm_ref, hg_ref, co_ref, no_ref, mo_ref, *, rows):
    D = M_HEAD_DIM
    lane = lax.broadcasted_iota(jnp.int32, (rows, LANES), 1)
    rowi = lax.broadcasted_iota(jnp.int32, (rows, D), 0)
    tail = tail_ref[...]
    m_in = m_ref[...]
    m_out = jnp.zeros((rows, LANES), F32)
    for h in range(M_HEADS):
        sl = slice(D * h, D * (h + 1))
        q = qkv_ref[:, sl]
        k = qkv_ref[:, M_WIDTH + D * h:M_WIDTH + D * (h + 1)]
        v = qkv_ref[:, 2 * M_WIDTH + D * h:2 * M_WIDTH + D * (h + 1)]
        ig = tail[:, TAIL_IG + h:TAIL_IG + h + 1]
        lf = tail[:, TAIL_LF + h:TAIL_LF + h + 1]
        m_prev = m_in[:, h:h + 1]
        n_prev = n_ref[:, h, :]
        a = lf + m_prev
        mt = jnp.maximum(a, ig)
        w_in = jnp.exp(ig - mt)
        w_st = jnp.exp(a - mt)
        s = jnp.sum(q * k, axis=1, keepdims=True) * w_in
        cq = jnp.zeros((rows, D), F32)
        for g in range(rows):
            cq = jnp.where(rowi == g, _dot_nt(q, c_ref[g, h]), cq)
        num = w_st * cq + s * v
        den = w_st * jnp.sum(n_prev * q, axis=1, keepdims=True) + s
        hs = num / jnp.maximum(jnp.abs(den), jnp.exp(-mt))
        vw_t = jnp.transpose(v * w_in)
        for g in range(rows):
            co_ref[g, h] = w_st[g:g + 1] * c_ref[g, h] + vw_t[:, g:g + 1] * k[g:g + 1, :]
        no_ref[:, h, :] = w_st * n_prev + w_in * k
        m_out = jnp.where(lane == h, mt, m_out)
        mu = jnp.mean(hs, axis=1, keepdims=True)
        d = hs - mu
        y = d * lax.rsqrt(jnp.mean(d * d, axis=1, keepdims=True) + EPS) * gm_ref[:, sl]
        hg_ref[:, sl] = (y * og_ref[:, sl]).astype(BF16)
    mo_ref[...] = m_out


def _mlstm_step(qkvm, tail, c0, n0, m0_pad, og, g_mhead, *, rows=STEP_SEQS):
    nb = qkvm.shape[0]
    row = lambda i: (i, 0)
    return pl.pallas_call(
        functools.partial(_mlstm_step_kernel, rows=rows),
        grid=(nb // rows,),
        in_specs=[
            pl.BlockSpec((rows, 3 * M_WIDTH), row),
            pl.BlockSpec((rows, LANES), row),
            pl.BlockSpec((rows, M_HEADS, M_HEAD_DIM, M_HEAD_DIM), lambda i: (i, 0, 0, 0)),
            pl.BlockSpec((rows, M_HEADS, M_HEAD_DIM), lambda i: (i, 0, 0)),
            pl.BlockSpec((rows, LANES), row),
            pl.BlockSpec((rows, M_WIDTH), row),
            pl.BlockSpec((1, M_WIDTH), lambda i: (0, 0)),
        ],
        out_specs=[
            pl.BlockSpec((rows, M_WIDTH), row),
            pl.BlockSpec((rows, M_HEADS, M_HEAD_DIM, M_HEAD_DIM), lambda i: (i, 0, 0, 0)),
            pl.BlockSpec((rows, M_HEADS, M_HEAD_DIM), lambda i: (i, 0, 0)),
            pl.BlockSpec((rows, LANES), row),
        ],
        out_shape=[
            jax.ShapeDtypeStruct((nb, M_WIDTH), BF16),
            jax.ShapeDtypeStruct((nb, M_HEADS, M_HEAD_DIM, M_HEAD_DIM), F32),
            jax.ShapeDtypeStruct((nb, M_HEADS, M_HEAD_DIM), F32),
            jax.ShapeDtypeStruct((nb, LANES), F32),
        ],
        compiler_params=_params(1), name="mlstm_step",
    )(qkvm, tail, c0, n0, m0_pad, og, g_mhead)


def _mla_prefill_kernel(q_ref, k_ref, v_ref, o_ref, s_scr, p_scr, m_scr, acc_scr, *, blk, nq, heads, rows):
    i = pl.program_id(2)
    reps = blk // LANES
    m_scr[...] = jnp.full(m_scr.shape, -jnp.inf, F32)
    acc_scr[...] = jnp.zeros(acc_scr.shape, F32)

    def block(j, masked):
        start = j * blk
        slot = j % 2
        for hh in range(heads):
            sl = slice(HEAD_SLOT * hh, HEAD_SLOT * (hh + 1))
            s_scr[slot, hh] = _dot_nt(q_ref[:, sl], k_ref[pl.ds(start, blk), sl])
        for hh in range(heads):
            for r in range(blk // rows):
                rs = slice(rows * r, rows * (r + 1))
                s = s_scr[slot, hh, rs, :]
                if masked:
                    qi = lax.broadcasted_iota(jnp.int32, (rows, blk), 0) + rows * r
                    ki = lax.broadcasted_iota(jnp.int32, (rows, blk), 1)
                    s = jnp.where(ki <= qi, s, -jnp.inf)
                m_old = m_scr[hh, rs, :]
                m_new = jnp.maximum(m_old, jnp.max(s, axis=1, keepdims=True))
                p_scr[hh, rs, :] = jnp.exp(s - jnp.concatenate([m_new] * reps, axis=1)).astype(BF16)
                acc_scr[hh, rs, :] = acc_scr[hh, rs, :] * jnp.exp(m_old - m_new)
                m_scr[hh, rs, :] = m_new
        for hh in range(heads):
            sl = slice(HEAD_SLOT * hh, HEAD_SLOT * (hh + 1))
            acc_scr[hh] += jnp.dot(p_scr[hh], v_ref[pl.ds(start, blk), sl], preferred_element_type=F32)

    for n_blocks in range(1, nq + 1):
        @pl.when(i == n_blocks - 1)
        def _():
            for j in range(n_blocks):
                block(j, j == n_blocks - 1)

    o0, o1 = (acc_scr[hh] / acc_scr[hh][:, V_HEAD:V_HEAD + 1] for hh in range(2))
    lane = lax.broadcasted_iota(jnp.int32, (blk, HEAD_SLOT), 1)
    o_ref[...] = jnp.where(lane < V_HEAD, o0, pltpu.roll(o1, V_HEAD, 1)).astype(BF16)


def _mla_prefill(qcat, kcat, vcat, *, batch, seq, blk=ATTN_BLOCK, rows=ATTN_STRIP):
    heads = 2
    nq = seq // blk
    wide = heads * HEAD_SLOT
    return pl.pallas_call(
        functools.partial(_mla_prefill_kernel, blk=blk, nq=nq, heads=heads, rows=rows),
        grid=(batch, A_HEADS // heads, nq),
        in_specs=[
            pl.BlockSpec((blk, wide), lambda b, h, i: (b * nq + i, h)),
            pl.BlockSpec((seq, wide), lambda b, h, i: (b, h)),
            pl.BlockSpec((seq, wide), lambda b, h, i: (b, h)),
        ],
        out_specs=pl.BlockSpec((blk, heads * V_HEAD), lambda b, h, i: (b * nq + i, h)),
        out_shape=jax.ShapeDtypeStruct((batch * seq, A_HEADS * V_HEAD), BF16),
        scratch_shapes=[pltpu.VMEM((2, heads, blk, blk), F32), pltpu.VMEM((heads, blk, blk), BF16),
                        pltpu.VMEM((heads, blk, LANES), F32), pltpu.VMEM((heads, blk, HEAD_SLOT), F32)],
        compiler_params=_params(3), name="mla_prefill",
    )(qcat, kcat, vcat)


def _mla_decode_kernel(pt_ref, q_ref, ckv_hbm, kr_hbm, cself_ref, tself_ref, o_ref,
                       cbuf, rbuf, kbuf, sems, *, n_chunks, chunk_pages, ahead):
    b = pl.program_id(0)
    nb = pl.num_programs(0)

    def page_copies(bb, c, slot, p):
        pg = pt_ref[bb, c * chunk_pages + p]
        dst = pl.ds(p * PAGE_SIZE, PAGE_SIZE)
        return (pltpu.make_async_copy(ckv_hbm.at[pg], cbuf.at[slot, dst], sems.at[0, slot]),
                pltpu.make_async_copy(kr_hbm.at[pg], rbuf.at[slot, :, dst], sems.at[1, slot]))

    def issue(bb, c, slot):
        for p in range(chunk_pages):
            for cp in page_copies(bb, c, slot, p):
                cp.start()

    def wait(slot):
        for p in range(chunk_pages):
            for cp in page_copies(0, 0, slot, p):
                cp.wait()

    @pl.when(b == 0)
    def _():
        for c in range(ahead):
            issue(0, c, c)

    q = q_ref[0]
    q_lat = q[:, 0:KV_LORA]
    q_rope = q[:, KV_LORA:KV_LORA + QK_ROPE]

    def scores(slot):
        kc = cbuf[slot].astype(BF16)
        kbuf[slot % 2] = kc
        kr = rbuf[slot].astype(BF16)
        return _dot_nt(q_lat, kc) + jnp.dot(q_rope, kr, preferred_element_type=F32)

    def accumulate(carry, s, slot):
        m_old, l, acc = carry
        m_new = jnp.maximum(m_old, jnp.max(s, axis=1, keepdims=True))
        p = jnp.exp(s - m_new)
        alpha = jnp.exp(m_old - m_new)
        return (m_new, alpha * l + jnp.sum(p, axis=1, keepdims=True),
                alpha * acc + jnp.dot(p.astype(BF16), kbuf[slot], preferred_element_type=F32))

    carry = (jnp.full((A_HEADS, 1), -jnp.inf, F32), jnp.zeros((A_HEADS, 1), F32), jnp.zeros((A_HEADS, KV_LORA), F32))
    s_prev = None
    for c in range(n_chunks):
        nxt = c + ahead
        if nxt < n_chunks:
            issue(b, nxt, nxt)
        else:
            @pl.when(b + 1 < nb)
            def _():
                issue(b + 1, nxt - n_chunks, nxt - n_chunks)
        wait(c)
        s_cur = scores(c)
        if c > 0:
            carry = accumulate(carry, s_prev, (c - 1) % 2)
        s_prev = s_cur
    m_old, l, acc = accumulate(carry, s_prev, (n_chunks - 1) % 2)

    c_self = cself_ref[0].astype(BF16).astype(F32)
    r_self = tself_ref[0][:, 0:QK_ROPE].astype(BF16).astype(F32)
    s_self = (jnp.sum(q_lat.astype(F32) * c_self, axis=1, keepdims=True)
              + jnp.sum(q_rope.astype(F32) * r_self, axis=1, keepdims=True))
    m_new = jnp.maximum(m_old, s_self)
    p_self = jnp.exp(s_self - m_new)
    alpha = jnp.exp(m_old - m_new)
    l = alpha * l + p_self
    acc = alpha * acc + p_self.astype(BF16).astype(F32) * c_self
    o_ref[0] = acc / l


def _mla_decode(page_table, qabs, cache_ckv, cache_krope, ckv_self, tail_self, *,
                chunk_pages=DECODE_CHUNK_PAGES, ahead=DECODE_AHEAD):
    nb, n_pages = page_table.shape
    n_chunks = n_pages // chunk_pages
    assert n_chunks * chunk_pages == n_pages and ahead < n_chunks
    keys = chunk_pages * PAGE_SIZE
    grid_spec = pltpu.PrefetchScalarGridSpec(
        num_scalar_prefetch=1,
        grid=(nb,),
        in_specs=[
            pl.BlockSpec((1, A_HEADS, ABS_SLOT), lambda b, pt: (b, 0, 0)),
            pl.BlockSpec(memory_space=pl.ANY),
            pl.BlockSpec(memory_space=pl.ANY),
            pl.BlockSpec((1, 1, KV_LORA), lambda b, pt: (b, 0, 0)),
            pl.BlockSpec((1, 1, LANES), lambda b, pt: (b, 0, 0)),
        ],
        out_specs=pl.BlockSpec((1, A_HEADS, KV_LORA), lambda b, pt: (b, 0, 0)),
        scratch_shapes=[
            pltpu.VMEM((n_chunks, keys, KV_LORA), F32),
            pltpu.VMEM((n_chunks, QK_ROPE, keys), F32),
            pltpu.VMEM((2, keys, KV_LORA), BF16),
            pltpu.SemaphoreType.DMA((2, n_chunks)),
        ],
    )
    return pl.pallas_call(
        functools.partial(_mla_decode_kernel, n_chunks=n_chunks, chunk_pages=chunk_pages, ahead=ahead),
        grid_spec=grid_spec,
        out_shape=jax.ShapeDtypeStruct((nb, A_HEADS, KV_LORA), F32),
        compiler_params=_params(1), name="mla_decode",
    )(page_table, qabs.reshape(nb, A_HEADS, ABS_SLOT), cache_ckv, cache_krope,
      ckv_self.reshape(nb, 1, KV_LORA), tail_self.reshape(nb, 1, LANES))


def _merge_kernel(x_ref, hg_ref, ob_ref, sg_ref, wa_ref, wb_ref, wout_ref, gpost_ref, gmem_ref, wmq_ref,
                  *rest, from_latent):
    if from_latent:
        wuv_ref, x1_ref, qm_ref = rest
        ob = sum(_dot(ob_ref[:, KV_LORA * hh:KV_LORA * (hh + 1)], wuv_ref[hh]) for hh in range(A_HEADS)).astype(BF16)
    else:
        x1_ref, qm_ref = rest
        ob = ob_ref[...]
    ya = jnp.dot(hg_ref[...], wa_ref[...], preferred_element_type=F32)
    yb = jnp.dot(ob, wb_ref[...], preferred_element_type=F32)
    mix = sg_ref[:, 0:D_MODEL] * ya + sg_ref[:, D_MODEL:] * yb
    y = _dot(mix, wout_ref[...])
    x1 = x_ref[...] + _rms(y, gpost_ref[...])
    x1_ref[...] = x1
    qm = _dot(_rms(x1, gmem_ref[...]), wmq_ref[...]) * (X_HEAD_DIM ** -0.5)
    qm_ref[...] = qm.astype(qm_ref.dtype)


def _merge(x, hg, ob, sg, w, *, tm, from_latent, qdtype):
    m = x.shape[0]
    row = lambda i: (i, 0)
    a_width = A_HEADS * V_HEAD
    ob_w = ob.shape[1]
    in_specs = [
        pl.BlockSpec((tm, D_MODEL), row), pl.BlockSpec((tm, M_WIDTH), row), pl.BlockSpec((tm, ob_w), row),
        pl.BlockSpec((tm, 2 * D_MODEL), row),
        _const_spec((M_WIDTH, D_MODEL)), _const_spec((a_width, D_MODEL)), _const_spec((D_MODEL, D_MODEL)),
        _const_spec((1, D_MODEL)), _const_spec((1, D_MODEL)), _const_spec((D_MODEL, X_WIDTH)),
    ]
    args = [x, hg, ob, sg, w["w_branch_a"], w["w_branch_b"], w["w_out"], w["g_post_mix"], w["g_pre_mem"], w["w_mq"]]
    if from_latent:
        in_specs += [_const_spec((A_HEADS, KV_LORA, a_width))]
        args += [w["wuv_heads"]]
    return pl.pallas_call(
        functools.partial(_merge_kernel, from_latent=from_latent),
        grid=(m // tm,), in_specs=in_specs,
        out_specs=[pl.BlockSpec((tm, D_MODEL), row), pl.BlockSpec((tm, X_WIDTH), row)],
        out_shape=[jax.ShapeDtypeStruct((m, D_MODEL), F32), jax.ShapeDtypeStruct((m, X_WIDTH), qdtype)],
        compiler_params=_params(1), name="merge_lat" if from_latent else "merge",
    )(*args)


def _memkv_kernel(mem_ref, g_ref, wk_ref, wv_ref, k_ref, v_ref):
    mn = _rms(mem_ref[...], g_ref[...]).astype(BF16)
    k_ref[...] = jnp.dot(mn, wk_ref[...], preferred_element_type=F32)
    v_ref[...] = jnp.dot(mn, wv_ref[...], preferred_element_type=F32)


def _memkv(mem, w, *, tm):
    m = mem.shape[0]
    row = lambda i: (i, 0)
    return pl.pallas_call(
        _memkv_kernel, grid=(m // tm,),
        in_specs=[pl.BlockSpec((tm, D_MODEL), row), _const_spec((1, D_MODEL)),
                  _const_spec((D_MODEL, X_WIDTH)), _const_spec((D_MODEL, X_WIDTH))],
        out_specs=[pl.BlockSpec((tm, X_WIDTH), row)] * 2,
        out_shape=[jax.ShapeDtypeStruct((m, X_WIDTH), F32)] * 2,
        compiler_params=_params(1), name="memkv",
    )(mem, w["g_mem"], w["w_mk"], w["w_mv"])


def _memattn_kernel(q_ref, k_ref, v_ref, o_ref):
    kb = k_ref[...].astype(BF16)
    vb = v_ref[...].astype(BF16)
    for h in range(X_HEADS):
        sl = slice(X_HEAD_DIM * h, X_HEAD_DIM * (h + 1))
        s = _dot_nt(q_ref[:, sl], kb[:, sl])
        p = jnp.exp(s - jnp.max(s, axis=1, keepdims=True))
        l = jnp.sum(p, axis=1, keepdims=True)
        o_ref[:, sl] = (jnp.dot(p.astype(BF16), vb[:, sl], preferred_element_type=F32) / l).astype(BF16)


def _memattn(qm, mem_k, mem_v, *, batch, seq, tm):
    nt = seq // tm
    tok = lambda b, i: (b * nt + i, 0)
    kv = lambda b, i: (b, 0)
    return pl.pallas_call(
        _memattn_kernel, grid=(batch, nt),
        in_specs=[pl.BlockSpec((tm, X_WIDTH), tok), pl.BlockSpec((N_MEM, X_WIDTH), kv),
                  pl.BlockSpec((N_MEM, X_WIDTH), kv)],
        out_specs=pl.BlockSpec((tm, X_WIDTH), tok),
        out_shape=jax.ShapeDtypeStruct((batch * seq, X_WIDTH), BF16),
        compiler_params=_params(2), name="memattn",
    )(qm, mem_k, mem_v)


def _memattn_step_kernel(q_ref, k_ref, v_ref, o_ref, *, rows):
    for g in range(rows):
        q4 = [q_ref[g:g + 1, X_HEAD_DIM * h:X_HEAD_DIM * (h + 1)] for h in range(X_HEADS)]
        q8 = jnp.concatenate(q4 + q4, axis=0)
        s = jnp.sum(k_ref[g] * q8[None], axis=2, keepdims=True)
        mx = jnp.max(s, axis=0)
        mx = jnp.maximum(mx, pltpu.roll(mx, X_HEADS, 0))
        p = jnp.exp(s - mx[None])
        l8 = jnp.sum(p, axis=0)
        o8 = jnp.sum(p * v_ref[g], axis=0)
        o_ref[g] = (o8[0:X_HEADS] + o8[X_HEADS:]) / (l8[0:X_HEADS] + l8[X_HEADS:])


def _memattn_step(qm, mem_k, mem_v, *, rows=STEP_SEQS):
    nb = qm.shape[0]
    row = lambda i: (i, 0)
    kv = lambda i: (i, 0, 0, 0)
    kv_block = (rows, N_MEM // 2, 2 * X_HEADS, X_HEAD_DIM)
    return pl.pallas_call(
        functools.partial(_memattn_step_kernel, rows=rows), grid=(nb // rows,),
        in_specs=[pl.BlockSpec((rows, X_WIDTH), row), pl.BlockSpec(kv_block, kv), pl.BlockSpec(kv_block, kv)],
        out_specs=pl.BlockSpec((rows, X_HEADS, X_HEAD_DIM), lambda i: (i, 0, 0)),
        out_shape=jax.ShapeDtypeStruct((nb, X_HEADS, X_HEAD_DIM), F32),
        compiler_params=_params(1), name="memattn_step",
    )(qm, mem_k, mem_v).reshape(nb, X_WIDTH)


def _ffn_kernel(x1_ref, o_ref, wmo_ref, gpm_ref, gpf_ref, win_ref, wdn_ref, gpost_ref, y_ref, *, f_chunk):
    x2 = x1_ref[...] + _rms(_dot(o_ref[...], wmo_ref[...]), gpm_ref[...])
    h = _rms(x2, gpf_ref[...]).astype(BF16)
    acc = jnp.zeros(x2.shape, F32)
    for c in range(D_FF // f_chunk):
        sl = slice(f_chunk * c, f_chunk * (c + 1))
        g = jnp.dot(h, win_ref[:, sl], preferred_element_type=F32)
        u = jnp.dot(h, win_ref[:, D_FF + f_chunk * c:D_FF + f_chunk * (c + 1)], preferred_element_type=F32)
        acc = acc + jnp.dot((g * jax.nn.sigmoid(g) * u).astype(BF16), wdn_ref[sl, :], preferred_element_type=F32)
    y_ref[...] = x2 + _rms(acc, gpost_ref[...])


def _ffn(x1, o, w, *, tm, f_chunk=D_FF):
    m = x1.shape[0]
    row = lambda i: (i, 0)
    return pl.pallas_call(
        functools.partial(_ffn_kernel, f_chunk=f_chunk), grid=(m // tm,),
        in_specs=[pl.BlockSpec((tm, D_MODEL), row), pl.BlockSpec((tm, X_WIDTH), row),
                  _const_spec((X_WIDTH, D_MODEL)), _const_spec((1, D_MODEL)), _const_spec((1, D_MODEL)),
                  _const_spec((D_MODEL, 2 * D_FF)), _const_spec((D_FF, D_MODEL)),
                  _const_spec((1, D_MODEL))],
        out_specs=pl.BlockSpec((tm, D_MODEL), row),
        out_shape=jax.ShapeDtypeStruct((m, D_MODEL), F32),
        compiler_params=_params(1), name="ffn",
    )(x1, o, w["w_mo"], w["g_post_mem"], w["g_pre_ffn"], w["w_ffn_in"], w["w_ffn_out"], w["g_post_ffn"])


def _prep_weights(p):
    w = {}
    for name in ("g_pre_mix", "g_q_lora", "g_kv_lora", "g_mhead", "g_post_mix", "g_pre_mem", "g_mem", "g_post_mem",
                 "g_pre_ffn", "g_post_ffn"):
        w[name] = p[name].reshape(1, -1).astype(F32)
    w_in = p["w_in"]
    offs = np.cumsum((0, M_WIDTH, M_WIDTH, M_WIDTH, M_HEADS, M_HEADS, M_WIDTH, Q_LORA, KV_LORA, QK_ROPE,
                      D_MODEL, D_MODEL))
    seg = lambda i: w_in[:, offs[i]:offs[i + 1]]
    mq, mk, mv, mi, mf, mo, cq, ckv, kr, ga, gb = (seg(i) for i in range(11))
    half = QK_ROPE // 2
    swap = lambda a: jnp.concatenate([a[..., half:], a[..., :half]], axis=-1)
    tail = jnp.concatenate([kr, swap(kr), mi, mf,
                            jnp.zeros((D_MODEL, LANES - 2 * QK_ROPE - 2 * M_HEADS), w_in.dtype)], axis=1)
    w["w_qkv"] = w_in[:, offs[0]:offs[3]].astype(BF16)
    w["w_mo_gate"] = mo.astype(BF16)
    w["w_lat"] = w_in[:, offs[6]:offs[8]].astype(BF16)
    w["w_g"] = w_in[:, offs[9]:offs[11]].astype(BF16)
    w["w_tail"] = tail.astype(BF16)
    w["wv_t"] = mv.T.astype(BF16)
    w["b_tail"] = jnp.zeros((1, LANES), F32).at[0, TAIL_IG:TAIL_IG + 2 * M_HEADS].set(p["b_if"].astype(F32))

    pad = HEAD_SLOT - QK_NOPE - QK_ROPE
    wq = p["w_uq"].reshape(Q_LORA, A_HEADS, QK_NOPE + QK_ROPE)
    zq = jnp.zeros((Q_LORA, A_HEADS, pad), wq.dtype)
    w["wq_cat"] = jnp.concatenate([wq, zq], axis=2).reshape(Q_LORA, A_HEADS * HEAD_SLOT).astype(BF16)
    w_uk, w_uv = p["w_uk"], p["w_uv"]
    zk = jnp.zeros((KV_LORA, A_HEADS, HEAD_SLOT - QK_NOPE), w_uk.dtype)
    w["wuk_cat"] = jnp.concatenate([w_uk, zk], axis=2).reshape(KV_LORA, A_HEADS * HEAD_SLOT).astype(BF16)
    w["wuv_cat"] = jnp.concatenate([w_uv, zk], axis=2).reshape(KV_LORA, A_HEADS * HEAD_SLOT).astype(BF16)
    w["wuv_heads"] = (w_uv.transpose(1, 0, 2)[:, :, None, :]
                      * jnp.eye(A_HEADS, dtype=w_uv.dtype)[:, None, :, None]).reshape(
                          A_HEADS, KV_LORA, A_HEADS * V_HEAD).astype(BF16)
    pass_np = np.zeros((A_HEADS, HEAD_SLOT - QK_NOPE, ABS_SLOT), np.float32)
    pass_np[:, np.arange(QK_ROPE), KV_LORA + np.arange(QK_ROPE)] = 1.0
    w["w_abs"] = jnp.concatenate(
        [jnp.concatenate([w_uk.transpose(1, 2, 0), jnp.zeros((A_HEADS, QK_NOPE, ABS_SLOT - KV_LORA), w_uk.dtype)],
                         axis=2).astype(BF16), jnp.asarray(pass_np, BF16)], axis=1)
    for name in ("w_branch_a", "w_branch_b", "w_out", "w_mq", "w_mk", "w_mv", "w_mo", "w_ffn_out"):
        w[name] = p[name].astype(BF16)
    w["w_ffn_in"] = p["w_ffn_in"].astype(BF16)
    return w


def _rope_tables(pos0, n):
    pos = (pos0 + jnp.arange(n)).astype(F32)
    inv = ROPE_BASE ** (-jnp.arange(0, QK_ROPE, 2, dtype=F32) / QK_ROPE)
    ang = pos[:, None] * inv[None, :]
    cos, sin = jnp.cos(ang), jnp.sin(ang)
    c32 = jnp.concatenate([cos, cos], axis=1)
    s32 = jnp.concatenate([-sin, sin], axis=1)
    one = lambda k: jnp.ones((n, k), F32)
    zero = lambda k: jnp.zeros((n, k), F32)
    ct = jnp.concatenate([c32, one(LANES - QK_ROPE)], axis=1)
    st = jnp.concatenate([s32, zero(LANES - QK_ROPE)], axis=1)
    c128 = jnp.concatenate([one(QK_NOPE), c32, one(HEAD_SLOT - QK_NOPE - QK_ROPE)], axis=1)
    s128 = jnp.concatenate([zero(QK_NOPE), s32, zero(HEAD_SLOT - QK_NOPE - QK_ROPE)], axis=1)
    return ct, st, c128, s128


def kernel(x_prompt, x_sample, cache_ckv, cache_krope, cache_mem_k, cache_mem_v, state_C, state_n, state_m, page_table, mem_prompt, g_pre_mix, w_in, b_if, g_mhead, g_q_lora, w_uq, g_kv_lora, w_uk, w_uv, w_branch_a, w_branch_b, w_out, g_post_mix, g_pre_mem, g_mem, w_mq, w_mk, w_mv, w_mo, g_post_mem, g_pre_ffn, w_ffn_in, w_ffn_out, g_post_ffn):
    params = dict(g_pre_mix=g_pre_mix, w_in=w_in, b_if=b_if, g_mhead=g_mhead, g_q_lora=g_q_lora, w_uq=w_uq,
                  g_kv_lora=g_kv_lora, w_uk=w_uk, w_uv=w_uv, w_branch_a=w_branch_a, w_branch_b=w_branch_b,
                  w_out=w_out, g_post_mix=g_post_mix, g_pre_mem=g_pre_mem, g_mem=g_mem, w_mq=w_mq, w_mk=w_mk,
                  w_mv=w_mv, w_mo=w_mo, g_post_mem=g_post_mem, g_pre_ffn=g_pre_ffn, w_ffn_in=w_ffn_in,
                  w_ffn_out=w_ffn_out, g_post_ffn=g_post_ffn)
    depth = w_in.shape[0]
    assert depth == 1, "single-layer stack only"
    bp, seq, _ = x_prompt.shape
    bs, dec_seq, _ = x_sample.shape
    assert dec_seq == 1, "one new token per sample sequence"
    past_len = page_table.shape[1] * PAGE_SIZE
    w = _prep_weights({name: a[0] for name, a in params.items()})

    xp = x_prompt.reshape(bp * seq, D_MODEL)
    qkvm, og, ckv_p, sg, krope_t, gates_t, qcat, kcat, vcat, vt = _inproj(
        xp, w, _rope_tables(0, seq), tm=TOKEN_TILE, absorbed=False, mdtype=BF16)
    hg, c_p, n_p, m_p = _mlstm_prompt(qkvm, vt, gates_t, og, w["g_mhead"], batch=bp, seq=seq)
    ob = _mla_prefill(qcat, kcat, vcat, batch=bp, seq=seq)
    x1, qm = _merge(xp, hg, ob, sg, w, tm=TOKEN_TILE, from_latent=False, qdtype=BF16)
    mem_k, mem_v = _memkv(mem_prompt.reshape(bp * N_MEM, D_MODEL), w, tm=TOKEN_TILE)
    om = _memattn(qm, mem_k, mem_v, batch=bp, seq=seq, tm=TOKEN_TILE)
    y_prompt = _ffn(x1, om, w, tm=TOKEN_TILE).reshape(bp, seq, D_MODEL)

    xs = x_sample.reshape(bs, D_MODEL)
    tables_s = tuple(jnp.broadcast_to(t, (bs, LANES)) for t in _rope_tables(past_len, 1))
    qkvm_s, og_s, ckv_s, sg_s, tail_s, qabs = _inproj(xs, w, tables_s, tm=bs, absorbed=True, mdtype=F32)
    m0_pad = jnp.pad(state_m.reshape(bs, M_HEADS).astype(F32), ((0, 0), (0, LANES - M_HEADS)))
    hg_s, c_s, n_s, m_s = _mlstm_step(qkvm_s, tail_s, state_C.reshape(bs, M_HEADS, M_HEAD_DIM, M_HEAD_DIM),
                                      state_n.reshape(bs, M_HEADS, M_HEAD_DIM), m0_pad, og_s, w["g_mhead"])
    n_phys = cache_ckv.shape[1]
    cache_krope_t = jnp.transpose(cache_krope.reshape(n_phys, PAGE_SIZE, QK_ROPE), (0, 2, 1))
    o_lat = _mla_decode(page_table, qabs, cache_ckv.reshape(n_phys, PAGE_SIZE, KV_LORA), cache_krope_t, ckv_s, tail_s)
    x1_s, qm_s = _merge(xs, hg_s, o_lat.reshape(bs, A_HEADS * KV_LORA), sg_s, w, tm=bs, from_latent=True, qdtype=F32)
    om_s = _memattn_step(qm_s, cache_mem_k.reshape(bs, N_MEM // 2, 2 * X_HEADS, X_HEAD_DIM),
                         cache_mem_v.reshape(bs, N_MEM // 2, 2 * X_HEADS, X_HEAD_DIM))
    y_sample = _ffn(x1_s, om_s, w, tm=bs).reshape(bs, 1, D_MODEL)

    return (y_prompt, y_sample,
            ckv_p.reshape(1, bp, seq, KV_LORA), jnp.transpose(krope_t, (0, 2, 1)).reshape(1, bp, seq, QK_ROPE),
            c_p.reshape(1, bp, M_HEADS, M_HEAD_DIM, M_HEAD_DIM), n_p.reshape(1, bp, M_HEADS, M_HEAD_DIM),
            m_p[:, :, 0].reshape(1, bp, M_HEADS),
            mem_k.reshape(1, bp, N_MEM, X_HEADS, X_HEAD_DIM), mem_v.reshape(1, bp, N_MEM, X_HEADS, X_HEAD_DIM),
            ckv_s.reshape(1, bs, 1, KV_LORA), tail_s[:, :QK_ROPE].reshape(1, bs, 1, QK_ROPE),
            c_s.reshape(1, bs, M_HEADS, M_HEAD_DIM, M_HEAD_DIM), n_s.reshape(1, bs, M_HEADS, M_HEAD_DIM),
            m_s[:, :M_HEADS].reshape(1, bs, M_HEADS))
```

```python
import functools

import jax
import jax.numpy as jnp
import numpy as np
from jax import lax
from jax.experimental import pallas as pl
from jax.experimental.pallas import tpu as pltpu

F32 = jnp.float32
BF16 = jnp.bfloat16

D_MODEL = 1024
PAGE_SIZE = 128
M_HEADS = 4
M_HEAD_DIM = 128
M_WIDTH = M_HEADS * M_HEAD_DIM
M_CHUNK = 128
A_HEADS = 8
QK_NOPE = 64
QK_ROPE = 32
V_HEAD = 64
Q_LORA = 384
KV_LORA = 256
ROPE_BASE = 10000.0
N_MEM = 256
X_HEADS = 4
X_HEAD_DIM = 128
X_WIDTH = X_HEADS * X_HEAD_DIM
D_FF = 2816
EPS = 1e-6

LANES = 128
HEAD_SLOT = 128
ABS_SLOT = 384
ATT_SCALE = (QK_NOPE + QK_ROPE) ** -0.5

V7X_VMEM_BYTES = 64 * 1024 * 1024
VMEM_LIMIT = V7X_VMEM_BYTES * 13 // 16
TOKEN_TILE = 512
ATTN_BLOCK = 512
ATTN_STRIP = 32
MLSTM_SEQS = 8
GATE_SEQS = 4
STEP_SEQS = 16
DECODE_CHUNK_PAGES = 32
DECODE_AHEAD = 2

TAIL_IG = 64
TAIL_LF = 68


def _rms(x, g):
    return x * lax.rsqrt(jnp.mean(x * x, axis=-1, keepdims=True) + EPS) * g


def _dot(a, b):
    return jnp.dot(a.astype(BF16), b.astype(BF16), preferred_element_type=F32)


def _dot_nt(a, b):
    return lax.dot_general(a.astype(BF16), b.astype(BF16), (((1,), (1,)), ((), ())),
                           preferred_element_type=F32)


def _log_sigmoid(x):
    return jnp.minimum(x, 0.0) - jnp.log1p(jnp.exp(-jnp.abs(x)))


def _const_spec(shape):
    nd = len(shape)
    return pl.BlockSpec(shape, lambda *_: (0,) * nd, pipeline_mode=pl.Buffered(1))


def _params(n_axes):
    return pltpu.CompilerParams(dimension_semantics=("arbitrary",) * n_axes, vmem_limit_bytes=VMEM_LIMIT)


def _inproj_kernel(x_ref, gpre_ref, wqkv_ref, wmo_ref, wlat_ref, wg_ref, wtail_ref, btail_ref, ct_ref, st_ref,
                   gq_ref, wq_ref, c128_ref, s128_ref, gkv_ref, *rest, absorbed):
    if absorbed:
        wabs_ref, qkvm_ref, og_ref, ckv_ref, sg_ref, tail_ref, qabs_ref = rest
    else:
        (wuk_ref, wuv_ref, wvt_ref,
         qkvm_ref, og_ref, ckv_ref, sg_ref, krt_ref, gt_ref, qcat_ref, kcat_ref, vcat_ref, vt_ref) = rest
    h = _rms(x_ref[...], gpre_ref[...]).astype(BF16)

    zm = jnp.dot(h, wqkv_ref[...], preferred_element_type=F32)
    qkvm_ref[:, 0:M_WIDTH] = zm[:, 0:M_WIDTH].astype(qkvm_ref.dtype)
    qkvm_ref[:, M_WIDTH:2 * M_WIDTH] = (zm[:, M_WIDTH:2 * M_WIDTH] * (M_HEAD_DIM ** -0.5)).astype(qkvm_ref.dtype)
    qkvm_ref[:, 2 * M_WIDTH:] = zm[:, 2 * M_WIDTH:].astype(qkvm_ref.dtype)
    og_ref[...] = jax.nn.sigmoid(jnp.dot(h, wmo_ref[...], preferred_element_type=F32))
    sg_ref[...] = jax.nn.sigmoid(jnp.dot(h, wg_ref[...], preferred_element_type=F32))

    t = jnp.dot(h, wtail_ref[...], preferred_element_type=F32) + btail_ref[...]
    t = t * ct_ref[...] + pltpu.roll(t, LANES - QK_ROPE, 1) * st_ref[...]
    lane = lax.broadcasted_iota(jnp.int32, t.shape, 1)
    is_lf = jnp.logical_and(lane >= TAIL_LF, lane < TAIL_LF + M_HEADS)
    t = jnp.where(is_lf, _log_sigmoid(t), t)
    if absorbed:
        tail_ref[...] = t
    else:
        tt = jnp.transpose(t)
        krt_ref[0] = tt[0:QK_ROPE]
        gt_ref[...] = tt[TAIL_IG:TAIL_IG + 2 * M_HEADS]

    cqn = _rms(jnp.dot(h, wlat_ref[:, 0:Q_LORA], preferred_element_type=F32), gq_ref[...]).astype(BF16)
    ckvn = _rms(jnp.dot(h, wlat_ref[:, Q_LORA:], preferred_element_type=F32), gkv_ref[...])
    ckv_ref[...] = ckvn

    qc = jnp.dot(cqn, wq_ref[...], preferred_element_type=F32)
    width = A_HEADS * HEAD_SLOT
    first_half = (lax.broadcasted_iota(jnp.int32, qc.shape, 1) % HEAD_SLOT) < QK_NOPE + QK_ROPE // 2
    qs = jnp.where(first_half, pltpu.roll(qc, width - QK_ROPE // 2, 1), pltpu.roll(qc, QK_ROPE // 2, 1))
    c128 = c128_ref[...]
    s128 = s128_ref[...]
    for hh in range(A_HEADS):
        sl = slice(HEAD_SLOT * hh, HEAD_SLOT * (hh + 1))
        qh = ((qc[:, sl] * c128 + qs[:, sl] * s128) * ATT_SCALE).astype(BF16)
        if absorbed:
            qabs_ref[:, ABS_SLOT * hh:ABS_SLOT * (hh + 1)] = jnp.dot(
                qh, wabs_ref[hh], preferred_element_type=F32).astype(BF16)
        else:
            qcat_ref[:, sl] = qh
    if not absorbed:
        cb = ckvn.astype(BF16)
        kr_slot = jnp.where(jnp.logical_and(lane >= QK_NOPE, lane < QK_NOPE + QK_ROPE), pltpu.roll(t, QK_NOPE, 1), 0.0)
        kn = jnp.dot(cb, wuk_ref[...], preferred_element_type=F32)
        for hh in range(A_HEADS):
            sl = slice(HEAD_SLOT * hh, HEAD_SLOT * (hh + 1))
            kcat_ref[:, sl] = (kn[:, sl] + kr_slot).astype(BF16)
        vlane = lax.broadcasted_iota(jnp.int32, (1, A_HEADS * HEAD_SLOT), 1) % HEAD_SLOT
        vcat = jnp.dot(cb, wuv_ref[...], preferred_element_type=F32)
        vcat_ref[...] = jnp.where(vlane == V_HEAD, 1.0, vcat).astype(BF16)
        vt_ref[...] = _dot_nt(wvt_ref[...], h).astype(BF16)


def _inproj(x, w, tables, *, tm, absorbed, mdtype):
    m = x.shape[0]
    ct, st, c128, s128 = tables
    seq = ct.shape[0]
    nt = seq // tm
    grid = (m // tm,)
    row = lambda i: (i, 0)
    tab = lambda i: (i % nt, 0)
    wide = A_HEADS * HEAD_SLOT
    in_specs = [
        pl.BlockSpec((tm, D_MODEL), row), _const_spec((1, D_MODEL)),
        _const_spec((D_MODEL, 3 * M_WIDTH)), _const_spec((D_MODEL, M_WIDTH)), _const_spec((D_MODEL, Q_LORA + KV_LORA)),
        _const_spec((D_MODEL, 2 * D_MODEL)), _const_spec((D_MODEL, LANES)), _const_spec((1, LANES)),
        pl.BlockSpec((tm, LANES), tab), pl.BlockSpec((tm, LANES), tab),
        _const_spec((1, Q_LORA)), _const_spec((Q_LORA, wide)),
        pl.BlockSpec((tm, LANES), tab), pl.BlockSpec((tm, LANES), tab),
        _const_spec((1, KV_LORA)),
    ]
    args = [x, w["g_pre_mix"], w["w_qkv"], w["w_mo_gate"], w["w_lat"], w["w_g"], w["w_tail"], w["b_tail"], ct, st,
            w["g_q_lora"], w["wq_cat"], c128, s128, w["g_kv_lora"]]
    outs = [
        (jax.ShapeDtypeStruct((m, 3 * M_WIDTH), mdtype), pl.BlockSpec((tm, 3 * M_WIDTH), row)),
        (jax.ShapeDtypeStruct((m, M_WIDTH), F32), pl.BlockSpec((tm, M_WIDTH), row)),
        (jax.ShapeDtypeStruct((m, KV_LORA), F32), pl.BlockSpec((tm, KV_LORA), row)),
        (jax.ShapeDtypeStruct((m, 2 * D_MODEL), F32), pl.BlockSpec((tm, 2 * D_MODEL), row)),
    ]
    if absorbed:
        in_specs += [_const_spec((A_HEADS, HEAD_SLOT, ABS_SLOT))]
        args += [w["w_abs"]]
        outs += [(jax.ShapeDtypeStruct((m, LANES), F32), pl.BlockSpec((tm, LANES), row)),
                 (jax.ShapeDtypeStruct((m, A_HEADS * ABS_SLOT), BF16), pl.BlockSpec((tm, A_HEADS * ABS_SLOT), row))]
    else:
        in_specs += [_const_spec((KV_LORA, wide)), _const_spec((KV_LORA, wide)), _const_spec((M_WIDTH, D_MODEL))]
        args += [w["wuk_cat"], w["wuv_cat"], w["wv_t"]]
        outs += [(jax.ShapeDtypeStruct((m // seq, QK_ROPE, seq), F32),
                  pl.BlockSpec((1, QK_ROPE, tm), lambda i: (i // nt, 0, i % nt))),
                 (jax.ShapeDtypeStruct((2 * M_HEADS, m), F32), pl.BlockSpec((2 * M_HEADS, tm), lambda i: (0, i)))]
        outs += [(jax.ShapeDtypeStruct((m, wide), BF16), pl.BlockSpec((tm, wide), row))] * 3
        outs += [(jax.ShapeDtypeStruct((M_WIDTH, m), BF16), pl.BlockSpec((M_WIDTH, tm), lambda i: (0, i)))]
    return pl.pallas_call(
        functools.partial(_inproj_kernel, absorbed=absorbed),
        grid=grid, in_specs=in_specs,
        out_specs=[o[1] for o in outs], out_shape=[o[0] for o in outs],
        compiler_params=_params(1), name="inproj_abs" if absorbed else "inproj",
    )(*args)


def _split3_dot(x, m01):
    hi = x.astype(BF16)
    r1 = x - hi.astype(F32)
    mid = r1.astype(BF16)
    lo = (r1 - mid.astype(F32)).astype(BF16)
    return sum(jnp.dot(part, m01, preferred_element_type=F32) for part in (hi, mid, lo))


def _cumsum_lanes(x, upper):
    return _split3_dot(x, jnp.where(upper, 1.0, 0.0).astype(BF16))


def _cummax_lanes(x):
    lane = lax.broadcasted_iota(jnp.int32, x.shape, 1)
    shift = 1
    while shift < x.shape[1]:
        x = jnp.maximum(x, jnp.where(lane >= shift, pltpu.roll(x, shift, 1), -jnp.inf))
        shift *= 2
    return x


def _mlstm_gates_kernel(g_ref, urow_ref, scal_ref, cols_ref, *, nc, nseq):
    L = M_CHUNK
    H = M_HEADS
    rows = nseq * nc * 2 * H
    g = g_ref[...]
    r = jnp.concatenate([g[:, L * c:L * (c + 1)] for c in range(nseq * nc)], axis=0)
    ri = lax.broadcasted_iota(jnp.int32, (rows, L), 0)
    ci = lax.broadcasted_iota(jnp.int32, (rows, L), 1)
    top = (ri % (2 * H)) < H
    b_all = _cumsum_lanes(r, lax.broadcasted_iota(jnp.int32, (L, L), 0) <= lax.broadcasted_iota(jnp.int32, (L, L), 1))
    b = pltpu.roll(b_all, rows - H, 0)
    u = r - b
    cm = _cummax_lanes(u)
    g_last = jnp.max(jnp.where(ci == L - 1, b, -jnp.inf), axis=1, keepdims=True)
    wlog = g_last - b + r
    wmax = jnp.max(wlog, axis=1, keepdims=True)
    urow_ref[...] = jnp.where(top, u, pltpu.roll(wlog, H, 0)).reshape(nseq, nc, 2 * H, L)
    scal = jnp.where(top, jnp.broadcast_to(g_last, (rows, L)), pltpu.roll(jnp.broadcast_to(wmax, (rows, L)), H, 0))
    scal_ref[...] = scal.reshape(nseq, nc, 2 * H, L)
    first = jnp.where(top, b, pltpu.roll(cm, H, 0))
    for c in range(nseq * nc):
        sl = slice(2 * H * c, 2 * H * (c + 1))
        cols_ref[c // nc, c % nc] = jnp.transpose(jnp.concatenate([first[sl], wlog[sl]], axis=0))


def _mlstm_gates(g, *, batch, nc, nseq=GATE_SEQS):
    blk = lambda b: (b, 0, 0, 0)
    return pl.pallas_call(
        functools.partial(_mlstm_gates_kernel, nc=nc, nseq=nseq),
        grid=(batch // nseq,),
        in_specs=[pl.BlockSpec((2 * M_HEADS, nseq * nc * M_CHUNK), lambda b: (0, b))],
        out_specs=[pl.BlockSpec((nseq, nc, 2 * M_HEADS, M_CHUNK), blk),
                   pl.BlockSpec((nseq, nc, 2 * M_HEADS, M_CHUNK), blk),
                   pl.BlockSpec((nseq, nc, M_CHUNK, 4 * M_HEADS), blk)],
        out_shape=[jax.ShapeDtypeStruct((batch, nc, 2 * M_HEADS, M_CHUNK), F32),
                   jax.ShapeDtypeStruct((batch, nc, 2 * M_HEADS, M_CHUNK), F32),
                   jax.ShapeDtypeStruct((batch, nc, M_CHUNK, 4 * M_HEADS), F32)],
        compiler_params=_params(1), name="mlstm_gates",
    )(g)


def _mlstm_kernel(*refs, nb):
    qkv_ref = refs[0]
    vt_refs = refs[1:1 + nb]
    urow_ref, scal_ref, cols_ref, sel_ref, og_ref, gm_ref, hg_ref, c_ref, n_ref, m_ref = refs[1 + nb:]
    L = M_CHUNK

    @pl.when(pl.program_id(1) == 0)
    def _():
        c_ref[...] = jnp.zeros_like(c_ref)
        n_ref[...] = jnp.zeros_like(n_ref)
        m_ref[...] = jnp.zeros_like(m_ref)

    row = lax.broadcasted_iota(jnp.int32, (L, L), 0)
    col = lax.broadcasted_iota(jnp.int32, (L, L), 1)
    tril = col <= row

    for bb in range(nb):
        vt_ref = vt_refs[bb]
        rows8 = urow_ref[bb, 0]
        u4 = rows8[0:M_HEADS]
        wlog4 = rows8[M_HEADS:2 * M_HEADS]
        scal = scal_ref[bb, 0]
        g_last4 = scal[0:M_HEADS, 0:1]
        wmax4 = scal[M_HEADS:2 * M_HEADS, 0:1]
        cols = _split3_dot(cols_ref[bb, 0], sel_ref[...])
        m_prev4 = m_ref[bb][:, 0:1]
        n_prev4 = n_ref[bb]
        m_new4 = jnp.maximum(g_last4 + m_prev4, wmax4)
        decay4 = jnp.exp(g_last4 + m_prev4 - m_new4)
        ws4 = jnp.exp(wlog4 - m_new4)
        m_ref[bb] = jnp.broadcast_to(m_new4, (M_HEADS, LANES))
        n_rows = []

        for h in range(M_HEADS):
            sl = slice(M_HEAD_DIM * h, M_HEAD_DIM * (h + 1))
            q = qkv_ref[bb, :, sl]
            k = qkv_ref[bb, :, M_WIDTH + M_HEAD_DIM * h:M_WIDTH + M_HEAD_DIM * (h + 1)]
            v = qkv_ref[bb, :, 2 * M_WIDTH + M_HEAD_DIM * h:2 * M_WIDTH + M_HEAD_DIM * (h + 1)]
            m_prev = m_prev4[h:h + 1]
            c_prev = c_ref[bb, h]
            n_prev = n_prev4[h:h + 1]
            b_col = cols[:, LANES * h:LANES * (h + 1)]
            mm_col = jnp.maximum(cols[:, LANES * (M_HEADS + h):LANES * (M_HEADS + h + 1)], m_prev)
            w_intra = jnp.where(tril, jnp.exp(u4[h:h + 1] - mm_col), 0.0)
            w_inter = jnp.exp(m_prev - mm_col)
            s = _dot_nt(q, k) * w_intra
            cn = jnp.concatenate([c_prev.astype(BF16), jnp.broadcast_to(n_prev.astype(BF16), (L, M_HEAD_DIM))],
                                 axis=0)
            qcn = _dot_nt(q, cn)
            num = w_inter * qcn[:, 0:M_HEAD_DIM] + _dot(s, v)
            den = w_inter * qcn[:, M_HEAD_DIM:] + jnp.sum(s, axis=1, keepdims=True)
            hs = num / jnp.maximum(jnp.abs(den), jnp.exp(-(b_col + mm_col)))

            ws = ws4[h:h + 1]
            lhs = jnp.concatenate([(vt_ref[sl, :].astype(F32) * ws).astype(BF16),
                                   jnp.broadcast_to(ws.astype(BF16), (16, L))], axis=0)
            upd = jnp.dot(lhs, k, preferred_element_type=F32)
            decay = decay4[h:h + 1]
            c_ref[bb, h] = decay * c_prev + upd[0:M_HEAD_DIM]
            n_rows.append(decay * n_prev + upd[M_HEAD_DIM:M_HEAD_DIM + 1])

            mu = jnp.mean(hs, axis=1, keepdims=True)
            d = hs - mu
            y = d * lax.rsqrt(jnp.mean(d * d, axis=1, keepdims=True) + EPS) * gm_ref[:, sl]
            hg_ref[bb, :, sl] = (y * og_ref[bb, :, sl]).astype(BF16)
        n_ref[bb] = jnp.concatenate(n_rows, axis=0)


def _mlstm_prompt(qkvm, vt, gates, og, g_mhead, *, batch, seq, nb=MLSTM_SEQS):
    nc = seq // M_CHUNK
    tok = lambda b, c: (b, c, 0)
    chunk = lambda b, c: (b, c, 0, 0)
    urow, scal, cols = _mlstm_gates(gates, batch=batch, nc=nc)
    sel = jnp.asarray(np.kron(np.eye(4 * M_HEADS, 2 * M_HEADS), np.ones((1, LANES))), BF16)
    vt_specs = [pl.BlockSpec((M_WIDTH, M_CHUNK), functools.partial(lambda b, c, bb: (0, (nb * b + bb) * nc + c), bb=bb))
                for bb in range(nb)]
    hg, c_p, n_p, m_p = pl.pallas_call(
        functools.partial(_mlstm_kernel, nb=nb),
        grid=(batch // nb, nc),
        in_specs=[pl.BlockSpec((nb, M_CHUNK, 3 * M_WIDTH), tok)] + vt_specs + [
            pl.BlockSpec((nb, 1, 2 * M_HEADS, M_CHUNK), chunk),
            pl.BlockSpec((nb, 1, 2 * M_HEADS, M_CHUNK), chunk),
            pl.BlockSpec((nb, 1, M_CHUNK, 4 * M_HEADS), chunk),
            pl.BlockSpec((4 * M_HEADS, 2 * M_HEADS * LANES), lambda b, c: (0, 0)),
            pl.BlockSpec((nb, M_CHUNK, M_WIDTH), tok),
            pl.BlockSpec((1, M_WIDTH), lambda b, c: (0, 0)),
        ],
        out_specs=[
            pl.BlockSpec((nb, M_CHUNK, M_WIDTH), tok),
            pl.BlockSpec((nb, M_HEADS, M_HEAD_DIM, M_HEAD_DIM), lambda b, c: (b, 0, 0, 0)),
            pl.BlockSpec((nb, M_HEADS, M_HEAD_DIM), lambda b, c: (b, 0, 0)),
            pl.BlockSpec((nb, M_HEADS, LANES), lambda b, c: (b, 0, 0)),
        ],
        out_shape=[
            jax.ShapeDtypeStruct((batch, seq, M_WIDTH), BF16),
            jax.ShapeDtypeStruct((batch, M_HEADS, M_HEAD_DIM, M_HEAD_DIM), F32),
            jax.ShapeDtypeStruct((batch, M_HEADS, M_HEAD_DIM), F32),
            jax.ShapeDtypeStruct((batch, M_HEADS, LANES), F32),
        ],
        compiler_params=_params(2), name="mlstm_prompt",
    )(qkvm.reshape(batch, seq, 3 * M_WIDTH), *([vt] * nb), urow, scal, cols, sel,
      og.reshape(batch, seq, M_WIDTH), g_mhead)
    return hg.reshape(batch * seq, M_WIDTH), c_p, n_p, m_p


def _mlstm_step_kernel(qkv_ref, tail_ref, c_ref, n_ref, m_ref, og_ref, gm_ref, hg_ref, co_ref, no_ref, mo_ref, *, rows):
    D = M_HEAD_DIM
    lane = lax.broadcasted_iota(jnp.int32, (rows, LANES), 1)
    rowi = lax.broadcasted_iota(jnp.int32, (rows, D), 0)
    tail = tail_ref[...]
    m_in = m_ref[...]
    m_out = jnp.zeros((rows, LANES), F32)
    for h in range(M_HEADS):
        sl = slice(D * h, D * (h + 1))
        q = qkv_ref[:, sl]
        k = qkv_ref[:, M_WIDTH + D * h:M_WIDTH + D * (h + 1)]
        v = qkv_ref[:, 2 * M_WIDTH + D * h:2 * M_WIDTH + D * (h + 1)]
        ig = tail[:, TAIL_IG + h:TAIL_IG + h + 1]
        lf = tail[:, TAIL_LF + h:TAIL_LF + h + 1]
        m_prev = m_in[:, h:h + 1]
        n_prev = n_ref[:, h, :]
        a = lf + m_prev
        mt = jnp.maximum(a, ig)
        w_in = jnp.exp(ig - mt)
        w_st = jnp.exp(a - mt)
        s = jnp.sum(q * k, axis=1, keepdims=True) * w_in
        cq = jnp.zeros((rows, D), F32)
        for g in range(rows):
            cq = jnp.where(rowi == g, _dot_nt(q, c_ref[g, h]), cq)
        num = w_st * cq + s * v
        den = w_st * jnp.sum(n_prev * q, axis=1, keepdims=True) + s
        hs = num / jnp.maximum(jnp.abs(den), jnp.exp(-mt))
        vw_t = jnp.transpose(v * w_in)
        for g in range(rows):
            co_ref[g, h] = w_st[g:g + 1] * c_ref[g, h] + vw_t[:, g:g + 1] * k[g:g + 1, :]
        no_ref[:, h, :] = w_st * n_prev + w_in * k
        m_out = jnp.where(lane == h, mt, m_out)
        mu = jnp.mean(hs, axis=1, keepdims=True)
        d = hs - mu
        y = d * lax.rsqrt(jnp.mean(d * d, axis=1, keepdims=True) + EPS) * gm_ref[:, sl]
        hg_ref[:, sl] = (y * og_ref[:, sl]).astype(BF16)
    mo_ref[...] = m_out


def _mlstm_step(qkvm, tail, c0, n0, m0_pad, og, g_mhead, *, rows=STEP_SEQS):
    nb = qkvm.shape[0]
    row = lambda i: (i, 0)
    return pl.pallas_call(
        functools.partial(_mlstm_step_kernel, rows=rows),
        grid=(nb // rows,),
        in_specs=[
            pl.BlockSpec((rows, 3 * M_WIDTH), row),
            pl.BlockSpec((rows, LANES), row),
            pl.BlockSpec((rows, M_HEADS, M_HEAD_DIM, M_HEAD_DIM), lambda i: (i, 0, 0, 0)),
            pl.BlockSpec((rows, M_HEADS, M_HEAD_DIM), lambda i: (i, 0, 0)),
            pl.BlockSpec((rows, LANES), row),
            pl.BlockSpec((rows, M_WIDTH), row),
            pl.BlockSpec((1, M_WIDTH), lambda i: (0, 0)),
        ],
        out_specs=[
            pl.BlockSpec((rows, M_WIDTH), row),
            pl.BlockSpec((rows, M_HEADS, M_HEAD_DIM, M_HEAD_DIM), lambda i: (i, 0, 0, 0)),
            pl.BlockSpec((rows, M_HEADS, M_HEAD_DIM), lambda i: (i, 0, 0)),
            pl.BlockSpec((rows, LANES), row),
        ],
        out_shape=[
            jax.ShapeDtypeStruct((nb, M_WIDTH), BF16),
            jax.ShapeDtypeStruct((nb, M_HEADS, M_HEAD_DIM, M_HEAD_DIM), F32),
            jax.ShapeDtypeStruct((nb, M_HEADS, M_HEAD_DIM), F32),
            jax.ShapeDtypeStruct((nb, LANES), F32),
        ],
        compiler_params=_params(1), name="mlstm_step",
    )(qkvm, tail, c0, n0, m0_pad, og, g_mhead)


def _mla_prefill_kernel(q_ref, k_ref, v_ref, o_ref, s_scr, p_scr, m_scr, acc_scr, *, blk, nq, heads, rows):
    reps = blk // LANES
    lane = lax.broadcasted_iota(jnp.int32, (blk, HEAD_SLOT), 1)
    n_done = 0
    for i in range(nq):
        par = i % 2
        q_rows = slice(blk * i, blk * (i + 1))
        m_scr[par] = jnp.full(m_scr.shape[1:], -jnp.inf, F32)
        acc_scr[par] = jnp.zeros(acc_scr.shape[1:], F32)
        for j in range(i + 1):
            masked = j == i
            k_rows = slice(blk * j, blk * (j + 1))
            slot = n_done % 2
            n_done += 1
            for hh in range(heads):
                sl = slice(HEAD_SLOT * hh, HEAD_SLOT * (hh + 1))
                s_scr[slot, hh] = _dot_nt(q_ref[q_rows, sl], k_ref[k_rows, sl])
            for hh in range(heads):
                for r in range(blk // rows):
                    rs = slice(rows * r, rows * (r + 1))
                    s = s_scr[slot, hh, rs, :]
                    if masked:
                        qi = lax.broadcasted_iota(jnp.int32, (rows, blk), 0) + rows * r
                        ki = lax.broadcasted_iota(jnp.int32, (rows, blk), 1)
                        s = jnp.where(ki <= qi, s, -jnp.inf)
                    m_old = m_scr[par, hh, rs, :]
                    m_new = jnp.maximum(m_old, jnp.max(s, axis=1, keepdims=True))
                    p_scr[hh, rs, :] = jnp.exp(s - jnp.concatenate([m_new] * reps, axis=1)).astype(BF16)
                    acc_scr[par, hh, rs, :] = acc_scr[par, hh, rs, :] * jnp.exp(m_old - m_new)
                    m_scr[par, hh, rs, :] = m_new
            for hh in range(heads):
                sl = slice(HEAD_SLOT * hh, HEAD_SLOT * (hh + 1))
                acc_scr[par, hh] += jnp.dot(p_scr[hh], v_ref[k_rows, sl], preferred_element_type=F32)
        o0, o1 = (acc_scr[par, hh] / acc_scr[par, hh][:, V_HEAD:V_HEAD + 1] for hh in range(2))
        o_ref[q_rows, :] = jnp.where(lane < V_HEAD, o0, pltpu.roll(o1, V_HEAD, 1)).astype(BF16)


def _mla_prefill(qcat, kcat, vcat, *, batch, seq, blk=ATTN_BLOCK, rows=ATTN_STRIP):
    heads = 2
    nq = seq // blk
    wide = heads * HEAD_SLOT
    per_seq = lambda b, h: (b, h)
    return pl.pallas_call(
        functools.partial(_mla_prefill_kernel, blk=blk, nq=nq, heads=heads, rows=rows),
        grid=(batch, A_HEADS // heads),
        in_specs=[pl.BlockSpec((seq, wide), per_seq)] * 3,
        out_specs=pl.BlockSpec((seq, heads * V_HEAD), per_seq),
        out_shape=jax.ShapeDtypeStruct((batch * seq, A_HEADS * V_HEAD), BF16),
        scratch_shapes=[pltpu.VMEM((2, heads, blk, blk), F32), pltpu.VMEM((heads, blk, blk), BF16),
                        pltpu.VMEM((2, heads, blk, LANES), F32), pltpu.VMEM((2, heads, blk, HEAD_SLOT), F32)],
        compiler_params=_params(2), name="mla_prefill",
    )(qcat, kcat, vcat)


def _mla_decode_kernel(pt_ref, q_ref, ckv_hbm, kr_hbm, cself_ref, tself_ref, o_ref,
                       cbuf, rbuf, kbuf, sems, *, n_chunks, chunk_pages, ahead):
    b = pl.program_id(0)
    nb = pl.num_programs(0)

    def page_copies(bb, c, slot, p):
        pg = pt_ref[bb, c * chunk_pages + p]
        dst = pl.ds(p * PAGE_SIZE, PAGE_SIZE)
        return (pltpu.make_async_copy(ckv_hbm.at[pg], cbuf.at[slot, dst], sems.at[0, slot]),
                pltpu.make_async_copy(kr_hbm.at[pg], rbuf.at[slot, :, dst], sems.at[1, slot]))

    def issue(bb, c, slot):
        for p in range(chunk_pages):
            for cp in page_copies(bb, c, slot, p):
                cp.start()

    def wait(slot):
        for p in range(chunk_pages):
            for cp in page_copies(0, 0, slot, p):
                cp.wait()

    @pl.when(b == 0)
    def _():
        for c in range(ahead):
            issue(0, c, c)

    q = q_ref[0]
    q_lat = q[:, 0:KV_LORA]
    q_rope = q[:, KV_LORA:KV_LORA + QK_ROPE]

    def scores(slot):
        kc = cbuf[slot].astype(BF16)
        kbuf[slot % 2] = kc
        kr = rbuf[slot].astype(BF16)
        return _dot_nt(q_lat, kc) + jnp.dot(q_rope, kr, preferred_element_type=F32)

    def accumulate(carry, s, slot):
        m_old, l, acc = carry
        m_new = jnp.maximum(m_old, jnp.max(s, axis=1, keepdims=True))
        p = jnp.exp(s - m_new)
        alpha = jnp.exp(m_old - m_new)
        return (m_new, alpha * l + jnp.sum(p, axis=1, keepdims=True),
                alpha * acc + jnp.dot(p.astype(BF16), kbuf[slot], preferred_element_type=F32))

    carry = (jnp.full((A_HEADS, 1), -jnp.inf, F32), jnp.zeros((A_HEADS, 1), F32), jnp.zeros((A_HEADS, KV_LORA), F32))
    s_prev = None
    for c in range(n_chunks):
        nxt = c + ahead
        if nxt < n_chunks:
            issue(b, nxt, nxt)
        else:
            @pl.when(b + 1 < nb)
            def _():
                issue(b + 1, nxt - n_chunks, nxt - n_chunks)
        wait(c)
        s_cur = scores(c)
        if c > 0:
            carry = accumulate(carry, s_prev, (c - 1) % 2)
        s_prev = s_cur
    m_old, l, acc = accumulate(carry, s_prev, (n_chunks - 1) % 2)

    c_self = cself_ref[0].astype(BF16).astype(F32)
    r_self = tself_ref[0][:, 0:QK_ROPE].astype(BF16).astype(F32)
    s_self = (jnp.sum(q_lat.astype(F32) * c_self, axis=1, keepdims=True)
              + jnp.sum(q_rope.astype(F32) * r_self, axis=1, keepdims=True))
    m_new = jnp.maximum(m_old, s_self)
    p_self = jnp.exp(s_self - m_new)
    alpha = jnp.exp(m_old - m_new)
    l = alpha * l + p_self
    acc = alpha * acc + p_self.astype(BF16).astype(F32) * c_self
    o_ref[0] = acc / l


def _mla_decode(page_table, qabs, cache_ckv, cache_krope, ckv_self, tail_self, *,
                chunk_pages=DECODE_CHUNK_PAGES, ahead=DECODE_AHEAD):
    nb, n_pages = page_table.shape
    n_chunks = n_pages // chunk_pages
    assert n_chunks * chunk_pages == n_pages and ahead < n_chunks
    keys = chunk_pages * PAGE_SIZE
    grid_spec = pltpu.PrefetchScalarGridSpec(
        num_scalar_prefetch=1,
        grid=(nb,),
        in_specs=[
            pl.BlockSpec((1, A_HEADS, ABS_SLOT), lambda b, pt: (b, 0, 0)),
            pl.BlockSpec(memory_space=pl.ANY),
            pl.BlockSpec(memory_space=pl.ANY),
            pl.BlockSpec((1, 1, KV_LORA), lambda b, pt: (b, 0, 0)),
            pl.BlockSpec((1, 1, LANES), lambda b, pt: (b, 0, 0)),
        ],
        out_specs=pl.BlockSpec((1, A_HEADS, KV_LORA), lambda b, pt: (b, 0, 0)),
        scratch_shapes=[
            pltpu.VMEM((n_chunks, keys, KV_LORA), F32),
            pltpu.VMEM((n_chunks, QK_ROPE, keys), F32),
            pltpu.VMEM((2, keys, KV_LORA), BF16),
            pltpu.SemaphoreType.DMA((2, n_chunks)),
        ],
    )
    return pl.pallas_call(
        functools.partial(_mla_decode_kernel, n_chunks=n_chunks, chunk_pages=chunk_pages, ahead=ahead),
        grid_spec=grid_spec,
        out_shape=jax.ShapeDtypeStruct((nb, A_HEADS, KV_LORA), F32),
        compiler_params=_params(1), name="mla_decode",
    )(page_table, qabs.reshape(nb, A_HEADS, ABS_SLOT), cache_ckv, cache_krope,
      ckv_self.reshape(nb, 1, KV_LORA), tail_self.reshape(nb, 1, LANES))


def _merge_kernel(x_ref, hg_ref, ob_ref, sg_ref, wa_ref, wb_ref, wout_ref, gpost_ref, gmem_ref, wmq_ref,
                  *rest, from_latent):
    if from_latent:
        wuv_ref, x1_ref, qm_ref = rest
        ob = sum(_dot(ob_ref[:, KV_LORA * hh:KV_LORA * (hh + 1)], wuv_ref[hh]) for hh in range(A_HEADS)).astype(BF16)
    else:
        x1_ref, qm_ref = rest
        ob = ob_ref[...]
    ya = jnp.dot(hg_ref[...], wa_ref[...], preferred_element_type=F32)
    yb = jnp.dot(ob, wb_ref[...], preferred_element_type=F32)
    mix = sg_ref[:, 0:D_MODEL] * ya + sg_ref[:, D_MODEL:] * yb
    y = _dot(mix, wout_ref[...])
    x1 = x_ref[...] + _rms(y, gpost_ref[...])
    x1_ref[...] = x1
    qm = _dot(_rms(x1, gmem_ref[...]), wmq_ref[...]) * (X_HEAD_DIM ** -0.5)
    qm_ref[...] = qm.astype(qm_ref.dtype)


def _merge(x, hg, ob, sg, w, *, tm, from_latent, qdtype):
    m = x.shape[0]
    row = lambda i: (i, 0)
    a_width = A_HEADS * V_HEAD
    ob_w = ob.shape[1]
    in_specs = [
        pl.BlockSpec((tm, D_MODEL), row), pl.BlockSpec((tm, M_WIDTH), row), pl.BlockSpec((tm, ob_w), row),
        pl.BlockSpec((tm, 2 * D_MODEL), row),
        _const_spec((M_WIDTH, D_MODEL)), _const_spec((a_width, D_MODEL)), _const_spec((D_MODEL, D_MODEL)),
        _const_spec((1, D_MODEL)), _const_spec((1, D_MODEL)), _const_spec((D_MODEL, X_WIDTH)),
    ]
    args = [x, hg, ob, sg, w["w_branch_a"], w["w_branch_b"], w["w_out"], w["g_post_mix"], w["g_pre_mem"], w["w_mq"]]
    if from_latent:
        in_specs += [_const_spec((A_HEADS, KV_LORA, a_width))]
        args += [w["wuv_heads"]]
    return pl.pallas_call(
        functools.partial(_merge_kernel, from_latent=from_latent),
        grid=(m // tm,), in_specs=in_specs,
        out_specs=[pl.BlockSpec((tm, D_MODEL), row), pl.BlockSpec((tm, X_WIDTH), row)],
        out_shape=[jax.ShapeDtypeStruct((m, D_MODEL), F32), jax.ShapeDtypeStruct((m, X_WIDTH), qdtype)],
        compiler_params=_params(1), name="merge_lat" if from_latent else "merge",
    )(*args)


def _memkv_kernel(mem_ref, g_ref, wk_ref, wv_ref, k_ref, v_ref):
    mn = _rms(mem_ref[...], g_ref[...]).astype(BF16)
    k_ref[...] = jnp.dot(mn, wk_ref[...], preferred_element_type=F32)
    v_ref[...] = jnp.dot(mn, wv_ref[...], preferred_element_type=F32)


def _memkv(mem, w, *, tm):
    m = mem.shape[0]
    row = lambda i: (i, 0)
    return pl.pallas_call(
        _memkv_kernel, grid=(m // tm,),
        in_specs=[pl.BlockSpec((tm, D_MODEL), row), _const_spec((1, D_MODEL)),
                  _const_spec((D_MODEL, X_WIDTH)), _const_spec((D_MODEL, X_WIDTH))],
        out_specs=[pl.BlockSpec((tm, X_WIDTH), row)] * 2,
        out_shape=[jax.ShapeDtypeStruct((m, X_WIDTH), F32)] * 2,
        compiler_params=_params(1), name="memkv",
    )(mem, w["g_mem"], w["w_mk"], w["w_mv"])


def _memattn_kernel(q_ref, k_ref, v_ref, o_ref):
    kb = k_ref[...].astype(BF16)
    vb = v_ref[...].astype(BF16)
    for h in range(X_HEADS):
        sl = slice(X_HEAD_DIM * h, X_HEAD_DIM * (h + 1))
        s = _dot_nt(q_ref[:, sl], kb[:, sl])
        p = jnp.exp(s - jnp.max(s, axis=1, keepdims=True))
        l = jnp.sum(p, axis=1, keepdims=True)
        o_ref[:, sl] = (jnp.dot(p.astype(BF16), vb[:, sl], preferred_element_type=F32) / l).astype(BF16)


def _memattn(qm, mem_k, mem_v, *, batch, seq, tm):
    nt = seq // tm
    tok = lambda b, i: (b * nt + i, 0)
    kv = lambda b, i: (b, 0)
    return pl.pallas_call(
        _memattn_kernel, grid=(batch, nt),
        in_specs=[pl.BlockSpec((tm, X_WIDTH), tok), pl.BlockSpec((N_MEM, X_WIDTH), kv),
                  pl.BlockSpec((N_MEM, X_WIDTH), kv)],
        out_specs=pl.BlockSpec((tm, X_WIDTH), tok),
        out_shape=jax.ShapeDtypeStruct((batch * seq, X_WIDTH), BF16),
        compiler_params=_params(2), name="memattn",
    )(qm, mem_k, mem_v)


def _memattn_step_kernel(q_ref, k_ref, v_ref, o_ref, *, rows):
    for g in range(rows):
        q4 = [q_ref[g:g + 1, X_HEAD_DIM * h:X_HEAD_DIM * (h + 1)] for h in range(X_HEADS)]
        q8 = jnp.concatenate(q4 + q4, axis=0)
        s = jnp.sum(k_ref[g] * q8[None], axis=2, keepdims=True)
        mx = jnp.max(s, axis=0)
        mx = jnp.maximum(mx, pltpu.roll(mx, X_HEADS, 0))
        p = jnp.exp(s - mx[None])
        l8 = jnp.sum(p, axis=0)
        o8 = jnp.sum(p * v_ref[g], axis=0)
        o_ref[g] = (o8[0:X_HEADS] + o8[X_HEADS:]) / (l8[0:X_HEADS] + l8[X_HEADS:])


def _memattn_step(qm, mem_k, mem_v, *, rows=STEP_SEQS):
    nb = qm.shape[0]
    row = lambda i: (i, 0)
    kv = lambda i: (i, 0, 0, 0)
    kv_block = (rows, N_MEM // 2, 2 * X_HEADS, X_HEAD_DIM)
    return pl.pallas_call(
        functools.partial(_memattn_step_kernel, rows=rows), grid=(nb // rows,),
        in_specs=[pl.BlockSpec((rows, X_WIDTH), row), pl.BlockSpec(kv_block, kv), pl.BlockSpec(kv_block, kv)],
        out_specs=pl.BlockSpec((rows, X_HEADS, X_HEAD_DIM), lambda i: (i, 0, 0)),
        out_shape=jax.ShapeDtypeStruct((nb, X_HEADS, X_HEAD_DIM), F32),
        compiler_params=_params(1), name="memattn_step",
    )(qm, mem_k, mem_v).reshape(nb, X_WIDTH)


def _ffn_kernel(x1_ref, o_ref, wmo_ref, gpm_ref, gpf_ref, win_ref, wdn_ref, gpost_ref, y_ref, *, f_chunk):
    x2 = x1_ref[...] + _rms(_dot(o_ref[...], wmo_ref[...]), gpm_ref[...])
    h = _rms(x2, gpf_ref[...]).astype(BF16)
    acc = jnp.zeros(x2.shape, F32)
    for c in range(D_FF // f_chunk):
        sl = slice(f_chunk * c, f_chunk * (c + 1))
        g = jnp.dot(h, win_ref[:, sl], preferred_element_type=F32)
        u = jnp.dot(h, win_ref[:, D_FF + f_chunk * c:D_FF + f_chunk * (c + 1)], preferred_element_type=F32)
        acc = acc + jnp.dot((g * jax.nn.sigmoid(g) * u).astype(BF16), wdn_ref[sl, :], preferred_element_type=F32)
    y_ref[...] = x2 + _rms(acc, gpost_ref[...])


def _ffn(x1, o, w, *, tm, f_chunk=D_FF):
    m = x1.shape[0]
    row = lambda i: (i, 0)
    return pl.pallas_call(
        functools.partial(_ffn_kernel, f_chunk=f_chunk), grid=(m // tm,),
        in_specs=[pl.BlockSpec((tm, D_MODEL), row), pl.BlockSpec((tm, X_WIDTH), row),
                  _const_spec((X_WIDTH, D_MODEL)), _const_spec((1, D_MODEL)), _const_spec((1, D_MODEL)),
                  _const_spec((D_MODEL, 2 * D_FF)), _const_spec((D_FF, D_MODEL)),
                  _const_spec((1, D_MODEL))],
        out_specs=pl.BlockSpec((tm, D_MODEL), row),
        out_shape=jax.ShapeDtypeStruct((m, D_MODEL), F32),
        compiler_params=_params(1), name="ffn",
    )(x1, o, w["w_mo"], w["g_post_mem"], w["g_pre_ffn"], w["w_ffn_in"], w["w_ffn_out"], w["g_post_ffn"])


def _prep_weights(p):
    w = {}
    for name in ("g_pre_mix", "g_q_lora", "g_kv_lora", "g_mhead", "g_post_mix", "g_pre_mem", "g_mem", "g_post_mem",
                 "g_pre_ffn", "g_post_ffn"):
        w[name] = p[name].reshape(1, -1).astype(F32)
    w_in = p["w_in"]
    offs = np.cumsum((0, M_WIDTH, M_WIDTH, M_WIDTH, M_HEADS, M_HEADS, M_WIDTH, Q_LORA, KV_LORA, QK_ROPE,
                      D_MODEL, D_MODEL))
    seg = lambda i: w_in[:, offs[i]:offs[i + 1]]
    mq, mk, mv, mi, mf, mo, cq, ckv, kr, ga, gb = (seg(i) for i in range(11))
    half = QK_ROPE // 2
    swap = lambda a: jnp.concatenate([a[..., half:], a[..., :half]], axis=-1)
    tail = jnp.concatenate([kr, swap(kr), mi, mf,
                            jnp.zeros((D_MODEL, LANES - 2 * QK_ROPE - 2 * M_HEADS), w_in.dtype)], axis=1)
    w["w_qkv"] = w_in[:, offs[0]:offs[3]].astype(BF16)
    w["w_mo_gate"] = mo.astype(BF16)
    w["w_lat"] = w_in[:, offs[6]:offs[8]].astype(BF16)
    w["w_g"] = w_in[:, offs[9]:offs[11]].astype(BF16)
    w["w_tail"] = tail.astype(BF16)
    w["wv_t"] = mv.T.astype(BF16)
    w["b_tail"] = jnp.zeros((1, LANES), F32).at[0, TAIL_IG:TAIL_IG + 2 * M_HEADS].set(p["b_if"].astype(F32))

    pad = HEAD_SLOT - QK_NOPE - QK_ROPE
    wq = p["w_uq"].reshape(Q_LORA, A_HEADS, QK_NOPE + QK_ROPE)
    zq = jnp.zeros((Q_LORA, A_HEADS, pad), wq.dtype)
    w["wq_cat"] = jnp.concatenate([wq, zq], axis=2).reshape(Q_LORA, A_HEADS * HEAD_SLOT).astype(BF16)
    w_uk, w_uv = p["w_uk"], p["w_uv"]
    zk = jnp.zeros((KV_LORA, A_HEADS, HEAD_SLOT - QK_NOPE), w_uk.dtype)
    w["wuk_cat"] = jnp.concatenate([w_uk, zk], axis=2).reshape(KV_LORA, A_HEADS * HEAD_SLOT).astype(BF16)
    w["wuv_cat"] = jnp.concatenate([w_uv, zk], axis=2).reshape(KV_LORA, A_HEADS * HEAD_SLOT).astype(BF16)
    w["wuv_heads"] = (w_uv.transpose(1, 0, 2)[:, :, None, :]
                      * jnp.eye(A_HEADS, dtype=w_uv.dtype)[:, None, :, None]).reshape(
                          A_HEADS, KV_LORA, A_HEADS * V_HEAD).astype(BF16)
    pass_np = np.zeros((A_HEADS, HEAD_SLOT - QK_NOPE, ABS_SLOT), np.float32)
    pass_np[:, np.arange(QK_ROPE), KV_LORA + np.arange(QK_ROPE)] = 1.0
    w["w_abs"] = jnp.concatenate(
        [jnp.concatenate([w_uk.transpose(1, 2, 0), jnp.zeros((A_HEADS, QK_NOPE, ABS_SLOT - KV_LORA), w_uk.dtype)],
                         axis=2).astype(BF16), jnp.asarray(pass_np, BF16)], axis=1)
    for name in ("w_branch_a", "w_branch_b", "w_out", "w_mq", "w_mk", "w_mv", "w_mo", "w_ffn_out"):
        w[name] = p[name].astype(BF16)
    w["w_ffn_in"] = p["w_ffn_in"].astype(BF16)
    return w


def _rope_tables(pos0, n):
    pos = (pos0 + jnp.arange(n)).astype(F32)
    inv = ROPE_BASE ** (-jnp.arange(0, QK_ROPE, 2, dtype=F32) / QK_ROPE)
    ang = pos[:, None] * inv[None, :]
    cos, sin = jnp.cos(ang), jnp.sin(ang)
    c32 = jnp.concatenate([cos, cos], axis=1)
    s32 = jnp.concatenate([-sin, sin], axis=1)
    one = lambda k: jnp.ones((n, k), F32)
    zero = lambda k: jnp.zeros((n, k), F32)
    ct = jnp.concatenate([c32, one(LANES - QK_ROPE)], axis=1)
    st = jnp.concatenate([s32, zero(LANES - QK_ROPE)], axis=1)
    c128 = jnp.concatenate([one(QK_NOPE), c32, one(HEAD_SLOT - QK_NOPE - QK_ROPE)], axis=1)
    s128 = jnp.concatenate([zero(QK_NOPE), s32, zero(HEAD_SLOT - QK_NOPE - QK_ROPE)], axis=1)
    return ct, st, c128, s128


def kernel(x_prompt, x_sample, cache_ckv, cache_krope, cache_mem_k, cache_mem_v, state_C, state_n, state_m, page_table, mem_prompt, g_pre_mix, w_in, b_if, g_mhead, g_q_lora, w_uq, g_kv_lora, w_uk, w_uv, w_branch_a, w_branch_b, w_out, g_post_mix, g_pre_mem, g_mem, w_mq, w_mk, w_mv, w_mo, g_post_mem, g_pre_ffn, w_ffn_in, w_ffn_out, g_post_ffn):
    params = dict(g_pre_mix=g_pre_mix, w_in=w_in, b_if=b_if, g_mhead=g_mhead, g_q_lora=g_q_lora, w_uq=w_uq,
                  g_kv_lora=g_kv_lora, w_uk=w_uk, w_uv=w_uv, w_branch_a=w_branch_a, w_branch_b=w_branch_b,
                  w_out=w_out, g_post_mix=g_post_mix, g_pre_mem=g_pre_mem, g_mem=g_mem, w_mq=w_mq, w_mk=w_mk,
                  w_mv=w_mv, w_mo=w_mo, g_post_mem=g_post_mem, g_pre_ffn=g_pre_ffn, w_ffn_in=w_ffn_in,
                  w_ffn_out=w_ffn_out, g_post_ffn=g_post_ffn)
    depth = w_in.shape[0]
    assert depth == 1, "single-layer stack only"
    bp, seq, _ = x_prompt.shape
    bs, dec_seq, _ = x_sample.shape
    assert dec_seq == 1, "one new token per sample sequence"
    past_len = page_table.shape[1] * PAGE_SIZE
    w = _prep_weights({name: a[0] for name, a in params.items()})

    xp = x_prompt.reshape(bp * seq, D_MODEL)
    qkvm, og, ckv_p, sg, krope_t, gates_t, qcat, kcat, vcat, vt = _inproj(
        xp, w, _rope_tables(0, seq), tm=TOKEN_TILE, absorbed=False, mdtype=BF16)
    hg, c_p, n_p, m_p = _mlstm_prompt(qkvm, vt, gates_t, og, w["g_mhead"], batch=bp, seq=seq)
    ob = _mla_prefill(qcat, kcat, vcat, batch=bp, seq=seq)
    x1, qm = _merge(xp, hg, ob, sg, w, tm=TOKEN_TILE, from_latent=False, qdtype=BF16)
    mem_k, mem_v = _memkv(mem_prompt.reshape(bp * N_MEM, D_MODEL), w, tm=TOKEN_TILE)
    om = _memattn(qm, mem_k, mem_v, batch=bp, seq=seq, tm=2 * TOKEN_TILE)
    y_prompt = _ffn(x1, om, w, tm=TOKEN_TILE).reshape(bp, seq, D_MODEL)

    xs = x_sample.reshape(bs, D_MODEL)
    tables_s = tuple(jnp.broadcast_to(t, (bs, LANES)) for t in _rope_tables(past_len, 1))
    qkvm_s, og_s, ckv_s, sg_s, tail_s, qabs = _inproj(xs, w, tables_s, tm=bs, absorbed=True, mdtype=F32)
    m0_pad = jnp.pad(state_m.reshape(bs, M_HEADS).astype(F32), ((0, 0), (0, LANES - M_HEADS)))
    hg_s, c_s, n_s, m_s = _mlstm_step(qkvm_s, tail_s, state_C.reshape(bs, M_HEADS, M_HEAD_DIM, M_HEAD_DIM),
                                      state_n.reshape(bs, M_HEADS, M_HEAD_DIM), m0_pad, og_s, w["g_mhead"])
    n_phys = cache_ckv.shape[1]
    cache_krope_t = jnp.transpose(cache_krope.reshape(n_phys, PAGE_SIZE, QK_ROPE), (0, 2, 1))
    o_lat = _mla_decode(page_table, qabs, cache_ckv.reshape(n_phys, PAGE_SIZE, KV_LORA), cache_krope_t, ckv_s, tail_s)
    x1_s, qm_s = _merge(xs, hg_s, o_lat.reshape(bs, A_HEADS * KV_LORA), sg_s, w, tm=bs, from_latent=True, qdtype=F32)
    om_s = _memattn_step(qm_s, cache_mem_k.reshape(bs, N_MEM // 2, 2 * X_HEADS, X_HEAD_DIM),
                         cache_mem_v.reshape(bs, N_MEM // 2, 2 * X_HEADS, X_HEAD_DIM))
    y_sample = _ffn(x1_s, om_s, w, tm=bs).reshape(bs, 1, D_MODEL)

    return (y_prompt, y_sample,
            ckv_p.reshape(1, bp, seq, KV_LORA), jnp.transpose(krope_t, (0, 2, 1)).reshape(1, bp, seq, QK_ROPE),
            c_p.reshape(1, bp, M_HEADS, M_HEAD_DIM, M_HEAD_DIM), n_p.reshape(1, bp, M_HEADS, M_HEAD_DIM),
            m_p[:, :, 0].reshape(1, bp, M_HEADS),
            mem_k.reshape(1, bp, N_MEM, X_HEADS, X_HEAD_DIM), mem_v.reshape(1, bp, N_MEM, X_HEADS, X_HEAD_DIM),
            ckv_s.reshape(1, bs, 1, KV_LORA), tail_s[:, :QK_ROPE].reshape(1, bs, 1, QK_ROPE),
            c_s.reshape(1, bs, M_HEADS, M_HEAD_DIM, M_HEAD_DIM), n_s.reshape(1, bs, M_HEADS, M_HEAD_DIM),
            m_s[:, :M_HEADS].reshape(1, bs, M_HEADS))
```

```python
import functools

import jax
import jax.numpy as jnp
import numpy as np
from jax import lax
from jax.experimental import pallas as pl
from jax.experimental.pallas import tpu as pltpu

F32 = jnp.float32
BF16 = jnp.bfloat16

D_MODEL = 1024
PAGE_SIZE = 128
M_HEADS = 4
M_HEAD_DIM = 128
M_WIDTH = M_HEADS * M_HEAD_DIM
M_CHUNK = 128
A_HEADS = 8
QK_NOPE = 64
QK_ROPE = 32
V_HEAD = 64
Q_LORA = 384
KV_LORA = 256
ROPE_BASE = 10000.0
N_MEM = 256
X_HEADS = 4
X_HEAD_DIM = 128
X_WIDTH = X_HEADS * X_HEAD_DIM
D_FF = 2816
EPS = 1e-6

LANES = 128
HEAD_SLOT = 128
ABS_SLOT = 384
ATT_SCALE = (QK_NOPE + QK_ROPE) ** -0.5

V7X_VMEM_BYTES = 64 * 1024 * 1024
VMEM_LIMIT = V7X_VMEM_BYTES * 13 // 16
TOKEN_TILE = 512
ATTN_BLOCK = 512
ATTN_STRIP = 32
MLSTM_SEQS = 8
GATE_SEQS = 4
STEP_SEQS = 16
DECODE_SEQS = 4
DECODE_CHUNK_PAGES = 32
DECODE_AHEAD = 2

TAIL_IG = 64
TAIL_LF = 68


def _rms(x, g):
    return x * lax.rsqrt(jnp.mean(x * x, axis=-1, keepdims=True) + EPS) * g


def _dot(a, b):
    return jnp.dot(a.astype(BF16), b.astype(BF16), preferred_element_type=F32)


def _dot_nt(a, b):
    return lax.dot_general(a.astype(BF16), b.astype(BF16), (((1,), (1,)), ((), ())),
                           preferred_element_type=F32)


def _log_sigmoid(x):
    return jnp.minimum(x, 0.0) - jnp.log1p(jnp.exp(-jnp.abs(x)))


def _const_spec(shape):
    nd = len(shape)
    return pl.BlockSpec(shape, lambda *_: (0,) * nd, pipeline_mode=pl.Buffered(1))


def _params(n_axes):
    return pltpu.CompilerParams(dimension_semantics=("arbitrary",) * n_axes, vmem_limit_bytes=VMEM_LIMIT)


def _inproj_kernel(x_ref, gpre_ref, wqkv_ref, wmo_ref, wlat_ref, wg_ref, wtail_ref, btail_ref, ct_ref, st_ref,
                   gq_ref, wq_ref, c128_ref, s128_ref, gkv_ref, *rest, absorbed):
    if absorbed:
        wabs_ref, qkvm_ref, og_ref, ckv_ref, sg_ref, tail_ref, qabs_ref = rest
    else:
        (wuk_ref, wuv_ref, wvt_ref,
         qkvm_ref, og_ref, ckv_ref, sg_ref, krt_ref, gt_ref, qcat_ref, kcat_ref, vcat_ref, vt_ref) = rest
    h = _rms(x_ref[...], gpre_ref[...]).astype(BF16)

    zm = jnp.dot(h, wqkv_ref[...], preferred_element_type=F32)
    qkvm_ref[:, 0:M_WIDTH] = zm[:, 0:M_WIDTH].astype(qkvm_ref.dtype)
    qkvm_ref[:, M_WIDTH:2 * M_WIDTH] = (zm[:, M_WIDTH:2 * M_WIDTH] * (M_HEAD_DIM ** -0.5)).astype(qkvm_ref.dtype)
    qkvm_ref[:, 2 * M_WIDTH:] = zm[:, 2 * M_WIDTH:].astype(qkvm_ref.dtype)
    og_ref[...] = jax.nn.sigmoid(jnp.dot(h, wmo_ref[...], preferred_element_type=F32))
    sg_ref[...] = jax.nn.sigmoid(jnp.dot(h, wg_ref[...], preferred_element_type=F32))

    t = jnp.dot(h, wtail_ref[...], preferred_element_type=F32) + btail_ref[...]
    t = t * ct_ref[...] + pltpu.roll(t, LANES - QK_ROPE, 1) * st_ref[...]
    lane = lax.broadcasted_iota(jnp.int32, t.shape, 1)
    is_lf = jnp.logical_and(lane >= TAIL_LF, lane < TAIL_LF + M_HEADS)
    t = jnp.where(is_lf, _log_sigmoid(t), t)
    if absorbed:
        tail_ref[...] = t
    else:
        tt = jnp.transpose(t)
        krt_ref[0] = tt[0:QK_ROPE]
        gt_ref[...] = tt[TAIL_IG:TAIL_IG + 2 * M_HEADS]

    cqn = _rms(jnp.dot(h, wlat_ref[:, 0:Q_LORA], preferred_element_type=F32), gq_ref[...]).astype(BF16)
    ckvn = _rms(jnp.dot(h, wlat_ref[:, Q_LORA:], preferred_element_type=F32), gkv_ref[...])
    ckv_ref[...] = ckvn

    qc = jnp.dot(cqn, wq_ref[...], preferred_element_type=F32)
    width = A_HEADS * HEAD_SLOT
    first_half = (lax.broadcasted_iota(jnp.int32, qc.shape, 1) % HEAD_SLOT) < QK_NOPE + QK_ROPE // 2
    qs = jnp.where(first_half, pltpu.roll(qc, width - QK_ROPE // 2, 1), pltpu.roll(qc, QK_ROPE // 2, 1))
    c128 = c128_ref[...]
    s128 = s128_ref[...]
    for hh in range(A_HEADS):
        sl = slice(HEAD_SLOT * hh, HEAD_SLOT * (hh + 1))
        qh = ((qc[:, sl] * c128 + qs[:, sl] * s128) * ATT_SCALE).astype(BF16)
        if absorbed:
            qabs_ref[:, ABS_SLOT * hh:ABS_SLOT * (hh + 1)] = jnp.dot(
                qh, wabs_ref[hh], preferred_element_type=F32).astype(BF16)
        else:
            qcat_ref[:, sl] = qh
    if not absorbed:
        cb = ckvn.astype(BF16)
        kr_slot = jnp.where(jnp.logical_and(lane >= QK_NOPE, lane < QK_NOPE + QK_ROPE), pltpu.roll(t, QK_NOPE, 1), 0.0)
        kn = jnp.dot(cb, wuk_ref[...], preferred_element_type=F32)
        for hh in range(A_HEADS):
            sl = slice(HEAD_SLOT * hh, HEAD_SLOT * (hh + 1))
            kcat_ref[:, sl] = (kn[:, sl] + kr_slot).astype(BF16)
        vlane = lax.broadcasted_iota(jnp.int32, (1, A_HEADS * HEAD_SLOT), 1) % HEAD_SLOT
        vcat = jnp.dot(cb, wuv_ref[...], preferred_element_type=F32)
        vcat_ref[...] = jnp.where(vlane == V_HEAD, 1.0, vcat).astype(BF16)
        vt_ref[...] = _dot_nt(wvt_ref[...], h).astype(BF16)


def _inproj(x, w, tables, *, tm, absorbed, mdtype):
    m = x.shape[0]
    ct, st, c128, s128 = tables
    seq = ct.shape[0]
    nt = seq // tm
    grid = (m // tm,)
    row = lambda i: (i, 0)
    tab = lambda i: (i % nt, 0)
    wide = A_HEADS * HEAD_SLOT
    in_specs = [
        pl.BlockSpec((tm, D_MODEL), row), _const_spec((1, D_MODEL)),
        _const_spec((D_MODEL, 3 * M_WIDTH)), _const_spec((D_MODEL, M_WIDTH)), _const_spec((D_MODEL, Q_LORA + KV_LORA)),
        _const_spec((D_MODEL, 2 * D_MODEL)), _const_spec((D_MODEL, LANES)), _const_spec((1, LANES)),
        pl.BlockSpec((tm, LANES), tab), pl.BlockSpec((tm, LANES), tab),
        _const_spec((1, Q_LORA)), _const_spec((Q_LORA, wide)),
        pl.BlockSpec((tm, LANES), tab), pl.BlockSpec((tm, LANES), tab),
        _const_spec((1, KV_LORA)),
    ]
    args = [x, w["g_pre_mix"], w["w_qkv"], w["w_mo_gate"], w["w_lat"], w["w_g"], w["w_tail"], w["b_tail"], ct, st,
            w["g_q_lora"], w["wq_cat"], c128, s128, w["g_kv_lora"]]
    outs = [
        (jax.ShapeDtypeStruct((m, 3 * M_WIDTH), mdtype), pl.BlockSpec((tm, 3 * M_WIDTH), row)),
        (jax.ShapeDtypeStruct((m, M_WIDTH), F32), pl.BlockSpec((tm, M_WIDTH), row)),
        (jax.ShapeDtypeStruct((m, KV_LORA), F32), pl.BlockSpec((tm, KV_LORA), row)),
        (jax.ShapeDtypeStruct((m, 2 * D_MODEL), F32), pl.BlockSpec((tm, 2 * D_MODEL), row)),
    ]
    if absorbed:
        in_specs += [_const_spec((A_HEADS, HEAD_SLOT, ABS_SLOT))]
        args += [w["w_abs"]]
        outs += [(jax.ShapeDtypeStruct((m, LANES), F32), pl.BlockSpec((tm, LANES), row)),
                 (jax.ShapeDtypeStruct((m, A_HEADS * ABS_SLOT), BF16), pl.BlockSpec((tm, A_HEADS * ABS_SLOT), row))]
    else:
        in_specs += [_const_spec((KV_LORA, wide)), _const_spec((KV_LORA, wide)), _const_spec((M_WIDTH, D_MODEL))]
        args += [w["wuk_cat"], w["wuv_cat"], w["wv_t"]]
        outs += [(jax.ShapeDtypeStruct((m // seq, QK_ROPE, seq), F32),
                  pl.BlockSpec((1, QK_ROPE, tm), lambda i: (i // nt, 0, i % nt))),
                 (jax.ShapeDtypeStruct((2 * M_HEADS, m), F32), pl.BlockSpec((2 * M_HEADS, tm), lambda i: (0, i)))]
        outs += [(jax.ShapeDtypeStruct((m, wide), BF16), pl.BlockSpec((tm, wide), row))] * 3
        outs += [(jax.ShapeDtypeStruct((M_WIDTH, m), BF16), pl.BlockSpec((M_WIDTH, tm), lambda i: (0, i)))]
    return pl.pallas_call(
        functools.partial(_inproj_kernel, absorbed=absorbed),
        grid=grid, in_specs=in_specs,
        out_specs=[o[1] for o in outs], out_shape=[o[0] for o in outs],
        compiler_params=_params(1), name="inproj_abs" if absorbed else "inproj",
    )(*args)


def _split3_dot(x, m01):
    hi = x.astype(BF16)
    r1 = x - hi.astype(F32)
    mid = r1.astype(BF16)
    lo = (r1 - mid.astype(F32)).astype(BF16)
    return sum(jnp.dot(part, m01, preferred_element_type=F32) for part in (hi, mid, lo))


def _cumsum_lanes(x, upper):
    return _split3_dot(x, jnp.where(upper, 1.0, 0.0).astype(BF16))


def _cummax_lanes(x):
    lane = lax.broadcasted_iota(jnp.int32, x.shape, 1)
    shift = 1
    while shift < x.shape[1]:
        x = jnp.maximum(x, jnp.where(lane >= shift, pltpu.roll(x, shift, 1), -jnp.inf))
        shift *= 2
    return x


def _mlstm_gates_kernel(g_ref, urow_ref, scal_ref, cols_ref, *, nc, nseq):
    L = M_CHUNK
    H = M_HEADS
    rows = nseq * nc * 2 * H
    g = g_ref[...]
    r = jnp.concatenate([g[:, L * c:L * (c + 1)] for c in range(nseq * nc)], axis=0)
    ri = lax.broadcasted_iota(jnp.int32, (rows, L), 0)
    ci = lax.broadcasted_iota(jnp.int32, (rows, L), 1)
    top = (ri % (2 * H)) < H
    b_all = _cumsum_lanes(r, lax.broadcasted_iota(jnp.int32, (L, L), 0) <= lax.broadcasted_iota(jnp.int32, (L, L), 1))
    b = pltpu.roll(b_all, rows - H, 0)
    u = r - b
    cm = _cummax_lanes(u)
    g_last = jnp.max(jnp.where(ci == L - 1, b, -jnp.inf), axis=1, keepdims=True)
    wlog = g_last - b + r
    wmax = jnp.max(wlog, axis=1, keepdims=True)
    urow_ref[...] = jnp.where(top, u, pltpu.roll(wlog, H, 0)).reshape(nseq, nc, 2 * H, L)
    scal = jnp.where(top, jnp.broadcast_to(g_last, (rows, L)), pltpu.roll(jnp.broadcast_to(wmax, (rows, L)), H, 0))
    scal_ref[...] = scal.reshape(nseq, nc, 2 * H, L)
    first = jnp.where(top, b, pltpu.roll(cm, H, 0))
    for c in range(nseq * nc):
        sl = slice(2 * H * c, 2 * H * (c + 1))
        cols_ref[c // nc, c % nc] = jnp.transpose(jnp.concatenate([first[sl], wlog[sl]], axis=0))


def _mlstm_gates(g, *, batch, nc, nseq=GATE_SEQS):
    blk = lambda b: (b, 0, 0, 0)
    return pl.pallas_call(
        functools.partial(_mlstm_gates_kernel, nc=nc, nseq=nseq),
        grid=(batch // nseq,),
        in_specs=[pl.BlockSpec((2 * M_HEADS, nseq * nc * M_CHUNK), lambda b: (0, b))],
        out_specs=[pl.BlockSpec((nseq, nc, 2 * M_HEADS, M_CHUNK), blk),
                   pl.BlockSpec((nseq, nc, 2 * M_HEADS, M_CHUNK), blk),
                   pl.BlockSpec((nseq, nc, M_CHUNK, 4 * M_HEADS), blk)],
        out_shape=[jax.ShapeDtypeStruct((batch, nc, 2 * M_HEADS, M_CHUNK), F32),
                   jax.ShapeDtypeStruct((batch, nc, 2 * M_HEADS, M_CHUNK), F32),
                   jax.ShapeDtypeStruct((batch, nc, M_CHUNK, 4 * M_HEADS), F32)],
        compiler_params=_params(1), name="mlstm_gates",
    )(g)


def _mlstm_kernel(*refs, nb):
    qkv_ref = refs[0]
    vt_refs = refs[1:1 + nb]
    urow_ref, scal_ref, cols_ref, sel_ref, og_ref, gm_ref, hg_ref, c_ref, n_ref, m_ref = refs[1 + nb:]
    L = M_CHUNK

    @pl.when(pl.program_id(1) == 0)
    def _():
        c_ref[...] = jnp.zeros_like(c_ref)
        n_ref[...] = jnp.zeros_like(n_ref)
        m_ref[...] = jnp.zeros_like(m_ref)

    row = lax.broadcasted_iota(jnp.int32, (L, L), 0)
    col = lax.broadcasted_iota(jnp.int32, (L, L), 1)
    tril = col <= row

    for bb in range(nb):
        vt_ref = vt_refs[bb]
        rows8 = urow_ref[bb, 0]
        u4 = rows8[0:M_HEADS]
        wlog4 = rows8[M_HEADS:2 * M_HEADS]
        scal = scal_ref[bb, 0]
        g_last4 = scal[0:M_HEADS, 0:1]
        wmax4 = scal[M_HEADS:2 * M_HEADS, 0:1]
        cols = _split3_dot(cols_ref[bb, 0], sel_ref[...])
        m_prev4 = m_ref[bb][:, 0:1]
        n_prev4 = n_ref[bb]
        m_new4 = jnp.maximum(g_last4 + m_prev4, wmax4)
        decay4 = jnp.exp(g_last4 + m_prev4 - m_new4)
        ws4 = jnp.exp(wlog4 - m_new4)
        m_ref[bb] = jnp.broadcast_to(m_new4, (M_HEADS, LANES))
        n_rows = []

        for h in range(M_HEADS):
            sl = slice(M_HEAD_DIM * h, M_HEAD_DIM * (h + 1))
            q = qkv_ref[bb, :, sl]
            k = qkv_ref[bb, :, M_WIDTH + M_HEAD_DIM * h:M_WIDTH + M_HEAD_DIM * (h + 1)]
            v = qkv_ref[bb, :, 2 * M_WIDTH + M_HEAD_DIM * h:2 * M_WIDTH + M_HEAD_DIM * (h + 1)]
            m_prev = m_prev4[h:h + 1]
            c_prev = c_ref[bb, h]
            n_prev = n_prev4[h:h + 1]
            b_col = cols[:, LANES * h:LANES * (h + 1)]
            mm_col = jnp.maximum(cols[:, LANES * (M_HEADS + h):LANES * (M_HEADS + h + 1)], m_prev)
            w_intra = jnp.where(tril, jnp.exp(u4[h:h + 1] - mm_col), 0.0)
            w_inter = jnp.exp(m_prev - mm_col)
            s = _dot_nt(q, k) * w_intra
            cn = jnp.concatenate([c_prev.astype(BF16), jnp.broadcast_to(n_prev.astype(BF16), (L, M_HEAD_DIM))],
                                 axis=0)
            qcn = _dot_nt(q, cn)
            num = w_inter * qcn[:, 0:M_HEAD_DIM] + _dot(s, v)
            den = w_inter * qcn[:, M_HEAD_DIM:] + jnp.sum(s, axis=1, keepdims=True)
            hs = num / jnp.maximum(jnp.abs(den), jnp.exp(-(b_col + mm_col)))

            ws = ws4[h:h + 1]
            lhs = jnp.concatenate([(vt_ref[sl, :].astype(F32) * ws).astype(BF16),
                                   jnp.broadcast_to(ws.astype(BF16), (16, L))], axis=0)
            upd = jnp.dot(lhs, k, preferred_element_type=F32)
            decay = decay4[h:h + 1]
            c_ref[bb, h] = decay * c_prev + upd[0:M_HEAD_DIM]
            n_rows.append(decay * n_prev + upd[M_HEAD_DIM:M_HEAD_DIM + 1])

            mu = jnp.mean(hs, axis=1, keepdims=True)
            d = hs - mu
            y = d * lax.rsqrt(jnp.mean(d * d, axis=1, keepdims=True) + EPS) * gm_ref[:, sl]
            hg_ref[bb, :, sl] = (y * og_ref[bb, :, sl]).astype(BF16)
        n_ref[bb] = jnp.concatenate(n_rows, axis=0)


def _mlstm_prompt(qkvm, vt, gates, og, g_mhead, *, batch, seq, nb=MLSTM_SEQS):
    nc = seq // M_CHUNK
    tok = lambda b, c: (b, c, 0)
    chunk = lambda b, c: (b, c, 0, 0)
    urow, scal, cols = _mlstm_gates(gates, batch=batch, nc=nc)
    sel = jnp.asarray(np.kron(np.eye(4 * M_HEADS, 2 * M_HEADS), np.ones((1, LANES))), BF16)
    vt_specs = [pl.BlockSpec((M_WIDTH, M_CHUNK), functools.partial(lambda b, c, bb: (0, (nb * b + bb) * nc + c), bb=bb))
                for bb in range(nb)]
    hg, c_p, n_p, m_p = pl.pallas_call(
        functools.partial(_mlstm_kernel, nb=nb),
        grid=(batch // nb, nc),
        in_specs=[pl.BlockSpec((nb, M_CHUNK, 3 * M_WIDTH), tok)] + vt_specs + [
            pl.BlockSpec((nb, 1, 2 * M_HEADS, M_CHUNK), chunk),
            pl.BlockSpec((nb, 1, 2 * M_HEADS, M_CHUNK), chunk),
            pl.BlockSpec((nb, 1, M_CHUNK, 4 * M_HEADS), chunk),
            pl.BlockSpec((4 * M_HEADS, 2 * M_HEADS * LANES), lambda b, c: (0, 0)),
            pl.BlockSpec((nb, M_CHUNK, M_WIDTH), tok),
            pl.BlockSpec((1, M_WIDTH), lambda b, c: (0, 0)),
        ],
        out_specs=[
            pl.BlockSpec((nb, M_CHUNK, M_WIDTH), tok),
            pl.BlockSpec((nb, M_HEADS, M_HEAD_DIM, M_HEAD_DIM), lambda b, c: (b, 0, 0, 0)),
            pl.BlockSpec((nb, M_HEADS, M_HEAD_DIM), lambda b, c: (b, 0, 0)),
            pl.BlockSpec((nb, M_HEADS, LANES), lambda b, c: (b, 0, 0)),
        ],
        out_shape=[
            jax.ShapeDtypeStruct((batch, seq, M_WIDTH), BF16),
            jax.ShapeDtypeStruct((batch, M_HEADS, M_HEAD_DIM, M_HEAD_DIM), F32),
            jax.ShapeDtypeStruct((batch, M_HEADS, M_HEAD_DIM), F32),
            jax.ShapeDtypeStruct((batch, M_HEADS, LANES), F32),
        ],
        compiler_params=_params(2), name="mlstm_prompt",
    )(qkvm.reshape(batch, seq, 3 * M_WIDTH), *([vt] * nb), urow, scal, cols, sel,
      og.reshape(batch, seq, M_WIDTH), g_mhead)
    return hg.reshape(batch * seq, M_WIDTH), c_p, n_p, m_p


def _mlstm_step_kernel(qkv_ref, tail_ref, c_ref, n_ref, m_ref, og_ref, gm_ref, hg_ref, co_ref, no_ref, mo_ref, *, rows):
    D = M_HEAD_DIM
    lane = lax.broadcasted_iota(jnp.int32, (rows, LANES), 1)
    rowi = lax.broadcasted_iota(jnp.int32, (rows, D), 0)
    tail = tail_ref[...]
    m_in = m_ref[...]
    m_out = jnp.zeros((rows, LANES), F32)
    for h in range(M_HEADS):
        sl = slice(D * h, D * (h + 1))
        q = qkv_ref[:, sl]
        k = qkv_ref[:, M_WIDTH + D * h:M_WIDTH + D * (h + 1)]
        v = qkv_ref[:, 2 * M_WIDTH + D * h:2 * M_WIDTH + D * (h + 1)]
        ig = tail[:, TAIL_IG + h:TAIL_IG + h + 1]
        lf = tail[:, TAIL_LF + h:TAIL_LF + h + 1]
        m_prev = m_in[:, h:h + 1]
        n_prev = n_ref[:, h, :]
        a = lf + m_prev
        mt = jnp.maximum(a, ig)
        w_in = jnp.exp(ig - mt)
        w_st = jnp.exp(a - mt)
        s = jnp.sum(q * k, axis=1, keepdims=True) * w_in
        cq = jnp.zeros((rows, D), F32)
        for g in range(rows):
            cq = jnp.where(rowi == g, _dot_nt(q, c_ref[g, h]), cq)
        num = w_st * cq + s * v
        den = w_st * jnp.sum(n_prev * q, axis=1, keepdims=True) + s
        hs = num / jnp.maximum(jnp.abs(den), jnp.exp(-mt))
        vw_t = jnp.transpose(v * w_in)
        for g in range(rows):
            co_ref[g, h] = w_st[g:g + 1] * c_ref[g, h] + vw_t[:, g:g + 1] * k[g:g + 1, :]
        no_ref[:, h, :] = w_st * n_prev + w_in * k
        m_out = jnp.where(lane == h, mt, m_out)
        mu = jnp.mean(hs, axis=1, keepdims=True)
        d = hs - mu
        y = d * lax.rsqrt(jnp.mean(d * d, axis=1, keepdims=True) + EPS) * gm_ref[:, sl]
        hg_ref[:, sl] = (y * og_ref[:, sl]).astype(BF16)
    mo_ref[...] = m_out


def _mlstm_step(qkvm, tail, c0, n0, m0_pad, og, g_mhead, *, rows=STEP_SEQS):
    nb = qkvm.shape[0]
    row = lambda i: (i, 0)
    return pl.pallas_call(
        functools.partial(_mlstm_step_kernel, rows=rows),
        grid=(nb // rows,),
        in_specs=[
            pl.BlockSpec((rows, 3 * M_WIDTH), row),
            pl.BlockSpec((rows, LANES), row),
            pl.BlockSpec((rows, M_HEADS, M_HEAD_DIM, M_HEAD_DIM), lambda i: (i, 0, 0, 0)),
            pl.BlockSpec((rows, M_HEADS, M_HEAD_DIM), lambda i: (i, 0, 0)),
            pl.BlockSpec((rows, LANES), row),
            pl.BlockSpec((rows, M_WIDTH), row),
            pl.BlockSpec((1, M_WIDTH), lambda i: (0, 0)),
        ],
        out_specs=[
            pl.BlockSpec((rows, M_WIDTH), row),
            pl.BlockSpec((rows, M_HEADS, M_HEAD_DIM, M_HEAD_DIM), lambda i: (i, 0, 0, 0)),
            pl.BlockSpec((rows, M_HEADS, M_HEAD_DIM), lambda i: (i, 0, 0)),
            pl.BlockSpec((rows, LANES), row),
        ],
        out_shape=[
            jax.ShapeDtypeStruct((nb, M_WIDTH), BF16),
            jax.ShapeDtypeStruct((nb, M_HEADS, M_HEAD_DIM, M_HEAD_DIM), F32),
            jax.ShapeDtypeStruct((nb, M_HEADS, M_HEAD_DIM), F32),
            jax.ShapeDtypeStruct((nb, LANES), F32),
        ],
        compiler_params=_params(1), name="mlstm_step",
    )(qkvm, tail, c0, n0, m0_pad, og, g_mhead)


def _mla_prefill_kernel(q_ref, k_ref, v_ref, o_ref, s_scr, p_scr, m_scr, acc_scr, *, blk, nq, heads, rows):
    reps = blk // LANES
    lane = lax.broadcasted_iota(jnp.int32, (blk, HEAD_SLOT), 1)
    n_done = 0
    for i in range(nq):
        par = i % 2
        q_rows = slice(blk * i, blk * (i + 1))
        m_scr[par] = jnp.full(m_scr.shape[1:], -jnp.inf, F32)
        acc_scr[par] = jnp.zeros(acc_scr.shape[1:], F32)
        for j in range(i + 1):
            masked = j == i
            k_rows = slice(blk * j, blk * (j + 1))
            slot = n_done % 2
            n_done += 1
            for hh in range(heads):
                sl = slice(HEAD_SLOT * hh, HEAD_SLOT * (hh + 1))
                s_scr[slot, hh] = _dot_nt(q_ref[q_rows, sl], k_ref[k_rows, sl])
            for hh in range(heads):
                for r in range(blk // rows):
                    rs = slice(rows * r, rows * (r + 1))
                    s = s_scr[slot, hh, rs, :]
                    if masked:
                        qi = lax.broadcasted_iota(jnp.int32, (rows, blk), 0) + rows * r
                        ki = lax.broadcasted_iota(jnp.int32, (rows, blk), 1)
                        s = jnp.where(ki <= qi, s, -jnp.inf)
                    m_old = m_scr[par, hh, rs, :]
                    m_new = jnp.maximum(m_old, jnp.max(s, axis=1, keepdims=True))
                    p_scr[hh, rs, :] = jnp.exp(s - jnp.concatenate([m_new] * reps, axis=1)).astype(BF16)
                    acc_scr[par, hh, rs, :] = acc_scr[par, hh, rs, :] * jnp.exp(m_old - m_new)
                    m_scr[par, hh, rs, :] = m_new
            for hh in range(heads):
                sl = slice(HEAD_SLOT * hh, HEAD_SLOT * (hh + 1))
                acc_scr[par, hh] += jnp.dot(p_scr[hh], v_ref[k_rows, sl], preferred_element_type=F32)
        o0, o1 = (acc_scr[par, hh] / acc_scr[par, hh][:, V_HEAD:V_HEAD + 1] for hh in range(2))
        o_ref[q_rows, :] = jnp.where(lane < V_HEAD, o0, pltpu.roll(o1, V_HEAD, 1)).astype(BF16)


def _mla_prefill(qcat, kcat, vcat, *, batch, seq, blk=ATTN_BLOCK, rows=ATTN_STRIP):
    heads = 2
    nq = seq // blk
    wide = heads * HEAD_SLOT
    per_seq = lambda b, h: (b, h)
    return pl.pallas_call(
        functools.partial(_mla_prefill_kernel, blk=blk, nq=nq, heads=heads, rows=rows),
        grid=(batch, A_HEADS // heads),
        in_specs=[pl.BlockSpec((seq, wide), per_seq)] * 3,
        out_specs=pl.BlockSpec((seq, heads * V_HEAD), per_seq),
        out_shape=jax.ShapeDtypeStruct((batch * seq, A_HEADS * V_HEAD), BF16),
        scratch_shapes=[pltpu.VMEM((2, heads, blk, blk), F32), pltpu.VMEM((heads, blk, blk), BF16),
                        pltpu.VMEM((2, heads, blk, LANES), F32), pltpu.VMEM((2, heads, blk, HEAD_SLOT), F32)],
        compiler_params=_params(2), name="mla_prefill",
    )(qcat, kcat, vcat)


def _mla_decode_kernel(pt_ref, q_ref, ckv_hbm, kr_hbm, cself_ref, tself_ref, o_ref,
                       cbuf, rbuf, kbuf, sems, *, seqs, n_chunks, chunk_pages, ahead):
    i = pl.program_id(0)
    n_steps = pl.num_programs(0)

    def page_copies(seq, c, p):
        pg = pt_ref[seq, c * chunk_pages + p]
        dst = pl.ds(p * PAGE_SIZE, PAGE_SIZE)
        return (pltpu.make_async_copy(ckv_hbm.at[pg], cbuf.at[c, dst], sems.at[0, c]),
                pltpu.make_async_copy(kr_hbm.at[pg], rbuf.at[c, :, dst], sems.at[1, c]))

    def issue(seq, c):
        for p in range(chunk_pages):
            for cp in page_copies(seq, c, p):
                cp.start()

    def wait(c):
        for p in range(chunk_pages):
            for cp in page_copies(0, c, p):
                cp.wait()

    @pl.when(i == 0)
    def _():
        for c in range(ahead):
            issue(0, c)

    for q in range(seqs):
        seq = i * seqs + q
        qv = q_ref[q]
        q_lat = qv[:, 0:KV_LORA]
        q_rope = qv[:, KV_LORA:KV_LORA + QK_ROPE]

        def scores(c):
            kc = cbuf[c].astype(BF16)
            kbuf[c % 2] = kc
            kr = rbuf[c].astype(BF16)
            return _dot_nt(q_lat, kc) + jnp.dot(q_rope, kr, preferred_element_type=F32)

        def accumulate(carry, s, c):
            m_old, l, acc = carry
            m_new = jnp.maximum(m_old, jnp.max(s, axis=1, keepdims=True))
            p = jnp.exp(s - m_new)
            alpha = jnp.exp(m_old - m_new)
            return (m_new, alpha * l + jnp.sum(p, axis=1, keepdims=True),
                    alpha * acc + jnp.dot(p.astype(BF16), kbuf[c % 2], preferred_element_type=F32))

        carry = (jnp.full((A_HEADS, 1), -jnp.inf, F32), jnp.zeros((A_HEADS, 1), F32),
                 jnp.zeros((A_HEADS, KV_LORA), F32))
        s_prev = None
        for c in range(n_chunks):
            nxt = c + ahead
            if nxt < n_chunks:
                issue(seq, nxt)
            elif q + 1 < seqs:
                issue(seq + 1, nxt - n_chunks)
            else:
                @pl.when(i + 1 < n_steps)
                def _():
                    issue(seq + 1, nxt - n_chunks)
            wait(c)
            s_cur = scores(c)
            if c > 0:
                carry = accumulate(carry, s_prev, c - 1)
            s_prev = s_cur
        m_old, l, acc = accumulate(carry, s_prev, n_chunks - 1)

        c_self = cself_ref[q].astype(BF16).astype(F32)
        r_self = tself_ref[q][:, 0:QK_ROPE].astype(BF16).astype(F32)
        s_self = (jnp.sum(q_lat.astype(F32) * c_self, axis=1, keepdims=True)
                  + jnp.sum(q_rope.astype(F32) * r_self, axis=1, keepdims=True))
        m_new = jnp.maximum(m_old, s_self)
        p_self = jnp.exp(s_self - m_new)
        alpha = jnp.exp(m_old - m_new)
        l = alpha * l + p_self
        acc = alpha * acc + p_self.astype(BF16).astype(F32) * c_self
        o_ref[q] = acc / l


def _mla_decode(page_table, qabs, cache_ckv, cache_krope, ckv_self, tail_self, *,
                seqs=DECODE_SEQS, chunk_pages=DECODE_CHUNK_PAGES, ahead=DECODE_AHEAD):
    nb, n_pages = page_table.shape
    n_chunks = n_pages // chunk_pages
    assert n_chunks * chunk_pages == n_pages and ahead + 2 <= n_chunks and nb % seqs == 0
    keys = chunk_pages * PAGE_SIZE
    per_seq = lambda i, pt: (i, 0, 0)
    grid_spec = pltpu.PrefetchScalarGridSpec(
        num_scalar_prefetch=1,
        grid=(nb // seqs,),
        in_specs=[
            pl.BlockSpec((seqs, A_HEADS, ABS_SLOT), per_seq),
            pl.BlockSpec(memory_space=pl.ANY),
            pl.BlockSpec(memory_space=pl.ANY),
            pl.BlockSpec((seqs, 1, KV_LORA), per_seq),
            pl.BlockSpec((seqs, 1, LANES), per_seq),
        ],
        out_specs=pl.BlockSpec((seqs, A_HEADS, KV_LORA), per_seq),
        scratch_shapes=[
            pltpu.VMEM((n_chunks, keys, KV_LORA), F32),
            pltpu.VMEM((n_chunks, QK_ROPE, keys), F32),
            pltpu.VMEM((2, keys, KV_LORA), BF16),
            pltpu.SemaphoreType.DMA((2, n_chunks)),
        ],
    )
    return pl.pallas_call(
        functools.partial(_mla_decode_kernel, seqs=seqs, n_chunks=n_chunks, chunk_pages=chunk_pages, ahead=ahead),
        grid_spec=grid_spec,
        out_shape=jax.ShapeDtypeStruct((nb, A_HEADS, KV_LORA), F32),
        compiler_params=_params(1), name="mla_decode",
    )(page_table, qabs.reshape(nb, A_HEADS, ABS_SLOT), cache_ckv, cache_krope,
      ckv_self.reshape(nb, 1, KV_LORA), tail_self.reshape(nb, 1, LANES))


def _merge_kernel(x_ref, hg_ref, ob_ref, sg_ref, wa_ref, wb_ref, wout_ref, gpost_ref, gmem_ref, wmq_ref,
                  *rest, from_latent):
    if from_latent:
        wuv_ref, x1_ref, qm_ref = rest
        ob = sum(_dot(ob_ref[:, KV_LORA * hh:KV_LORA * (hh + 1)], wuv_ref[hh]) for hh in range(A_HEADS)).astype(BF16)
    else:
        x1_ref, qm_ref = rest
        ob = ob_ref[...]
    ya = jnp.dot(hg_ref[...], wa_ref[...], preferred_element_type=F32)
    yb = jnp.dot(ob, wb_ref[...], preferred_element_type=F32)
    mix = sg_ref[:, 0:D_MODEL] * ya + sg_ref[:, D_MODEL:] * yb
    y = _dot(mix, wout_ref[...])
    x1 = x_ref[...] + _rms(y, gpost_ref[...])
    x1_ref[...] = x1
    qm = _dot(_rms(x1, gmem_ref[...]), wmq_ref[...]) * (X_HEAD_DIM ** -0.5)
    qm_ref[...] = qm.astype(qm_ref.dtype)


def _merge(x, hg, ob, sg, w, *, tm, from_latent, qdtype):
    m = x.shape[0]
    row = lambda i: (i, 0)
    a_width = A_HEADS * V_HEAD
    ob_w = ob.shape[1]
    in_specs = [
        pl.BlockSpec((tm, D_MODEL), row), pl.BlockSpec((tm, M_WIDTH), row), pl.BlockSpec((tm, ob_w), row),
        pl.BlockSpec((tm, 2 * D_MODEL), row),
        _const_spec((M_WIDTH, D_MODEL)), _const_spec((a_width, D_MODEL)), _const_spec((D_MODEL, D_MODEL)),
        _const_spec((1, D_MODEL)), _const_spec((1, D_MODEL)), _const_spec((D_MODEL, X_WIDTH)),
    ]
    args = [x, hg, ob, sg, w["w_branch_a"], w["w_branch_b"], w["w_out"], w["g_post_mix"], w["g_pre_mem"], w["w_mq"]]
    if from_latent:
        in_specs += [_const_spec((A_HEADS, KV_LORA, a_width))]
        args += [w["wuv_heads"]]
    return pl.pallas_call(
        functools.partial(_merge_kernel, from_latent=from_latent),
        grid=(m // tm,), in_specs=in_specs,
        out_specs=[pl.BlockSpec((tm, D_MODEL), row), pl.BlockSpec((tm, X_WIDTH), row)],
        out_shape=[jax.ShapeDtypeStruct((m, D_MODEL), F32), jax.ShapeDtypeStruct((m, X_WIDTH), qdtype)],
        compiler_params=_params(1), name="merge_lat" if from_latent else "merge",
    )(*args)


def _memkv_kernel(mem_ref, g_ref, wk_ref, wv_ref, k_ref, v_ref):
    mn = _rms(mem_ref[...], g_ref[...]).astype(BF16)
    k_ref[...] = jnp.dot(mn, wk_ref[...], preferred_element_type=F32)
    v_ref[...] = jnp.dot(mn, wv_ref[...], preferred_element_type=F32)


def _memkv(mem, w, *, tm):
    m = mem.shape[0]
    row = lambda i: (i, 0)
    return pl.pallas_call(
        _memkv_kernel, grid=(m // tm,),
        in_specs=[pl.BlockSpec((tm, D_MODEL), row), _const_spec((1, D_MODEL)),
                  _const_spec((D_MODEL, X_WIDTH)), _const_spec((D_MODEL, X_WIDTH))],
        out_specs=[pl.BlockSpec((tm, X_WIDTH), row)] * 2,
        out_shape=[jax.ShapeDtypeStruct((m, X_WIDTH), F32)] * 2,
        compiler_params=_params(1), name="memkv",
    )(mem, w["g_mem"], w["w_mk"], w["w_mv"])


def _memattn_kernel(q_ref, k_ref, v_ref, o_ref):
    kb = k_ref[...].astype(BF16)
    vb = v_ref[...].astype(BF16)
    for h in range(X_HEADS):
        sl = slice(X_HEAD_DIM * h, X_HEAD_DIM * (h + 1))
        s = _dot_nt(q_ref[:, sl], kb[:, sl])
        p = jnp.exp(s - jnp.max(s, axis=1, keepdims=True))
        l = jnp.sum(p, axis=1, keepdims=True)
        o_ref[:, sl] = (jnp.dot(p.astype(BF16), vb[:, sl], preferred_element_type=F32) / l).astype(BF16)


def _memattn(qm, mem_k, mem_v, *, batch, seq, tm):
    nt = seq // tm
    tok = lambda b, i: (b * nt + i, 0)
    kv = lambda b, i: (b, 0)
    return pl.pallas_call(
        _memattn_kernel, grid=(batch, nt),
        in_specs=[pl.BlockSpec((tm, X_WIDTH), tok), pl.BlockSpec((N_MEM, X_WIDTH), kv),
                  pl.BlockSpec((N_MEM, X_WIDTH), kv)],
        out_specs=pl.BlockSpec((tm, X_WIDTH), tok),
        out_shape=jax.ShapeDtypeStruct((batch * seq, X_WIDTH), BF16),
        compiler_params=_params(2), name="memattn",
    )(qm, mem_k, mem_v)


def _memattn_step_kernel(q_ref, k_ref, v_ref, o_ref, *, rows):
    for g in range(rows):
        q4 = [q_ref[g:g + 1, X_HEAD_DIM * h:X_HEAD_DIM * (h + 1)] for h in range(X_HEADS)]
        q8 = jnp.concatenate(q4 + q4, axis=0)
        s = jnp.sum(k_ref[g] * q8[None], axis=2, keepdims=True)
        mx = jnp.max(s, axis=0)
        mx = jnp.maximum(mx, pltpu.roll(mx, X_HEADS, 0))
        p = jnp.exp(s - mx[None])
        l8 = jnp.sum(p, axis=0)
        o8 = jnp.sum(p * v_ref[g], axis=0)
        o_ref[g] = (o8[0:X_HEADS] + o8[X_HEADS:]) / (l8[0:X_HEADS] + l8[X_HEADS:])


def _memattn_step(qm, mem_k, mem_v, *, rows=STEP_SEQS):
    nb = qm.shape[0]
    row = lambda i: (i, 0)
    kv = lambda i: (i, 0, 0, 0)
    kv_block = (rows, N_MEM // 2, 2 * X_HEADS, X_HEAD_DIM)
    return pl.pallas_call(
        functools.partial(_memattn_step_kernel, rows=rows), grid=(nb // rows,),
        in_specs=[pl.BlockSpec((rows, X_WIDTH), row), pl.BlockSpec(kv_block, kv), pl.BlockSpec(kv_block, kv)],
        out_specs=pl.BlockSpec((rows, X_HEADS, X_HEAD_DIM), lambda i: (i, 0, 0)),
        out_shape=jax.ShapeDtypeStruct((nb, X_HEADS, X_HEAD_DIM), F32),
        compiler_params=_params(1), name="memattn_step",
    )(qm, mem_k, mem_v).reshape(nb, X_WIDTH)


def _ffn_kernel(x1_ref, o_ref, wmo_ref, gpm_ref, gpf_ref, win_ref, wdn_ref, gpost_ref, y_ref, *, f_chunk):
    x2 = x1_ref[...] + _rms(_dot(o_ref[...], wmo_ref[...]), gpm_ref[...])
    h = _rms(x2, gpf_ref[...]).astype(BF16)
    acc = jnp.zeros(x2.shape, F32)
    for c in range(D_FF // f_chunk):
        sl = slice(f_chunk * c, f_chunk * (c + 1))
        g = jnp.dot(h, win_ref[:, sl], preferred_element_type=F32)
        u = jnp.dot(h, win_ref[:, D_FF + f_chunk * c:D_FF + f_chunk * (c + 1)], preferred_element_type=F32)
        acc = acc + jnp.dot((g * jax.nn.sigmoid(g) * u).astype(BF16), wdn_ref[sl, :], preferred_element_type=F32)
    y_ref[...] = x2 + _rms(acc, gpost_ref[...])


def _ffn(x1, o, w, *, tm, f_chunk=D_FF):
    m = x1.shape[0]
    row = lambda i: (i, 0)
    return pl.pallas_call(
        functools.partial(_ffn_kernel, f_chunk=f_chunk), grid=(m // tm,),
        in_specs=[pl.BlockSpec((tm, D_MODEL), row), pl.BlockSpec((tm, X_WIDTH), row),
                  _const_spec((X_WIDTH, D_MODEL)), _const_spec((1, D_MODEL)), _const_spec((1, D_MODEL)),
                  _const_spec((D_MODEL, 2 * D_FF)), _const_spec((D_FF, D_MODEL)),
                  _const_spec((1, D_MODEL))],
        out_specs=pl.BlockSpec((tm, D_MODEL), row),
        out_shape=jax.ShapeDtypeStruct((m, D_MODEL), F32),
        compiler_params=_params(1), name="ffn",
    )(x1, o, w["w_mo"], w["g_post_mem"], w["g_pre_ffn"], w["w_ffn_in"], w["w_ffn_out"], w["g_post_ffn"])


def _prep_weights(p):
    w = {}
    for name in ("g_pre_mix", "g_q_lora", "g_kv_lora", "g_mhead", "g_post_mix", "g_pre_mem", "g_mem", "g_post_mem",
                 "g_pre_ffn", "g_post_ffn"):
        w[name] = p[name].reshape(1, -1).astype(F32)
    w_in = p["w_in"]
    offs = np.cumsum((0, M_WIDTH, M_WIDTH, M_WIDTH, M_HEADS, M_HEADS, M_WIDTH, Q_LORA, KV_LORA, QK_ROPE,
                      D_MODEL, D_MODEL))
    seg = lambda i: w_in[:, offs[i]:offs[i + 1]]
    mq, mk, mv, mi, mf, mo, cq, ckv, kr, ga, gb = (seg(i) for i in range(11))
    half = QK_ROPE // 2
    swap = lambda a: jnp.concatenate([a[..., half:], a[..., :half]], axis=-1)
    tail = jnp.concatenate([kr, swap(kr), mi, mf,
                            jnp.zeros((D_MODEL, LANES - 2 * QK_ROPE - 2 * M_HEADS), w_in.dtype)], axis=1)
    w["w_qkv"] = w_in[:, offs[0]:offs[3]].astype(BF16)
    w["w_mo_gate"] = mo.astype(BF16)
    w["w_lat"] = w_in[:, offs[6]:offs[8]].astype(BF16)
    w["w_g"] = w_in[:, offs[9]:offs[11]].astype(BF16)
    w["w_tail"] = tail.astype(BF16)
    w["wv_t"] = mv.T.astype(BF16)
    w["b_tail"] = jnp.zeros((1, LANES), F32).at[0, TAIL_IG:TAIL_IG + 2 * M_HEADS].set(p["b_if"].astype(F32))

    pad = HEAD_SLOT - QK_NOPE - QK_ROPE
    wq = p["w_uq"].reshape(Q_LORA, A_HEADS, QK_NOPE + QK_ROPE)
    zq = jnp.zeros((Q_LORA, A_HEADS, pad), wq.dtype)
    w["wq_cat"] = jnp.concatenate([wq, zq], axis=2).reshape(Q_LORA, A_HEADS * HEAD_SLOT).astype(BF16)
    w_uk, w_uv = p["w_uk"], p["w_uv"]
    zk = jnp.zeros((KV_LORA, A_HEADS, HEAD_SLOT - QK_NOPE), w_uk.dtype)
    w["wuk_cat"] = jnp.concatenate([w_uk, zk], axis=2).reshape(KV_LORA, A_HEADS * HEAD_SLOT).astype(BF16)
    w["wuv_cat"] = jnp.concatenate([w_uv, zk], axis=2).reshape(KV_LORA, A_HEADS * HEAD_SLOT).astype(BF16)
    w["wuv_heads"] = (w_uv.transpose(1, 0, 2)[:, :, None, :]
                      * jnp.eye(A_HEADS, dtype=w_uv.dtype)[:, None, :, None]).reshape(
                          A_HEADS, KV_LORA, A_HEADS * V_HEAD).astype(BF16)
    pass_np = np.zeros((A_HEADS, HEAD_SLOT - QK_NOPE, ABS_SLOT), np.float32)
    pass_np[:, np.arange(QK_ROPE), KV_LORA + np.arange(QK_ROPE)] = 1.0
    w["w_abs"] = jnp.concatenate(
        [jnp.concatenate([w_uk.transpose(1, 2, 0), jnp.zeros((A_HEADS, QK_NOPE, ABS_SLOT - KV_LORA), w_uk.dtype)],
                         axis=2).astype(BF16), jnp.asarray(pass_np, BF16)], axis=1)
    for name in ("w_branch_a", "w_branch_b", "w_out", "w_mq", "w_mk", "w_mv", "w_mo", "w_ffn_out"):
        w[name] = p[name].astype(BF16)
    w["w_ffn_in"] = p["w_ffn_in"].astype(BF16)
    return w


def _rope_tables(pos0, n):
    pos = (pos0 + jnp.arange(n)).astype(F32)
    inv = ROPE_BASE ** (-jnp.arange(0, QK_ROPE, 2, dtype=F32) / QK_ROPE)
    ang = pos[:, None] * inv[None, :]
    cos, sin = jnp.cos(ang), jnp.sin(ang)
    c32 = jnp.concatenate([cos, cos], axis=1)
    s32 = jnp.concatenate([-sin, sin], axis=1)
    one = lambda k: jnp.ones((n, k), F32)
    zero = lambda k: jnp.zeros((n, k), F32)
    ct = jnp.concatenate([c32, one(LANES - QK_ROPE)], axis=1)
    st = jnp.concatenate([s32, zero(LANES - QK_ROPE)], axis=1)
    c128 = jnp.concatenate([one(QK_NOPE), c32, one(HEAD_SLOT - QK_NOPE - QK_ROPE)], axis=1)
    s128 = jnp.concatenate([zero(QK_NOPE), s32, zero(HEAD_SLOT - QK_NOPE - QK_ROPE)], axis=1)
    return ct, st, c128, s128


def kernel(x_prompt, x_sample, cache_ckv, cache_krope, cache_mem_k, cache_mem_v, state_C, state_n, state_m, page_table, mem_prompt, g_pre_mix, w_in, b_if, g_mhead, g_q_lora, w_uq, g_kv_lora, w_uk, w_uv, w_branch_a, w_branch_b, w_out, g_post_mix, g_pre_mem, g_mem, w_mq, w_mk, w_mv, w_mo, g_post_mem, g_pre_ffn, w_ffn_in, w_ffn_out, g_post_ffn):
    params = dict(g_pre_mix=g_pre_mix, w_in=w_in, b_if=b_if, g_mhead=g_mhead, g_q_lora=g_q_lora, w_uq=w_uq,
                  g_kv_lora=g_kv_lora, w_uk=w_uk, w_uv=w_uv, w_branch_a=w_branch_a, w_branch_b=w_branch_b,
                  w_out=w_out, g_post_mix=g_post_mix, g_pre_mem=g_pre_mem, g_mem=g_mem, w_mq=w_mq, w_mk=w_mk,
                  w_mv=w_mv, w_mo=w_mo, g_post_mem=g_post_mem, g_pre_ffn=g_pre_ffn, w_ffn_in=w_ffn_in,
                  w_ffn_out=w_ffn_out, g_post_ffn=g_post_ffn)
    depth = w_in.shape[0]
    assert depth == 1, "single-layer stack only"
    bp, seq, _ = x_prompt.shape
    bs, dec_seq, _ = x_sample.shape
    assert dec_seq == 1, "one new token per sample sequence"
    past_len = page_table.shape[1] * PAGE_SIZE
    w = _prep_weights({name: a[0] for name, a in params.items()})

    xp = x_prompt.reshape(bp * seq, D_MODEL)
    qkvm, og, ckv_p, sg, krope_t, gates_t, qcat, kcat, vcat, vt = _inproj(
        xp, w, _rope_tables(0, seq), tm=TOKEN_TILE, absorbed=False, mdtype=BF16)
    hg, c_p, n_p, m_p = _mlstm_prompt(qkvm, vt, gates_t, og, w["g_mhead"], batch=bp, seq=seq)
    ob = _mla_prefill(qcat, kcat, vcat, batch=bp, seq=seq)
    x1, qm = _merge(xp, hg, ob, sg, w, tm=TOKEN_TILE, from_latent=False, qdtype=BF16)
    mem_k, mem_v = _memkv(mem_prompt.reshape(bp * N_MEM, D_MODEL), w, tm=TOKEN_TILE)
    om = _memattn(qm, mem_k, mem_v, batch=bp, seq=seq, tm=2 * TOKEN_TILE)
    y_prompt = _ffn(x1, om, w, tm=TOKEN_TILE).reshape(bp, seq, D_MODEL)

    xs = x_sample.reshape(bs, D_MODEL)
    tables_s = tuple(jnp.broadcast_to(t, (bs, LANES)) for t in _rope_tables(past_len, 1))
    qkvm_s, og_s, ckv_s, sg_s, tail_s, qabs = _inproj(xs, w, tables_s, tm=bs, absorbed=True, mdtype=F32)
    m0_pad = jnp.pad(state_m.reshape(bs, M_HEADS).astype(F32), ((0, 0), (0, LANES - M_HEADS)))
    hg_s, c_s, n_s, m_s = _mlstm_step(qkvm_s, tail_s, state_C.reshape(bs, M_HEADS, M_HEAD_DIM, M_HEAD_DIM),
                                      state_n.reshape(bs, M_HEADS, M_HEAD_DIM), m0_pad, og_s, w["g_mhead"])
    n_phys = cache_ckv.shape[1]
    cache_krope_t = jnp.transpose(cache_krope.reshape(n_phys, PAGE_SIZE, QK_ROPE), (0, 2, 1))
    o_lat = _mla_decode(page_table, qabs, cache_ckv.reshape(n_phys, PAGE_SIZE, KV_LORA), cache_krope_t, ckv_s, tail_s)
    x1_s, qm_s = _merge(xs, hg_s, o_lat.reshape(bs, A_HEADS * KV_LORA), sg_s, w, tm=bs, from_latent=True, qdtype=F32)
    om_s = _memattn_step(qm_s, cache_mem_k.reshape(bs, N_MEM // 2, 2 * X_HEADS, X_HEAD_DIM),
                         cache_mem_v.reshape(bs, N_MEM // 2, 2 * X_HEADS, X_HEAD_DIM))
    y_sample = _ffn(x1_s, om_s, w, tm=bs).reshape(bs, 1, D_MODEL)

    return (y_prompt, y_sample,
            ckv_p.reshape(1, bp, seq, KV_LORA), jnp.transpose(krope_t, (0, 2, 1)).reshape(1, bp, seq, QK_ROPE),
            c_p.reshape(1, bp, M_HEADS, M_HEAD_DIM, M_HEAD_DIM), n_p.reshape(1, bp, M_HEADS, M_HEAD_DIM),
            m_p[:, :, 0].reshape(1, bp, M_HEADS),
            mem_k.reshape(1, bp, N_MEM, X_HEADS, X_HEAD_DIM), mem_v.reshape(1, bp, N_MEM, X_HEADS, X_HEAD_DIM),
            ckv_s.reshape(1, bs, 1, KV_LORA), tail_s[:, :QK_ROPE].reshape(1, bs, 1, QK_ROPE),
            c_s.reshape(1, bs, M_HEADS, M_HEAD_DIM, M_HEAD_DIM), n_s.reshape(1, bs, M_HEADS, M_HEAD_DIM),
            m_s[:, :M_HEADS].reshape(1, bs, M_HEADS))
```

```python
import functools

import jax
import jax.numpy as jnp
import numpy as np
from jax import lax
from jax.experimental import pallas as pl
from jax.experimental.pallas import tpu as pltpu

F32 = jnp.float32
BF16 = jnp.bfloat16

D_MODEL = 1024
PAGE_SIZE = 128
M_HEADS = 4
M_HEAD_DIM = 128
M_WIDTH = M_HEADS * M_HEAD_DIM
M_CHUNK = 128
A_HEADS = 8
QK_NOPE = 64
QK_ROPE = 32
V_HEAD = 64
Q_LORA = 384
KV_LORA = 256
ROPE_BASE = 10000.0
N_MEM = 256
X_HEADS = 4
X_HEAD_DIM = 128
X_WIDTH = X_HEADS * X_HEAD_DIM
D_FF = 2816
EPS = 1e-6

LANES = 128
HEAD_SLOT = 128
ABS_SLOT = 384
ATT_SCALE = (QK_NOPE + QK_ROPE) ** -0.5

V7X_VMEM_BYTES = 64 * 1024 * 1024
VMEM_LIMIT = V7X_VMEM_BYTES * 13 // 16
TOKEN_TILE = 512
ATTN_BLOCK = 512
ATTN_STRIP = 32
MLSTM_SEQS = 8
GATE_SEQS = 4
STEP_SEQS = 16
DECODE_CHUNK_PAGES = 16
DECODE_AHEAD = 4

TAIL_IG = 64
TAIL_LF = 68


def _rms(x, g):
    return x * lax.rsqrt(jnp.mean(x * x, axis=-1, keepdims=True) + EPS) * g


def _dot(a, b):
    return jnp.dot(a.astype(BF16), b.astype(BF16), preferred_element_type=F32)


def _dot_nt(a, b):
    return lax.dot_general(a.astype(BF16), b.astype(BF16), (((1,), (1,)), ((), ())),
                           preferred_element_type=F32)


def _log_sigmoid(x):
    return jnp.minimum(x, 0.0) - jnp.log1p(jnp.exp(-jnp.abs(x)))


def _const_spec(shape):
    nd = len(shape)
    return pl.BlockSpec(shape, lambda *_: (0,) * nd, pipeline_mode=pl.Buffered(1))


def _params(n_axes):
    return pltpu.CompilerParams(dimension_semantics=("arbitrary",) * n_axes, vmem_limit_bytes=VMEM_LIMIT)


def _inproj_kernel(x_ref, gpre_ref, wqkv_ref, wmo_ref, wlat_ref, wg_ref, wtail_ref, btail_ref, ct_ref, st_ref,
                   gq_ref, wq_ref, c128_ref, s128_ref, gkv_ref, *rest, absorbed):
    if absorbed:
        wabs_ref, qkvm_ref, og_ref, ckv_ref, sg_ref, tail_ref, qabs_ref = rest
    else:
        (wuk_ref, wuv_ref, wvt_ref,
         qkvm_ref, og_ref, ckv_ref, sg_ref, krt_ref, gt_ref, qcat_ref, kcat_ref, vcat_ref, vt_ref) = rest
    h = _rms(x_ref[...], gpre_ref[...]).astype(BF16)

    zm = jnp.dot(h, wqkv_ref[...], preferred_element_type=F32)
    qkvm_ref[:, 0:M_WIDTH] = zm[:, 0:M_WIDTH].astype(qkvm_ref.dtype)
    qkvm_ref[:, M_WIDTH:2 * M_WIDTH] = (zm[:, M_WIDTH:2 * M_WIDTH] * (M_HEAD_DIM ** -0.5)).astype(qkvm_ref.dtype)
    qkvm_ref[:, 2 * M_WIDTH:] = zm[:, 2 * M_WIDTH:].astype(qkvm_ref.dtype)
    og_ref[...] = jax.nn.sigmoid(jnp.dot(h, wmo_ref[...], preferred_element_type=F32))
    sg_ref[...] = jax.nn.sigmoid(jnp.dot(h, wg_ref[...], preferred_element_type=F32))

    t = jnp.dot(h, wtail_ref[...], preferred_element_type=F32) + btail_ref[...]
    t = t * ct_ref[...] + pltpu.roll(t, LANES - QK_ROPE, 1) * st_ref[...]
    lane = lax.broadcasted_iota(jnp.int32, t.shape, 1)
    is_lf = jnp.logical_and(lane >= TAIL_LF, lane < TAIL_LF + M_HEADS)
    t = jnp.where(is_lf, _log_sigmoid(t), t)
    if absorbed:
        tail_ref[...] = t
    else:
        tt = jnp.transpose(t)
        krt_ref[0] = tt[0:QK_ROPE]
        gt_ref[...] = tt[TAIL_IG:TAIL_IG + 2 * M_HEADS]

    cqn = _rms(jnp.dot(h, wlat_ref[:, 0:Q_LORA], preferred_element_type=F32), gq_ref[...]).astype(BF16)
    ckvn = _rms(jnp.dot(h, wlat_ref[:, Q_LORA:], preferred_element_type=F32), gkv_ref[...])
    ckv_ref[...] = ckvn

    qc = jnp.dot(cqn, wq_ref[...], preferred_element_type=F32)
    width = A_HEADS * HEAD_SLOT
    first_half = (lax.broadcasted_iota(jnp.int32, qc.shape, 1) % HEAD_SLOT) < QK_NOPE + QK_ROPE // 2
    qs = jnp.where(first_half, pltpu.roll(qc, width - QK_ROPE // 2, 1), pltpu.roll(qc, QK_ROPE // 2, 1))
    c128 = c128_ref[...]
    s128 = s128_ref[...]
    for hh in range(A_HEADS):
        sl = slice(HEAD_SLOT * hh, HEAD_SLOT * (hh + 1))
        qh = ((qc[:, sl] * c128 + qs[:, sl] * s128) * ATT_SCALE).astype(BF16)
        if absorbed:
            qabs_ref[:, ABS_SLOT * hh:ABS_SLOT * (hh + 1)] = jnp.dot(
                qh, wabs_ref[hh], preferred_element_type=F32).astype(BF16)
        else:
            qcat_ref[:, sl] = qh
    if not absorbed:
        cb = ckvn.astype(BF16)
        kr_slot = jnp.where(jnp.logical_and(lane >= QK_NOPE, lane < QK_NOPE + QK_ROPE), pltpu.roll(t, QK_NOPE, 1), 0.0)
        kn = jnp.dot(cb, wuk_ref[...], preferred_element_type=F32)
        for hh in range(A_HEADS):
            sl = slice(HEAD_SLOT * hh, HEAD_SLOT * (hh + 1))
            kcat_ref[:, sl] = (kn[:, sl] + kr_slot).astype(BF16)
        vlane = lax.broadcasted_iota(jnp.int32, (1, A_HEADS * HEAD_SLOT), 1) % HEAD_SLOT
        vcat = jnp.dot(cb, wuv_ref[...], preferred_element_type=F32)
        vcat_ref[...] = jnp.where(vlane == V_HEAD, 1.0, vcat).astype(BF16)
        vt_ref[...] = _dot_nt(wvt_ref[...], h).astype(BF16)


def _inproj(x, w, tables, *, tm, absorbed, mdtype):
    m = x.shape[0]
    ct, st, c128, s128 = tables
    seq = ct.shape[0]
    nt = seq // tm
    grid = (m // tm,)
    row = lambda i: (i, 0)
    tab = lambda i: (i % nt, 0)
    wide = A_HEADS * HEAD_SLOT
    in_specs = [
        pl.BlockSpec((tm, D_MODEL), row), _const_spec((1, D_MODEL)),
        _const_spec((D_MODEL, 3 * M_WIDTH)), _const_spec((D_MODEL, M_WIDTH)), _const_spec((D_MODEL, Q_LORA + KV_LORA)),
        _const_spec((D_MODEL, 2 * D_MODEL)), _const_spec((D_MODEL, LANES)), _const_spec((1, LANES)),
        pl.BlockSpec((tm, LANES), tab), pl.BlockSpec((tm, LANES), tab),
        _const_spec((1, Q_LORA)), _const_spec((Q_LORA, wide)),
        pl.BlockSpec((tm, LANES), tab), pl.BlockSpec((tm, LANES), tab),
        _const_spec((1, KV_LORA)),
    ]
    args = [x, w["g_pre_mix"], w["w_qkv"], w["w_mo_gate"], w["w_lat"], w["w_g"], w["w_tail"], w["b_tail"], ct, st,
            w["g_q_lora"], w["wq_cat"], c128, s128, w["g_kv_lora"]]
    outs = [
        (jax.ShapeDtypeStruct((m, 3 * M_WIDTH), mdtype), pl.BlockSpec((tm, 3 * M_WIDTH), row)),
        (jax.ShapeDtypeStruct((m, M_WIDTH), F32), pl.BlockSpec((tm, M_WIDTH), row)),
        (jax.ShapeDtypeStruct((m, KV_LORA), F32), pl.BlockSpec((tm, KV_LORA), row)),
        (jax.ShapeDtypeStruct((m, 2 * D_MODEL), F32), pl.BlockSpec((tm, 2 * D_MODEL), row)),
    ]
    if absorbed:
        in_specs += [_const_spec((A_HEADS, HEAD_SLOT, ABS_SLOT))]
        args += [w["w_abs"]]
        outs += [(jax.ShapeDtypeStruct((m, LANES), F32), pl.BlockSpec((tm, LANES), row)),
                 (jax.ShapeDtypeStruct((m, A_HEADS * ABS_SLOT), BF16), pl.BlockSpec((tm, A_HEADS * ABS_SLOT), row))]
    else:
        in_specs += [_const_spec((KV_LORA, wide)), _const_spec((KV_LORA, wide)), _const_spec((M_WIDTH, D_MODEL))]
        args += [w["wuk_cat"], w["wuv_cat"], w["wv_t"]]
        outs += [(jax.ShapeDtypeStruct((m // seq, QK_ROPE, seq), F32),
                  pl.BlockSpec((1, QK_ROPE, tm), lambda i: (i // nt, 0, i % nt))),
                 (jax.ShapeDtypeStruct((2 * M_HEADS, m), F32), pl.BlockSpec((2 * M_HEADS, tm), lambda i: (0, i)))]
        outs += [(jax.ShapeDtypeStruct((m, wide), BF16), pl.BlockSpec((tm, wide), row))] * 3
        outs += [(jax.ShapeDtypeStruct((M_WIDTH, m), BF16), pl.BlockSpec((M_WIDTH, tm), lambda i: (0, i)))]
    return pl.pallas_call(
        functools.partial(_inproj_kernel, absorbed=absorbed),
        grid=grid, in_specs=in_specs,
        out_specs=[o[1] for o in outs], out_shape=[o[0] for o in outs],
        compiler_params=_params(1), name="inproj_abs" if absorbed else "inproj",
    )(*args)


def _split3_dot(x, m01):
    hi = x.astype(BF16)
    r1 = x - hi.astype(F32)
    mid = r1.astype(BF16)
    lo = (r1 - mid.astype(F32)).astype(BF16)
    return sum(jnp.dot(part, m01, preferred_element_type=F32) for part in (hi, mid, lo))


def _cumsum_lanes(x, upper):
    return _split3_dot(x, jnp.where(upper, 1.0, 0.0).astype(BF16))


def _cummax_lanes(x):
    lane = lax.broadcasted_iota(jnp.int32, x.shape, 1)
    shift = 1
    while shift < x.shape[1]:
        x = jnp.maximum(x, jnp.where(lane >= shift, pltpu.roll(x, shift, 1), -jnp.inf))
        shift *= 2
    return x


def _mlstm_gates_kernel(g_ref, urow_ref, scal_ref, cols_ref, *, nc, nseq):
    L = M_CHUNK
    H = M_HEADS
    rows = nseq * nc * 2 * H
    g = g_ref[...]
    r = jnp.concatenate([g[:, L * c:L * (c + 1)] for c in range(nseq * nc)], axis=0)
    ri = lax.broadcasted_iota(jnp.int32, (rows, L), 0)
    ci = lax.broadcasted_iota(jnp.int32, (rows, L), 1)
    top = (ri % (2 * H)) < H
    b_all = _cumsum_lanes(r, lax.broadcasted_iota(jnp.int32, (L, L), 0) <= lax.broadcasted_iota(jnp.int32, (L, L), 1))
    b = pltpu.roll(b_all, rows - H, 0)
    u = r - b
    cm = _cummax_lanes(u)
    g_last = jnp.max(jnp.where(ci == L - 1, b, -jnp.inf), axis=1, keepdims=True)
    wlog = g_last - b + r
    wmax = jnp.max(wlog, axis=1, keepdims=True)
    urow_ref[...] = jnp.where(top, u, pltpu.roll(wlog, H, 0)).reshape(nseq, nc, 2 * H, L)
    scal = jnp.where(top, jnp.broadcast_to(g_last, (rows, L)), pltpu.roll(jnp.broadcast_to(wmax, (rows, L)), H, 0))
    scal_ref[...] = scal.reshape(nseq, nc, 2 * H, L)
    first = jnp.where(top, b, pltpu.roll(cm, H, 0))
    for c in range(nseq * nc):
        sl = slice(2 * H * c, 2 * H * (c + 1))
        cols_ref[c // nc, c % nc] = jnp.transpose(jnp.concatenate([first[sl], wlog[sl]], axis=0))


def _mlstm_gates(g, *, batch, nc, nseq=GATE_SEQS):
    blk = lambda b: (b, 0, 0, 0)
    return pl.pallas_call(
        functools.partial(_mlstm_gates_kernel, nc=nc, nseq=nseq),
        grid=(batch // nseq,),
        in_specs=[pl.BlockSpec((2 * M_HEADS, nseq * nc * M_CHUNK), lambda b: (0, b))],
        out_specs=[pl.BlockSpec((nseq, nc, 2 * M_HEADS, M_CHUNK), blk),
                   pl.BlockSpec((nseq, nc, 2 * M_HEADS, M_CHUNK), blk),
                   pl.BlockSpec((nseq, nc, M_CHUNK, 4 * M_HEADS), blk)],
        out_shape=[jax.ShapeDtypeStruct((batch, nc, 2 * M_HEADS, M_CHUNK), F32),
                   jax.ShapeDtypeStruct((batch, nc, 2 * M_HEADS, M_CHUNK), F32),
                   jax.ShapeDtypeStruct((batch, nc, M_CHUNK, 4 * M_HEADS), F32)],
        compiler_params=_params(1), name="mlstm_gates",
    )(g)


def _mlstm_kernel(*refs, nb):
    qkv_ref = refs[0]
    vt_refs = refs[1:1 + nb]
    urow_ref, scal_ref, cols_ref, sel_ref, og_ref, gm_ref, hg_ref, c_ref, n_ref, m_ref = refs[1 + nb:]
    L = M_CHUNK

    @pl.when(pl.program_id(1) == 0)
    def _():
        c_ref[...] = jnp.zeros_like(c_ref)
        n_ref[...] = jnp.zeros_like(n_ref)
        m_ref[...] = jnp.zeros_like(m_ref)

    row = lax.broadcasted_iota(jnp.int32, (L, L), 0)
    col = lax.broadcasted_iota(jnp.int32, (L, L), 1)
    tril = col <= row

    for bb in range(nb):
        vt_ref = vt_refs[bb]
        rows8 = urow_ref[bb, 0]
        u4 = rows8[0:M_HEADS]
        wlog4 = rows8[M_HEADS:2 * M_HEADS]
        scal = scal_ref[bb, 0]
        g_last4 = scal[0:M_HEADS, 0:1]
        wmax4 = scal[M_HEADS:2 * M_HEADS, 0:1]
        cols = _split3_dot(cols_ref[bb, 0], sel_ref[...])
        m_prev4 = m_ref[bb][:, 0:1]
        n_prev4 = n_ref[bb]
        m_new4 = jnp.maximum(g_last4 + m_prev4, wmax4)
        decay4 = jnp.exp(g_last4 + m_prev4 - m_new4)
        ws4 = jnp.exp(wlog4 - m_new4)
        m_ref[bb] = jnp.broadcast_to(m_new4, (M_HEADS, LANES))
        n_rows = []

        for h in range(M_HEADS):
            sl = slice(M_HEAD_DIM * h, M_HEAD_DIM * (h + 1))
            q = qkv_ref[bb, :, sl]
            k = qkv_ref[bb, :, M_WIDTH + M_HEAD_DIM * h:M_WIDTH + M_HEAD_DIM * (h + 1)]
            v = qkv_ref[bb, :, 2 * M_WIDTH + M_HEAD_DIM * h:2 * M_WIDTH + M_HEAD_DIM * (h + 1)]
            m_prev = m_prev4[h:h + 1]
            c_prev = c_ref[bb, h]
            n_prev = n_prev4[h:h + 1]
            b_col = cols[:, LANES * h:LANES * (h + 1)]
            mm_col = jnp.maximum(cols[:, LANES * (M_HEADS + h):LANES * (M_HEADS + h + 1)], m_prev)
            w_intra = jnp.where(tril, jnp.exp(u4[h:h + 1] - mm_col), 0.0)
            w_inter = jnp.exp(m_prev - mm_col)
            s = _dot_nt(q, k) * w_intra
            cn = jnp.concatenate([c_prev.astype(BF16), jnp.broadcast_to(n_prev.astype(BF16), (L, M_HEAD_DIM))],
                                 axis=0)
            qcn = _dot_nt(q, cn)
            num = w_inter * qcn[:, 0:M_HEAD_DIM] + _dot(s, v)
            den = w_inter * qcn[:, M_HEAD_DIM:] + jnp.sum(s, axis=1, keepdims=True)
            hs = num / jnp.maximum(jnp.abs(den), jnp.exp(-(b_col + mm_col)))

            ws = ws4[h:h + 1]
            lhs = jnp.concatenate([(vt_ref[sl, :].astype(F32) * ws).astype(BF16),
                                   jnp.broadcast_to(ws.astype(BF16), (16, L))], axis=0)
            upd = jnp.dot(lhs, k, preferred_element_type=F32)
            decay = decay4[h:h + 1]
            c_ref[bb, h] = decay * c_prev + upd[0:M_HEAD_DIM]
            n_rows.append(decay * n_prev + upd[M_HEAD_DIM:M_HEAD_DIM + 1])

            mu = jnp.mean(hs, axis=1, keepdims=True)
            d = hs - mu
            y = d * lax.rsqrt(jnp.mean(d * d, axis=1, keepdims=True) + EPS) * gm_ref[:, sl]
            hg_ref[bb, :, sl] = (y * og_ref[bb, :, sl]).astype(BF16)
        n_ref[bb] = jnp.concatenate(n_rows, axis=0)


def _mlstm_prompt(qkvm, vt, gates, og, g_mhead, *, batch, seq, nb=MLSTM_SEQS):
    nc = seq // M_CHUNK
    tok = lambda b, c: (b, c, 0)
    chunk = lambda b, c: (b, c, 0, 0)
    urow, scal, cols = _mlstm_gates(gates, batch=batch, nc=nc)
    sel = jnp.asarray(np.kron(np.eye(4 * M_HEADS, 2 * M_HEADS), np.ones((1, LANES))), BF16)
    vt_specs = [pl.BlockSpec((M_WIDTH, M_CHUNK), functools.partial(lambda b, c, bb: (0, (nb * b + bb) * nc + c), bb=bb))
                for bb in range(nb)]
    hg, c_p, n_p, m_p = pl.pallas_call(
        functools.partial(_mlstm_kernel, nb=nb),
        grid=(batch // nb, nc),
        in_specs=[pl.BlockSpec((nb, M_CHUNK, 3 * M_WIDTH), tok)] + vt_specs + [
            pl.BlockSpec((nb, 1, 2 * M_HEADS, M_CHUNK), chunk),
            pl.BlockSpec((nb, 1, 2 * M_HEADS, M_CHUNK), chunk),
            pl.BlockSpec((nb, 1, M_CHUNK, 4 * M_HEADS), chunk),
            pl.BlockSpec((4 * M_HEADS, 2 * M_HEADS * LANES), lambda b, c: (0, 0)),
            pl.BlockSpec((nb, M_CHUNK, M_WIDTH), tok),
            pl.BlockSpec((1, M_WIDTH), lambda b, c: (0, 0)),
        ],
        out_specs=[
            pl.BlockSpec((nb, M_CHUNK, M_WIDTH), tok),
            pl.BlockSpec((nb, M_HEADS, M_HEAD_DIM, M_HEAD_DIM), lambda b, c: (b, 0, 0, 0)),
            pl.BlockSpec((nb, M_HEADS, M_HEAD_DIM), lambda b, c: (b, 0, 0)),
            pl.BlockSpec((nb, M_HEADS, LANES), lambda b, c: (b, 0, 0)),
        ],
        out_shape=[
            jax.ShapeDtypeStruct((batch, seq, M_WIDTH), BF16),
            jax.ShapeDtypeStruct((batch, M_HEADS, M_HEAD_DIM, M_HEAD_DIM), F32),
            jax.ShapeDtypeStruct((batch, M_HEADS, M_HEAD_DIM), F32),
            jax.ShapeDtypeStruct((batch, M_HEADS, LANES), F32),
        ],
        compiler_params=_params(2), name="mlstm_prompt",
    )(qkvm.reshape(batch, seq, 3 * M_WIDTH), *([vt] * nb), urow, scal, cols, sel,
      og.reshape(batch, seq, M_WIDTH), g_mhead)
    return hg.reshape(batch * seq, M_WIDTH), c_p, n_p, m_p


def _mlstm_step_kernel(qkv_ref, tail_ref, c_ref, n_ref, m_ref, og_ref, gm_ref, hg_ref, co_ref, no_ref, mo_ref, *, rows):
    D = M_HEAD_DIM
    lane = lax.broadcasted_iota(jnp.int32, (rows, LANES), 1)
    rowi = lax.broadcasted_iota(jnp.int32, (rows, D), 0)
    tail = tail_ref[...]
    m_in = m_ref[...]
    m_out = jnp.zeros((rows, LANES), F32)
    for h in range(M_HEADS):
        sl = slice(D * h, D * (h + 1))
        q = qkv_ref[:, sl]
        k = qkv_ref[:, M_WIDTH + D * h:M_WIDTH + D * (h + 1)]
        v = qkv_ref[:, 2 * M_WIDTH + D * h:2 * M_WIDTH + D * (h + 1)]
        ig = tail[:, TAIL_IG + h:TAIL_IG + h + 1]
        lf = tail[:, TAIL_LF + h:TAIL_LF + h + 1]
        m_prev = m_in[:, h:h + 1]
        n_prev = n_ref[:, h, :]
        a = lf + m_prev
        mt = jnp.maximum(a, ig)
        w_in = jnp.exp(ig - mt)
        w_st = jnp.exp(a - mt)
        s = jnp.sum(q * k, axis=1, keepdims=True) * w_in
        cq = jnp.zeros((rows, D), F32)
        for g in range(rows):
            cq = jnp.where(rowi == g, _dot_nt(q, c_ref[g, h]), cq)
        num = w_st * cq + s * v
        den = w_st * jnp.sum(n_prev * q, axis=1, keepdims=True) + s
        hs = num / jnp.maximum(jnp.abs(den), jnp.exp(-mt))
        vw_t = jnp.transpose(v * w_in)
        for g in range(rows):
            co_ref[g, h] = w_st[g:g + 1] * c_ref[g, h] + vw_t[:, g:g + 1] * k[g:g + 1, :]
        no_ref[:, h, :] = w_st * n_prev + w_in * k
        m_out = jnp.where(lane == h, mt, m_out)
        mu = jnp.mean(hs, axis=1, keepdims=True)
        d = hs - mu
        y = d * lax.rsqrt(jnp.mean(d * d, axis=1, keepdims=True) + EPS) * gm_ref[:, sl]
        hg_ref[:, sl] = (y * og_ref[:, sl]).astype(BF16)
    mo_ref[...] = m_out


def _mlstm_step(qkvm, tail, c0, n0, m0_pad, og, g_mhead, *, rows=STEP_SEQS):
    nb = qkvm.shape[0]
    row = lambda i: (i, 0)
    return pl.pallas_call(
        functools.partial(_mlstm_step_kernel, rows=rows),
        grid=(nb // rows,),
        in_specs=[
            pl.BlockSpec((rows, 3 * M_WIDTH), row),
            pl.BlockSpec((rows, LANES), row),
            pl.BlockSpec((rows, M_HEADS, M_HEAD_DIM, M_HEAD_DIM), lambda i: (i, 0, 0, 0)),
            pl.BlockSpec((rows, M_HEADS, M_HEAD_DIM), lambda i: (i, 0, 0)),
            pl.BlockSpec((rows, LANES), row),
            pl.BlockSpec((rows, M_WIDTH), row),
            pl.BlockSpec((1, M_WIDTH), lambda i: (0, 0)),
        ],
        out_specs=[
            pl.BlockSpec((rows, M_WIDTH), row),
            pl.BlockSpec((rows, M_HEADS, M_HEAD_DIM, M_HEAD_DIM), lambda i: (i, 0, 0, 0)),
            pl.BlockSpec((rows, M_HEADS, M_HEAD_DIM), lambda i: (i, 0, 0)),
            pl.BlockSpec((rows, LANES), row),
        ],
        out_shape=[
            jax.ShapeDtypeStruct((nb, M_WIDTH), BF16),
            jax.ShapeDtypeStruct((nb, M_HEADS, M_HEAD_DIM, M_HEAD_DIM), F32),
            jax.ShapeDtypeStruct((nb, M_HEADS, M_HEAD_DIM), F32),
            jax.ShapeDtypeStruct((nb, LANES), F32),
        ],
        compiler_params=_params(1), name="mlstm_step",
    )(qkvm, tail, c0, n0, m0_pad, og, g_mhead)


def _mla_prefill_kernel(q_ref, k_ref, v_ref, o_ref, s_scr, p_scr, m_scr, acc_scr, *, blk, nq, heads, rows):
    reps = blk // LANES
    lane = lax.broadcasted_iota(jnp.int32, (blk, HEAD_SLOT), 1)
    n_done = 0
    for i in range(nq):
        par = i % 2
        q_rows = slice(blk * i, blk * (i + 1))
        m_scr[par] = jnp.full(m_scr.shape[1:], -jnp.inf, F32)
        acc_scr[par] = jnp.zeros(acc_scr.shape[1:], F32)
        for j in range(i + 1):
            masked = j == i
            k_rows = slice(blk * j, blk * (j + 1))
            slot = n_done % 2
            n_done += 1
            for hh in range(heads):
                sl = slice(HEAD_SLOT * hh, HEAD_SLOT * (hh + 1))
                s_scr[slot, hh] = _dot_nt(q_ref[q_rows, sl], k_ref[k_rows, sl])
            for hh in range(heads):
                for r in range(blk // rows):
                    rs = slice(rows * r, rows * (r + 1))
                    s = s_scr[slot, hh, rs, :]
                    if masked:
                        qi = lax.broadcasted_iota(jnp.int32, (rows, blk), 0) + rows * r
                        ki = lax.broadcasted_iota(jnp.int32, (rows, blk), 1)
                        s = jnp.where(ki <= qi, s, -jnp.inf)
                    m_old = m_scr[par, hh, rs, :]
                    m_new = jnp.maximum(m_old, jnp.max(s, axis=1, keepdims=True))
                    p_scr[hh, rs, :] = jnp.exp(s - jnp.concatenate([m_new] * reps, axis=1)).astype(BF16)
                    acc_scr[par, hh, rs, :] = acc_scr[par, hh, rs, :] * jnp.exp(m_old - m_new)
                    m_scr[par, hh, rs, :] = m_new
            for hh in range(heads):
                sl = slice(HEAD_SLOT * hh, HEAD_SLOT * (hh + 1))
                acc_scr[par, hh] += jnp.dot(p_scr[hh], v_ref[k_rows, sl], preferred_element_type=F32)
        o0, o1 = (acc_scr[par, hh] / acc_scr[par, hh][:, V_HEAD:V_HEAD + 1] for hh in range(2))
        o_ref[q_rows, :] = jnp.where(lane < V_HEAD, o0, pltpu.roll(o1, V_HEAD, 1)).astype(BF16)


def _mla_prefill(qcat, kcat, vcat, *, batch, seq, blk=ATTN_BLOCK, rows=ATTN_STRIP):
    heads = 2
    nq = seq // blk
    wide = heads * HEAD_SLOT
    per_seq = lambda b, h: (b, h)
    return pl.pallas_call(
        functools.partial(_mla_prefill_kernel, blk=blk, nq=nq, heads=heads, rows=rows),
        grid=(batch, A_HEADS // heads),
        in_specs=[pl.BlockSpec((seq, wide), per_seq)] * 3,
        out_specs=pl.BlockSpec((seq, heads * V_HEAD), per_seq),
        out_shape=jax.ShapeDtypeStruct((batch * seq, A_HEADS * V_HEAD), BF16),
        scratch_shapes=[pltpu.VMEM((2, heads, blk, blk), F32), pltpu.VMEM((heads, blk, blk), BF16),
                        pltpu.VMEM((2, heads, blk, LANES), F32), pltpu.VMEM((2, heads, blk, HEAD_SLOT), F32)],
        compiler_params=_params(2), name="mla_prefill",
    )(qcat, kcat, vcat)


def _mla_decode_kernel(pt_ref, q_ref, ckv_hbm, kr_hbm, cself_ref, tself_ref, o_ref,
                       cbuf, rbuf, kbuf, sems, *, n_chunks, chunk_pages, ahead):
    b = pl.program_id(0)
    nb = pl.num_programs(0)

    def page_copies(bb, c, slot, p):
        pg = pt_ref[bb, c * chunk_pages + p]
        dst = pl.ds(p * PAGE_SIZE, PAGE_SIZE)
        return (pltpu.make_async_copy(ckv_hbm.at[pg], cbuf.at[slot, dst], sems.at[0, slot]),
                pltpu.make_async_copy(kr_hbm.at[pg], rbuf.at[slot, :, dst], sems.at[1, slot]))

    def issue(bb, c, slot):
        for p in range(chunk_pages):
            for cp in page_copies(bb, c, slot, p):
                cp.start()

    def wait(slot):
        for p in range(chunk_pages):
            for cp in page_copies(0, 0, slot, p):
                cp.wait()

    @pl.when(b == 0)
    def _():
        for c in range(ahead):
            issue(0, c, c)

    q = q_ref[0]
    q_lat = q[:, 0:KV_LORA]
    q_rope = q[:, KV_LORA:KV_LORA + QK_ROPE]

    def scores(slot):
        kc = cbuf[slot].astype(BF16)
        kbuf[slot % 2] = kc
        kr = rbuf[slot].astype(BF16)
        return _dot_nt(q_lat, kc) + jnp.dot(q_rope, kr, preferred_element_type=F32)

    def accumulate(carry, s, slot):
        m_old, l, acc = carry
        m_new = jnp.maximum(m_old, jnp.max(s, axis=1, keepdims=True))
        p = jnp.exp(s - m_new)
        alpha = jnp.exp(m_old - m_new)
        return (m_new, alpha * l + jnp.sum(p, axis=1, keepdims=True),
                alpha * acc + jnp.dot(p.astype(BF16), kbuf[slot], preferred_element_type=F32))

    carry = (jnp.full((A_HEADS, 1), -jnp.inf, F32), jnp.zeros((A_HEADS, 1), F32), jnp.zeros((A_HEADS, KV_LORA), F32))
    s_prev = None
    for c in range(n_chunks):
        nxt = c + ahead
        if nxt < n_chunks:
            issue(b, nxt, nxt)
        else:
            @pl.when(b + 1 < nb)
            def _():
                issue(b + 1, nxt - n_chunks, nxt - n_chunks)
        wait(c)
        s_cur = scores(c)
        if c > 0:
            carry = accumulate(carry, s_prev, (c - 1) % 2)
        s_prev = s_cur
    m_old, l, acc = accumulate(carry, s_prev, (n_chunks - 1) % 2)

    c_self = cself_ref[0].astype(BF16).astype(F32)
    r_self = tself_ref[0][:, 0:QK_ROPE].astype(BF16).astype(F32)
    s_self = (jnp.sum(q_lat.astype(F32) * c_self, axis=1, keepdims=True)
              + jnp.sum(q_rope.astype(F32) * r_self, axis=1, keepdims=True))
    m_new = jnp.maximum(m_old, s_self)
    p_self = jnp.exp(s_self - m_new)
    alpha = jnp.exp(m_old - m_new)
    l = alpha * l + p_self
    acc = alpha * acc + p_self.astype(BF16).astype(F32) * c_self
    o_ref[0] = acc / l


def _mla_decode(page_table, qabs, cache_ckv, cache_krope, ckv_self, tail_self, *,
                chunk_pages=DECODE_CHUNK_PAGES, ahead=DECODE_AHEAD):
    nb, n_pages = page_table.shape
    n_chunks = n_pages // chunk_pages
    assert n_chunks * chunk_pages == n_pages and ahead < n_chunks
    keys = chunk_pages * PAGE_SIZE
    grid_spec = pltpu.PrefetchScalarGridSpec(
        num_scalar_prefetch=1,
        grid=(nb,),
        in_specs=[
            pl.BlockSpec((1, A_HEADS, ABS_SLOT), lambda b, pt: (b, 0, 0)),
            pl.BlockSpec(memory_space=pl.ANY),
            pl.BlockSpec(memory_space=pl.ANY),
            pl.BlockSpec((1, 1, KV_LORA), lambda b, pt: (b, 0, 0)),
            pl.BlockSpec((1, 1, LANES), lambda b, pt: (b, 0, 0)),
        ],
        out_specs=pl.BlockSpec((1, A_HEADS, KV_LORA), lambda b, pt: (b, 0, 0)),
        scratch_shapes=[
            pltpu.VMEM((n_chunks, keys, KV_LORA), F32),
            pltpu.VMEM((n_chunks, QK_ROPE, keys), F32),
            pltpu.VMEM((2, keys, KV_LORA), BF16),
            pltpu.SemaphoreType.DMA((2, n_chunks)),
        ],
    )
    return pl.pallas_call(
        functools.partial(_mla_decode_kernel, n_chunks=n_chunks, chunk_pages=chunk_pages, ahead=ahead),
        grid_spec=grid_spec,
        out_shape=jax.ShapeDtypeStruct((nb, A_HEADS, KV_LORA), F32),
        compiler_params=_params(1), name="mla_decode",
    )(page_table, qabs.reshape(nb, A_HEADS, ABS_SLOT), cache_ckv, cache_krope,
      ckv_self.reshape(nb, 1, KV_LORA), tail_self.reshape(nb, 1, LANES))


def _merge_kernel(x_ref, hg_ref, ob_ref, sg_ref, wa_ref, wb_ref, wout_ref, gpost_ref, gmem_ref, wmq_ref,
                  *rest, from_latent):
    if from_latent:
        wuv_ref, x1_ref, qm_ref = rest
        ob = sum(_dot(ob_ref[:, KV_LORA * hh:KV_LORA * (hh + 1)], wuv_ref[hh]) for hh in range(A_HEADS)).astype(BF16)
    else:
        x1_ref, qm_ref = rest
        ob = ob_ref[...]
    ya = jnp.dot(hg_ref[...], wa_ref[...], preferred_element_type=F32)
    yb = jnp.dot(ob, wb_ref[...], preferred_element_type=F32)
    mix = sg_ref[:, 0:D_MODEL] * ya + sg_ref[:, D_MODEL:] * yb
    y = _dot(mix, wout_ref[...])
    x1 = x_ref[...] + _rms(y, gpost_ref[...])
    x1_ref[...] = x1
    qm = _dot(_rms(x1, gmem_ref[...]), wmq_ref[...]) * (X_HEAD_DIM ** -0.5)
    qm_ref[...] = qm.astype(qm_ref.dtype)


def _merge(x, hg, ob, sg, w, *, tm, from_latent, qdtype):
    m = x.shape[0]
    row = lambda i: (i, 0)
    a_width = A_HEADS * V_HEAD
    ob_w = ob.shape[1]
    in_specs = [
        pl.BlockSpec((tm, D_MODEL), row), pl.BlockSpec((tm, M_WIDTH), row), pl.BlockSpec((tm, ob_w), row),
        pl.BlockSpec((tm, 2 * D_MODEL), row),
        _const_spec((M_WIDTH, D_MODEL)), _const_spec((a_width, D_MODEL)), _const_spec((D_MODEL, D_MODEL)),
        _const_spec((1, D_MODEL)), _const_spec((1, D_MODEL)), _const_spec((D_MODEL, X_WIDTH)),
    ]
    args = [x, hg, ob, sg, w["w_branch_a"], w["w_branch_b"], w["w_out"], w["g_post_mix"], w["g_pre_mem"], w["w_mq"]]
    if from_latent:
        in_specs += [_const_spec((A_HEADS, KV_LORA, a_width))]
        args += [w["wuv_heads"]]
    return pl.pallas_call(
        functools.partial(_merge_kernel, from_latent=from_latent),
        grid=(m // tm,), in_specs=in_specs,
        out_specs=[pl.BlockSpec((tm, D_MODEL), row), pl.BlockSpec((tm, X_WIDTH), row)],
        out_shape=[jax.ShapeDtypeStruct((m, D_MODEL), F32), jax.ShapeDtypeStruct((m, X_WIDTH), qdtype)],
        compiler_params=_params(1), name="merge_lat" if from_latent else "merge",
    )(*args)


def _memkv_kernel(mem_ref, g_ref, wk_ref, wv_ref, k_ref, v_ref):
    mn = _rms(mem_ref[...], g_ref[...]).astype(BF16)
    k_ref[...] = jnp.dot(mn, wk_ref[...], preferred_element_type=F32)
    v_ref[...] = jnp.dot(mn, wv_ref[...], preferred_element_type=F32)


def _memkv(mem, w, *, tm):
    m = mem.shape[0]
    row = lambda i: (i, 0)
    return pl.pallas_call(
        _memkv_kernel, grid=(m // tm,),
        in_specs=[pl.BlockSpec((tm, D_MODEL), row), _const_spec((1, D_MODEL)),
                  _const_spec((D_MODEL, X_WIDTH)), _const_spec((D_MODEL, X_WIDTH))],
        out_specs=[pl.BlockSpec((tm, X_WIDTH), row)] * 2,
        out_shape=[jax.ShapeDtypeStruct((m, X_WIDTH), F32)] * 2,
        compiler_params=_params(1), name="memkv",
    )(mem, w["g_mem"], w["w_mk"], w["w_mv"])


def _memattn_kernel(q_ref, k_ref, v_ref, o_ref):
    kb = k_ref[...].astype(BF16)
    vb = v_ref[...].astype(BF16)
    for h in range(X_HEADS):
        sl = slice(X_HEAD_DIM * h, X_HEAD_DIM * (h + 1))
        s = _dot_nt(q_ref[:, sl], kb[:, sl])
        p = jnp.exp(s - jnp.max(s, axis=1, keepdims=True))
        l = jnp.sum(p, axis=1, keepdims=True)
        o_ref[:, sl] = (jnp.dot(p.astype(BF16), vb[:, sl], preferred_element_type=F32) / l).astype(BF16)


def _memattn(qm, mem_k, mem_v, *, batch, seq, tm):
    nt = seq // tm
    tok = lambda b, i: (b * nt + i, 0)
    kv = lambda b, i: (b, 0)
    return pl.pallas_call(
        _memattn_kernel, grid=(batch, nt),
        in_specs=[pl.BlockSpec((tm, X_WIDTH), tok), pl.BlockSpec((N_MEM, X_WIDTH), kv),
                  pl.BlockSpec((N_MEM, X_WIDTH), kv)],
        out_specs=pl.BlockSpec((tm, X_WIDTH), tok),
        out_shape=jax.ShapeDtypeStruct((batch * seq, X_WIDTH), BF16),
        compiler_params=_params(2), name="memattn",
    )(qm, mem_k, mem_v)


def _memattn_step_kernel(q_ref, k_ref, v_ref, o_ref, *, rows):
    for g in range(rows):
        q4 = [q_ref[g:g + 1, X_HEAD_DIM * h:X_HEAD_DIM * (h + 1)] for h in range(X_HEADS)]
        q8 = jnp.concatenate(q4 + q4, axis=0)
        s = jnp.sum(k_ref[g] * q8[None], axis=2, keepdims=True)
        mx = jnp.max(s, axis=0)
        mx = jnp.maximum(mx, pltpu.roll(mx, X_HEADS, 0))
        p = jnp.exp(s - mx[None])
        l8 = jnp.sum(p, axis=0)
        o8 = jnp.sum(p * v_ref[g], axis=0)
        o_ref[g] = (o8[0:X_HEADS] + o8[X_HEADS:]) / (l8[0:X_HEADS] + l8[X_HEADS:])


def _memattn_step(qm, mem_k, mem_v, *, rows=STEP_SEQS):
    nb = qm.shape[0]
    row = lambda i: (i, 0)
    kv = lambda i: (i, 0, 0, 0)
    kv_block = (rows, N_MEM // 2, 2 * X_HEADS, X_HEAD_DIM)
    return pl.pallas_call(
        functools.partial(_memattn_step_kernel, rows=rows), grid=(nb // rows,),
        in_specs=[pl.BlockSpec((rows, X_WIDTH), row), pl.BlockSpec(kv_block, kv), pl.BlockSpec(kv_block, kv)],
        out_specs=pl.BlockSpec((rows, X_HEADS, X_HEAD_DIM), lambda i: (i, 0, 0)),
        out_shape=jax.ShapeDtypeStruct((nb, X_HEADS, X_HEAD_DIM), F32),
        compiler_params=_params(1), name="memattn_step",
    )(qm, mem_k, mem_v).reshape(nb, X_WIDTH)


def _ffn_kernel(x1_ref, o_ref, wmo_ref, gpm_ref, gpf_ref, win_ref, wdn_ref, gpost_ref, y_ref, *, f_chunk):
    x2 = x1_ref[...] + _rms(_dot(o_ref[...], wmo_ref[...]), gpm_ref[...])
    h = _rms(x2, gpf_ref[...]).astype(BF16)
    acc = jnp.zeros(x2.shape, F32)
    for c in range(D_FF // f_chunk):
        sl = slice(f_chunk * c, f_chunk * (c + 1))
        g = jnp.dot(h, win_ref[:, sl], preferred_element_type=F32)
        u = jnp.dot(h, win_ref[:, D_FF + f_chunk * c:D_FF + f_chunk * (c + 1)], preferred_element_type=F32)
        acc = acc + jnp.dot((g * jax.nn.sigmoid(g) * u).astype(BF16), wdn_ref[sl, :], preferred_element_type=F32)
    y_ref[...] = x2 + _rms(acc, gpost_ref[...])


def _ffn(x1, o, w, *, tm, f_chunk=D_FF):
    m = x1.shape[0]
    row = lambda i: (i, 0)
    return pl.pallas_call(
        functools.partial(_ffn_kernel, f_chunk=f_chunk), grid=(m // tm,),
        in_specs=[pl.BlockSpec((tm, D_MODEL), row), pl.BlockSpec((tm, X_WIDTH), row),
                  _const_spec((X_WIDTH, D_MODEL)), _const_spec((1, D_MODEL)), _const_spec((1, D_MODEL)),
                  _const_spec((D_MODEL, 2 * D_FF)), _const_spec((D_FF, D_MODEL)),
                  _const_spec((1, D_MODEL))],
        out_specs=pl.BlockSpec((tm, D_MODEL), row),
        out_shape=jax.ShapeDtypeStruct((m, D_MODEL), F32),
        compiler_params=_params(1), name="ffn",
    )(x1, o, w["w_mo"], w["g_post_mem"], w["g_pre_ffn"], w["w_ffn_in"], w["w_ffn_out"], w["g_post_ffn"])


def _prep_weights(p):
    w = {}
    for name in ("g_pre_mix", "g_q_lora", "g_kv_lora", "g_mhead", "g_post_mix", "g_pre_mem", "g_mem", "g_post_mem",
                 "g_pre_ffn", "g_post_ffn"):
        w[name] = p[name].reshape(1, -1).astype(F32)
    w_in = p["w_in"]
    offs = np.cumsum((0, M_WIDTH, M_WIDTH, M_WIDTH, M_HEADS, M_HEADS, M_WIDTH, Q_LORA, KV_LORA, QK_ROPE,
                      D_MODEL, D_MODEL))
    seg = lambda i: w_in[:, offs[i]:offs[i + 1]]
    mq, mk, mv, mi, mf, mo, cq, ckv, kr, ga, gb = (seg(i) for i in range(11))
    half = QK_ROPE // 2
    swap = lambda a: jnp.concatenate([a[..., half:], a[..., :half]], axis=-1)
    tail = jnp.concatenate([kr, swap(kr), mi, mf,
                            jnp.zeros((D_MODEL, LANES - 2 * QK_ROPE - 2 * M_HEADS), w_in.dtype)], axis=1)
    w["w_qkv"] = w_in[:, offs[0]:offs[3]].astype(BF16)
    w["w_mo_gate"] = mo.astype(BF16)
    w["w_lat"] = w_in[:, offs[6]:offs[8]].astype(BF16)
    w["w_g"] = w_in[:, offs[9]:offs[11]].astype(BF16)
    w["w_tail"] = tail.astype(BF16)
    w["wv_t"] = mv.T.astype(BF16)
    w["b_tail"] = jnp.zeros((1, LANES), F32).at[0, TAIL_IG:TAIL_IG + 2 * M_HEADS].set(p["b_if"].astype(F32))

    pad = HEAD_SLOT - QK_NOPE - QK_ROPE
    wq = p["w_uq"].reshape(Q_LORA, A_HEADS, QK_NOPE + QK_ROPE)
    zq = jnp.zeros((Q_LORA, A_HEADS, pad), wq.dtype)
    w["wq_cat"] = jnp.concatenate([wq, zq], axis=2).reshape(Q_LORA, A_HEADS * HEAD_SLOT).astype(BF16)
    w_uk, w_uv = p["w_uk"], p["w_uv"]
    zk = jnp.zeros((KV_LORA, A_HEADS, HEAD_SLOT - QK_NOPE), w_uk.dtype)
    w["wuk_cat"] = jnp.concatenate([w_uk, zk], axis=2).reshape(KV_LORA, A_HEADS * HEAD_SLOT).astype(BF16)
    w["wuv_cat"] = jnp.concatenate([w_uv, zk], axis=2).reshape(KV_LORA, A_HEADS * HEAD_SLOT).astype(BF16)
    w["wuv_heads"] = (w_uv.transpose(1, 0, 2)[:, :, None, :]
                      * jnp.eye(A_HEADS, dtype=w_uv.dtype)[:, None, :, None]).reshape(
                          A_HEADS, KV_LORA, A_HEADS * V_HEAD).astype(BF16)
    pass_np = np.zeros((A_HEADS, HEAD_SLOT - QK_NOPE, ABS_SLOT), np.float32)
    pass_np[:, np.arange(QK_ROPE), KV_LORA + np.arange(QK_ROPE)] = 1.0
    w["w_abs"] = jnp.concatenate(
        [jnp.concatenate([w_uk.transpose(1, 2, 0), jnp.zeros((A_HEADS, QK_NOPE, ABS_SLOT - KV_LORA), w_uk.dtype)],
                         axis=2).astype(BF16), jnp.asarray(pass_np, BF16)], axis=1)
    for name in ("w_branch_a", "w_branch_b", "w_out", "w_mq", "w_mk", "w_mv", "w_mo", "w_ffn_out"):
        w[name] = p[name].astype(BF16)
    w["w_ffn_in"] = p["w_ffn_in"].astype(BF16)
    return w


def _rope_tables(pos0, n):
    pos = (pos0 + jnp.arange(n)).astype(F32)
    inv = ROPE_BASE ** (-jnp.arange(0, QK_ROPE, 2, dtype=F32) / QK_ROPE)
    ang = pos[:, None] * inv[None, :]
    cos, sin = jnp.cos(ang), jnp.sin(ang)
    c32 = jnp.concatenate([cos, cos], axis=1)
    s32 = jnp.concatenate([-sin, sin], axis=1)
    one = lambda k: jnp.ones((n, k), F32)
    zero = lambda k: jnp.zeros((n, k), F32)
    ct = jnp.concatenate([c32, one(LANES - QK_ROPE)], axis=1)
    st = jnp.concatenate([s32, zero(LANES - QK_ROPE)], axis=1)
    c128 = jnp.concatenate([one(QK_NOPE), c32, one(HEAD_SLOT - QK_NOPE - QK_ROPE)], axis=1)
    s128 = jnp.concatenate([zero(QK_NOPE), s32, zero(HEAD_SLOT - QK_NOPE - QK_ROPE)], axis=1)
    return ct, st, c128, s128


def kernel(x_prompt, x_sample, cache_ckv, cache_krope, cache_mem_k, cache_mem_v, state_C, state_n, state_m, page_table, mem_prompt, g_pre_mix, w_in, b_if, g_mhead, g_q_lora, w_uq, g_kv_lora, w_uk, w_uv, w_branch_a, w_branch_b, w_out, g_post_mix, g_pre_mem, g_mem, w_mq, w_mk, w_mv, w_mo, g_post_mem, g_pre_ffn, w_ffn_in, w_ffn_out, g_post_ffn):
    params = dict(g_pre_mix=g_pre_mix, w_in=w_in, b_if=b_if, g_mhead=g_mhead, g_q_lora=g_q_lora, w_uq=w_uq,
                  g_kv_lora=g_kv_lora, w_uk=w_uk, w_uv=w_uv, w_branch_a=w_branch_a, w_branch_b=w_branch_b,
                  w_out=w_out, g_post_mix=g_post_mix, g_pre_mem=g_pre_mem, g_mem=g_mem, w_mq=w_mq, w_mk=w_mk,
                  w_mv=w_mv, w_mo=w_mo, g_post_mem=g_post_mem, g_pre_ffn=g_pre_ffn, w_ffn_in=w_ffn_in,
                  w_ffn_out=w_ffn_out, g_post_ffn=g_post_ffn)
    depth = w_in.shape[0]
    assert depth == 1, "single-layer stack only"
    bp, seq, _ = x_prompt.shape
    bs, dec_seq, _ = x_sample.shape
    assert dec_seq == 1, "one new token per sample sequence"
    past_len = page_table.shape[1] * PAGE_SIZE
    w = _prep_weights({name: a[0] for name, a in params.items()})

    xp = x_prompt.reshape(bp * seq, D_MODEL)
    qkvm, og, ckv_p, sg, krope_t, gates_t, qcat, kcat, vcat, vt = _inproj(
        xp, w, _rope_tables(0, seq), tm=TOKEN_TILE, absorbed=False, mdtype=BF16)
    hg, c_p, n_p, m_p = _mlstm_prompt(qkvm, vt, gates_t, og, w["g_mhead"], batch=bp, seq=seq)
    ob = _mla_prefill(qcat, kcat, vcat, batch=bp, seq=seq)
    x1, qm = _merge(xp, hg, ob, sg, w, tm=TOKEN_TILE, from_latent=False, qdtype=BF16)
    mem_k, mem_v = _memkv(mem_prompt.reshape(bp * N_MEM, D_MODEL), w, tm=TOKEN_TILE)
    om = _memattn(qm, mem_k, mem_v, batch=bp, seq=seq, tm=2 * TOKEN_TILE)
    y_prompt = _ffn(x1, om, w, tm=TOKEN_TILE).reshape(bp, seq, D_MODEL)

    xs = x_sample.reshape(bs, D_MODEL)
    tables_s = tuple(jnp.broadcast_to(t, (bs, LANES)) for t in _rope_tables(past_len, 1))
    qkvm_s, og_s, ckv_s, sg_s, tail_s, qabs = _inproj(xs, w, tables_s, tm=bs, absorbed=True, mdtype=F32)
    m0_pad = jnp.pad(state_m.reshape(bs, M_HEADS).astype(F32), ((0, 0), (0, LANES - M_HEADS)))
    hg_s, c_s, n_s, m_s = _mlstm_step(qkvm_s, tail_s, state_C.reshape(bs, M_HEADS, M_HEAD_DIM, M_HEAD_DIM),
                                      state_n.reshape(bs, M_HEADS, M_HEAD_DIM), m0_pad, og_s, w["g_mhead"])
    n_phys = cache_ckv.shape[1]
    cache_krope_t = jnp.transpose(cache_krope.reshape(n_phys, PAGE_SIZE, QK_ROPE), (0, 2, 1))
    o_lat = _mla_decode(page_table, qabs, cache_ckv.reshape(n_phys, PAGE_SIZE, KV_LORA), cache_krope_t, ckv_s, tail_s)
    x1_s, qm_s = _merge(xs, hg_s, o_lat.reshape(bs, A_HEADS * KV_LORA), sg_s, w, tm=bs, from_latent=True, qdtype=F32)
    om_s = _memattn_step(qm_s, cache_mem_k.reshape(bs, N_MEM // 2, 2 * X_HEADS, X_HEAD_DIM),
                         cache_mem_v.reshape(bs, N_MEM // 2, 2 * X_HEADS, X_HEAD_DIM))
    y_sample = _ffn(x1_s, om_s, w, tm=bs).reshape(bs, 1, D_MODEL)

    return (y_prompt, y_sample,
            ckv_p.reshape(1, bp, seq, KV_LORA), jnp.transpose(krope_t, (0, 2, 1)).reshape(1, bp, seq, QK_ROPE),
            c_p.reshape(1, bp, M_HEADS, M_HEAD_DIM, M_HEAD_DIM), n_p.reshape(1, bp, M_HEADS, M_HEAD_DIM),
            m_p[:, :, 0].reshape(1, bp, M_HEADS),
            mem_k.reshape(1, bp, N_MEM, X_HEADS, X_HEAD_DIM), mem_v.reshape(1, bp, N_MEM, X_HEADS, X_HEAD_DIM),
            ckv_s.reshape(1, bs, 1, KV_LORA), tail_s[:, :QK_ROPE].reshape(1, bs, 1, QK_ROPE),
            c_s.reshape(1, bs, M_HEADS, M_HEAD_DIM, M_HEAD_DIM), n_s.reshape(1, bs, M_HEADS, M_HEAD_DIM),
            m_s[:, :M_HEADS].reshape(1, bs, M_HEADS))
```
